```python
import jax, jax.numpy as jnp
from jax import lax
import numpy as np

D_MODEL = 2048
BATCH = 8
SEQ = 8192
DEPTH = 4

CHUNK = 64
DN_HEAD_DIM = 128
DN_WIDTH = D_MODEL // 2
DN_HEADS = DN_WIDTH // DN_HEAD_DIM
DN_CONV = 4
SG_WIDTH = D_MODEL // 4
SG_GROUPS = 4
SG_GROUP_DIM = SG_WIDTH // SG_GROUPS
SG_BLOCK = 128
CV_WIDTH = D_MODEL // 4
CV_GROUPS = 4
CV_KERNEL = 31
D_MIX = DN_WIDTH + SG_WIDTH + CV_WIDTH
IN_WIDTHS = (DN_WIDTH, DN_WIDTH, DN_WIDTH, DN_WIDTH, DN_HEADS, DN_HEADS,
             SG_WIDTH, SG_WIDTH, SG_WIDTH, CV_WIDTH, CV_WIDTH, CV_WIDTH)
D_IN = sum(IN_WIDTHS)
EPS = 1e-6
LN_EPS = 1e-5

kernel_name = "hybrid_deltanet_gmlp_conformer_trunk"


def rmsnorm(x, g):
    xf = x.astype(jnp.float32)
    y = xf * lax.rsqrt(jnp.mean(xf * xf, axis=-1, keepdims=True) + EPS)
    return (y * g.astype(jnp.float32)).astype(x.dtype)


def group_layernorm(x, g, b, groups):
    xf = x.astype(jnp.float32).reshape(x.shape[:-1] + (groups, -1))
    mu = jnp.mean(xf, axis=-1, keepdims=True)
    var = jnp.mean(jnp.square(xf - mu), axis=-1, keepdims=True)
    y = ((xf - mu) * lax.rsqrt(var + LN_EPS)).reshape(x.shape)
    return (y * g.astype(jnp.float32) + b.astype(jnp.float32)).astype(x.dtype)


def l2norm(t):
    return t * lax.rsqrt(jnp.sum(t * t, axis=-1, keepdims=True) + EPS)


def causal_dwconv(x, w):
    k = w.shape[0]
    return lax.conv_general_dilated(x, w[:, None, :], window_strides=(1,), padding=[(k - 1, 0)],
                                    dimension_numbers=('NWC', 'WIO', 'NWC'),
                                    feature_group_count=x.shape[-1])


def gated_delta_rule(q, k, v, g, beta):
    bsz, t_len, n_h, dk = q.shape
    dv = v.shape[-1]
    n = t_len // CHUNK

    def to_chunks(t):
        t = t.reshape((bsz, n, CHUNK, n_h) + t.shape[3:])
        return jnp.moveaxis(t, 3, 1)

    q = to_chunks(q) * (dk ** -0.5)
    k, v = to_chunks(k), to_chunks(v)
    beta, g = to_chunks(beta), to_chunks(g)
    gc = jnp.cumsum(g, axis=-1)
    tri_incl = jnp.tril(jnp.ones((CHUNK, CHUNK), dtype=bool))
    tri_strict = jnp.tril(jnp.ones((CHUNK, CHUNK), dtype=bool), k=-1)
    diff = gc[..., :, None] - gc[..., None, :]
    decay = jnp.where(tri_incl, jnp.exp(jnp.where(tri_incl, diff, 0.0)), 0.0)
    k_beta = k * beta[..., None]
    kk = jnp.einsum('bhncd,bhnsd->bhncs', k_beta, k) * decay
    a_mat = jnp.eye(CHUNK, dtype=jnp.float32) + jnp.where(tri_strict, kk, 0.0)
    rhs = jnp.concatenate([v * beta[..., None], k_beta * jnp.exp(gc)[..., None]], axis=-1)
    sol = lax.linalg.triangular_solve(a_mat, rhs, left_side=True, lower=True, unit_diagonal=True)
    u, w = sol[..., :dv], sol[..., dv:]
    qk = jnp.einsum('bhncd,bhnsd->bhncs', q, k) * decay

    xs = tuple(jnp.moveaxis(t, 2, 0) for t in (q, k, u, w, qk, gc))

    def step(state, inp):
        q_i, k_i, u_i, w_i, qk_i, g_i = inp
        v_new = u_i - jnp.einsum('bhck,bhkv->bhcv', w_i, state)
        o = (jnp.einsum('bhck,bhkv->bhcv', q_i * jnp.exp(g_i)[..., None], state)
             + jnp.einsum('bhcs,bhsv->bhcv', qk_i, v_new))
        g_last = g_i[..., -1:]
        state = (state * jnp.exp(g_last)[..., None]
                 + jnp.einsum('bhck,bhcv->bhkv', k_i * jnp.exp(g_last - g_i)[..., None], v_new))
        return state, o

    s0 = jnp.zeros((bsz, n_h, dk, dv), jnp.float32)
    _, o = lax.scan(step, s0, xs)
    return jnp.transpose(o, (1, 0, 3, 2, 4)).reshape(bsz, t_len, n_h, dv)


def hybrid_layer(x, mod, norm_g, w_in, conv_qkv, a_log, dt_bias, dn_norm_g,
                 sg_ln_g, sg_ln_b, sg_w, sg_b, cv_w, cv_b, cv_ln_g, cv_ln_b, w_out):
    bsz, t_len, _ = x.shape
    shift, scale, gate = jnp.split(mod, 3, axis=-1)
    h = rmsnorm(x, norm_g) * (1 + scale[:, None, :]) + shift[:, None, :]
    p = h @ w_in
    split_points = [int(s) for s in np.cumsum(IN_WIDTHS)[:-1]]
    (q, k, v, z, b_dn, a_dn, u_sg, v_sg, gate_sg, a_cv, b_cv, gate_cv) = jnp.split(p, split_points, axis=-1)

    qkv = jax.nn.silu(causal_dwconv(jnp.concatenate([q, k, v], axis=-1), conv_qkv))
    q, k, v = jnp.split(qkv, 3, axis=-1)
    heads = lambda t: t.reshape(bsz, t_len, DN_HEADS, DN_HEAD_DIM).astype(jnp.float32)
    q, k, v = l2norm(heads(q)), l2norm(heads(k)), heads(v)
    beta = jax.nn.sigmoid(b_dn.astype(jnp.float32))
    g = -jnp.exp(a_log.astype(jnp.float32)) * jax.nn.softplus(a_dn.astype(jnp.float32) + dt_bias.astype(jnp.float32))
    o = rmsnorm(gated_delta_rule(q, k, v, g, beta), dn_norm_g)
    y_dn = o.reshape(bsz, t_len, DN_WIDTH).astype(x.dtype) * jax.nn.silu(z)

    u_sg = jax.nn.gelu(u_sg, approximate=False)
    v_sg = group_layernorm(jax.nn.gelu(v_sg, approximate=False), sg_ln_g, sg_ln_b, SG_GROUPS)
    v_blk = v_sg.reshape(bsz, t_len // SG_BLOCK, SG_BLOCK, SG_GROUPS, SG_GROUP_DIM)
    pos_chunk = jnp.arange(SG_BLOCK) // CHUNK
    block_causal = pos_chunk[:, None] >= pos_chunk[None, :]
    w_s = jnp.where(block_causal, sg_w, 0)
    mixed = jnp.einsum('gts,bnsgc->bntgc', w_s, v_blk) + sg_b.T[None, None, :, :, None]
    y_sg = u_sg * mixed.reshape(bsz, t_len, SG_WIDTH) * jax.nn.silu(gate_sg)

    glu = a_cv * jax.nn.sigmoid(b_cv)
    dw = causal_dwconv(glu, cv_w) + cv_b
    y_cv = jax.nn.silu(group_layernorm(dw, cv_ln_g, cv_ln_b, CV_GROUPS)) * jax.nn.silu(gate_cv)

    y = jnp.concatenate([y_dn, y_sg, y_cv], axis=-1) @ w_out
    return x + gate[:, None, :] * y


def _fwd_setup_inputs(seed: int = 0) -> dict:
    key = jax.random.key(seed)
    ks = jax.random.split(key, 24)
    f32 = jnp.float32
    nrm = lambda k, shape, s: jax.random.normal(k, shape, f32) * s
    L, D = DEPTH, D_MODEL
    dt = jnp.exp(jax.random.uniform(ks[7], (L, DN_HEADS), f32, np.log(1e-3), np.log(1e-1)))
    return {
        'x': nrm(ks[0], (BATCH, SEQ, D), 1.0),
        'c': nrm(ks[1], (BATCH, D), 1.0),
        'norm_g': 1.0 + nrm(ks[2], (L, D), 0.02),
        'w_ada': nrm(ks[3], (L, D, 3 * D), 0.5 * D ** -0.5),
        'b_ada': nrm(ks[4], (L, 3 * D), 0.02),
        'w_in': nrm(ks[5], (L, D, D_IN), D ** -0.5),
        'conv_qkv': nrm(ks[6], (L, DN_CONV, 3 * DN_WIDTH), DN_CONV ** -0.5),
        'a_log': jnp.log(jax.random.uniform(ks[8], (L, DN_HEADS), f32, 1.0, 16.0)),
        'dt_bias': dt + jnp.log(-jnp.expm1(-dt)),
        'dn_norm_g': 1.0 + nrm(ks[9], (L, DN_HEAD_DIM), 0.02),
        'sg_ln_g': 1.0 + nrm(ks[10], (L, SG_WIDTH), 0.02),
        'sg_ln_b': nrm(ks[11], (L, SG_WIDTH), 0.02),
        'sg_w': nrm(ks[12], (L, SG_GROUPS, SG_BLOCK, SG_BLOCK), SG_BLOCK ** -0.5),
        'sg_b': 1.0 + nrm(ks[13], (L, SG_GROUPS, SG_BLOCK), 0.02),
        'cv_w': nrm(ks[14], (L, CV_KERNEL, CV_WIDTH), CV_KERNEL ** -0.5),
        'cv_b': nrm(ks[15], (L, CV_WIDTH), 0.02),
        'cv_ln_g': 1.0 + nrm(ks[16], (L, CV_WIDTH), 0.02),
        'cv_ln_b': nrm(ks[17], (L, CV_WIDTH), 0.02),
        'w_out': nrm(ks[18], (L, D_MIX, D), D_MIX ** -0.5),
        'final_g': 1.0 + nrm(ks[19], (D,), 0.02),
    }


def _fwd_reference(x, c, norm_g, w_ada, b_ada, w_in, conv_qkv, a_log, dt_bias, dn_norm_g,
              sg_ln_g, sg_ln_b, sg_w, sg_b, cv_w, cv_b, cv_ln_g, cv_ln_b, w_out, final_g):
    c_act = jax.nn.silu(c)
    for l in range(DEPTH):
        mod = c_act @ w_ada[l] + b_ada[l]
        x = hybrid_layer(x, mod, norm_g[l], w_in[l], conv_qkv[l], a_log[l], dt_bias[l], dn_norm_g[l],
                         sg_ln_g[l], sg_ln_b[l], sg_w[l], sg_b[l], cv_w[l], cv_b[l], cv_ln_g[l], cv_ln_b[l],
                         w_out[l])
    return rmsnorm(x, final_g)


import jax as _jax
import jax.numpy as _jnp

TWIN_FORMAT = 'train_step'
FWD_PARAMS = ['x', 'c', 'norm_g', 'w_ada', 'b_ada', 'w_in', 'conv_qkv', 'a_log', 'dt_bias', 'dn_norm_g', 'sg_ln_g', 'sg_ln_b', 'sg_w', 'sg_b', 'cv_w', 'cv_b', 'cv_ln_g', 'cv_ln_b', 'w_out', 'final_g']
TWIN_WEIGHTS = ['norm_g', 'w_ada', 'b_ada', 'w_in', 'conv_qkv', 'a_log', 'dt_bias', 'dn_norm_g', 'sg_ln_g', 'sg_ln_b', 'sg_w', 'sg_b', 'cv_w', 'cv_b', 'cv_ln_g', 'cv_ln_b', 'w_out', 'final_g']
TWIN_DIFF_INPUT = 'x'
TWIN_INPUTS = ['x', 'c', 'norm_g', 'w_ada', 'b_ada', 'w_in', 'conv_qkv', 'a_log', 'dt_bias', 'dn_norm_g', 'sg_ln_g', 'sg_ln_b', 'sg_w', 'sg_b', 'cv_w', 'cv_b', 'cv_ln_g', 'cv_ln_b', 'w_out', 'final_g', 'loss_target', 'm_norm_g', 'm_w_ada', 'm_b_ada', 'm_w_in', 'm_conv_qkv', 'm_a_log', 'm_dt_bias', 'm_dn_norm_g', 'm_sg_ln_g', 'm_sg_ln_b', 'm_sg_w', 'm_sg_b', 'm_cv_w', 'm_cv_b', 'm_cv_ln_g', 'm_cv_ln_b', 'm_w_out', 'm_final_g', 'v_norm_g', 'v_w_ada', 'v_b_ada', 'v_w_in', 'v_conv_qkv', 'v_a_log', 'v_dt_bias', 'v_dn_norm_g', 'v_sg_ln_g', 'v_sg_ln_b', 'v_sg_w', 'v_sg_b', 'v_cv_w', 'v_cv_b', 'v_cv_ln_g', 'v_cv_ln_b', 'v_w_out', 'v_final_g']
TWIN_OUTPUTS = ['loss', 'grad_x', 'grad_norm_g', 'grad_w_ada', 'grad_b_ada', 'grad_w_in', 'grad_conv_qkv', 'grad_a_log', 'grad_dt_bias', 'grad_dn_norm_g', 'grad_sg_ln_g', 'grad_sg_ln_b', 'grad_sg_w', 'grad_sg_b', 'grad_cv_w', 'grad_cv_b', 'grad_cv_ln_g', 'grad_cv_ln_b', 'grad_w_out', 'grad_final_g', 'delta_norm_g', 'delta_w_ada', 'delta_b_ada', 'delta_w_in', 'delta_conv_qkv', 'delta_a_log', 'delta_dt_bias', 'delta_dn_norm_g', 'delta_sg_ln_g', 'delta_sg_ln_b', 'delta_sg_w', 'delta_sg_b', 'delta_cv_w', 'delta_cv_b', 'delta_cv_ln_g', 'delta_cv_ln_b', 'delta_w_out', 'delta_final_g', 'new_m_norm_g', 'new_m_w_ada', 'new_m_b_ada', 'new_m_w_in', 'new_m_conv_qkv', 'new_m_a_log', 'new_m_dt_bias', 'new_m_dn_norm_g', 'new_m_sg_ln_g', 'new_m_sg_ln_b', 'new_m_sg_w', 'new_m_sg_b', 'new_m_cv_w', 'new_m_cv_b', 'new_m_cv_ln_g', 'new_m_cv_ln_b', 'new_m_w_out', 'new_m_final_g', 'new_v_norm_g', 'new_v_w_ada', 'new_v_b_ada', 'new_v_w_in', 'new_v_conv_qkv', 'new_v_a_log', 'new_v_dt_bias', 'new_v_dn_norm_g', 'new_v_sg_ln_g', 'new_v_sg_ln_b', 'new_v_sg_w', 'new_v_sg_b', 'new_v_cv_w', 'new_v_cv_b', 'new_v_cv_ln_g', 'new_v_cv_ln_b', 'new_v_w_out', 'new_v_final_g']
TWIN_LEAF_KINDS = {'loss': 'loss', 'grad_x': 'grad_x', 'grad_norm_g': 'grad_w', 'grad_w_ada': 'grad_w', 'grad_b_ada': 'grad_w', 'grad_w_in': 'grad_w', 'grad_conv_qkv': 'grad_w', 'grad_a_log': 'grad_w', 'grad_dt_bias': 'grad_w', 'grad_dn_norm_g': 'grad_w', 'grad_sg_ln_g': 'grad_w', 'grad_sg_ln_b': 'grad_w', 'grad_sg_w': 'grad_w', 'grad_sg_b': 'grad_w', 'grad_cv_w': 'grad_w', 'grad_cv_b': 'grad_w', 'grad_cv_ln_g': 'grad_w', 'grad_cv_ln_b': 'grad_w', 'grad_w_out': 'grad_w', 'grad_final_g': 'grad_w', 'delta_norm_g': 'delta_w', 'delta_w_ada': 'delta_w', 'delta_b_ada': 'delta_w', 'delta_w_in': 'delta_w', 'delta_conv_qkv': 'delta_w', 'delta_a_log': 'delta_w', 'delta_dt_bias': 'delta_w', 'delta_dn_norm_g': 'delta_w', 'delta_sg_ln_g': 'delta_w', 'delta_sg_ln_b': 'delta_w', 'delta_sg_w': 'delta_w', 'delta_sg_b': 'delta_w', 'delta_cv_w': 'delta_w', 'delta_cv_b': 'delta_w', 'delta_cv_ln_g': 'delta_w', 'delta_cv_ln_b': 'delta_w', 'delta_w_out': 'delta_w', 'delta_final_g': 'delta_w', 'new_m_norm_g': 'new_m', 'new_m_w_ada': 'new_m', 'new_m_b_ada': 'new_m', 'new_m_w_in': 'new_m', 'new_m_conv_qkv': 'new_m', 'new_m_a_log': 'new_m', 'new_m_dt_bias': 'new_m', 'new_m_dn_norm_g': 'new_m', 'new_m_sg_ln_g': 'new_m', 'new_m_sg_ln_b': 'new_m', 'new_m_sg_w': 'new_m', 'new_m_sg_b': 'new_m', 'new_m_cv_w': 'new_m', 'new_m_cv_b': 'new_m', 'new_m_cv_ln_g': 'new_m', 'new_m_cv_ln_b': 'new_m', 'new_m_w_out': 'new_m', 'new_m_final_g': 'new_m', 'new_v_norm_g': 'new_v', 'new_v_w_ada': 'new_v', 'new_v_b_ada': 'new_v', 'new_v_w_in': 'new_v', 'new_v_conv_qkv': 'new_v', 'new_v_a_log': 'new_v', 'new_v_dt_bias': 'new_v', 'new_v_dn_norm_g': 'new_v', 'new_v_sg_ln_g': 'new_v', 'new_v_sg_ln_b': 'new_v', 'new_v_sg_w': 'new_v', 'new_v_sg_b': 'new_v', 'new_v_cv_w': 'new_v', 'new_v_cv_b': 'new_v', 'new_v_cv_ln_g': 'new_v', 'new_v_cv_ln_b': 'new_v', 'new_v_w_out': 'new_v', 'new_v_final_g': 'new_v'}


def _forward(args):
    return _fwd_reference(*[args[k] for k in FWD_PARAMS])


def _output_shape():
    def fwd():
        inp = _fwd_setup_inputs(0)
        return _fwd_reference(*[inp[k] for k in FWD_PARAMS])
    out = _jax.eval_shape(fwd)
    return out.shape, out.dtype

N_MICROBATCH = 1
ADAM_LR = 0.001
ADAM_B1 = 0.9
ADAM_B2 = 0.999
ADAM_EPS = 1e-08
ADAM_WD = 0.01
ADAM_STEP = 10
PER_EXAMPLE_BATCH_AXIS = {'x': 0, 'c': 0, 'loss_target': 0}
SHARED_INPUTS = []
_WEIGHT_DTYPES = {'norm_g': _jnp.float32, 'w_ada': _jnp.float32, 'b_ada': _jnp.float32, 'w_in': _jnp.float32, 'conv_qkv': _jnp.float32, 'a_log': _jnp.float32, 'dt_bias': _jnp.float32, 'dn_norm_g': _jnp.float32, 'sg_ln_g': _jnp.float32, 'sg_ln_b': _jnp.float32, 'sg_w': _jnp.float32, 'sg_b': _jnp.float32, 'cv_w': _jnp.float32, 'cv_b': _jnp.float32, 'cv_ln_g': _jnp.float32, 'cv_ln_b': _jnp.float32, 'w_out': _jnp.float32, 'final_g': _jnp.float32}
MOMENT_SCALE = {'norm_g': 3.448434e-02, 'w_ada': 3.151745e-02, 'b_ada': 5.473731e-02, 'w_in': 1.942093e-02, 'conv_qkv': 1.840875e-02, 'a_log': 9.107481e-02, 'dt_bias': 8.845632e-02, 'dn_norm_g': 6.925649e-02, 'sg_ln_g': 1.609404e-02, 'sg_ln_b': 1.585620e-02, 'sg_w': 1.539471e-02, 'sg_b': 1.999954e-02, 'cv_w': 1.531356e-02, 'cv_b': 2.973076e-02, 'cv_ln_g': 1.785468e-02, 'cv_ln_b': 1.482902e-02, 'w_out': 2.209795e-02, 'final_g': 3.198005e+01}


def _to_microbatches(a, axis):
    t = _jnp.moveaxis(a, axis, 0)
    t = t.reshape((N_MICROBATCH, t.shape[0] // N_MICROBATCH) + t.shape[1:])
    return _jnp.moveaxis(t, 1, axis + 1)


def setup_inputs(seed: int = 0) -> dict:
    inp = _fwd_setup_inputs(seed)
    key = _jax.random.fold_in(_jax.random.key(seed), 7919)
    shape, _ = _output_shape()
    out = dict(inp)
    out["loss_target"] = _jax.random.normal(_jax.random.fold_in(key, 0), shape, _jnp.float32)
    for i, name in enumerate(TWIN_WEIGHTS):
        w = inp[name].astype(_jnp.float32)
        if MOMENT_SCALE is None:
            s = _jnp.sqrt(_jnp.mean(_jnp.square(w)) + 1e-30)
        else:
            s = MOMENT_SCALE[name]
        km, kv = _jax.random.split(_jax.random.fold_in(key, i + 1))
        out[name] = w
        out["m_" + name] = s * _jax.random.normal(km, w.shape, _jnp.float32)
        out["v_" + name] = (s * s) * _jax.random.uniform(kv, w.shape, _jnp.float32, 0.5, 1.5)
    if N_MICROBATCH > 1:
        for name, axis in PER_EXAMPLE_BATCH_AXIS.items():
            out[name] = _to_microbatches(out[name], axis)
    return {'x': out['x'], 'c': out['c'], 'norm_g': out['norm_g'], 'w_ada': out['w_ada'], 'b_ada': out['b_ada'], 'w_in': out['w_in'], 'conv_qkv': out['conv_qkv'], 'a_log': out['a_log'], 'dt_bias': out['dt_bias'], 'dn_norm_g': out['dn_norm_g'], 'sg_ln_g': out['sg_ln_g'], 'sg_ln_b': out['sg_ln_b'], 'sg_w': out['sg_w'], 'sg_b': out['sg_b'], 'cv_w': out['cv_w'], 'cv_b': out['cv_b'], 'cv_ln_g': out['cv_ln_g'], 'cv_ln_b': out['cv_ln_b'], 'w_out': out['w_out'], 'final_g': out['final_g'], 'loss_target': out['loss_target'], 'm_norm_g': out['m_norm_g'], 'm_w_ada': out['m_w_ada'], 'm_b_ada': out['m_b_ada'], 'm_w_in': out['m_w_in'], 'm_conv_qkv': out['m_conv_qkv'], 'm_a_log': out['m_a_log'], 'm_dt_bias': out['m_dt_bias'], 'm_dn_norm_g': out['m_dn_norm_g'], 'm_sg_ln_g': out['m_sg_ln_g'], 'm_sg_ln_b': out['m_sg_ln_b'], 'm_sg_w': out['m_sg_w'], 'm_sg_b': out['m_sg_b'], 'm_cv_w': out['m_cv_w'], 'm_cv_b': out['m_cv_b'], 'm_cv_ln_g': out['m_cv_ln_g'], 'm_cv_ln_b': out['m_cv_ln_b'], 'm_w_out': out['m_w_out'], 'm_final_g': out['m_final_g'], 'v_norm_g': out['v_norm_g'], 'v_w_ada': out['v_w_ada'], 'v_b_ada': out['v_b_ada'], 'v_w_in': out['v_w_in'], 'v_conv_qkv': out['v_conv_qkv'], 'v_a_log': out['v_a_log'], 'v_dt_bias': out['v_dt_bias'], 'v_dn_norm_g': out['v_dn_norm_g'], 'v_sg_ln_g': out['v_sg_ln_g'], 'v_sg_ln_b': out['v_sg_ln_b'], 'v_sg_w': out['v_sg_w'], 'v_sg_b': out['v_sg_b'], 'v_cv_w': out['v_cv_w'], 'v_cv_b': out['v_cv_b'], 'v_cv_ln_g': out['v_cv_ln_g'], 'v_cv_ln_b': out['v_cv_ln_b'], 'v_w_out': out['v_w_out'], 'v_final_g': out['v_final_g']}


def _loss(weights, diff, rest, loss_target):
    with _jax.named_scope("forward"):
        args = {**rest, TWIN_DIFF_INPUT: diff, **{k: w.astype(_WEIGHT_DTYPES[k]) for k, w in weights.items()}}
        y = _forward(args)
    with _jax.named_scope("loss_head"):
        err = _jnp.square(y.astype(_jnp.float32) - loss_target)
        return 0.5 * _jnp.sum(_jnp.mean(err, axis=-1)) if err.ndim else 0.5 * err


def _adamw(w, g, m, v):
    m = ADAM_B1 * m + (1.0 - ADAM_B1) * g
    v = ADAM_B2 * v + (1.0 - ADAM_B2) * _jnp.square(g)
    m_hat = m / (1.0 - ADAM_B1 ** ADAM_STEP)
    v_hat = v / (1.0 - ADAM_B2 ** ADAM_STEP)
    delta = -ADAM_LR * (m_hat / (_jnp.sqrt(v_hat) + ADAM_EPS) + ADAM_WD * w)
    return delta, m, v


def reference(x, c, norm_g, w_ada, b_ada, w_in, conv_qkv, a_log, dt_bias, dn_norm_g, sg_ln_g, sg_ln_b, sg_w, sg_b, cv_w, cv_b, cv_ln_g, cv_ln_b, w_out, final_g, loss_target, m_norm_g, m_w_ada, m_b_ada, m_w_in, m_conv_qkv, m_a_log, m_dt_bias, m_dn_norm_g, m_sg_ln_g, m_sg_ln_b, m_sg_w, m_sg_b, m_cv_w, m_cv_b, m_cv_ln_g, m_cv_ln_b, m_w_out, m_final_g, v_norm_g, v_w_ada, v_b_ada, v_w_in, v_conv_qkv, v_a_log, v_dt_bias, v_dn_norm_g, v_sg_ln_g, v_sg_ln_b, v_sg_w, v_sg_b, v_cv_w, v_cv_b, v_cv_ln_g, v_cv_ln_b, v_w_out, v_final_g):
    given = dict(x=x, c=c, norm_g=norm_g, w_ada=w_ada, b_ada=b_ada, w_in=w_in, conv_qkv=conv_qkv, a_log=a_log, dt_bias=dt_bias, dn_norm_g=dn_norm_g, sg_ln_g=sg_ln_g, sg_ln_b=sg_ln_b, sg_w=sg_w, sg_b=sg_b, cv_w=cv_w, cv_b=cv_b, cv_ln_g=cv_ln_g, cv_ln_b=cv_ln_b, w_out=w_out, final_g=final_g, loss_target=loss_target, m_norm_g=m_norm_g, m_w_ada=m_w_ada, m_b_ada=m_b_ada, m_w_in=m_w_in, m_conv_qkv=m_conv_qkv, m_a_log=m_a_log, m_dt_bias=m_dt_bias, m_dn_norm_g=m_dn_norm_g, m_sg_ln_g=m_sg_ln_g, m_sg_ln_b=m_sg_ln_b, m_sg_w=m_sg_w, m_sg_b=m_sg_b, m_cv_w=m_cv_w, m_cv_b=m_cv_b, m_cv_ln_g=m_cv_ln_g, m_cv_ln_b=m_cv_ln_b, m_w_out=m_w_out, m_final_g=m_final_g, v_norm_g=v_norm_g, v_w_ada=v_w_ada, v_b_ada=v_b_ada, v_w_in=v_w_in, v_conv_qkv=v_conv_qkv, v_a_log=v_a_log, v_dt_bias=v_dt_bias, v_dn_norm_g=v_dn_norm_g, v_sg_ln_g=v_sg_ln_g, v_sg_ln_b=v_sg_ln_b, v_sg_w=v_sg_w, v_sg_b=v_sg_b, v_cv_w=v_cv_w, v_cv_b=v_cv_b, v_cv_ln_g=v_cv_ln_g, v_cv_ln_b=v_cv_ln_b, v_w_out=v_w_out, v_final_g=v_final_g)
    weights = {n: given[n] for n in TWIN_WEIGHTS}
    shared = {n: given[n] for n in SHARED_INPUTS}
    per_example = {n: given[n] for n in ['x', 'c']}
    grad_fn = _jax.value_and_grad(_loss, argnums=(0, 1))

    def one_microbatch(ex, loss_target):
        ex = dict(ex)
        diff = ex.pop(TWIN_DIFF_INPUT)
        return grad_fn(weights, diff, {**shared, **ex}, loss_target)

    if N_MICROBATCH == 1:
        loss, (grad_w, grad_x) = one_microbatch(per_example, given["loss_target"])
    else:
        def body(carry, xs):
            loss_sum, grad_sum = carry
            l_k, (gw_k, gx_k) = one_microbatch(xs[0], xs[1])
            with _jax.named_scope("update"):
                return (loss_sum + l_k, _jax.tree.map(_jnp.add, grad_sum, gw_k)), gx_k

        init = (_jnp.zeros((), _jnp.float32), _jax.tree.map(_jnp.zeros_like, weights))
        (loss, grad_w), grad_x = _jax.lax.scan(body, init, (per_example, given["loss_target"]))
    with _jax.named_scope("update"):
        delta_w, new_m, new_v = {}, {}, {}
        for n in TWIN_WEIGHTS:
            delta_w[n], new_m[n], new_v[n] = _adamw(weights[n], grad_w[n], given["m_" + n], given["v_" + n])
    return (loss, grad_x, *[grad_w[n] for n in TWIN_WEIGHTS], *[delta_w[n] for n in TWIN_WEIGHTS],
            *[new_m[n] for n in TWIN_WEIGHTS], *[new_v[n] for n in TWIN_WEIGHTS])
```

```python
import functools

import jax
import jax.numpy as jnp
from jax import lax
from jax.experimental import pallas as pl
from jax.experimental.pallas import tpu as pltpu

f32 = jnp.float32
BF = jnp.bfloat16
N_DEV = 8
LANES = 128
CHUNK = 64
HD = 128
DN_CONV = 4
SG_BLOCK = 128
SG_GROUPS = 4
CV_GROUPS = 4
CV_K = 31
CV_HALO = 32
QK_HALO = 16
EPS = 1e-6
LN_EPS = 1e-5
ADAM_LR, ADAM_B1, ADAM_B2, ADAM_EPS, ADAM_WD, ADAM_STEP = 0.001, 0.9, 0.999, 1e-08, 0.01, 10
VMEM_LIMIT = 56 * 1024 * 1024
HI = lax.Precision.HIGHEST
MESH = pl.DeviceIdType.MESH


def _cp(sem=None, vmem=VMEM_LIMIT):
    return pltpu.CompilerParams(dimension_semantics=sem, vmem_limit_bytes=vmem)


def _pc(body, **kw):
    return pl.pallas_call(body, **kw)


def _dot(a, b):
    return jnp.dot(a.astype(BF), b.astype(BF), preferred_element_type=f32)


def _dot_nt(a, b):
    return lax.dot_general(a.astype(BF), b.astype(BF), (((1,), (1,)), ((), ())), preferred_element_type=f32)


def _dot_tn(a, b):
    return lax.dot_general(a.astype(BF), b.astype(BF), (((0,), (0,)), ((), ())), preferred_element_type=f32)


def _dot_hi(a, b):
    return jnp.dot(a, b, preferred_element_type=f32, precision=HI)


def _sigmoid(x):
    return 1.0 / (1.0 + jnp.exp(-x))


def _silu(x):
    return x * _sigmoid(x)


def _dsilu(x):
    s = _sigmoid(x)
    return s * (1.0 + x * (1.0 - s))


def _softplus(x):
    return jnp.maximum(x, 0.0) + jnp.log(1.0 + jnp.exp(-jnp.abs(x)))


def _gelu(x):
    return 0.5 * x * (1.0 + lax.erf(x * 0.7071067811865476))


def _dgelu(x):
    return 0.5 * (1.0 + lax.erf(x * 0.7071067811865476)) + x * jnp.exp(-0.5 * x * x) * 0.3989422804014327


def _lanecol(x, lane_iota, j):
    return jnp.sum(jnp.where(lane_iota == j, x, 0.0), axis=1, keepdims=True)


def _tri_inv(l_mat):
    n = l_mat.shape[0]
    eye = (lax.broadcasted_iota(jnp.int32, (n, n), 0) == lax.broadcasted_iota(jnp.int32, (n, n), 1)).astype(f32)
    x = eye - l_mat
    p = l_mat
    for _ in range(5):
        p = _dot_hi(p, p)
        x = x + _dot_hi(x, p)
    return x


def _me():
    x, y, c = lax.axis_index("x"), lax.axis_index("y"), lax.axis_index("c")
    return x, y, c, 4 * x + 2 * y + c


def _exchange(arrs, kinds, name):
    n = len(arrs)

    def body(*refs):
        ins, outs = refs[:n], refs[n:2 * n]
        send, recv, loc = refs[2 * n:]
        x, y, c, me = _me()
        pending = []
        for a in range(n):
            own = ins[a].at[me] if kinds[a] == "scatter" else ins[a]
            lc = pltpu.make_async_copy(own, outs[a].at[me], loc.at[a])
            lc.start()
            pending.append(lc)
        remote = []
        for a in range(n):
            for k in range(1, N_DEV):
                px = jnp.bitwise_xor(x, (k >> 2) & 1)
                py = jnp.bitwise_xor(y, (k >> 1) & 1)
                pc = jnp.bitwise_xor(c, k & 1)
                peer = 4 * px + 2 * py + pc
                src = ins[a].at[peer] if kinds[a] == "scatter" else ins[a]
                r = pltpu.make_async_remote_copy(src_ref=src, dst_ref=outs[a].at[me], send_sem=send.at[a, k - 1],
                                                 recv_sem=recv.at[a, k - 1], device_id=(px, py, pc), device_id_type=MESH)
                r.start()
                remote.append(r)
        for r in remote:
            r.wait()
        for lc in pending:
            lc.wait()

    out_shape = []
    for a in range(n):
        shp = arrs[a].shape if kinds[a] == "scatter" else (N_DEV,) + arrs[a].shape
        out_shape.append(jax.ShapeDtypeStruct(shp, arrs[a].dtype))
    any_spec = pl.BlockSpec(memory_space=pl.ANY)
    return _pc(body, name=name, out_shape=tuple(out_shape), in_specs=[any_spec] * n, out_specs=tuple([any_spec] * n),
               scratch_shapes=[pltpu.SemaphoreType.DMA((n, N_DEV - 1)), pltpu.SemaphoreType.DMA((n, N_DEV - 1)),
                               pltpu.SemaphoreType.DMA((n,))])(*arrs)


def _ada_fwd(c_all, w_ada, b_my):
    nl, d, cols = w_ada.shape

    def body(c_ref, w_ref, b_ref, o_ref):
        ca = _silu(c_ref[...])
        o_ref[0] = _dot(ca, w_ref[0]) + b_ref[0]

    return _pc(body, name="ada_fwd", grid=(nl,), out_shape=jax.ShapeDtypeStruct((nl, N_DEV, cols), f32),
               in_specs=[pl.BlockSpec((N_DEV, d), lambda l: (0, 0)), pl.BlockSpec((1, d, cols), lambda l: (l, 0, 0)),
                         pl.BlockSpec((1, 1, cols), lambda l: (l, 0, 0))],
               out_specs=pl.BlockSpec((1, N_DEV, cols), lambda l: (l, 0, 0)), compiler_params=_cp(("arbitrary",)))(c_all, w_ada, b_my)


def _adam_math(w, g, m, v):
    m = ADAM_B1 * m + (1.0 - ADAM_B1) * g
    v = ADAM_B2 * v + (1.0 - ADAM_B2) * (g * g)
    m_hat = m / (1.0 - ADAM_B1 ** ADAM_STEP)
    v_hat = v / (1.0 - ADAM_B2 ** ADAM_STEP)
    delta = -ADAM_LR * (m_hat / (jnp.sqrt(v_hat) + ADAM_EPS) + ADAM_WD * w)
    return delta, m, v


def _ada_bwd_adam(c_all, dmod_my, w, m, v):
    nl, d, cols = w.shape
    tk = 512

    def body(c_ref, dm_ref, w_ref, m_ref, v_ref, g_ref, dl_ref, nm_ref, nv_ref):
        ca = _silu(c_ref[...])
        g = _dot_tn(ca, dm_ref[0])
        dl, nm, nv = _adam_math(w_ref[0], g, m_ref[0], v_ref[0])
        g_ref[0] = g
        dl_ref[0] = dl
        nm_ref[0] = nm
        nv_ref[0] = nv

    wspec = pl.BlockSpec((1, tk, cols), lambda l, k: (l, k, 0))
    shp = jax.ShapeDtypeStruct(w.shape, f32)
    return _pc(body, name="ada_bwd_adam", grid=(nl, d // tk), out_shape=(shp, shp, shp, shp),
               in_specs=[pl.BlockSpec((N_DEV, tk), lambda l, k: (0, k)), pl.BlockSpec((1, N_DEV, cols), lambda l, k: (l, 0, 0)),
                         wspec, wspec, wspec],
               out_specs=(wspec, wspec, wspec, wspec), compiler_params=_cp(("arbitrary", "arbitrary")))(c_all, dmod_my, w, m, v)


def _adam_slabs(slabs, w, m, v, name, tr):
    n, rows, cols = slabs.shape
    tr = min(tr, rows)

    def body(s_ref, w_ref, m_ref, v_ref, g_ref, dl_ref, nm_ref, nv_ref):
        g = s_ref[0].astype(f32)
        for j in range(1, n):
            g = g + s_ref[j].astype(f32)
        dl, nm, nv = _adam_math(w_ref[...], g, m_ref[...], v_ref[...])
        g_ref[...] = g
        dl_ref[...] = dl
        nm_ref[...] = nm
        nv_ref[...] = nv

    spec = pl.BlockSpec((tr, cols), lambda i: (i, 0))
    shp = jax.ShapeDtypeStruct((rows, cols), f32)
    return _pc(body, name=name, grid=(rows // tr,), out_shape=(shp, shp, shp, shp),
               in_specs=[pl.BlockSpec((n, tr, cols), lambda i: (0, i, 0)), spec, spec, spec],
               out_specs=(spec, spec, spec, spec), compiler_params=_cp(("arbitrary",)))(slabs, w, m, v)


def _sum_slabs(slabs, name):
    n, rows, cols = slabs.shape
    tr = min(rows, 1024)

    def body(s_ref, o_ref):
        g = s_ref[0]
        for j in range(1, n):
            g = g + s_ref[j]
        o_ref[...] = g

    return _pc(body, name=name, grid=(rows // tr,), out_shape=jax.ShapeDtypeStruct((rows, cols), f32),
               in_specs=[pl.BlockSpec((n, tr, cols), lambda i: (0, i, 0))], out_specs=pl.BlockSpec((tr, cols), lambda i: (i, 0)),
               compiler_params=_cp(("arbitrary",)))(slabs)


def _inproj(x, ng, sc1, sh, wmain, wsmall):
    t, d = x.shape
    n = wmain.shape[1]
    tm, tn, rb = min(t, 1024), 1024, 128

    def body(x_ref, ng_ref, sc_ref, sh_ref, wm_ref, ws_ref, p_ref, ps_ref, h_ref, hs_ref):
        @pl.when(pl.program_id(1) == 0)
        def _():
            def rows(i, carry):
                r0 = pl.multiple_of(i * rb, rb)
                xv = x_ref[pl.ds(r0, rb), :]
                r = lax.rsqrt(jnp.mean(xv * xv, axis=1, keepdims=True) + EPS)
                hb = (((xv * r) * ng_ref[...]) * sc_ref[...] + sh_ref[...]).astype(BF)
                hs_ref[pl.ds(r0, rb), :] = hb
                h_ref[pl.ds(r0, rb), :] = hb
                return carry
            lax.fori_loop(0, tm // rb, rows, 0)
            ps_ref[...] = jnp.dot(hs_ref[...], ws_ref[...], preferred_element_type=f32)
        p_ref[...] = jnp.dot(hs_ref[...], wm_ref[...], preferred_element_type=f32).astype(BF)

    vec = pl.BlockSpec((1, d), lambda i, j: (0, 0))
    return _pc(body, name="inproj", grid=(t // tm, n // tn),
               out_shape=(jax.ShapeDtypeStruct((t, n), BF), jax.ShapeDtypeStruct((t, LANES), f32), jax.ShapeDtypeStruct((t, d), BF)),
               in_specs=[pl.BlockSpec((tm, d), lambda i, j: (i, 0)), vec, vec, vec,
                         pl.BlockSpec((d, tn), lambda i, j: (0, j)), pl.BlockSpec((d, LANES), lambda i, j: (0, 0))],
               out_specs=(pl.BlockSpec((tm, tn), lambda i, j: (i, j)), pl.BlockSpec((tm, LANES), lambda i, j: (i, 0)),
                          pl.BlockSpec((tm, d), lambda i, j: (i, 0))),
               scratch_shapes=[pltpu.VMEM((tm, d), BF)], compiler_params=_cp(("arbitrary", "arbitrary")))(x, ng, sc1, sh, wmain, wsmall)


P_Z, P_SGU, P_SGV, P_SGG, P_CVA, P_CVB, P_CVG, P_Q = 0, 1024, 1536, 2048, 2560, 3072, 3584, 4096


def _dn_specs(t, tb, rev):
    nt = t // tb
    ti = (lambda i: nt - 1 - i) if rev else (lambda i: i)
    qb = P_Q // 1024
    hb = tb // QK_HALO
    specs = []
    for s in range(3):
        specs.append(pl.BlockSpec((tb, 1024), lambda i, s=s: (ti(i), qb + s)))
    for s in range(3):
        specs.append(pl.BlockSpec((QK_HALO, 1024), lambda i, s=s: (jnp.maximum(ti(i) * hb - 1, 0), qb + s)))
    specs.append(pl.BlockSpec((DN_CONV, 3072), lambda i: (0, 0)))
    specs.append(pl.BlockSpec((tb, LANES), lambda i: (ti(i), 0)))
    specs.append(pl.BlockSpec((tb // CHUNK, 16, CHUNK), lambda i: (ti(i), 0, 0)))
    specs.append(pl.BlockSpec((1, LANES), lambda i: (0, 0)))
    specs.append(pl.BlockSpec((1, LANES), lambda i: (0, 0)))
    specs.append(pl.BlockSpec((16, CHUNK), lambda i: (0, 0)))
    specs.append(pl.BlockSpec((16, CHUNK), lambda i: (0, 0)))
    return specs, ti


def _dn_conv(first, tiles, halos, cw_ref, xs_ref, cpre_ref, tb):
    for s in range(3):
        c0 = s * 1024
        xs_ref[0:QK_HALO, c0:c0 + 1024] = jnp.where(first, 0.0, halos[s][...].astype(f32))
        xs_ref[QK_HALO:QK_HALO + tb, c0:c0 + 1024] = tiles[s][...].astype(f32)
    for sub in range(3072 // 256):
        cs = slice(sub * 256, (sub + 1) * 256)
        acc = cw_ref[0:1, cs] * xs_ref[pl.ds(QK_HALO - 3, tb), cs]
        for j in range(1, DN_CONV):
            acc = acc + cw_ref[j:j + 1, cs] * xs_ref[pl.ds(QK_HALO - 3 + j, tb), cs]
        cpre_ref[:, cs] = acc


def _dn_head(cpre_ref, r0, h, beta_all, gc_all, gcrow_ref, lane, tri_incl, tri_strict):
    rows = pl.ds(r0, CHUNK)
    cq = cpre_ref[rows, pl.ds(HD * h, HD)]
    ck = cpre_ref[rows, pl.ds(1024 + HD * h, HD)]
    cv = cpre_ref[rows, pl.ds(2048 + HD * h, HD)]
    sq, sk, v = _silu(cq), _silu(ck), _silu(cv)
    rq = lax.rsqrt(jnp.sum(sq * sq, axis=1, keepdims=True) + EPS)
    rk = lax.rsqrt(jnp.sum(sk * sk, axis=1, keepdims=True) + EPS)
    q, k = sq * rq, sk * rk
    beta = _lanecol(beta_all, lane, h)
    gcc = _lanecol(gc_all, lane, 8 + h)
    gcr = gcrow_ref[pl.ds(8 + h, 1), :]
    diff = gcc - gcr
    dmat = jnp.where(tri_incl, jnp.exp(jnp.where(tri_incl, diff, 0.0)), 0.0)
    gam = jnp.exp(gcc)
    glast = gcr[:, CHUNK - 1:CHUNK]
    gam_last = jnp.exp(glast)
    kscale = jnp.exp(glast - gcc)
    qs = q * (HD ** -0.5)
    kb = k * beta
    vb = v * beta
    kbg = kb * gam
    kk = _dot_nt(kb, k)
    lmat = jnp.where(tri_strict, kk * dmat, 0.0)
    pmat = _dot_nt(qs, k) * dmat
    return dict(cq=cq, ck=ck, cv=cv, sq=sq, sk=sk, v=v, rq=rq, rk=rk, q=q, k=k, beta=beta, gcc=gcc, dmat=dmat, gam=gam,
                gam_last=gam_last, kscale=kscale, qs=qs, kb=kb, vb=vb, kbg=kbg, lmat=lmat, pmat=pmat)


def _dn_gates(ps_ref, arow_ref, dtb_ref, pst_ref, acol_ref, dtc_ref):
    ps = ps_ref[...]
    beta_all = _sigmoid(ps)
    g_all = arow_ref[...] * _softplus(ps + dtb_ref[...])
    return ps, beta_all, g_all


def _dn_fwd(p, ps, pst, convw, arow, dtb, acol, dtc):
    t = p.shape[0]
    nh = 1024 // HD
    tb = min(t, 256)
    nc = tb // CHUNK
    specs, _ = _dn_specs(t, tb, False)

    def body(q_ref, k_ref, v_ref, hq_ref, hk_ref, hv_ref, cw_ref, ps_ref, pst_ref, arow_ref, dtb_ref, acol_ref, dtc_ref,
             o_ref, ss_ref, ti_ref, xs_ref, cpre_ref, st_ref, gcrow_ref, ball_ref, gall_ref):
        i = pl.program_id(0)

        @pl.when(i == 0)
        def _():
            st_ref[...] = jnp.zeros_like(st_ref)

        _dn_conv(i == 0, (q_ref, k_ref, v_ref), (hq_ref, hk_ref, hv_ref), cw_ref, xs_ref, cpre_ref, tb)
        _, beta_all, g_all = _dn_gates(ps_ref, arow_ref, dtb_ref, pst_ref, acol_ref, dtc_ref)
        ball_ref[...] = beta_all
        gall_ref[...] = g_all
        ri = lax.broadcasted_iota(jnp.int32, (CHUNK, CHUNK), 0)
        ci = lax.broadcasted_iota(jnp.int32, (CHUNK, CHUNK), 1)
        tri_incl, tri_strict = ri >= ci, ri > ci
        lower = tri_incl.astype(f32)
        upper = (ri <= ci).astype(f32)
        lane = lax.broadcasted_iota(jnp.int32, (CHUNK, LANES), 1)

        def chunk(c, carry):
            r0 = pl.multiple_of(c * CHUNK, CHUNK)
            gc_all = _dot_hi(lower, gall_ref[pl.ds(r0, CHUNK), :])
            g_rows = acol_ref[...] * _softplus(pst_ref[c] + dtc_ref[...])
            gcrow_ref[...] = _dot_hi(g_rows, upper)
            beta_c = ball_ref[pl.ds(r0, CHUNK), :]
            for h in range(nh):
                d = _dn_head(cpre_ref, r0, h, beta_c, gc_all, gcrow_ref, lane, tri_incl, tri_strict)
                tinv = _tri_inv(d["lmat"])
                u = _dot(tinv, d["vb"])
                w = _dot(tinv, d["kbg"])
                s = st_ref[h]
                vn = u - _dot(w, s)
                o = _dot(d["qs"] * d["gam"], s) + _dot(d["pmat"], vn)
                o_ref[pl.ds(r0, CHUNK), pl.ds(HD * h, HD)] = o
                ss_ref[c, h] = s.astype(BF)
                ti_ref[c, h] = tinv
                st_ref[h] = s * d["gam_last"] + _dot_tn(d["k"] * d["kscale"], vn)
            return carry

        lax.fori_loop(0, nc, chunk, 0)

    return _pc(body, name="dn_fwd", grid=(t // tb,),
               out_shape=(jax.ShapeDtypeStruct((t, 1024), f32), jax.ShapeDtypeStruct((t // CHUNK, nh, HD, HD), BF),
                          jax.ShapeDtypeStruct((t // CHUNK, nh, CHUNK, CHUNK), f32)),
               in_specs=specs,
               out_specs=(pl.BlockSpec((tb, 1024), lambda i: (i, 0)), pl.BlockSpec((nc, nh, HD, HD), lambda i: (i, 0, 0, 0)),
                          pl.BlockSpec((nc, nh, CHUNK, CHUNK), lambda i: (i, 0, 0, 0))),
               scratch_shapes=[pltpu.VMEM((tb + QK_HALO, 3072), f32), pltpu.VMEM((tb, 3072), f32), pltpu.VMEM((nh, HD, HD), f32),
                               pltpu.VMEM((16, CHUNK), f32), pltpu.VMEM((tb, LANES), f32), pltpu.VMEM((tb, LANES), f32)],
               compiler_params=_cp(("arbitrary",)))(p, p, p, p, p, p, convw, ps, pst, arow, dtb, acol, dtc)


def _dn_bwd(p, ps, pst, convw, arow, dtb, acol, dtc, do, ss, tinv_all):
    t = p.shape[0]
    nh = 1024 // HD
    tb = min(t, 256)
    nc = tb // CHUNK
    nt = t // tb
    specs, ti = _dn_specs(t, tb, True)
    specs = specs + [pl.BlockSpec((tb, 1024), lambda i: (ti(i), 0)), pl.BlockSpec((nc, nh, HD, HD), lambda i: (ti(i), 0, 0, 0)),
                     pl.BlockSpec((nc, nh, CHUNK, CHUNK), lambda i: (ti(i), 0, 0, 0))]

    def body(q_ref, k_ref, v_ref, hq_ref, hk_ref, hv_ref, cw_ref, ps_ref, pst_ref, arow_ref, dtb_ref, acol_ref, dtc_ref,
             do_ref, ss_ref, ti_ref,
             dp_ref, dps_ref, dcw_ref, dal_ref, ddt_ref,
             xs_ref, cpre_ref, dst_ref, gcrow_ref, ball_ref, gall_ref, dcs_ref, dball_ref, dgcall_ref):
        i = pl.program_id(0)

        @pl.when(i == 0)
        def _():
            dst_ref[...] = jnp.zeros_like(dst_ref)
            dcs_ref[...] = jnp.zeros_like(dcs_ref)
            dcw_ref[...] = jnp.zeros_like(dcw_ref)
            dal_ref[...] = jnp.zeros_like(dal_ref)
            ddt_ref[...] = jnp.zeros_like(ddt_ref)

        _dn_conv(i == nt - 1, (q_ref, k_ref, v_ref), (hq_ref, hk_ref, hv_ref), cw_ref, xs_ref, cpre_ref, tb)
        ps, beta_all, g_all = _dn_gates(ps_ref, arow_ref, dtb_ref, pst_ref, acol_ref, dtc_ref)
        ball_ref[...] = beta_all
        gall_ref[...] = g_all
        ri = lax.broadcasted_iota(jnp.int32, (CHUNK, CHUNK), 0)
        ci = lax.broadcasted_iota(jnp.int32, (CHUNK, CHUNK), 1)
        tri_incl, tri_strict = ri >= ci, ri > ci
        lower = tri_incl.astype(f32)
        upper = (ri <= ci).astype(f32)
        lane = lax.broadcasted_iota(jnp.int32, (CHUNK, LANES), 1)
        row_last = lax.broadcasted_iota(jnp.int32, (CHUNK, 1), 0) == CHUNK - 1
        ones_cl = jnp.ones((CHUNK, LANES), f32)

        def chunk(cc, carry):
            c = nc - 1 - cc
            r0 = pl.multiple_of(c * CHUNK, CHUNK)
            rows = pl.ds(r0, CHUNK)
            gc_all = _dot_hi(lower, gall_ref[rows, :])
            g_rows = acol_ref[...] * _softplus(pst_ref[c] + dtc_ref[...])
            gcrow_ref[...] = _dot_hi(g_rows, upper)
            beta_c = ball_ref[rows, :]
            dbeta_all = jnp.zeros((CHUNK, LANES), f32)
            dgc_all = jnp.zeros((CHUNK, LANES), f32)
            for h in range(nh):
                d = _dn_head(cpre_ref, r0, h, beta_c, gc_all, gcrow_ref, lane, tri_incl, tri_strict)
                tinv = ti_ref[c, h]
                s = ss_ref[c, h].astype(f32)
                u = _dot(tinv, d["vb"])
                w = _dot(tinv, d["kbg"])
                vn = u - _dot(w, s)
                qg = d["qs"] * d["gam"]
                kt = d["k"] * d["kscale"]
                do_h = do_ref[rows, pl.ds(HD * h, HD)]
                dsn = dst_ref[h]
                dvn = _dot_tn(d["pmat"], do_h) + _dot(kt, dsn)
                dpm = _dot_nt(do_h, vn)
                dqg = _dot_nt(do_h, s)
                dkt = _dot_nt(vn, dsn)
                dw = -_dot_nt(dvn, s)
                dgam_last = jnp.sum(jnp.sum(s * dsn, axis=1, keepdims=True), axis=0, keepdims=True)
                dst_ref[h] = _dot_tn(qg, do_h) + d["gam_last"] * dsn - _dot_tn(w, dvn)
                dtm = _dot_nt(dvn, d["vb"]) + _dot_nt(dw, d["kbg"])
                dvb = _dot_tn(tinv, dvn)
                dkbg = _dot_tn(tinv, dw)
                dl = -_dot_nt(_dot_tn(tinv, dtm), tinv)
                dl = jnp.where(tri_strict, dl, 0.0)
                dpm = jnp.where(tri_incl, dpm, 0.0)
                mm = dl * d["lmat"] + dpm * d["pmat"]
                dkk = dl * d["dmat"]
                dqk = dpm * d["dmat"]
                dkb = _dot(dkk, d["k"]) + dkbg * d["gam"]
                dk = _dot_tn(dkk, d["kb"]) + _dot_tn(dqk, d["qs"]) + dkt * d["kscale"] + dkb * d["beta"]
                dqs = _dot(dqk, d["k"]) + dqg * d["gam"]
                dq = dqs * (HD ** -0.5)
                dbeta = jnp.sum(dkb * d["k"] + dvb * d["v"], axis=1, keepdims=True)
                dv = dvb * d["beta"]
                dgam = jnp.sum(dkbg * d["kb"] + dqg * d["qs"], axis=1, keepdims=True)
                ktdk = jnp.sum(dkt * kt, axis=1, keepdims=True)
                colsum = lax.dot_general(mm, ones_cl, (((0,), (0,)), ((), ())), preferred_element_type=f32, precision=HI)[:, 0:1]
                dgc = jnp.sum(mm, axis=1, keepdims=True) - colsum + dgam * d["gam"] - ktdk
                extra = jnp.sum(ktdk, axis=0, keepdims=True) + dgam_last * d["gam_last"]
                dgc = dgc + jnp.where(row_last, extra, 0.0)
                dbeta_all = jnp.where(lane == h, dbeta, dbeta_all)
                dgc_all = jnp.where(lane == 8 + h, dgc, dgc_all)
                dsq = d["rq"] * (dq - d["q"] * jnp.sum(dq * d["q"], axis=1, keepdims=True))
                dsk = d["rk"] * (dk - d["k"] * jnp.sum(dk * d["k"], axis=1, keepdims=True))
                dcs_ref[rows, pl.ds(HD * h, HD)] = dsq * _dsilu(d["cq"])
                dcs_ref[rows, pl.ds(1024 + HD * h, HD)] = dsk * _dsilu(d["ck"])
                dcs_ref[rows, pl.ds(2048 + HD * h, HD)] = dv * _dsilu(d["cv"])
            dball_ref[rows, :] = dbeta_all
            dgcall_ref[rows, :] = _dot_hi(upper, dgc_all)
            return carry

        lax.fori_loop(0, nc, chunk, 0)

        dbeta_t = dball_ref[...]
        dg_t = dgcall_ref[...]
        sg = _sigmoid(ps + dtb_ref[...])
        da = dg_t * arow_ref[...] * sg
        dps_ref[...] = dbeta_t * beta_all * (1.0 - beta_all) + da
        ddt_ref[...] += jnp.sum(da, axis=0, keepdims=True)
        dal_ref[...] += jnp.sum(dg_t * g_all, axis=0, keepdims=True)

        for sub in range(3072 // 256):
            cs = slice(sub * 256, (sub + 1) * 256)
            dc = dcs_ref[0:tb, cs]
            acc = cw_ref[DN_CONV - 1:DN_CONV, cs] * dc
            for j in range(DN_CONV - 1):
                acc = acc + cw_ref[j:j + 1, cs] * dcs_ref[pl.ds(DN_CONV - 1 - j, tb), cs]
            dp_ref[:, cs] = acc.astype(BF)
            for j in range(DN_CONV):
                dcw_ref[j:j + 1, cs] += jnp.sum(dc * xs_ref[pl.ds(QK_HALO - 3 + j, tb), cs], axis=0, keepdims=True)
        dcs_ref[tb:tb + QK_HALO, :] = dcs_ref[0:QK_HALO, :]

    vecl = pl.BlockSpec((1, LANES), lambda i: (0, 0))
    return _pc(body, name="dn_bwd", grid=(nt,),
               out_shape=(jax.ShapeDtypeStruct((t, 3072), BF), jax.ShapeDtypeStruct((t, LANES), f32),
                          jax.ShapeDtypeStruct((DN_CONV, 3072), f32), jax.ShapeDtypeStruct((1, LANES), f32),
                          jax.ShapeDtypeStruct((1, LANES), f32)),
               in_specs=specs,
               out_specs=(pl.BlockSpec((tb, 3072), lambda i: (ti(i), 0)), pl.BlockSpec((tb, LANES), lambda i: (ti(i), 0)),
                          pl.BlockSpec((DN_CONV, 3072), lambda i: (0, 0)), vecl, vecl),
               scratch_shapes=[pltpu.VMEM((tb + QK_HALO, 3072), f32), pltpu.VMEM((tb, 3072), f32), pltpu.VMEM((nh, HD, HD), f32),
                               pltpu.VMEM((16, CHUNK), f32), pltpu.VMEM((tb, LANES), f32), pltpu.VMEM((tb, LANES), f32),
                               pltpu.VMEM((tb + QK_HALO, 3072), f32), pltpu.VMEM((tb, LANES), f32), pltpu.VMEM((tb, LANES), f32)],
               compiler_params=_cp(("arbitrary",)))(p, p, p, p, p, p, convw, ps, pst, arow, dtb, acol, dtc, do, ss, tinv_all)


def _ln_fwd(x, g, b):
    mu = jnp.mean(x, axis=1, keepdims=True)
    xc = x - mu
    rstd = lax.rsqrt(jnp.mean(xc * xc, axis=1, keepdims=True) + LN_EPS)
    xhat = xc * rstd
    return xhat * g + b, xhat, rstd


def _ln_bwd(dy, xhat, rstd, g):
    dxh = dy * g
    return rstd * (dxh - jnp.mean(dxh, axis=1, keepdims=True) - xhat * jnp.mean(dxh * xhat, axis=1, keepdims=True))


def _br_specs(t, tb, rev):
    nt = t // tb
    ti = (lambda i: nt - 1 - i) if rev else (lambda i: i)
    hb = tb // CV_HALO
    col = lambda off: off // 512
    specs = [pl.BlockSpec((tb, 1024), lambda i: (ti(i), 0)),
             pl.BlockSpec((tb, 1024), lambda i: (ti(i), 0))]
    for off in (P_SGU, P_SGV, P_SGG, P_CVA, P_CVB, P_CVG):
        specs.append(pl.BlockSpec((tb, 512), lambda i, off=off: (ti(i), col(off))))
    for off in (P_CVA, P_CVB):
        specs.append(pl.BlockSpec((CV_HALO, 512), lambda i, off=off: (jnp.maximum(ti(i) * hb - 1, 0), col(off))))
    v512 = pl.BlockSpec((1, 512), lambda i: (0, 0))
    specs += [pl.BlockSpec((1, HD), lambda i: (0, 0)), v512, v512,
              pl.BlockSpec((SG_GROUPS, SG_BLOCK, SG_BLOCK), lambda i: (0, 0, 0)),
              pl.BlockSpec((SG_BLOCK, 512), lambda i: (0, 0)),
              pl.BlockSpec((CV_HALO, 512), lambda i: (0, 0)),
              v512, v512, v512]
    return specs, ti


def _sg_mask():
    r = lax.broadcasted_iota(jnp.int32, (SG_BLOCK, SG_BLOCK), 0) // CHUNK
    c = lax.broadcasted_iota(jnp.int32, (SG_BLOCK, SG_BLOCK), 1) // CHUNK
    return r >= c


def _cv_glu(first, a_ref, b_ref, ha_ref, hb_ref, gs_ref, tb):
    ha = jnp.where(first, 0.0, ha_ref[...].astype(f32))
    gs_ref[0:CV_HALO, :] = ha * _sigmoid(hb_ref[...].astype(f32))
    gs_ref[CV_HALO:CV_HALO + tb, :] = a_ref[...].astype(f32) * _sigmoid(b_ref[...].astype(f32))


def _cv_conv(gs_ref, cw_ref, cb_ref, tb, cs):
    acc = cb_ref[:, cs] + cw_ref[0:1, cs] * gs_ref[pl.ds(CV_HALO - (CV_K - 1), tb), cs]
    for j in range(1, CV_K):
        acc = acc + cw_ref[j:j + 1, cs] * gs_ref[pl.ds(CV_HALO - (CV_K - 1) + j, tb), cs]
    return acc


def _branch_fwd(o, p, dng, slg, slb, sgw, sgbias, cvw, cvb, clg, clb):
    t = o.shape[0]
    tb = min(t, 256)
    specs, _ = _br_specs(t, tb, False)

    def body(o_ref, z_ref, su_ref, sv_ref, sgt_ref, ca_ref, cb_ref, cg_ref, ha_ref, hb_ref,
             dng_ref, slg_ref, slb_ref, sgw_ref, sgb_ref, cvw_ref, cvb_ref, clg_ref, clb_ref, y_ref, gs_ref, vln_ref):
        i = pl.program_id(0)
        for h in range(1024 // HD):
            cs = slice(h * HD, (h + 1) * HD)
            oh = o_ref[:, cs]
            r = lax.rsqrt(jnp.mean(oh * oh, axis=1, keepdims=True) + EPS)
            y_ref[:, cs] = ((oh * r) * dng_ref[...] * _silu(z_ref[:, cs].astype(f32))).astype(BF)
        mask = _sg_mask()
        for g in range(SG_GROUPS):
            cs = slice(g * HD, (g + 1) * HD)
            vg = _gelu(sv_ref[:, cs].astype(f32))
            vln, _, _ = _ln_fwd(vg, slg_ref[:, cs], slb_ref[:, cs])
            vln_ref[...] = vln
            ws = jnp.where(mask, sgw_ref[g], 0.0)
            for nb in range(tb // SG_BLOCK):
                rs = slice(nb * SG_BLOCK, (nb + 1) * SG_BLOCK)
                mixed = _dot(ws, vln_ref[rs, :]) + sgb_ref[:, cs]
                u = _gelu(su_ref[rs, cs].astype(f32))
                y_ref[rs, 1024 + g * HD:1024 + (g + 1) * HD] = (u * mixed * _silu(sgt_ref[rs, cs].astype(f32))).astype(BF)
        _cv_glu(i == 0, ca_ref, cb_ref, ha_ref, hb_ref, gs_ref, tb)
        for g in range(CV_GROUPS):
            cs = slice(g * HD, (g + 1) * HD)
            dw = _cv_conv(gs_ref, cvw_ref, cvb_ref, tb, cs)
            ln, _, _ = _ln_fwd(dw, clg_ref[:, cs], clb_ref[:, cs])
            y_ref[:, 1536 + g * HD:1536 + (g + 1) * HD] = (_silu(ln) * _silu(cg_ref[:, cs].astype(f32))).astype(BF)

    return _pc(body, name="branch_fwd", grid=(t // tb,), out_shape=jax.ShapeDtypeStruct((t, 2048), BF),
               in_specs=specs, out_specs=pl.BlockSpec((tb, 2048), lambda i: (i, 0)),
               scratch_shapes=[pltpu.VMEM((tb + CV_HALO, 512), f32), pltpu.VMEM((tb, HD), f32)],
               compiler_params=_cp(("arbitrary",)))(o, p, p, p, p, p, p, p, p, p, dng, slg, slb, sgw, sgbias, cvw, cvb, clg, clb)


def _branch_bwd(o, p, dng, slg, slb, sgw, sgbias, cvw, cvb, clg, clb, dy):
    t = o.shape[0]
    tb = min(t, 256)
    nt = t // tb
    specs, ti = _br_specs(t, tb, True)
    specs = specs + [pl.BlockSpec((tb, 2048), lambda i: (ti(i), 0))]

    def body(o_ref, z_ref, su_ref, sv_ref, sgt_ref, ca_ref, cb_ref, cg_ref, ha_ref, hb_ref,
             dng_ref, slg_ref, slb_ref, sgw_ref, sgb_ref, cvw_ref, cvb_ref, clg_ref, clb_ref, dy_ref,
             dp_ref, do_ref, ddng_ref, dslg_ref, dslb_ref, dsgw_ref, dsgb_ref, dcvw_ref, dcvb_ref, dclg_ref, dclb_ref,
             gs_ref, vln_ref, xh_ref, dvl_ref, ddw_ref, dbias_ref):
        i = pl.program_id(0)

        @pl.when(i == 0)
        def _():
            for r in (ddng_ref, dslg_ref, dslb_ref, dsgw_ref, dsgb_ref, dcvw_ref, dcvb_ref, dclg_ref, dclb_ref, ddw_ref, dbias_ref):
                r[...] = jnp.zeros_like(r)

        for h in range(1024 // HD):
            cs = slice(h * HD, (h + 1) * HD)
            oh = o_ref[:, cs]
            zz = z_ref[:, cs].astype(f32)
            dyh = dy_ref[:, cs].astype(f32)
            r = lax.rsqrt(jnp.mean(oh * oh, axis=1, keepdims=True) + EPS)
            nrm = oh * r
            sz = _silu(zz)
            dn = dyh * dng_ref[...] * sz
            ddng_ref[...] += jnp.sum(dyh * nrm * sz, axis=0, keepdims=True)
            dp_ref[:, cs] = (dyh * nrm * dng_ref[...] * _dsilu(zz)).astype(BF)
            do_ref[:, cs] = r * (dn - nrm * jnp.mean(dn * nrm, axis=1, keepdims=True))
        mask = _sg_mask()
        for g in range(SG_GROUPS):
            cs = slice(g * HD, (g + 1) * HD)
            sv = sv_ref[:, cs].astype(f32)
            vln, xhat, rstd = _ln_fwd(_gelu(sv), slg_ref[:, cs], slb_ref[:, cs])
            vln_ref[...] = vln
            xh_ref[...] = xhat
            ws = jnp.where(mask, sgw_ref[g], 0.0)
            dws = jnp.zeros((SG_BLOCK, SG_BLOCK), f32)
            for nb in range(tb // SG_BLOCK):
                rs = slice(nb * SG_BLOCK, (nb + 1) * SG_BLOCK)
                dyb = dy_ref[rs, 1024 + g * HD:1024 + (g + 1) * HD].astype(f32)
                su = su_ref[rs, cs].astype(f32)
                gt = sgt_ref[rs, cs].astype(f32)
                mixed = _dot(ws, vln_ref[rs, :]) + sgb_ref[:, cs]
                u = _gelu(su)
                sgt = _silu(gt)
                dmixed = dyb * u * sgt
                dp_ref[rs, P_SGU + g * HD:P_SGU + (g + 1) * HD] = (dyb * mixed * sgt * _dgelu(su)).astype(BF)
                dp_ref[rs, P_SGG + g * HD:P_SGG + (g + 1) * HD] = (dyb * u * mixed * _dsilu(gt)).astype(BF)
                dvl_ref[rs, :] = _dot_tn(ws, dmixed)
                dws = dws + _dot_nt(dmixed, vln_ref[rs, :])
                dbias_ref[:, cs] += dmixed
            dsgw_ref[g] += jnp.where(mask, dws, 0.0)
            dvl = dvl_ref[...]
            xhat = xh_ref[...]
            dslg_ref[:, cs] += jnp.sum(dvl * xhat, axis=0, keepdims=True)
            dslb_ref[:, cs] += jnp.sum(dvl, axis=0, keepdims=True)
            dvg = _ln_bwd(dvl, xhat, rstd, slg_ref[:, cs])
            dp_ref[:, P_SGV + g * HD:P_SGV + (g + 1) * HD] = (dvg * _dgelu(sv)).astype(BF)
        _cv_glu(i == nt - 1, ca_ref, cb_ref, ha_ref, hb_ref, gs_ref, tb)
        for g in range(CV_GROUPS):
            cs = slice(g * HD, (g + 1) * HD)
            dyc = dy_ref[:, 1536 + g * HD:1536 + (g + 1) * HD].astype(f32)
            cg = cg_ref[:, cs].astype(f32)
            dw = _cv_conv(gs_ref, cvw_ref, cvb_ref, tb, cs)
            ln, xhat, rstd = _ln_fwd(dw, clg_ref[:, cs], clb_ref[:, cs])
            dln = dyc * _silu(cg) * _dsilu(ln)
            dp_ref[:, P_CVG + g * HD:P_CVG + (g + 1) * HD] = (dyc * _silu(ln) * _dsilu(cg)).astype(BF)
            dclg_ref[:, cs] += jnp.sum(dln * xhat, axis=0, keepdims=True)
            dclb_ref[:, cs] += jnp.sum(dln, axis=0, keepdims=True)
            ddw = _ln_bwd(dln, xhat, rstd, clg_ref[:, cs])
            dcvb_ref[:, cs] += jnp.sum(ddw, axis=0, keepdims=True)
            ddw_ref[0:tb, cs] = ddw
            dglu = cvw_ref[CV_K - 1:CV_K, cs] * ddw
            for j in range(CV_K - 1):
                dglu = dglu + cvw_ref[j:j + 1, cs] * ddw_ref[pl.ds(CV_K - 1 - j, tb), cs]
            for j in range(CV_K):
                dcvw_ref[j:j + 1, cs] += jnp.sum(ddw * gs_ref[pl.ds(CV_HALO - (CV_K - 1) + j, tb), cs], axis=0, keepdims=True)
            a = ca_ref[:, cs].astype(f32)
            sb = _sigmoid(cb_ref[:, cs].astype(f32))
            dp_ref[:, P_CVA + g * HD:P_CVA + (g + 1) * HD] = (dglu * sb).astype(BF)
            dp_ref[:, P_CVB + g * HD:P_CVB + (g + 1) * HD] = (dglu * a * sb * (1.0 - sb)).astype(BF)
        ddw_ref[tb:tb + CV_HALO, :] = ddw_ref[0:CV_HALO, :]

        @pl.when(i == nt - 1)
        def _():
            lane = lax.broadcasted_iota(jnp.int32, (SG_BLOCK, LANES), 1)
            acc = jnp.zeros((SG_BLOCK, LANES), f32)
            for g in range(SG_GROUPS):
                acc = jnp.where(lane == g, jnp.sum(dbias_ref[:, g * HD:(g + 1) * HD], axis=1, keepdims=True), acc)
            dsgb_ref[...] = acc

    v512 = pl.BlockSpec((1, 512), lambda i: (0, 0))
    return _pc(body, name="branch_bwd", grid=(nt,),
               out_shape=(jax.ShapeDtypeStruct((t, 4096), BF), jax.ShapeDtypeStruct((t, 1024), f32),
                          jax.ShapeDtypeStruct((1, HD), f32), jax.ShapeDtypeStruct((1, 512), f32), jax.ShapeDtypeStruct((1, 512), f32),
                          jax.ShapeDtypeStruct((SG_GROUPS, SG_BLOCK, SG_BLOCK), f32), jax.ShapeDtypeStruct((SG_BLOCK, LANES), f32),
                          jax.ShapeDtypeStruct((CV_HALO, 512), f32), jax.ShapeDtypeStruct((1, 512), f32),
                          jax.ShapeDtypeStruct((1, 512), f32), jax.ShapeDtypeStruct((1, 512), f32)),
               in_specs=specs,
               out_specs=(pl.BlockSpec((tb, 4096), lambda i: (ti(i), 0)), pl.BlockSpec((tb, 1024), lambda i: (ti(i), 0)),
                          pl.BlockSpec((1, HD), lambda i: (0, 0)), v512, v512,
                          pl.BlockSpec((SG_GROUPS, SG_BLOCK, SG_BLOCK), lambda i: (0, 0, 0)),
                          pl.BlockSpec((SG_BLOCK, LANES), lambda i: (0, 0)), pl.BlockSpec((CV_HALO, 512), lambda i: (0, 0)),
                          v512, v512, v512),
               scratch_shapes=[pltpu.VMEM((tb + CV_HALO, 512), f32), pltpu.VMEM((tb, HD), f32), pltpu.VMEM((tb, HD), f32),
                               pltpu.VMEM((tb, HD), f32), pltpu.VMEM((tb + CV_HALO, 512), f32), pltpu.VMEM((SG_BLOCK, 512), f32)],
               compiler_params=_cp(("arbitrary",)))(o, p, p, p, p, p, p, p, p, p, dng, slg, slb, sgw, sgbias, cvw, cvb, clg, clb, dy)


def _outproj(ycat, w, x, gate):
    t, d = x.shape
    tm = min(t, 512)

    def body(y_ref, w_ref, x_ref, g_ref, o_ref):
        o_ref[...] = x_ref[...] + g_ref[...] * jnp.dot(y_ref[...], w_ref[...], preferred_element_type=f32)

    return _pc(body, name="outproj", grid=(t // tm,), out_shape=jax.ShapeDtypeStruct((t, d), f32),
               in_specs=[pl.BlockSpec((tm, ycat.shape[1]), lambda i: (i, 0)), pl.BlockSpec(w.shape, lambda i: (0, 0)),
                         pl.BlockSpec((tm, d), lambda i: (i, 0)), pl.BlockSpec((1, d), lambda i: (0, 0))],
               out_specs=pl.BlockSpec((tm, d), lambda i: (i, 0)), compiler_params=_cp(("arbitrary",)))(ycat, w, x, gate)


def _outproj_bwd(dxo, w, gate):
    t, d = dxo.shape
    tm = min(t, 512)

    def body(d_ref, w_ref, g_ref, o_ref):
        o_ref[...] = _dot_nt(d_ref[...] * g_ref[...], w_ref[...]).astype(BF)

    return _pc(body, name="outproj_bwd", grid=(t // tm,), out_shape=jax.ShapeDtypeStruct((t, w.shape[0]), BF),
               in_specs=[pl.BlockSpec((tm, d), lambda i: (i, 0)), pl.BlockSpec(w.shape, lambda i: (0, 0)),
                         pl.BlockSpec((1, d), lambda i: (0, 0))],
               out_specs=pl.BlockSpec((tm, w.shape[0]), lambda i: (i, 0)), compiler_params=_cp(("arbitrary",)))(dxo, w, gate)


def _tn_acc(a, b, name):
    kk, m = a.shape
    n = b.shape[1]
    tk, tn = min(kk, 512), min(n, 1024)

    def body(a_ref, b_ref, o_ref):
        @pl.when(pl.program_id(1) == 0)
        def _():
            o_ref[...] = jnp.zeros_like(o_ref)
        o_ref[...] += _dot_tn(a_ref[...], b_ref[...])

    return _pc(body, name=name, grid=(n // tn, kk // tk), out_shape=jax.ShapeDtypeStruct((m, n), f32),
               in_specs=[pl.BlockSpec((tk, m), lambda j, k: (k, 0)), pl.BlockSpec((tk, tn), lambda j, k: (k, j))],
               out_specs=pl.BlockSpec((m, tn), lambda j, k: (0, j)), compiler_params=_cp(("arbitrary", "arbitrary")))(a, b)


def _wout_grad(gmat, w, gate):
    m, n = gmat.shape
    tr = m // N_DEV

    def body(g_ref, w_ref, gt_ref, o_ref, dg_ref):
        @pl.when(pl.program_id(0) == 0)
        def _():
            dg_ref[...] = jnp.zeros_like(dg_ref)
        gm = g_ref[...]
        o_ref[0] = (gm * gt_ref[...]).astype(BF)
        dg_ref[...] += jnp.sum(gm * w_ref[...].astype(f32), axis=0, keepdims=True)

    return _pc(body, name="wout_grad", grid=(N_DEV,),
               out_shape=(jax.ShapeDtypeStruct((N_DEV, tr, n), BF), jax.ShapeDtypeStruct((1, n), f32)),
               in_specs=[pl.BlockSpec((tr, n), lambda i: (i, 0)), pl.BlockSpec((tr, n), lambda i: (i, 0)),
                         pl.BlockSpec((1, n), lambda i: (0, 0))],
               out_specs=(pl.BlockSpec((1, tr, n), lambda i: (i, 0, 0)), pl.BlockSpec((1, n), lambda i: (0, 0))),
               compiler_params=_cp(("arbitrary",)))(gmat, w, gate)


def _inproj_bwd(dpa, dpb, dps, wmain, wsmall, x, ng, sc1, dxo):
    t, d = x.shape
    na, nb = dpa.shape[1] // 1024, dpb.shape[1] // 1024
    nk = na + nb
    tm, rb = min(t, 512), 128

    def body(a_ref, b_ref, s_ref, wm_ref, ws_ref, x_ref, ng_ref, sc_ref, dxo_ref, dx_ref, dsh_ref, dsc_ref, dng_ref, acc_ref):
        i, k = pl.program_id(0), pl.program_id(1)

        @pl.when((i == 0) & (k == 0))
        def _():
            dsh_ref[...] = jnp.zeros_like(dsh_ref)
            dsc_ref[...] = jnp.zeros_like(dsc_ref)
            dng_ref[...] = jnp.zeros_like(dng_ref)

        @pl.when(k == 0)
        def _():
            acc_ref[...] = _dot_nt(s_ref[...], ws_ref[...])

        @pl.when(k < na)
        def _():
            acc_ref[...] += _dot_nt(a_ref[...], wm_ref[...])

        @pl.when(k >= na)
        def _():
            acc_ref[...] += _dot_nt(b_ref[...], wm_ref[...])

        @pl.when(k == nk - 1)
        def _():
            def rows(j, carry):
                r0 = pl.multiple_of(j * rb, rb)
                rr = pl.ds(r0, rb)
                xv = x_ref[rr, :]
                dh = acc_ref[rr, :]
                r = lax.rsqrt(jnp.mean(xv * xv, axis=1, keepdims=True) + EPS)
                xn = xv * r
                dsh_ref[...] += jnp.sum(dh, axis=0, keepdims=True)
                dsc_ref[...] += jnp.sum(dh * (xn * ng_ref[...]), axis=0, keepdims=True)
                dng_ref[...] += jnp.sum(dh * sc_ref[...] * xn, axis=0, keepdims=True)
                dxn = dh * (ng_ref[...] * sc_ref[...])
                dx_ref[rr, :] = r * (dxn - xn * jnp.mean(dxn * xn, axis=1, keepdims=True)) + dxo_ref[rr, :]
                return carry
            lax.fori_loop(0, tm // rb, rows, 0)

    vec = pl.BlockSpec((1, d), lambda i, k: (0, 0))
    row = pl.BlockSpec((tm, d), lambda i, k: (i, 0))
    shp = jax.ShapeDtypeStruct((1, d), f32)
    return _pc(body, name="inproj_bwd", grid=(t // tm, nk), out_shape=(jax.ShapeDtypeStruct((t, d), f32), shp, shp, shp),
               in_specs=[pl.BlockSpec((tm, 1024), lambda i, k: (i, jnp.minimum(k, na - 1))),
                         pl.BlockSpec((tm, 1024), lambda i, k: (i, jnp.clip(k - na, 0, nb - 1))),
                         pl.BlockSpec((tm, LANES), lambda i, k: (i, 0)),
                         pl.BlockSpec((d, 1024), lambda i, k: (0, k)), pl.BlockSpec((d, LANES), lambda i, k: (0, 0)),
                         row, vec, vec, row],
               out_specs=(row, vec, vec, vec), scratch_shapes=[pltpu.VMEM((tm, d), f32)],
               compiler_params=_cp(("arbitrary", "arbitrary")))(dpa, dpb, dps, wmain, wsmall, x, ng, sc1, dxo)


def _loss_head(x, fg, tgt):
    t, d = x.shape
    tm = min(t, 512)

    def body(x_ref, g_ref, t_ref, dx_ref, l_ref, dg_ref):
        @pl.when(pl.program_id(0) == 0)
        def _():
            l_ref[...] = jnp.zeros_like(l_ref)
            dg_ref[...] = jnp.zeros_like(dg_ref)
        xv = x_ref[...]
        r = lax.rsqrt(jnp.mean(xv * xv, axis=1, keepdims=True) + EPS)
        xn = xv * r
        err = xn * g_ref[...] - t_ref[...]
        l_ref[...] += 0.5 * jnp.sum(jnp.mean(err * err, axis=1, keepdims=True), axis=0, keepdims=True)
        dy = err * (1.0 / d)
        dg_ref[...] += jnp.sum(dy * xn, axis=0, keepdims=True)
        dxn = dy * g_ref[...]
        dx_ref[...] = r * (dxn - xn * jnp.mean(dxn * xn, axis=1, keepdims=True))

    row = pl.BlockSpec((tm, d), lambda i: (i, 0))
    vec = pl.BlockSpec((1, d), lambda i: (0, 0))
    return _pc(body, name="loss_head", grid=(t // tm,),
               out_shape=(jax.ShapeDtypeStruct((t, d), f32), jax.ShapeDtypeStruct((1, 1), f32), jax.ShapeDtypeStruct((1, d), f32)),
               in_specs=[row, vec, row], out_specs=(row, pl.BlockSpec((1, 1), lambda i: (0, 0)), vec),
               compiler_params=_cp(("arbitrary",)))(x, fg, tgt)


def _pack(arrs, rows_mult=8):
    flat = jnp.concatenate([a.reshape(-1).astype(f32) for a in arrs])
    n = flat.shape[0]
    per = rows_mult * LANES
    pad = (-n) % per
    if pad:
        flat = jnp.concatenate([flat, jnp.zeros((pad,), f32)])
    return flat.reshape(-1, LANES)


def _unpack(packed, shapes, lead=()):
    flat = packed.reshape(lead + (-1,))
    out, off = [], 0
    for s in shapes:
        n = 1
        for v in s:
            n *= v
        out.append(flat[..., off:off + n].reshape(lead + tuple(s)))
        off += n
    return out


def _pad_lanes(v, at):
    return jnp.pad(v.astype(f32), (at, LANES - at - v.shape[0])).reshape(1, LANES)


def kernel(x, c, norm_g, w_ada, b_ada, w_in, conv_qkv, a_log, dt_bias, dn_norm_g, sg_ln_g, sg_ln_b, sg_w, sg_b, cv_w, cv_b, cv_ln_g, cv_ln_b, w_out, final_g, loss_target, m_norm_g, m_w_ada, m_b_ada, m_w_in, m_conv_qkv, m_a_log, m_dt_bias, m_dn_norm_g, m_sg_ln_g, m_sg_ln_b, m_sg_w, m_sg_b, m_cv_w, m_cv_b, m_cv_ln_g, m_cv_ln_b, m_w_out, m_final_g, v_norm_g, v_w_ada, v_b_ada, v_w_in, v_conv_qkv, v_a_log, v_dt_bias, v_dn_norm_g, v_sg_ln_g, v_sg_ln_b, v_sg_w, v_sg_b, v_cv_w, v_cv_b, v_cv_ln_g, v_cv_ln_b, v_w_out, v_final_g):
    nl, d = norm_g.shape
    t = x.shape[1]
    nh = a_log.shape[1]
    xi, yi, ci = lax.axis_index("x"), lax.axis_index("y"), lax.axis_index("c")
    me = 4 * xi + 2 * yi + ci
    x0 = x[0]
    tgt = loss_target[0]
    ada_cols = w_ada.shape[2]
    in_cols = w_in.shape[2]
    cq_cols = conv_qkv.shape[2]
    cvw_cols = cv_w.shape[2]

    start_shapes = [c.shape, conv_qkv.shape, cv_w.shape]
    start_pack = _pack([c, conv_qkv, cv_w])
    g_small, g_win, g_wout = _exchange([start_pack, w_in.astype(BF), w_out.astype(BF)], ["gather"] * 3, "gather_weights")
    c_all, cq_g, cvw_g = _unpack(g_small, start_shapes, lead=(N_DEV,))
    c_all = c_all.reshape(N_DEV, d)
    conv_full = jnp.moveaxis(cq_g, 0, 2).reshape(nl, DN_CONV, N_DEV * cq_cols)
    cvw_full = jnp.moveaxis(cvw_g, 0, 2).reshape(nl, CV_K, N_DEV * cvw_cols)
    w_nat = jnp.moveaxis(g_win, 0, 2).reshape(nl, d, N_DEV * in_cols)
    w_main = jnp.concatenate([w_nat[:, :, 3072:4096], w_nat[:, :, 4112:], w_nat[:, :, :3072]], axis=2)
    w_small = jnp.pad(w_nat[:, :, 4096:4112], ((0, 0), (0, 0), (0, LANES - 16)))
    w_out_full = g_wout.transpose(1, 0, 2, 3).reshape(nl, N_DEV * w_out.shape[1], w_out.shape[2])

    b_my = lax.dynamic_slice_in_dim(b_ada, me * ada_cols, ada_cols, axis=1).reshape(nl, 1, ada_cols)
    mod_part = _ada_fwd(c_all, w_ada, b_my)
    (mod_g,) = _exchange([mod_part], ["gather"], "gather_mod")
    mod_mine = lax.dynamic_index_in_dim(mod_g, me, axis=2, keepdims=False)
    mod = jnp.moveaxis(mod_mine, 0, 1).reshape(nl, N_DEV * ada_cols)
    shift, scale, gate = mod[:, :d], mod[:, d:2 * d], mod[:, 2 * d:]

    arow = [_pad_lanes(-jnp.exp(a_log[l]), 8) for l in range(nl)]
    dtb = [_pad_lanes(dt_bias[l], 8) for l in range(nl)]
    saved = []
    xl = x0
    for l in range(nl):
        sc1 = (1.0 + scale[l]).reshape(1, d)
        p, ps, hb = _inproj(xl, norm_g[l].reshape(1, d), sc1, shift[l].reshape(1, d), w_main[l], w_small[l])
        pst = ps[:, :16].reshape(t // CHUNK, CHUNK, 16).transpose(0, 2, 1)
        acol = jnp.broadcast_to(arow[l][0, :16].reshape(16, 1), (16, CHUNK))
        dtc = jnp.broadcast_to(dtb[l][0, :16].reshape(16, 1), (16, CHUNK))
        o, ss, tinv = _dn_fwd(p, ps, pst, conv_full[l], arow[l], dtb[l], acol, dtc)
        sgbias = jnp.repeat(sg_b[l].T, HD, axis=1)
        cvw_pad = jnp.pad(cvw_full[l], ((0, CV_HALO - CV_K), (0, 0)))
        br_par = (dn_norm_g[l].reshape(1, HD), sg_ln_g[l].reshape(1, -1), sg_ln_b[l].reshape(1, -1), sg_w[l], sgbias, cvw_pad,
                  cv_b[l].reshape(1, -1), cv_ln_g[l].reshape(1, -1), cv_ln_b[l].reshape(1, -1))
        ycat = _branch_fwd(o, p, *br_par)
        xn = _outproj(ycat, w_out_full[l], xl, gate[l].reshape(1, d))
        saved.append((xl, p, ps, pst, hb, o, ss, tinv, ycat, sc1, acol, dtc, br_par))
        xl = xn

    dx, loss_part, dfinal_g = _loss_head(xl, final_g.reshape(1, d), tgt)
    loss = lax.psum(loss_part[0, 0], ("x", "y", "c"))

    small_grads = [None] * nl
    dmods = [None] * nl
    win_slabs = [None] * nl
    wout_slabs = [None] * nl
    for l in reversed(range(nl)):
        xl, p, ps, pst, hb, o, ss, tinv, ycat, sc1, acol, dtc, br_par = saved[l]
        gate_l = gate[l].reshape(1, d)
        dycat = _outproj_bwd(dx, w_out_full[l], gate_l)
        gmat = _tn_acc(ycat, dx, "wout_tn")
        wout_slabs[l], dgate = _wout_grad(gmat, w_out_full[l], gate_l)
        dpa, do, ddng, dslg, dslb, dsgw, dsgb, dcvw, dcvb, dclg, dclb = _branch_bwd(o, p, *br_par, dycat)
        dpb, dps, dconvw, dal, ddt = _dn_bwd(p, ps, pst, conv_full[l], arow[l], dtb[l], acol, dtc, do, ss, tinv)
        dx, dshift, dscale, dng = _inproj_bwd(dpa, dpb, dps, w_main[l], w_small[l], xl, norm_g[l].reshape(1, d), sc1, dx)
        gwa = _tn_acc(hb, dpa, "win_tn_a")
        gwb = _tn_acc(hb, dpb, "win_tn_b")
        gws = _tn_acc(hb, dps, "win_tn_s")
        g_nat = jnp.concatenate([gwb, gwa[:, :1024], gws[:, :16], gwa[:, 1024:]], axis=1)
        win_slabs[l] = g_nat.astype(BF).reshape(d, N_DEV, in_cols).transpose(1, 0, 2)
        dmods[l] = jnp.concatenate([dshift[0], dscale[0], dgate[0]])
        small_grads[l] = (dng[0], dal[0, 8:8 + nh], ddt[0, 8:8 + nh], ddng[0], dslg[0], dslb[0], dsgw, dsgb[:, :SG_GROUPS].T,
                          dcvb[0], dclg[0], dclb[0], dconvw, dcvw[:CV_K])

    stack = lambda j: jnp.stack([small_grads[l][j] for l in range(nl)])
    dmod = jnp.stack(dmods)
    rep_names = ["b_ada", "norm_g", "a_log", "dt_bias", "dn_norm_g", "sg_ln_g", "sg_ln_b", "sg_w", "sg_b", "cv_b", "cv_ln_g",
                 "cv_ln_b", "final_g"]
    rep_grads = [dmod] + [stack(j) for j in range(11)] + [dfinal_g[0]]
    full_grads = [stack(11), stack(12)]
    grad_shapes = [g.shape for g in rep_grads + full_grads]
    gpack = _pack(rep_grads + full_grads, 1024)
    win_all = jnp.stack(win_slabs, axis=1).reshape(N_DEV, nl * d, in_cols)
    wout_all = jnp.stack(wout_slabs, axis=1).reshape(N_DEV, nl * w_out.shape[1], w_out.shape[2])
    gpack_g, win_r, wout_r = _exchange([gpack, win_all, wout_all], ["gather", "scatter", "scatter"], "exchange_grads")

    gsum = _sum_slabs(gpack_g, "sum_small")
    gl = _unpack(gsum, grad_shapes)
    g_rep = dict(zip(rep_names, gl[:len(rep_names)]))
    g_conv = lax.dynamic_slice_in_dim(gl[-2], me * cq_cols, cq_cols, axis=2)
    g_cvw = lax.dynamic_slice_in_dim(gl[-1], me * cvw_cols, cvw_cols, axis=2)

    g_win, d_win, nm_win, nv_win = _adam_slabs(win_r, w_in.reshape(nl * d, in_cols), m_w_in.reshape(nl * d, in_cols),
                                               v_w_in.reshape(nl * d, in_cols), "adam_w_in", 256)
    wo_rows = nl * w_out.shape[1]
    g_wo, d_wo, nm_wo, nv_wo = _adam_slabs(wout_r, w_out.reshape(wo_rows, d), m_w_out.reshape(wo_rows, d),
                                           v_w_out.reshape(wo_rows, d), "adam_w_out", 256)
    dmod_all = _unpack(gpack_g, grad_shapes, lead=(N_DEV,))[0]
    dmod_my = jnp.moveaxis(lax.dynamic_slice_in_dim(dmod_all, me * ada_cols, ada_cols, axis=2), 0, 1)
    g_wa, d_wa, nm_wa, nv_wa = _ada_bwd_adam(c_all, dmod_my, w_ada, m_w_ada, v_w_ada)

    small_w = dict(b_ada=b_ada, norm_g=norm_g, a_log=a_log, dt_bias=dt_bias, dn_norm_g=dn_norm_g, sg_ln_g=sg_ln_g, sg_ln_b=sg_ln_b,
                   sg_w=sg_w, sg_b=sg_b, cv_b=cv_b, cv_ln_g=cv_ln_g, cv_ln_b=cv_ln_b, final_g=final_g, conv_qkv=conv_qkv, cv_w=cv_w)
    small_m = dict(b_ada=m_b_ada, norm_g=m_norm_g, a_log=m_a_log, dt_bias=m_dt_bias, dn_norm_g=m_dn_norm_g, sg_ln_g=m_sg_ln_g,
                   sg_ln_b=m_sg_ln_b, sg_w=m_sg_w, sg_b=m_sg_b, cv_b=m_cv_b, cv_ln_g=m_cv_ln_g, cv_ln_b=m_cv_ln_b,
                   final_g=m_final_g, conv_qkv=m_conv_qkv, cv_w=m_cv_w)
    small_v = dict(b_ada=v_b_ada, norm_g=v_norm_g, a_log=v_a_log, dt_bias=v_dt_bias, dn_norm_g=v_dn_norm_g, sg_ln_g=v_sg_ln_g,
                   sg_ln_b=v_sg_ln_b, sg_w=v_sg_w, sg_b=v_sg_b, cv_b=v_cv_b, cv_ln_g=v_cv_ln_g, cv_ln_b=v_cv_ln_b,
                   final_g=v_final_g, conv_qkv=v_conv_qkv, cv_w=v_cv_w)
    small_g = dict(g_rep, conv_qkv=g_conv, cv_w=g_cvw)
    names = rep_names + ["conv_qkv", "cv_w"]
    shapes = [small_w[n].shape for n in names]
    gp = _pack([small_g[n] for n in names], 1024)
    sg_, sd_, sm_, sv_ = _adam_slabs(gp.reshape((1,) + gp.shape), _pack([small_w[n] for n in names], 1024),
                                     _pack([small_m[n] for n in names], 1024), _pack([small_v[n] for n in names], 1024),
                                     "adam_small", 1024)
    sgrad = dict(zip(names, _unpack(sg_, shapes)))
    sdelta = dict(zip(names, _unpack(sd_, shapes)))
    snm = dict(zip(names, _unpack(sm_, shapes)))
    snv = dict(zip(names, _unpack(sv_, shapes)))
    for dct, big in ((sgrad, (g_wa, g_win, g_wo)), (sdelta, (d_wa, d_win, d_wo)), (snm, (nm_wa, nm_win, nm_wo)), (snv, (nv_wa, nv_win, nv_wo))):
        dct["w_ada"] = big[0]
        dct["w_in"] = big[1].reshape(w_in.shape)
        dct["w_out"] = big[2].reshape(w_out.shape)

    order = ["norm_g", "w_ada", "b_ada", "w_in", "conv_qkv", "a_log", "dt_bias", "dn_norm_g", "sg_ln_g", "sg_ln_b", "sg_w", "sg_b",
             "cv_w", "cv_b", "cv_ln_g", "cv_ln_b", "w_out", "final_g"]
    outs = [loss, dx.reshape(x.shape)]
    for dct in (sgrad, sdelta, snm, snv):
        outs += [dct[n] for n in order]
    return tuple(outs)
```

```python
import functools

import jax
import jax.numpy as jnp
from jax import lax
from jax.experimental import pallas as pl
from jax.experimental.pallas import tpu as pltpu

f32 = jnp.float32
BF = jnp.bfloat16
N_DEV = 8
LANES = 128
CHUNK = 64
HD = 128
DN_CONV = 4
SG_BLOCK = 128
SG_GROUPS = 4
CV_GROUPS = 4
CV_K = 31
CV_HALO = 32
QK_HALO = 16
EPS = 1e-6
LN_EPS = 1e-5
ADAM_LR, ADAM_B1, ADAM_B2, ADAM_EPS, ADAM_WD, ADAM_STEP = 0.001, 0.9, 0.999, 1e-08, 0.01, 10
VMEM_LIMIT = 56 * 1024 * 1024
HI = lax.Precision.HIGHEST
MESH = pl.DeviceIdType.MESH


def _cp(sem=None, vmem=VMEM_LIMIT):
    return pltpu.CompilerParams(dimension_semantics=sem, vmem_limit_bytes=vmem)


def _pc(body, **kw):
    return pl.pallas_call(body, **kw)


def _dot(a, b):
    return jnp.dot(a.astype(BF), b.astype(BF), preferred_element_type=f32)


def _dot_nt(a, b):
    return lax.dot_general(a.astype(BF), b.astype(BF), (((1,), (1,)), ((), ())), preferred_element_type=f32)


def _dot_tn(a, b):
    return lax.dot_general(a.astype(BF), b.astype(BF), (((0,), (0,)), ((), ())), preferred_element_type=f32)


def _dot_hi(a, b):
    return jnp.dot(a, b, preferred_element_type=f32, precision=HI)


def _sigmoid(x):
    return 1.0 / (1.0 + jnp.exp(-x))


def _silu(x):
    return x * _sigmoid(x)


def _dsilu(x):
    s = _sigmoid(x)
    return s * (1.0 + x * (1.0 - s))


def _softplus(x):
    return jnp.maximum(x, 0.0) + jnp.log(1.0 + jnp.exp(-jnp.abs(x)))


def _gelu(x):
    return 0.5 * x * (1.0 + lax.erf(x * 0.7071067811865476))


def _dgelu(x):
    return 0.5 * (1.0 + lax.erf(x * 0.7071067811865476)) + x * jnp.exp(-0.5 * x * x) * 0.3989422804014327


def _lanecol(x, lane_iota, j):
    return jnp.sum(jnp.where(lane_iota == j, x, 0.0), axis=1, keepdims=True)


def _split(a):
    hi = a.astype(BF)
    return hi, (a - hi.astype(f32)).astype(BF)


def _dot3(a, b):
    mm = lambda u, v: jnp.dot(u, v, preferred_element_type=f32)
    return mm(a[0], b[0]) + (mm(a[0], b[1]) + mm(a[1], b[0]))


def _tri_inv(l_mats):
    n = l_mats[0].shape[0]
    eye = (lax.broadcasted_iota(jnp.int32, (n, n), 0) == lax.broadcasted_iota(jnp.int32, (n, n), 1)).astype(f32)
    xs = [eye - l for l in l_mats]
    sp = [_split(l) for l in l_mats]
    for _ in range(5):
        ps = [_dot3(s, s) for s in sp]
        sp = [_split(p) for p in ps]
        sx = [_split(x) for x in xs]
        xs = [x + _dot3(a, b) for x, a, b in zip(xs, sx, sp)]
    return xs


def _me():
    x, y, c = lax.axis_index("x"), lax.axis_index("y"), lax.axis_index("c")
    return x, y, c, 4 * x + 2 * y + c


def _exchange(arrs, kinds, name):
    n = len(arrs)

    def body(*refs):
        copies = _comm_copies(refs[:n], refs[n:2 * n], kinds, *refs[2 * n:])
        _comm_start(copies)
        _comm_wait(copies)

    any_spec = pl.BlockSpec(memory_space=pl.ANY)
    return _pc(body, name=name, out_shape=_comm_out_shapes(arrs, kinds), in_specs=[any_spec] * n, out_specs=tuple([any_spec] * n),
               scratch_shapes=_comm_sems(n))(*arrs)


def _comm_out_shapes(arrs, kinds):
    return tuple(jax.ShapeDtypeStruct(a.shape if k == "scatter" else (N_DEV,) + a.shape, a.dtype) for a, k in zip(arrs, kinds))


def _comm_sems(n):
    return [pltpu.SemaphoreType.DMA((n, N_DEV - 1)), pltpu.SemaphoreType.DMA((n, N_DEV - 1)), pltpu.SemaphoreType.DMA((n,))]


def _comm_copies(ins, outs, kinds, send, recv, loc):
    x, y, c, me = _me()
    local, remote = [], []
    for a in range(len(ins)):
        own = ins[a].at[me] if kinds[a] == "scatter" else ins[a]
        local.append(pltpu.make_async_copy(own, outs[a].at[me], loc.at[a]))
        for k in range(1, N_DEV):
            px = jnp.bitwise_xor(x, (k >> 2) & 1)
            py = jnp.bitwise_xor(y, (k >> 1) & 1)
            pc = jnp.bitwise_xor(c, k & 1)
            src = ins[a].at[4 * px + 2 * py + pc] if kinds[a] == "scatter" else ins[a]
            remote.append(pltpu.make_async_remote_copy(src_ref=src, dst_ref=outs[a].at[me], send_sem=send.at[a, k - 1],
                                                       recv_sem=recv.at[a, k - 1], device_id=(px, py, pc), device_id_type=MESH))
    return local, remote


def _comm_start(copies):
    for cp in copies[0] + copies[1]:
        cp.start()


def _comm_wait(copies):
    for cp in copies[1] + copies[0]:
        cp.wait()


def _with_comm(compute, comm, name, steps, args, in_specs, out_shape, out_specs, scratch):
    arrs, kinds = comm if comm else ((), ())
    n, n_in, n_out, n_scr = len(arrs), len(args), len(out_shape), len(scratch)

    def body(*refs):
        ins, cin = refs[:n_in], refs[n_in:n_in + n]
        outs, cout = refs[n_in + n:n_in + n + n_out], refs[n_in + n + n_out:n_in + 2 * n + n_out]
        scr, sems = refs[n_in + 2 * n + n_out:n_in + 2 * n + n_out + n_scr], refs[n_in + 2 * n + n_out + n_scr:]
        if n:
            @pl.when(pl.program_id(0) == 0)
            def _():
                _comm_start(_comm_copies(cin, cout, kinds, *sems))
        compute(*ins, *outs, *scr)
        if n:
            @pl.when(pl.program_id(0) == steps - 1)
            def _():
                _comm_wait(_comm_copies(cin, cout, kinds, *sems))

    any_spec = pl.BlockSpec(memory_space=pl.ANY)
    return _pc(body, name=name + ("_comm" if n else ""), grid=(steps,),
               out_shape=tuple(out_shape) + (_comm_out_shapes(arrs, kinds) if n else ()),
               in_specs=list(in_specs) + [any_spec] * n, out_specs=tuple(out_specs) + (any_spec,) * n,
               scratch_shapes=list(scratch) + (_comm_sems(n) if n else []),
               compiler_params=_cp(("arbitrary",)))(*args, *arrs)


def _ada_fwd(c_all, w_ada, b_my):
    nl, d, cols = w_ada.shape

    def body(c_ref, w_ref, b_ref, o_ref):
        ca = _silu(c_ref[...])
        o_ref[0] = _dot(ca, w_ref[0]) + b_ref[0]

    return _pc(body, name="ada_fwd", grid=(nl,), out_shape=jax.ShapeDtypeStruct((nl, N_DEV, cols), f32),
               in_specs=[pl.BlockSpec((N_DEV, d), lambda l: (0, 0)), pl.BlockSpec((1, d, cols), lambda l: (l, 0, 0)),
                         pl.BlockSpec((1, 1, cols), lambda l: (l, 0, 0))],
               out_specs=pl.BlockSpec((1, N_DEV, cols), lambda l: (l, 0, 0)), compiler_params=_cp(("arbitrary",)))(c_all, w_ada, b_my)


def _adam_math(w, g, m, v):
    m = ADAM_B1 * m + (1.0 - ADAM_B1) * g
    v = ADAM_B2 * v + (1.0 - ADAM_B2) * (g * g)
    m_hat = m / (1.0 - ADAM_B1 ** ADAM_STEP)
    v_hat = v / (1.0 - ADAM_B2 ** ADAM_STEP)
    delta = -ADAM_LR * (m_hat / (jnp.sqrt(v_hat) + ADAM_EPS) + ADAM_WD * w)
    return delta, m, v


def _ada_bwd_adam(c_all, dmod_my, w, m, v):
    nl, d, cols = w.shape
    tk = 512

    def body(c_ref, dm_ref, w_ref, m_ref, v_ref, g_ref, dl_ref, nm_ref, nv_ref):
        ca = _silu(c_ref[...])
        g = _dot_tn(ca, dm_ref[0])
        dl, nm, nv = _adam_math(w_ref[0], g, m_ref[0], v_ref[0])
        g_ref[0] = g
        dl_ref[0] = dl
        nm_ref[0] = nm
        nv_ref[0] = nv

    wspec = pl.BlockSpec((1, tk, cols), lambda l, k: (l, k, 0))
    shp = jax.ShapeDtypeStruct(w.shape, f32)
    return _pc(body, name="ada_bwd_adam", grid=(nl, d // tk), out_shape=(shp, shp, shp, shp),
               in_specs=[pl.BlockSpec((N_DEV, tk), lambda l, k: (0, k)), pl.BlockSpec((1, N_DEV, cols), lambda l, k: (l, 0, 0)),
                         wspec, wspec, wspec],
               out_specs=(wspec, wspec, wspec, wspec), compiler_params=_cp(("arbitrary", "arbitrary")))(c_all, dmod_my, w, m, v)


def _adam_slabs(slabs, w, m, v, name, tr):
    n, rows, cols = slabs.shape
    tr = min(tr, rows)

    def body(s_ref, w_ref, m_ref, v_ref, g_ref, dl_ref, nm_ref, nv_ref):
        g = s_ref[0].astype(f32)
        for j in range(1, n):
            g = g + s_ref[j].astype(f32)
        dl, nm, nv = _adam_math(w_ref[...], g, m_ref[...], v_ref[...])
        g_ref[...] = g
        dl_ref[...] = dl
        nm_ref[...] = nm
        nv_ref[...] = nv

    spec = pl.BlockSpec((tr, cols), lambda i: (i, 0))
    shp = jax.ShapeDtypeStruct((rows, cols), f32)
    return _pc(body, name=name, grid=(rows // tr,), out_shape=(shp, shp, shp, shp),
               in_specs=[pl.BlockSpec((n, tr, cols), lambda i: (0, i, 0)), spec, spec, spec],
               out_specs=(spec, spec, spec, spec), compiler_params=_cp(("arbitrary",)))(slabs, w, m, v)


def _adam_layers(slabs, w, m, v, name, tr):
    nl, rows, cols = w.shape
    n = slabs[0].shape[0]
    tr = min(tr, rows)

    def body(*refs):
        s_refs, (w_ref, m_ref, v_ref, g_ref, dl_ref, nm_ref, nv_ref) = refs[:nl], refs[nl:]
        for l in range(nl):
            @pl.when(pl.program_id(0) == l)
            def _(l=l):
                g = s_refs[l][0].astype(f32)
                for j in range(1, n):
                    g = g + s_refs[l][j].astype(f32)
                dl, nm, nv = _adam_math(w_ref[0], g, m_ref[0], v_ref[0])
                g_ref[0] = g
                dl_ref[0] = dl
                nm_ref[0] = nm
                nv_ref[0] = nv

    spec = pl.BlockSpec((1, tr, cols), lambda l, i: (l, i, 0))
    s_specs = [pl.BlockSpec((n, tr, cols), lambda l, i, j=j: (0, jnp.where(l == j, i, 0), 0)) for j in range(nl)]
    shp = jax.ShapeDtypeStruct(w.shape, f32)
    return _pc(body, name=name, grid=(nl, rows // tr), out_shape=(shp, shp, shp, shp),
               in_specs=s_specs + [spec, spec, spec], out_specs=(spec, spec, spec, spec),
               compiler_params=_cp(("arbitrary", "arbitrary")))(*slabs, w, m, v)


def _sum_slabs(slabs, name):
    n, rows, cols = slabs.shape
    tr = min(rows, 1024)

    def body(s_ref, o_ref):
        g = s_ref[0]
        for j in range(1, n):
            g = g + s_ref[j]
        o_ref[...] = g

    return _pc(body, name=name, grid=(rows // tr,), out_shape=jax.ShapeDtypeStruct((rows, cols), f32),
               in_specs=[pl.BlockSpec((n, tr, cols), lambda i: (0, i, 0))], out_specs=pl.BlockSpec((tr, cols), lambda i: (i, 0)),
               compiler_params=_cp(("arbitrary",)))(slabs)


def _inproj(x, ng, sc1, sh, wmain, wsmall):
    t, d = x.shape
    n = wmain.shape[1]
    tm, tn, rb = min(t, 1024), 1024, 128

    def body(x_ref, ng_ref, sc_ref, sh_ref, wm_ref, ws_ref, p_ref, ps_ref, h_ref, hs_ref):
        @pl.when(pl.program_id(1) == 0)
        def _():
            def rows(i, carry):
                r0 = pl.multiple_of(i * rb, rb)
                xv = x_ref[pl.ds(r0, rb), :]
                r = lax.rsqrt(jnp.mean(xv * xv, axis=1, keepdims=True) + EPS)
                hb = (((xv * r) * ng_ref[...]) * sc_ref[...] + sh_ref[...]).astype(BF)
                hs_ref[pl.ds(r0, rb), :] = hb
                h_ref[pl.ds(r0, rb), :] = hb
                return carry
            lax.fori_loop(0, tm // rb, rows, 0)
            ps_ref[...] = jnp.dot(hs_ref[...], ws_ref[...], preferred_element_type=f32)
        p_ref[...] = jnp.dot(hs_ref[...], wm_ref[...], preferred_element_type=f32).astype(BF)

    vec = pl.BlockSpec((1, d), lambda i, j: (0, 0))
    return _pc(body, name="inproj", grid=(t // tm, n // tn),
               out_shape=(jax.ShapeDtypeStruct((t, n), BF), jax.ShapeDtypeStruct((t, LANES), f32), jax.ShapeDtypeStruct((t, d), BF)),
               in_specs=[pl.BlockSpec((tm, d), lambda i, j: (i, 0)), vec, vec, vec,
                         pl.BlockSpec((d, tn), lambda i, j: (0, j)), pl.BlockSpec((d, LANES), lambda i, j: (0, 0))],
               out_specs=(pl.BlockSpec((tm, tn), lambda i, j: (i, j)), pl.BlockSpec((tm, LANES), lambda i, j: (i, 0)),
                          pl.BlockSpec((tm, d), lambda i, j: (i, 0))),
               scratch_shapes=[pltpu.VMEM((tm, d), BF)], compiler_params=_cp(("arbitrary", "arbitrary")))(x, ng, sc1, sh, wmain, wsmall)


P_Z, P_SGU, P_SGV, P_SGG, P_CVA, P_CVB, P_CVG, P_Q = 0, 1024, 1536, 2048, 2560, 3072, 3584, 4096


def _dn_specs(t, tb, rev):
    nt = t // tb
    ti = (lambda i: nt - 1 - i) if rev else (lambda i: i)
    qb = P_Q // 1024
    hb = tb // QK_HALO
    specs = []
    for s in range(3):
        specs.append(pl.BlockSpec((tb, 1024), lambda i, s=s: (ti(i), qb + s)))
    for s in range(3):
        specs.append(pl.BlockSpec((QK_HALO, 1024), lambda i, s=s: (jnp.maximum(ti(i) * hb - 1, 0), qb + s)))
    specs.append(pl.BlockSpec((DN_CONV, 3072), lambda i: (0, 0)))
    specs.append(pl.BlockSpec((tb, LANES), lambda i: (ti(i), 0)))
    specs.append(pl.BlockSpec((tb // CHUNK, 16, CHUNK), lambda i: (ti(i), 0, 0)))
    specs.append(pl.BlockSpec((1, LANES), lambda i: (0, 0)))
    specs.append(pl.BlockSpec((1, LANES), lambda i: (0, 0)))
    specs.append(pl.BlockSpec((16, CHUNK), lambda i: (0, 0)))
    specs.append(pl.BlockSpec((16, CHUNK), lambda i: (0, 0)))
    return specs, ti


def _dn_conv(first, tiles, halos, cw_ref, xs_ref, cpre_ref, tb):
    for s in range(3):
        c0 = s * 1024
        xs_ref[0:QK_HALO, c0:c0 + 1024] = jnp.where(first, 0.0, halos[s][...].astype(f32))
        xs_ref[QK_HALO:QK_HALO + tb, c0:c0 + 1024] = tiles[s][...].astype(f32)
    for sub in range(3072 // 256):
        cs = slice(sub * 256, (sub + 1) * 256)
        acc = cw_ref[0:1, cs] * xs_ref[pl.ds(QK_HALO - 3, tb), cs]
        for j in range(1, DN_CONV):
            acc = acc + cw_ref[j:j + 1, cs] * xs_ref[pl.ds(QK_HALO - 3 + j, tb), cs]
        cpre_ref[:, cs] = acc


def _dn_head(cpre_ref, r0, h, beta_all, gc_all, gcrow_ref, lane, tri_incl, tri_strict):
    rows = pl.ds(r0, CHUNK)
    cq = cpre_ref[rows, pl.ds(HD * h, HD)]
    ck = cpre_ref[rows, pl.ds(1024 + HD * h, HD)]
    cv = cpre_ref[rows, pl.ds(2048 + HD * h, HD)]
    sq, sk, v = _silu(cq), _silu(ck), _silu(cv)
    rq = lax.rsqrt(jnp.sum(sq * sq, axis=1, keepdims=True) + EPS)
    rk = lax.rsqrt(jnp.sum(sk * sk, axis=1, keepdims=True) + EPS)
    q, k = sq * rq, sk * rk
    beta = _lanecol(beta_all, lane, h)
    gcc = _lanecol(gc_all, lane, 8 + h)
    gcr = gcrow_ref[pl.ds(8 + h, 1), :]
    diff = gcc - gcr
    dmat = jnp.where(tri_incl, jnp.exp(jnp.where(tri_incl, diff, 0.0)), 0.0)
    gam = jnp.exp(gcc)
    glast = gcr[:, CHUNK - 1:CHUNK]
    gam_last = jnp.exp(glast)
    kscale = jnp.exp(glast - gcc)
    qs = q * (HD ** -0.5)
    kb = k * beta
    vb = v * beta
    kbg = kb * gam
    kk = _dot_nt(kb, k)
    lmat = jnp.where(tri_strict, kk * dmat, 0.0)
    pmat = _dot_nt(qs, k) * dmat
    return dict(cq=cq, ck=ck, cv=cv, sq=sq, sk=sk, v=v, rq=rq, rk=rk, q=q, k=k, beta=beta, gcc=gcc, dmat=dmat, gam=gam,
                gam_last=gam_last, kscale=kscale, qs=qs, kb=kb, vb=vb, kbg=kbg, lmat=lmat, pmat=pmat)


def _dn_gates(ps_ref, arow_ref, dtb_ref, pst_ref, acol_ref, dtc_ref):
    ps = ps_ref[...]
    beta_all = _sigmoid(ps)
    g_all = arow_ref[...] * _softplus(ps + dtb_ref[...])
    return ps, beta_all, g_all


def _dn_fwd(p, ps, pst, convw, arow, dtb, acol, dtc, comm=()):
    t = p.shape[0]
    nh = 1024 // HD
    tb = min(t, 256)
    nc = tb // CHUNK
    specs, _ = _dn_specs(t, tb, False)

    def compute(q_ref, k_ref, v_ref, hq_ref, hk_ref, hv_ref, cw_ref, ps_ref, pst_ref, arow_ref, dtb_ref, acol_ref, dtc_ref,
             o_ref, ss_ref, ti_ref, xs_ref, cpre_ref, st_ref, gcrow_ref, ball_ref, gall_ref):
        i = pl.program_id(0)

        @pl.when(i == 0)
        def _():
            st_ref[...] = jnp.zeros_like(st_ref)

        _dn_conv(i == 0, (q_ref, k_ref, v_ref), (hq_ref, hk_ref, hv_ref), cw_ref, xs_ref, cpre_ref, tb)
        _, beta_all, g_all = _dn_gates(ps_ref, arow_ref, dtb_ref, pst_ref, acol_ref, dtc_ref)
        ball_ref[...] = beta_all
        gall_ref[...] = g_all
        ri = lax.broadcasted_iota(jnp.int32, (CHUNK, CHUNK), 0)
        ci = lax.broadcasted_iota(jnp.int32, (CHUNK, CHUNK), 1)
        tri_incl, tri_strict = ri >= ci, ri > ci
        lower = tri_incl.astype(f32)
        upper = (ri <= ci).astype(f32)
        lane = lax.broadcasted_iota(jnp.int32, (CHUNK, LANES), 1)

        def chunk(c, carry):
            r0 = pl.multiple_of(c * CHUNK, CHUNK)
            gc_all = _dot_hi(lower, gall_ref[pl.ds(r0, CHUNK), :])
            g_rows = acol_ref[...] * _softplus(pst_ref[c] + dtc_ref[...])
            gcrow_ref[...] = _dot_hi(g_rows, upper)
            beta_c = ball_ref[pl.ds(r0, CHUNK), :]
            hs = range(nh)
            mm = lambda a, b: jnp.dot(a, b, preferred_element_type=f32)
            hd = [_dn_head(cpre_ref, r0, h, beta_c, gc_all, gcrow_ref, lane, tri_incl, tri_strict) for h in hs]
            tinv = _tri_inv([d["lmat"] for d in hd])
            tib = [x.astype(BF) for x in tinv]
            u = [mm(tib[h], hd[h]["vb"].astype(BF)) for h in hs]
            w = [mm(tib[h], hd[h]["kbg"].astype(BF)) for h in hs]
            s = [st_ref[h] for h in hs]
            sb = [x.astype(BF) for x in s]
            vn = [u[h] - mm(w[h].astype(BF), sb[h]) for h in hs]
            vnb = [x.astype(BF) for x in vn]
            o = [mm((hd[h]["qs"] * hd[h]["gam"]).astype(BF), sb[h]) + mm(hd[h]["pmat"].astype(BF), vnb[h]) for h in hs]
            snew = [s[h] * hd[h]["gam_last"] + _dot_tn(hd[h]["k"] * hd[h]["kscale"], vnb[h]) for h in hs]
            for h in hs:
                o_ref[pl.ds(r0, CHUNK), pl.ds(HD * h, HD)] = o[h]
                ss_ref[c, h] = sb[h]
                ti_ref[c, h] = tinv[h]
                st_ref[h] = snew[h]
            return carry

        lax.fori_loop(0, nc, chunk, 0)

    return _with_comm(
        compute, comm, "dn_fwd", t // tb, (p, p, p, p, p, p, convw, ps, pst, arow, dtb, acol, dtc), specs,
        (jax.ShapeDtypeStruct((t, 1024), f32), jax.ShapeDtypeStruct((t // CHUNK, nh, HD, HD), BF),
         jax.ShapeDtypeStruct((t // CHUNK, nh, CHUNK, CHUNK), f32)),
        (pl.BlockSpec((tb, 1024), lambda i: (i, 0)), pl.BlockSpec((nc, nh, HD, HD), lambda i: (i, 0, 0, 0)),
         pl.BlockSpec((nc, nh, CHUNK, CHUNK), lambda i: (i, 0, 0, 0))),
        [pltpu.VMEM((tb + QK_HALO, 3072), f32), pltpu.VMEM((tb, 3072), f32), pltpu.VMEM((nh, HD, HD), f32),
         pltpu.VMEM((16, CHUNK), f32), pltpu.VMEM((tb, LANES), f32), pltpu.VMEM((tb, LANES), f32)])


def _dn_bwd(p, ps, pst, convw, arow, dtb, acol, dtc, do, ss, tinv_all, comm=()):
    t = p.shape[0]
    nh = 1024 // HD
    tb = min(t, 256)
    nc = tb // CHUNK
    nt = t // tb
    specs, ti = _dn_specs(t, tb, True)
    specs = specs + [pl.BlockSpec((tb, 1024), lambda i: (ti(i), 0)), pl.BlockSpec((nc, nh, HD, HD), lambda i: (ti(i), 0, 0, 0)),
                     pl.BlockSpec((nc, nh, CHUNK, CHUNK), lambda i: (ti(i), 0, 0, 0))]

    def compute(q_ref, k_ref, v_ref, hq_ref, hk_ref, hv_ref, cw_ref, ps_ref, pst_ref, arow_ref, dtb_ref, acol_ref, dtc_ref,
             do_ref, ss_ref, ti_ref,
             dp_ref, dps_ref, dcw_ref, dal_ref, ddt_ref,
             xs_ref, cpre_ref, dst_ref, gcrow_ref, ball_ref, gall_ref, dcs_ref, dball_ref, dgcall_ref):
        i = pl.program_id(0)

        @pl.when(i == 0)
        def _():
            dst_ref[...] = jnp.zeros_like(dst_ref)
            dcs_ref[...] = jnp.zeros_like(dcs_ref)
            dcw_ref[...] = jnp.zeros_like(dcw_ref)
            dal_ref[...] = jnp.zeros_like(dal_ref)
            ddt_ref[...] = jnp.zeros_like(ddt_ref)

        _dn_conv(i == nt - 1, (q_ref, k_ref, v_ref), (hq_ref, hk_ref, hv_ref), cw_ref, xs_ref, cpre_ref, tb)
        ps, beta_all, g_all = _dn_gates(ps_ref, arow_ref, dtb_ref, pst_ref, acol_ref, dtc_ref)
        ball_ref[...] = beta_all
        gall_ref[...] = g_all
        ri = lax.broadcasted_iota(jnp.int32, (CHUNK, CHUNK), 0)
        ci = lax.broadcasted_iota(jnp.int32, (CHUNK, CHUNK), 1)
        tri_incl, tri_strict = ri >= ci, ri > ci
        lower = tri_incl.astype(f32)
        upper = (ri <= ci).astype(f32)
        lane = lax.broadcasted_iota(jnp.int32, (CHUNK, LANES), 1)
        row_last = lax.broadcasted_iota(jnp.int32, (CHUNK, 1), 0) == CHUNK - 1

        def chunk(cc, carry):
            c = nc - 1 - cc
            r0 = pl.multiple_of(c * CHUNK, CHUNK)
            rows = pl.ds(r0, CHUNK)
            gc_all = _dot_hi(lower, gall_ref[rows, :])
            g_rows = acol_ref[...] * _softplus(pst_ref[c] + dtc_ref[...])
            gcrow_ref[...] = _dot_hi(g_rows, upper)
            beta_c = ball_ref[rows, :]
            hs = range(nh)
            bf = lambda a: a.astype(BF)
            mm = lambda a, b: jnp.dot(a, b, preferred_element_type=f32)
            mnt = lambda a, b: lax.dot_general(a, b, (((1,), (1,)), ((), ())), preferred_element_type=f32)
            mtn = lambda a, b: lax.dot_general(a, b, (((0,), (0,)), ((), ())), preferred_element_type=f32)
            rsum = lambda a: jnp.sum(a, axis=1, keepdims=True)
            hd = [_dn_head(cpre_ref, r0, h, beta_c, gc_all, gcrow_ref, lane, tri_incl, tri_strict) for h in hs]
            tib = [bf(ti_ref[c, h]) for h in hs]
            sb = [ss_ref[c, h] for h in hs]
            vbb = [bf(d["vb"]) for d in hd]
            kgb = [bf(d["kbg"]) for d in hd]
            u = [mm(tib[h], vbb[h]) for h in hs]
            w = [mm(tib[h], kgb[h]) for h in hs]
            wb = [bf(x) for x in w]
            vn = [u[h] - mm(wb[h], sb[h]) for h in hs]
            vnb = [bf(x) for x in vn]
            qgb = [bf(d["qs"] * d["gam"]) for d in hd]
            kt = [d["k"] * d["kscale"] for d in hd]
            ktb = [bf(x) for x in kt]
            pmb = [bf(d["pmat"]) for d in hd]
            dob = [bf(do_ref[rows, pl.ds(HD * h, HD)]) for h in hs]
            dsn = [dst_ref[h] for h in hs]
            dsnb = [bf(x) for x in dsn]
            dvn = [mtn(pmb[h], dob[h]) + mm(ktb[h], dsnb[h]) for h in hs]
            dvnb = [bf(x) for x in dvn]
            dpm = [jnp.where(tri_incl, mnt(dob[h], vnb[h]), 0.0) for h in hs]
            dqg = [mnt(dob[h], sb[h]) for h in hs]
            dkt = [mnt(vnb[h], dsnb[h]) for h in hs]
            dw = [-mnt(dvnb[h], sb[h]) for h in hs]
            dwb = [bf(x) for x in dw]
            dgl = [jnp.sum(rsum(sb[h].astype(f32) * dsn[h]), axis=0, keepdims=True) for h in hs]
            dsnew = [mtn(qgb[h], dob[h]) + hd[h]["gam_last"] * dsn[h] - mtn(wb[h], dvnb[h]) for h in hs]
            for h in hs:
                dst_ref[h] = dsnew[h]
            dtm = [mnt(dvnb[h], vbb[h]) + mnt(dwb[h], kgb[h]) for h in hs]
            dvb = [mtn(tib[h], dvnb[h]) for h in hs]
            dkbg = [mtn(tib[h], dwb[h]) for h in hs]
            x1 = [bf(mtn(tib[h], bf(dtm[h]))) for h in hs]
            dl = [jnp.where(tri_strict, -mnt(x1[h], tib[h]), 0.0) for h in hs]
            mmat = [dl[h] * hd[h]["lmat"] + dpm[h] * hd[h]["pmat"] for h in hs]
            dkkb = [bf(dl[h] * hd[h]["dmat"]) for h in hs]
            dqkb = [bf(dpm[h] * hd[h]["dmat"]) for h in hs]
            kb16 = [bf(d["k"]) for d in hd]
            dkb = [mm(dkkb[h], kb16[h]) + dkbg[h] * hd[h]["gam"] for h in hs]
            dk = [mtn(dkkb[h], bf(hd[h]["kb"])) + mtn(dqkb[h], bf(hd[h]["qs"])) + dkt[h] * hd[h]["kscale"] + dkb[h] * hd[h]["beta"]
                  for h in hs]
            dq = [(mm(dqkb[h], kb16[h]) + dqg[h] * hd[h]["gam"]) * (HD ** -0.5) for h in hs]
            dbeta = [rsum(dkb[h] * hd[h]["k"] + dvb[h] * hd[h]["v"]) for h in hs]
            dgam = [rsum(dkbg[h] * hd[h]["kb"] + dqg[h] * hd[h]["qs"]) for h in hs]
            ktdk = [rsum(dkt[h] * kt[h]) for h in hs]
            csum = [rsum(jnp.where(ri == ci, jnp.sum(mmat[h], axis=0, keepdims=True), 0.0)) for h in hs]
            dbeta_all = jnp.zeros((CHUNK, LANES), f32)
            dgc_all = jnp.zeros((CHUNK, LANES), f32)
            for h in hs:
                d = hd[h]
                extra = jnp.sum(ktdk[h], axis=0, keepdims=True) + dgl[h] * d["gam_last"]
                dgc = rsum(mmat[h]) - csum[h] + dgam[h] * d["gam"] - ktdk[h] + jnp.where(row_last, extra, 0.0)
                dbeta_all = jnp.where(lane == h, dbeta[h], dbeta_all)
                dgc_all = jnp.where(lane == 8 + h, dgc, dgc_all)
                dsq = d["rq"] * (dq[h] - d["q"] * rsum(dq[h] * d["q"]))
                dsk = d["rk"] * (dk[h] - d["k"] * rsum(dk[h] * d["k"]))
                dcs_ref[rows, pl.ds(HD * h, HD)] = dsq * _dsilu(d["cq"])
                dcs_ref[rows, pl.ds(1024 + HD * h, HD)] = dsk * _dsilu(d["ck"])
                dcs_ref[rows, pl.ds(2048 + HD * h, HD)] = (dvb[h] * d["beta"]) * _dsilu(d["cv"])
            dball_ref[rows, :] = dbeta_all
            dgcall_ref[rows, :] = _dot_hi(upper, dgc_all)
            return carry

        lax.fori_loop(0, nc, chunk, 0)

        dbeta_t = dball_ref[...]
        dg_t = dgcall_ref[...]
        sg = _sigmoid(ps + dtb_ref[...])
        da = dg_t * arow_ref[...] * sg
        dps_ref[...] = dbeta_t * beta_all * (1.0 - beta_all) + da
        ddt_ref[...] += jnp.sum(da, axis=0, keepdims=True)
        dal_ref[...] += jnp.sum(dg_t * g_all, axis=0, keepdims=True)

        for sub in range(3072 // 256):
            cs = slice(sub * 256, (sub + 1) * 256)
            dc = dcs_ref[0:tb, cs]
            acc = cw_ref[DN_CONV - 1:DN_CONV, cs] * dc
            for j in range(DN_CONV - 1):
                acc = acc + cw_ref[j:j + 1, cs] * dcs_ref[pl.ds(DN_CONV - 1 - j, tb), cs]
            dp_ref[:, cs] = acc.astype(BF)
            for j in range(DN_CONV):
                dcw_ref[j:j + 1, cs] += jnp.sum(dc * xs_ref[pl.ds(QK_HALO - 3 + j, tb), cs], axis=0, keepdims=True)
        dcs_ref[tb:tb + QK_HALO, :] = dcs_ref[0:QK_HALO, :]

    vecl = pl.BlockSpec((1, LANES), lambda i: (0, 0))
    return _with_comm(
        compute, comm, "dn_bwd", nt, (p, p, p, p, p, p, convw, ps, pst, arow, dtb, acol, dtc, do, ss, tinv_all), specs,
        (jax.ShapeDtypeStruct((t, 3072), BF), jax.ShapeDtypeStruct((t, LANES), f32), jax.ShapeDtypeStruct((DN_CONV, 3072), f32),
         jax.ShapeDtypeStruct((1, LANES), f32), jax.ShapeDtypeStruct((1, LANES), f32)),
        (pl.BlockSpec((tb, 3072), lambda i: (ti(i), 0)), pl.BlockSpec((tb, LANES), lambda i: (ti(i), 0)),
         pl.BlockSpec((DN_CONV, 3072), lambda i: (0, 0)), vecl, vecl),
        [pltpu.VMEM((tb + QK_HALO, 3072), f32), pltpu.VMEM((tb, 3072), f32), pltpu.VMEM((nh, HD, HD), f32),
         pltpu.VMEM((16, CHUNK), f32), pltpu.VMEM((tb, LANES), f32), pltpu.VMEM((tb, LANES), f32),
         pltpu.VMEM((tb + QK_HALO, 3072), f32), pltpu.VMEM((tb, LANES), f32), pltpu.VMEM((tb, LANES), f32)])


def _ln_fwd(x, g, b):
    mu = jnp.mean(x, axis=1, keepdims=True)
    xc = x - mu
    rstd = lax.rsqrt(jnp.mean(xc * xc, axis=1, keepdims=True) + LN_EPS)
    xhat = xc * rstd
    return xhat * g + b, xhat, rstd


def _ln_bwd(dy, xhat, rstd, g):
    dxh = dy * g
    return rstd * (dxh - jnp.mean(dxh, axis=1, keepdims=True) - xhat * jnp.mean(dxh * xhat, axis=1, keepdims=True))


def _br_specs(t, tb, rev):
    nt = t // tb
    ti = (lambda i: nt - 1 - i) if rev else (lambda i: i)
    hb = tb // CV_HALO
    col = lambda off: off // 512
    specs = [pl.BlockSpec((tb, 1024), lambda i: (ti(i), 0)),
             pl.BlockSpec((tb, 1024), lambda i: (ti(i), 0))]
    for off in (P_SGU, P_SGV, P_SGG, P_CVA, P_CVB, P_CVG):
        specs.append(pl.BlockSpec((tb, 512), lambda i, off=off: (ti(i), col(off))))
    for off in (P_CVA, P_CVB):
        specs.append(pl.BlockSpec((CV_HALO, 512), lambda i, off=off: (jnp.maximum(ti(i) * hb - 1, 0), col(off))))
    v512 = pl.BlockSpec((1, 512), lambda i: (0, 0))
    specs += [pl.BlockSpec((1, HD), lambda i: (0, 0)), v512, v512,
              pl.BlockSpec((SG_GROUPS, SG_BLOCK, SG_BLOCK), lambda i: (0, 0, 0)),
              pl.BlockSpec((SG_BLOCK, 512), lambda i: (0, 0)),
              pl.BlockSpec((CV_HALO, 512), lambda i: (0, 0)),
              v512, v512, v512]
    return specs, ti


def _sg_mask():
    r = lax.broadcasted_iota(jnp.int32, (SG_BLOCK, SG_BLOCK), 0) // CHUNK
    c = lax.broadcasted_iota(jnp.int32, (SG_BLOCK, SG_BLOCK), 1) // CHUNK
    return r >= c


def _cv_glu(first, a_ref, b_ref, ha_ref, hb_ref, gs_ref, tb):
    ha = jnp.where(first, 0.0, ha_ref[...].astype(f32))
    gs_ref[0:CV_HALO, :] = ha * _sigmoid(hb_ref[...].astype(f32))
    gs_ref[CV_HALO:CV_HALO + tb, :] = a_ref[...].astype(f32) * _sigmoid(b_ref[...].astype(f32))


def _cv_conv(gs_ref, cw_ref, cb_ref, tb, cs):
    acc = cb_ref[:, cs] + cw_ref[0:1, cs] * gs_ref[pl.ds(CV_HALO - (CV_K - 1), tb), cs]
    for j in range(1, CV_K):
        acc = acc + cw_ref[j:j + 1, cs] * gs_ref[pl.ds(CV_HALO - (CV_K - 1) + j, tb), cs]
    return acc


def _branch_fwd(o, p, dng, slg, slb, sgw, sgbias, cvw, cvb, clg, clb):
    t = o.shape[0]
    tb = min(t, 256)
    specs, _ = _br_specs(t, tb, False)

    def body(o_ref, z_ref, su_ref, sv_ref, sgt_ref, ca_ref, cb_ref, cg_ref, ha_ref, hb_ref,
             dng_ref, slg_ref, slb_ref, sgw_ref, sgb_ref, cvw_ref, cvb_ref, clg_ref, clb_ref, y_ref, gs_ref, vln_ref):
        i = pl.program_id(0)
        for h in range(1024 // HD):
            cs = slice(h * HD, (h + 1) * HD)
            oh = o_ref[:, cs]
            r = lax.rsqrt(jnp.mean(oh * oh, axis=1, keepdims=True) + EPS)
            y_ref[:, cs] = ((oh * r) * dng_ref[...] * _silu(z_ref[:, cs].astype(f32))).astype(BF)
        mask = _sg_mask()
        for g in range(SG_GROUPS):
            cs = slice(g * HD, (g + 1) * HD)
            vg = _gelu(sv_ref[:, cs].astype(f32))
            vln, _, _ = _ln_fwd(vg, slg_ref[:, cs], slb_ref[:, cs])
            vln_ref[...] = vln
            ws = jnp.where(mask, sgw_ref[g], 0.0)
            for nb in range(tb // SG_BLOCK):
                rs = slice(nb * SG_BLOCK, (nb + 1) * SG_BLOCK)
                mixed = _dot(ws, vln_ref[rs, :]) + sgb_ref[:, cs]
                u = _gelu(su_ref[rs, cs].astype(f32))
                y_ref[rs, 1024 + g * HD:1024 + (g + 1) * HD] = (u * mixed * _silu(sgt_ref[rs, cs].astype(f32))).astype(BF)
        _cv_glu(i == 0, ca_ref, cb_ref, ha_ref, hb_ref, gs_ref, tb)
        for g in range(CV_GROUPS):
            cs = slice(g * HD, (g + 1) * HD)
            dw = _cv_conv(gs_ref, cvw_ref, cvb_ref, tb, cs)
            ln, _, _ = _ln_fwd(dw, clg_ref[:, cs], clb_ref[:, cs])
            y_ref[:, 1536 + g * HD:1536 + (g + 1) * HD] = (_silu(ln) * _silu(cg_ref[:, cs].astype(f32))).astype(BF)

    return _pc(body, name="branch_fwd", grid=(t // tb,), out_shape=jax.ShapeDtypeStruct((t, 2048), BF),
               in_specs=specs, out_specs=pl.BlockSpec((tb, 2048), lambda i: (i, 0)),
               scratch_shapes=[pltpu.VMEM((tb + CV_HALO, 512), f32), pltpu.VMEM((tb, HD), f32)],
               compiler_params=_cp(("arbitrary",)))(o, p, p, p, p, p, p, p, p, p, dng, slg, slb, sgw, sgbias, cvw, cvb, clg, clb)


def _branch_bwd(o, p, dng, slg, slb, sgw, sgbias, cvw, cvb, clg, clb, dy):
    t = o.shape[0]
    tb = min(t, 256)
    nt = t // tb
    specs, ti = _br_specs(t, tb, True)
    specs = specs + [pl.BlockSpec((tb, 2048), lambda i: (ti(i), 0))]

    def body(o_ref, z_ref, su_ref, sv_ref, sgt_ref, ca_ref, cb_ref, cg_ref, ha_ref, hb_ref,
             dng_ref, slg_ref, slb_ref, sgw_ref, sgb_ref, cvw_ref, cvb_ref, clg_ref, clb_ref, dy_ref,
             dp_ref, do_ref, ddng_ref, dslg_ref, dslb_ref, dsgw_ref, dsgb_ref, dcvw_ref, dcvb_ref, dclg_ref, dclb_ref,
             gs_ref, vln_ref, xh_ref, dvl_ref, ddw_ref, dbias_ref):
        i = pl.program_id(0)

        @pl.when(i == 0)
        def _():
            for r in (ddng_ref, dslg_ref, dslb_ref, dsgw_ref, dsgb_ref, dcvw_ref, dcvb_ref, dclg_ref, dclb_ref, ddw_ref, dbias_ref):
                r[...] = jnp.zeros_like(r)

        for h in range(1024 // HD):
            cs = slice(h * HD, (h + 1) * HD)
            oh = o_ref[:, cs]
            zz = z_ref[:, cs].astype(f32)
            dyh = dy_ref[:, cs].astype(f32)
            r = lax.rsqrt(jnp.mean(oh * oh, axis=1, keepdims=True) + EPS)
            nrm = oh * r
            sz = _silu(zz)
            dn = dyh * dng_ref[...] * sz
            ddng_ref[...] += jnp.sum(dyh * nrm * sz, axis=0, keepdims=True)
            dp_ref[:, cs] = (dyh * nrm * dng_ref[...] * _dsilu(zz)).astype(BF)
            do_ref[:, cs] = r * (dn - nrm * jnp.mean(dn * nrm, axis=1, keepdims=True))
        mask = _sg_mask()
        for g in range(SG_GROUPS):
            cs = slice(g * HD, (g + 1) * HD)
            sv = sv_ref[:, cs].astype(f32)
            vln, xhat, rstd = _ln_fwd(_gelu(sv), slg_ref[:, cs], slb_ref[:, cs])
            vln_ref[...] = vln
            xh_ref[...] = xhat
            ws = jnp.where(mask, sgw_ref[g], 0.0)
            dws = jnp.zeros((SG_BLOCK, SG_BLOCK), f32)
            for nb in range(tb // SG_BLOCK):
                rs = slice(nb * SG_BLOCK, (nb + 1) * SG_BLOCK)
                dyb = dy_ref[rs, 1024 + g * HD:1024 + (g + 1) * HD].astype(f32)
                su = su_ref[rs, cs].astype(f32)
                gt = sgt_ref[rs, cs].astype(f32)
                mixed = _dot(ws, vln_ref[rs, :]) + sgb_ref[:, cs]
                u = _gelu(su)
                sgt = _silu(gt)
                dmixed = dyb * u * sgt
                dp_ref[rs, P_SGU + g * HD:P_SGU + (g + 1) * HD] = (dyb * mixed * sgt * _dgelu(su)).astype(BF)
                dp_ref[rs, P_SGG + g * HD:P_SGG + (g + 1) * HD] = (dyb * u * mixed * _dsilu(gt)).astype(BF)
                dvl_ref[rs, :] = _dot_tn(ws, dmixed)
                dws = dws + _dot_nt(dmixed, vln_ref[rs, :])
                dbias_ref[:, cs] += dmixed
            dsgw_ref[g] += jnp.where(mask, dws, 0.0)
            dvl = dvl_ref[...]
            xhat = xh_ref[...]
            dslg_ref[:, cs] += jnp.sum(dvl * xhat, axis=0, keepdims=True)
            dslb_ref[:, cs] += jnp.sum(dvl, axis=0, keepdims=True)
            dvg = _ln_bwd(dvl, xhat, rstd, slg_ref[:, cs])
            dp_ref[:, P_SGV + g * HD:P_SGV + (g + 1) * HD] = (dvg * _dgelu(sv)).astype(BF)
        _cv_glu(i == nt - 1, ca_ref, cb_ref, ha_ref, hb_ref, gs_ref, tb)
        for g in range(CV_GROUPS):
            cs = slice(g * HD, (g + 1) * HD)
            dyc = dy_ref[:, 1536 + g * HD:1536 + (g + 1) * HD].astype(f32)
            cg = cg_ref[:, cs].astype(f32)
            dw = _cv_conv(gs_ref, cvw_ref, cvb_ref, tb, cs)
            ln, xhat, rstd = _ln_fwd(dw, clg_ref[:, cs], clb_ref[:, cs])
            dln = dyc * _silu(cg) * _dsilu(ln)
            dp_ref[:, P_CVG + g * HD:P_CVG + (g + 1) * HD] = (dyc * _silu(ln) * _dsilu(cg)).astype(BF)
            dclg_ref[:, cs] += jnp.sum(dln * xhat, axis=0, keepdims=True)
            dclb_ref[:, cs] += jnp.sum(dln, axis=0, keepdims=True)
            ddw = _ln_bwd(dln, xhat, rstd, clg_ref[:, cs])
            dcvb_ref[:, cs] += jnp.sum(ddw, axis=0, keepdims=True)
            ddw_ref[0:tb, cs] = ddw
            dglu = cvw_ref[CV_K - 1:CV_K, cs] * ddw
            for j in range(CV_K - 1):
                dglu = dglu + cvw_ref[j:j + 1, cs] * ddw_ref[pl.ds(CV_K - 1 - j, tb), cs]
            for j in range(CV_K):
                dcvw_ref[j:j + 1, cs] += jnp.sum(ddw * gs_ref[pl.ds(CV_HALO - (CV_K - 1) + j, tb), cs], axis=0, keepdims=True)
            a = ca_ref[:, cs].astype(f32)
            sb = _sigmoid(cb_ref[:, cs].astype(f32))
            dp_ref[:, P_CVA + g * HD:P_CVA + (g + 1) * HD] = (dglu * sb).astype(BF)
            dp_ref[:, P_CVB + g * HD:P_CVB + (g + 1) * HD] = (dglu * a * sb * (1.0 - sb)).astype(BF)
        ddw_ref[tb:tb + CV_HALO, :] = ddw_ref[0:CV_HALO, :]

        @pl.when(i == nt - 1)
        def _():
            lane = lax.broadcasted_iota(jnp.int32, (SG_BLOCK, LANES), 1)
            acc = jnp.zeros((SG_BLOCK, LANES), f32)
            for g in range(SG_GROUPS):
                acc = jnp.where(lane == g, jnp.sum(dbias_ref[:, g * HD:(g + 1) * HD], axis=1, keepdims=True), acc)
            dsgb_ref[...] = acc

    v512 = pl.BlockSpec((1, 512), lambda i: (0, 0))
    return _pc(body, name="branch_bwd", grid=(nt,),
               out_shape=(jax.ShapeDtypeStruct((t, 4096), BF), jax.ShapeDtypeStruct((t, 1024), f32),
                          jax.ShapeDtypeStruct((1, HD), f32), jax.ShapeDtypeStruct((1, 512), f32), jax.ShapeDtypeStruct((1, 512), f32),
                          jax.ShapeDtypeStruct((SG_GROUPS, SG_BLOCK, SG_BLOCK), f32), jax.ShapeDtypeStruct((SG_BLOCK, LANES), f32),
                          jax.ShapeDtypeStruct((CV_HALO, 512), f32), jax.ShapeDtypeStruct((1, 512), f32),
                          jax.ShapeDtypeStruct((1, 512), f32), jax.ShapeDtypeStruct((1, 512), f32)),
               in_specs=specs,
               out_specs=(pl.BlockSpec((tb, 4096), lambda i: (ti(i), 0)), pl.BlockSpec((tb, 1024), lambda i: (ti(i), 0)),
                          pl.BlockSpec((1, HD), lambda i: (0, 0)), v512, v512,
                          pl.BlockSpec((SG_GROUPS, SG_BLOCK, SG_BLOCK), lambda i: (0, 0, 0)),
                          pl.BlockSpec((SG_BLOCK, LANES), lambda i: (0, 0)), pl.BlockSpec((CV_HALO, 512), lambda i: (0, 0)),
                          v512, v512, v512),
               scratch_shapes=[pltpu.VMEM((tb + CV_HALO, 512), f32), pltpu.VMEM((tb, HD), f32), pltpu.VMEM((tb, HD), f32),
                               pltpu.VMEM((tb, HD), f32), pltpu.VMEM((tb + CV_HALO, 512), f32), pltpu.VMEM((SG_BLOCK, 512), f32)],
               compiler_params=_cp(("arbitrary",)))(o, p, p, p, p, p, p, p, p, p, dng, slg, slb, sgw, sgbias, cvw, cvb, clg, clb, dy)


def _outproj(ycat, w, x, gate):
    t, d = x.shape
    tm = min(t, 512)

    def body(y_ref, w_ref, x_ref, g_ref, o_ref):
        o_ref[...] = x_ref[...] + g_ref[...] * jnp.dot(y_ref[...], w_ref[...], preferred_element_type=f32)

    return _pc(body, name="outproj", grid=(t // tm,), out_shape=jax.ShapeDtypeStruct((t, d), f32),
               in_specs=[pl.BlockSpec((tm, ycat.shape[1]), lambda i: (i, 0)), pl.BlockSpec(w.shape, lambda i: (0, 0)),
                         pl.BlockSpec((tm, d), lambda i: (i, 0)), pl.BlockSpec((1, d), lambda i: (0, 0))],
               out_specs=pl.BlockSpec((tm, d), lambda i: (i, 0)), compiler_params=_cp(("arbitrary",)))(ycat, w, x, gate)


def _outproj_bwd(dxo, w, gate):
    t, d = dxo.shape
    tm = min(t, 512)

    def body(d_ref, w_ref, g_ref, o_ref):
        o_ref[...] = _dot_nt(d_ref[...] * g_ref[...], w_ref[...]).astype(BF)

    return _pc(body, name="outproj_bwd", grid=(t // tm,), out_shape=jax.ShapeDtypeStruct((t, w.shape[0]), BF),
               in_specs=[pl.BlockSpec((tm, d), lambda i: (i, 0)), pl.BlockSpec(w.shape, lambda i: (0, 0)),
                         pl.BlockSpec((1, d), lambda i: (0, 0))],
               out_specs=pl.BlockSpec((tm, w.shape[0]), lambda i: (i, 0)), compiler_params=_cp(("arbitrary",)))(dxo, w, gate)


def _tn_acc(a, b, name):
    kk, m = a.shape
    n = b.shape[1]
    tk, tn = min(kk, 512), min(n, 1024)

    def body(a_ref, b_ref, o_ref):
        @pl.when(pl.program_id(1) == 0)
        def _():
            o_ref[...] = jnp.zeros_like(o_ref)
        o_ref[...] += _dot_tn(a_ref[...], b_ref[...])

    return _pc(body, name=name, grid=(n // tn, kk // tk), out_shape=jax.ShapeDtypeStruct((m, n), f32),
               in_specs=[pl.BlockSpec((tk, m), lambda j, k: (k, 0)), pl.BlockSpec((tk, tn), lambda j, k: (k, j))],
               out_specs=pl.BlockSpec((m, tn), lambda j, k: (0, j)), compiler_params=_cp(("arbitrary", "arbitrary")))(a, b)


def _wout_grad(gmat, w, gate):
    m, n = gmat.shape
    tr = m // N_DEV

    def body(g_ref, w_ref, gt_ref, o_ref, dg_ref):
        @pl.when(pl.program_id(0) == 0)
        def _():
            dg_ref[...] = jnp.zeros_like(dg_ref)
        gm = g_ref[...]
        o_ref[0] = (gm * gt_ref[...]).astype(BF)
        dg_ref[...] += jnp.sum(gm * w_ref[...].astype(f32), axis=0, keepdims=True)

    return _pc(body, name="wout_grad", grid=(N_DEV,),
               out_shape=(jax.ShapeDtypeStruct((N_DEV, tr, n), BF), jax.ShapeDtypeStruct((1, n), f32)),
               in_specs=[pl.BlockSpec((tr, n), lambda i: (i, 0)), pl.BlockSpec((tr, n), lambda i: (i, 0)),
                         pl.BlockSpec((1, n), lambda i: (0, 0))],
               out_specs=(pl.BlockSpec((1, tr, n), lambda i: (i, 0, 0)), pl.BlockSpec((1, n), lambda i: (0, 0))),
               compiler_params=_cp(("arbitrary",)))(gmat, w, gate)


def _inproj_bwd(dpa, dpb, dps, wmain, wsmall, x, ng, sc1, dxo):
    t, d = x.shape
    na, nb = dpa.shape[1] // 1024, dpb.shape[1] // 1024
    nk = na + nb
    tm, rb = min(t, 512), 128

    def body(a_ref, b_ref, s_ref, wm_ref, ws_ref, x_ref, ng_ref, sc_ref, dxo_ref, dx_ref, dsh_ref, dsc_ref, dng_ref, acc_ref):
        i, k = pl.program_id(0), pl.program_id(1)

        @pl.when((i == 0) & (k == 0))
        def _():
            dsh_ref[...] = jnp.zeros_like(dsh_ref)
            dsc_ref[...] = jnp.zeros_like(dsc_ref)
            dng_ref[...] = jnp.zeros_like(dng_ref)

        @pl.when(k == 0)
        def _():
            acc_ref[...] = _dot_nt(s_ref[...], ws_ref[...])

        @pl.when(k < na)
        def _():
            acc_ref[...] += _dot_nt(a_ref[...], wm_ref[...])

        @pl.when(k >= na)
        def _():
            acc_ref[...] += _dot_nt(b_ref[...], wm_ref[...])

        @pl.when(k == nk - 1)
        def _():
            def rows(j, carry):
                r0 = pl.multiple_of(j * rb, rb)
                rr = pl.ds(r0, rb)
                xv = x_ref[rr, :]
                dh = acc_ref[rr, :]
                r = lax.rsqrt(jnp.mean(xv * xv, axis=1, keepdims=True) + EPS)
                xn = xv * r
                dsh_ref[...] += jnp.sum(dh, axis=0, keepdims=True)
                dsc_ref[...] += jnp.sum(dh * (xn * ng_ref[...]), axis=0, keepdims=True)
                dng_ref[...] += jnp.sum(dh * sc_ref[...] * xn, axis=0, keepdims=True)
                dxn = dh * (ng_ref[...] * sc_ref[...])
                dx_ref[rr, :] = r * (dxn - xn * jnp.mean(dxn * xn, axis=1, keepdims=True)) + dxo_ref[rr, :]
                return carry
            lax.fori_loop(0, tm // rb, rows, 0)

    vec = pl.BlockSpec((1, d), lambda i, k: (0, 0))
    row = pl.BlockSpec((tm, d), lambda i, k: (i, 0))
    shp = jax.ShapeDtypeStruct((1, d), f32)
    return _pc(body, name="inproj_bwd", grid=(t // tm, nk), out_shape=(jax.ShapeDtypeStruct((t, d), f32), shp, shp, shp),
               in_specs=[pl.BlockSpec((tm, 1024), lambda i, k: (i, jnp.minimum(k, na - 1))),
                         pl.BlockSpec((tm, 1024), lambda i, k: (i, jnp.clip(k - na, 0, nb - 1))),
                         pl.BlockSpec((tm, LANES), lambda i, k: (i, 0)),
                         pl.BlockSpec((d, 1024), lambda i, k: (0, k)), pl.BlockSpec((d, LANES), lambda i, k: (0, 0)),
                         row, vec, vec, row],
               out_specs=(row, vec, vec, vec), scratch_shapes=[pltpu.VMEM((tm, d), f32)],
               compiler_params=_cp(("arbitrary", "arbitrary")))(dpa, dpb, dps, wmain, wsmall, x, ng, sc1, dxo)


def _loss_head(x, fg, tgt):
    t, d = x.shape
    tm = min(t, 512)

    def body(x_ref, g_ref, t_ref, dx_ref, l_ref, dg_ref):
        @pl.when(pl.program_id(0) == 0)
        def _():
            l_ref[...] = jnp.zeros_like(l_ref)
            dg_ref[...] = jnp.zeros_like(dg_ref)
        xv = x_ref[...]
        r = lax.rsqrt(jnp.mean(xv * xv, axis=1, keepdims=True) + EPS)
        xn = xv * r
        err = xn * g_ref[...] - t_ref[...]
        l_ref[...] += 0.5 * jnp.sum(jnp.mean(err * err, axis=1, keepdims=True), axis=0, keepdims=True)
        dy = err * (1.0 / d)
        dg_ref[...] += jnp.sum(dy * xn, axis=0, keepdims=True)
        dxn = dy * g_ref[...]
        dx_ref[...] = r * (dxn - xn * jnp.mean(dxn * xn, axis=1, keepdims=True))

    row = pl.BlockSpec((tm, d), lambda i: (i, 0))
    vec = pl.BlockSpec((1, d), lambda i: (0, 0))
    return _pc(body, name="loss_head", grid=(t // tm,),
               out_shape=(jax.ShapeDtypeStruct((t, d), f32), jax.ShapeDtypeStruct((1, 1), f32), jax.ShapeDtypeStruct((1, d), f32)),
               in_specs=[row, vec, row], out_specs=(row, pl.BlockSpec((1, 1), lambda i: (0, 0)), vec),
               compiler_params=_cp(("arbitrary",)))(x, fg, tgt)


def _pack(arrs, rows_mult=8):
    flat = jnp.concatenate([a.reshape(-1).astype(f32) for a in arrs])
    n = flat.shape[0]
    per = rows_mult * LANES
    pad = (-n) % per
    if pad:
        flat = jnp.concatenate([flat, jnp.zeros((pad,), f32)])
    return flat.reshape(-1, LANES)


def _unpack(packed, shapes, lead=()):
    flat = packed.reshape(lead + (-1,))
    out, off = [], 0
    for s in shapes:
        n = 1
        for v in s:
            n *= v
        out.append(flat[..., off:off + n].reshape(lead + tuple(s)))
        off += n
    return out


def _pad_lanes(v, at):
    return jnp.pad(v.astype(f32), (at, LANES - at - v.shape[0])).reshape(1, LANES)


def kernel(x, c, norm_g, w_ada, b_ada, w_in, conv_qkv, a_log, dt_bias, dn_norm_g, sg_ln_g, sg_ln_b, sg_w, sg_b, cv_w, cv_b, cv_ln_g, cv_ln_b, w_out, final_g, loss_target, m_norm_g, m_w_ada, m_b_ada, m_w_in, m_conv_qkv, m_a_log, m_dt_bias, m_dn_norm_g, m_sg_ln_g, m_sg_ln_b, m_sg_w, m_sg_b, m_cv_w, m_cv_b, m_cv_ln_g, m_cv_ln_b, m_w_out, m_final_g, v_norm_g, v_w_ada, v_b_ada, v_w_in, v_conv_qkv, v_a_log, v_dt_bias, v_dn_norm_g, v_sg_ln_g, v_sg_ln_b, v_sg_w, v_sg_b, v_cv_w, v_cv_b, v_cv_ln_g, v_cv_ln_b, v_w_out, v_final_g):
    nl, d = norm_g.shape
    t = x.shape[1]
    nh = a_log.shape[1]
    xi, yi, ci = lax.axis_index("x"), lax.axis_index("y"), lax.axis_index("c")
    me = 4 * xi + 2 * yi + ci
    x0 = x[0]
    tgt = loss_target[0]
    ada_cols = w_ada.shape[2]
    in_cols = w_in.shape[2]
    cq_cols = conv_qkv.shape[2]
    cvw_cols = cv_w.shape[2]

    start_shapes = [c.shape, conv_qkv.shape, cv_w.shape]
    start_pack = _pack([c, conv_qkv, cv_w])
    wi_b, wo_b = w_in.astype(BF), w_out.astype(BF)
    g_small, g_win0, g_wout0 = _exchange([start_pack, wi_b[0], wo_b[0]], ["gather"] * 3, "gather_start")
    c_all, cq_g, cvw_g = _unpack(g_small, start_shapes, lead=(N_DEV,))
    c_all = c_all.reshape(N_DEV, d)
    conv_full = jnp.moveaxis(cq_g, 0, 2).reshape(nl, DN_CONV, N_DEV * cq_cols)
    cvw_full = jnp.moveaxis(cvw_g, 0, 2).reshape(nl, CV_K, N_DEV * cvw_cols)

    def repack(g_win, g_wout):
        w_nat = jnp.moveaxis(g_win, 0, 1).reshape(d, N_DEV * in_cols)
        w_main_l = jnp.concatenate([w_nat[:, 3072:4096], w_nat[:, 4112:], w_nat[:, :3072]], axis=1)
        w_small_l = jnp.pad(w_nat[:, 4096:4112], ((0, 0), (0, LANES - 16)))
        return w_main_l, w_small_l, g_wout.reshape(N_DEV * w_out.shape[1], w_out.shape[2])

    weights = [repack(g_win0, g_wout0)]

    b_my = lax.dynamic_slice_in_dim(b_ada, me * ada_cols, ada_cols, axis=1).reshape(nl, 1, ada_cols)
    mod_part = _ada_fwd(c_all, w_ada, b_my)
    (mod_g,) = _exchange([mod_part], ["gather"], "gather_mod")
    mod_mine = lax.dynamic_index_in_dim(mod_g, me, axis=2, keepdims=False)
    mod = jnp.moveaxis(mod_mine, 0, 1).reshape(nl, N_DEV * ada_cols)
    shift, scale, gate = mod[:, :d], mod[:, d:2 * d], mod[:, 2 * d:]

    arow = [_pad_lanes(-jnp.exp(a_log[l]), 8) for l in range(nl)]
    dtb = [_pad_lanes(dt_bias[l], 8) for l in range(nl)]
    saved = []
    xl = x0
    for l in range(nl):
        sc1 = (1.0 + scale[l]).reshape(1, d)
        w_main_l, w_small_l, w_out_l = weights[l]
        p, ps, hb = _inproj(xl, norm_g[l].reshape(1, d), sc1, shift[l].reshape(1, d), w_main_l, w_small_l)
        pst = ps[:, :16].reshape(t // CHUNK, CHUNK, 16).transpose(0, 2, 1)
        acol = jnp.broadcast_to(arow[l][0, :16].reshape(16, 1), (16, CHUNK))
        dtc = jnp.broadcast_to(dtb[l][0, :16].reshape(16, 1), (16, CHUNK))
        if l + 1 < nl:
            o, ss, tinv, g_win_n, g_wout_n = _dn_fwd(p, ps, pst, conv_full[l], arow[l], dtb[l], acol, dtc,
                                                     comm=((wi_b[l + 1], wo_b[l + 1]), ("gather", "gather")))
            weights.append(repack(g_win_n, g_wout_n))
        else:
            o, ss, tinv = _dn_fwd(p, ps, pst, conv_full[l], arow[l], dtb[l], acol, dtc)
        sgbias = jnp.repeat(sg_b[l].T, HD, axis=1)
        cvw_pad = jnp.pad(cvw_full[l], ((0, CV_HALO - CV_K), (0, 0)))
        br_par = (dn_norm_g[l].reshape(1, HD), sg_ln_g[l].reshape(1, -1), sg_ln_b[l].reshape(1, -1), sg_w[l], sgbias, cvw_pad,
                  cv_b[l].reshape(1, -1), cv_ln_g[l].reshape(1, -1), cv_ln_b[l].reshape(1, -1))
        ycat = _branch_fwd(o, p, *br_par)
        xn = _outproj(ycat, w_out_l, xl, gate[l].reshape(1, d))
        saved.append((xl, p, ps, pst, hb, o, ss, tinv, ycat, sc1, acol, dtc, br_par))
        xl = xn

    dx, loss_part, dfinal_g = _loss_head(xl, final_g.reshape(1, d), tgt)
    loss = lax.psum(loss_part[0, 0], ("x", "y", "c"))

    small_grads = [None] * nl
    dmods = [None] * nl
    win_r = [None] * nl
    wout_r = [None] * nl
    slabs = None
    for l in reversed(range(nl)):
        xl, p, ps, pst, hb, o, ss, tinv, ycat, sc1, acol, dtc, br_par = saved[l]
        w_main_l, w_small_l, w_out_l = weights[l]
        gate_l = gate[l].reshape(1, d)
        dycat = _outproj_bwd(dx, w_out_l, gate_l)
        gmat = _tn_acc(ycat, dx, "wout_tn")
        wout_slab, dgate = _wout_grad(gmat, w_out_l, gate_l)
        dpa, do, ddng, dslg, dslb, dsgw, dsgb, dcvw, dcvb, dclg, dclb = _branch_bwd(o, p, *br_par, dycat)
        if slabs is not None:
            dpb, dps, dconvw, dal, ddt, win_r[l + 1], wout_r[l + 1] = _dn_bwd(
                p, ps, pst, conv_full[l], arow[l], dtb[l], acol, dtc, do, ss, tinv, comm=(slabs, ("scatter", "scatter")))
        else:
            dpb, dps, dconvw, dal, ddt = _dn_bwd(p, ps, pst, conv_full[l], arow[l], dtb[l], acol, dtc, do, ss, tinv)
        dx, dshift, dscale, dng = _inproj_bwd(dpa, dpb, dps, w_main_l, w_small_l, xl, norm_g[l].reshape(1, d), sc1, dx)
        gwa = _tn_acc(hb, dpa, "win_tn_a")
        gwb = _tn_acc(hb, dpb, "win_tn_b")
        gws = _tn_acc(hb, dps, "win_tn_s")
        g_nat = jnp.concatenate([gwb, gwa[:, :1024], gws[:, :16], gwa[:, 1024:]], axis=1)
        slabs = (g_nat.astype(BF).reshape(d, N_DEV, in_cols).transpose(1, 0, 2), wout_slab)
        dmods[l] = jnp.concatenate([dshift[0], dscale[0], dgate[0]])
        small_grads[l] = (dng[0], dal[0, 8:8 + nh], ddt[0, 8:8 + nh], ddng[0], dslg[0], dslb[0], dsgw, dsgb[:, :SG_GROUPS].T,
                          dcvb[0], dclg[0], dclb[0], dconvw, dcvw[:CV_K])

    stack = lambda j: jnp.stack([small_grads[l][j] for l in range(nl)])
    dmod = jnp.stack(dmods)
    rep_names = ["b_ada", "norm_g", "a_log", "dt_bias", "dn_norm_g", "sg_ln_g", "sg_ln_b", "sg_w", "sg_b", "cv_b", "cv_ln_g",
                 "cv_ln_b", "final_g"]
    rep_grads = [dmod] + [stack(j) for j in range(11)] + [dfinal_g[0]]
    full_grads = [stack(11), stack(12)]
    grad_shapes = [g.shape for g in rep_grads + full_grads]
    gpack = _pack(rep_grads + full_grads, 1024)
    gpack_g, win_r[0], wout_r[0] = _exchange([gpack, slabs[0], slabs[1]], ["gather", "scatter", "scatter"], "exchange_grads")

    gsum = _sum_slabs(gpack_g, "sum_small")
    gl = _unpack(gsum, grad_shapes)
    g_rep = dict(zip(rep_names, gl[:len(rep_names)]))
    g_conv = lax.dynamic_slice_in_dim(gl[-2], me * cq_cols, cq_cols, axis=2)
    g_cvw = lax.dynamic_slice_in_dim(gl[-1], me * cvw_cols, cvw_cols, axis=2)

    g_win, d_win, nm_win, nv_win = _adam_layers(win_r, w_in, m_w_in, v_w_in, "adam_w_in", 128)
    g_wo, d_wo, nm_wo, nv_wo = _adam_layers(wout_r, w_out, m_w_out, v_w_out, "adam_w_out", 64)
    dmod_all = _unpack(gpack_g, grad_shapes, lead=(N_DEV,))[0]
    dmod_my = jnp.moveaxis(lax.dynamic_slice_in_dim(dmod_all, me * ada_cols, ada_cols, axis=2), 0, 1)
    g_wa, d_wa, nm_wa, nv_wa = _ada_bwd_adam(c_all, dmod_my, w_ada, m_w_ada, v_w_ada)

    small_w = dict(b_ada=b_ada, norm_g=norm_g, a_log=a_log, dt_bias=dt_bias, dn_norm_g=dn_norm_g, sg_ln_g=sg_ln_g, sg_ln_b=sg_ln_b,
                   sg_w=sg_w, sg_b=sg_b, cv_b=cv_b, cv_ln_g=cv_ln_g, cv_ln_b=cv_ln_b, final_g=final_g, conv_qkv=conv_qkv, cv_w=cv_w)
    small_m = dict(b_ada=m_b_ada, norm_g=m_norm_g, a_log=m_a_log, dt_bias=m_dt_bias, dn_norm_g=m_dn_norm_g, sg_ln_g=m_sg_ln_g,
                   sg_ln_b=m_sg_ln_b, sg_w=m_sg_w, sg_b=m_sg_b, cv_b=m_cv_b, cv_ln_g=m_cv_ln_g, cv_ln_b=m_cv_ln_b,
                   final_g=m_final_g, conv_qkv=m_conv_qkv, cv_w=m_cv_w)
    small_v = dict(b_ada=v_b_ada, norm_g=v_norm_g, a_log=v_a_log, dt_bias=v_dt_bias, dn_norm_g=v_dn_norm_g, sg_ln_g=v_sg_ln_g,
                   sg_ln_b=v_sg_ln_b, sg_w=v_sg_w, sg_b=v_sg_b, cv_b=v_cv_b, cv_ln_g=v_cv_ln_g, cv_ln_b=v_cv_ln_b,
                   final_g=v_final_g, conv_qkv=v_conv_qkv, cv_w=v_cv_w)
    small_g = dict(g_rep, conv_qkv=g_conv, cv_w=g_cvw)
    names = rep_names + ["conv_qkv", "cv_w"]
    shapes = [small_w[n].shape for n in names]
    gp = _pack([small_g[n] for n in names], 1024)
    sg_, sd_, sm_, sv_ = _adam_slabs(gp.reshape((1,) + gp.shape), _pack([small_w[n] for n in names], 1024),
                                     _pack([small_m[n] for n in names], 1024), _pack([small_v[n] for n in names], 1024),
                                     "adam_small", 1024)
    sgrad = dict(zip(names, _unpack(sg_, shapes)))
    sdelta = dict(zip(names, _unpack(sd_, shapes)))
    snm = dict(zip(names, _unpack(sm_, shapes)))
    snv = dict(zip(names, _unpack(sv_, shapes)))
    for dct, big in ((sgrad, (g_wa, g_win, g_wo)), (sdelta, (d_wa, d_win, d_wo)), (snm, (nm_wa, nm_win, nm_wo)), (snv, (nv_wa, nv_win, nv_wo))):
        dct["w_ada"] = big[0]
        dct["w_in"] = big[1].reshape(w_in.shape)
        dct["w_out"] = big[2].reshape(w_out.shape)

    order = ["norm_g", "w_ada", "b_ada", "w_in", "conv_qkv", "a_log", "dt_bias", "dn_norm_g", "sg_ln_g", "sg_ln_b", "sg_w", "sg_b",
             "cv_w", "cv_b", "cv_ln_g", "cv_ln_b", "w_out", "final_g"]
    outs = [loss, dx.reshape(x.shape)]
    for dct in (sgrad, sdelta, snm, snv):
        outs += [dct[n] for n in order]
    return tuple(outs)
```

```python
import functools

import jax
import jax.numpy as jnp
from jax import lax
from jax.experimental import pallas as pl
from jax.experimental.pallas import tpu as pltpu

f32 = jnp.float32
BF = jnp.bfloat16
N_DEV = 8
LANES = 128
CHUNK = 64
HD = 128
DN_CONV = 4
SG_BLOCK = 128
SG_GROUPS = 4
CV_GROUPS = 4
CV_K = 31
CV_HALO = 32
QK_HALO = 16
EPS = 1e-6
LN_EPS = 1e-5
ADAM_LR, ADAM_B1, ADAM_B2, ADAM_EPS, ADAM_WD, ADAM_STEP = 0.001, 0.9, 0.999, 1e-08, 0.01, 10
VMEM_LIMIT = 56 * 1024 * 1024
HI = lax.Precision.HIGHEST
MESH = pl.DeviceIdType.MESH


def _cp(sem=None, vmem=VMEM_LIMIT):
    return pltpu.CompilerParams(dimension_semantics=sem, vmem_limit_bytes=vmem)


def _pc(body, **kw):
    return pl.pallas_call(body, **kw)


def _dot(a, b):
    return jnp.dot(a.astype(BF), b.astype(BF), preferred_element_type=f32)


def _dot_nt(a, b):
    return lax.dot_general(a.astype(BF), b.astype(BF), (((1,), (1,)), ((), ())), preferred_element_type=f32)


def _dot_tn(a, b):
    return lax.dot_general(a.astype(BF), b.astype(BF), (((0,), (0,)), ((), ())), preferred_element_type=f32)


def _dot_hi(a, b):
    return jnp.dot(a, b, preferred_element_type=f32, precision=HI)


def _sigmoid(x):
    return 1.0 / (1.0 + jnp.exp(-x))


def _silu(x):
    return x * _sigmoid(x)


def _dsilu(x):
    s = _sigmoid(x)
    return s * (1.0 + x * (1.0 - s))


def _softplus(x):
    return jnp.maximum(x, 0.0) + jnp.log(1.0 + jnp.exp(-jnp.abs(x)))


def _gelu(x):
    return 0.5 * x * (1.0 + lax.erf(x * 0.7071067811865476))


def _dgelu(x):
    return 0.5 * (1.0 + lax.erf(x * 0.7071067811865476)) + x * jnp.exp(-0.5 * x * x) * 0.3989422804014327


def _lanecol(x, lane_iota, j):
    return jnp.sum(jnp.where(lane_iota == j, x, 0.0), axis=1, keepdims=True)


def _split(a):
    hi = a.astype(BF)
    return hi, (a - hi.astype(f32)).astype(BF)


def _dot3(a, b):
    mm = lambda u, v: jnp.dot(u, v, preferred_element_type=f32)
    return mm(a[0], b[0]) + (mm(a[0], b[1]) + mm(a[1], b[0]))


INV_BASE = 8


def _tri_inv(l_mats):
    n = l_mats[0].shape[0]
    r = lax.broadcasted_iota(jnp.int32, (n, n), 0)
    c = lax.broadcasted_iota(jnp.int32, (n, n), 1)
    same = lambda size: (r // size) == (c // size)
    diag = [jnp.where(same(INV_BASE), l, 0.0) for l in l_mats]
    xs = [(r == c).astype(f32) - l for l in diag]
    sp = [_split(l) for l in diag]
    size = 2
    while size < INV_BASE:
        ps = [_dot3(s, s) for s in sp]
        sp = [_split(p) for p in ps]
        sx = [_split(x) for x in xs]
        xs = [x + _dot3(a, b) for x, a, b in zip(xs, sx, sp)]
        size *= 2
    blk = INV_BASE
    while blk < n:
        join = same(2 * blk) & jnp.logical_not(same(blk))
        lo = [_split(jnp.where(join, l, 0.0)) for l in l_mats]
        sx = [_split(x) for x in xs]
        ys = [_split(_dot3(a, b)) for a, b in zip(lo, sx)]
        xs = [x - _dot3(a, b) for x, a, b in zip(xs, sx, ys)]
        blk *= 2
    return xs


def _me():
    x, y, c = lax.axis_index("x"), lax.axis_index("y"), lax.axis_index("c")
    return x, y, c, 4 * x + 2 * y + c


def _exchange(arrs, kinds, name):
    n = len(arrs)

    def body(*refs):
        copies = _comm_copies(refs[:n], refs[n:2 * n], kinds, *refs[2 * n:])
        _comm_start(copies)
        _comm_wait(copies)

    any_spec = pl.BlockSpec(memory_space=pl.ANY)
    return _pc(body, name=name, out_shape=_comm_out_shapes(arrs, kinds), in_specs=[any_spec] * n, out_specs=tuple([any_spec] * n),
               scratch_shapes=_comm_sems(n))(*arrs)


def _comm_out_shapes(arrs, kinds):
    return tuple(jax.ShapeDtypeStruct(a.shape if k == "scatter" else (N_DEV,) + a.shape, a.dtype) for a, k in zip(arrs, kinds))


def _comm_sems(n):
    return [pltpu.SemaphoreType.DMA((n, N_DEV - 1)), pltpu.SemaphoreType.DMA((n, N_DEV - 1)), pltpu.SemaphoreType.DMA((n,))]


def _comm_copies(ins, outs, kinds, send, recv, loc):
    x, y, c, me = _me()
    local, remote = [], []
    for a in range(len(ins)):
        own = ins[a].at[me] if kinds[a] == "scatter" else ins[a]
        local.append(pltpu.make_async_copy(own, outs[a].at[me], loc.at[a]))
        for k in range(1, N_DEV):
            px = jnp.bitwise_xor(x, (k >> 2) & 1)
            py = jnp.bitwise_xor(y, (k >> 1) & 1)
            pc = jnp.bitwise_xor(c, k & 1)
            src = ins[a].at[4 * px + 2 * py + pc] if kinds[a] == "scatter" else ins[a]
            remote.append(pltpu.make_async_remote_copy(src_ref=src, dst_ref=outs[a].at[me], send_sem=send.at[a, k - 1],
                                                       recv_sem=recv.at[a, k - 1], device_id=(px, py, pc), device_id_type=MESH))
    return local, remote


def _comm_start(copies):
    for cp in copies[0] + copies[1]:
        cp.start()


def _comm_wait(copies):
    for cp in copies[1] + copies[0]:
        cp.wait()


def _with_comm(compute, comm, name, steps, args, in_specs, out_shape, out_specs, scratch):
    arrs, kinds = comm if comm else ((), ())
    n, n_in, n_out, n_scr = len(arrs), len(args), len(out_shape), len(scratch)

    def body(*refs):
        ins, cin = refs[:n_in], refs[n_in:n_in + n]
        outs, cout = refs[n_in + n:n_in + n + n_out], refs[n_in + n + n_out:n_in + 2 * n + n_out]
        scr, sems = refs[n_in + 2 * n + n_out:n_in + 2 * n + n_out + n_scr], refs[n_in + 2 * n + n_out + n_scr:]
        if n:
            @pl.when(pl.program_id(0) == 0)
            def _():
                _comm_start(_comm_copies(cin, cout, kinds, *sems))
        compute(*ins, *outs, *scr)
        if n:
            @pl.when(pl.program_id(0) == steps - 1)
            def _():
                _comm_wait(_comm_copies(cin, cout, kinds, *sems))

    any_spec = pl.BlockSpec(memory_space=pl.ANY)
    return _pc(body, name=name + ("_comm" if n else ""), grid=(steps,),
               out_shape=tuple(out_shape) + (_comm_out_shapes(arrs, kinds) if n else ()),
               in_specs=list(in_specs) + [any_spec] * n, out_specs=tuple(out_specs) + (any_spec,) * n,
               scratch_shapes=list(scratch) + (_comm_sems(n) if n else []),
               compiler_params=_cp(("arbitrary",)))(*args, *arrs)


def _ada_fwd(c_all, w_ada, b_my):
    nl, d, cols = w_ada.shape

    def body(c_ref, w_ref, b_ref, o_ref):
        ca = _silu(c_ref[...])
        o_ref[0] = _dot(ca, w_ref[0]) + b_ref[0]

    return _pc(body, name="ada_fwd", grid=(nl,), out_shape=jax.ShapeDtypeStruct((nl, N_DEV, cols), f32),
               in_specs=[pl.BlockSpec((N_DEV, d), lambda l: (0, 0)), pl.BlockSpec((1, d, cols), lambda l: (l, 0, 0)),
                         pl.BlockSpec((1, 1, cols), lambda l: (l, 0, 0))],
               out_specs=pl.BlockSpec((1, N_DEV, cols), lambda l: (l, 0, 0)), compiler_params=_cp(("arbitrary",)))(c_all, w_ada, b_my)


def _adam_math(w, g, m, v):
    m = ADAM_B1 * m + (1.0 - ADAM_B1) * g
    v = ADAM_B2 * v + (1.0 - ADAM_B2) * (g * g)
    m_hat = m / (1.0 - ADAM_B1 ** ADAM_STEP)
    v_hat = v / (1.0 - ADAM_B2 ** ADAM_STEP)
    delta = -ADAM_LR * (m_hat / (jnp.sqrt(v_hat) + ADAM_EPS) + ADAM_WD * w)
    return delta, m, v


def _ada_bwd_adam(c_all, dmod_my, w, m, v):
    nl, d, cols = w.shape
    tk = 512

    def body(c_ref, dm_ref, w_ref, m_ref, v_ref, g_ref, dl_ref, nm_ref, nv_ref):
        ca = _silu(c_ref[...])
        g = _dot_tn(ca, dm_ref[0])
        dl, nm, nv = _adam_math(w_ref[0], g, m_ref[0], v_ref[0])
        g_ref[0] = g
        dl_ref[0] = dl
        nm_ref[0] = nm
        nv_ref[0] = nv

    wspec = pl.BlockSpec((1, tk, cols), lambda l, k: (l, k, 0))
    shp = jax.ShapeDtypeStruct(w.shape, f32)
    return _pc(body, name="ada_bwd_adam", grid=(nl, d // tk), out_shape=(shp, shp, shp, shp),
               in_specs=[pl.BlockSpec((N_DEV, tk), lambda l, k: (0, k)), pl.BlockSpec((1, N_DEV, cols), lambda l, k: (l, 0, 0)),
                         wspec, wspec, wspec],
               out_specs=(wspec, wspec, wspec, wspec), compiler_params=_cp(("arbitrary", "arbitrary")))(c_all, dmod_my, w, m, v)


def _adam_slabs(slabs, w, m, v, name, tr):
    n, rows, cols = slabs.shape
    tr = min(tr, rows)

    def body(s_ref, w_ref, m_ref, v_ref, g_ref, dl_ref, nm_ref, nv_ref):
        g = s_ref[0].astype(f32)
        for j in range(1, n):
            g = g + s_ref[j].astype(f32)
        dl, nm, nv = _adam_math(w_ref[...], g, m_ref[...], v_ref[...])
        g_ref[...] = g
        dl_ref[...] = dl
        nm_ref[...] = nm
        nv_ref[...] = nv

    spec = pl.BlockSpec((tr, cols), lambda i: (i, 0))
    shp = jax.ShapeDtypeStruct((rows, cols), f32)
    return _pc(body, name=name, grid=(rows // tr,), out_shape=(shp, shp, shp, shp),
               in_specs=[pl.BlockSpec((n, tr, cols), lambda i: (0, i, 0)), spec, spec, spec],
               out_specs=(spec, spec, spec, spec), compiler_params=_cp(("arbitrary",)))(slabs, w, m, v)


def _adam_layers(slabs, w, m, v, name, tr):
    nl, rows, cols = w.shape
    n = slabs[0].shape[0]
    tr = min(tr, rows)

    def body(*refs):
        s_refs, (w_ref, m_ref, v_ref, g_ref, dl_ref, nm_ref, nv_ref) = refs[:nl], refs[nl:]
        for l in range(nl):
            @pl.when(pl.program_id(0) == l)
            def _(l=l):
                g = s_refs[l][0].astype(f32)
                for j in range(1, n):
                    g = g + s_refs[l][j].astype(f32)
                dl, nm, nv = _adam_math(w_ref[0], g, m_ref[0], v_ref[0])
                g_ref[0] = g
                dl_ref[0] = dl
                nm_ref[0] = nm
                nv_ref[0] = nv

    spec = pl.BlockSpec((1, tr, cols), lambda l, i: (l, i, 0))
    s_specs = [pl.BlockSpec((n, tr, cols), lambda l, i, j=j: (0, jnp.where(l == j, i, 0), 0)) for j in range(nl)]
    shp = jax.ShapeDtypeStruct(w.shape, f32)
    return _pc(body, name=name, grid=(nl, rows // tr), out_shape=(shp, shp, shp, shp),
               in_specs=s_specs + [spec, spec, spec], out_specs=(spec, spec, spec, spec),
               compiler_params=_cp(("arbitrary", "arbitrary")))(*slabs, w, m, v)


def _sum_slabs(slabs, name):
    n, rows, cols = slabs.shape
    tr = min(rows, 1024)

    def body(s_ref, o_ref):
        g = s_ref[0]
        for j in range(1, n):
            g = g + s_ref[j]
        o_ref[...] = g

    return _pc(body, name=name, grid=(rows // tr,), out_shape=jax.ShapeDtypeStruct((rows, cols), f32),
               in_specs=[pl.BlockSpec((n, tr, cols), lambda i: (0, i, 0))], out_specs=pl.BlockSpec((tr, cols), lambda i: (i, 0)),
               compiler_params=_cp(("arbitrary",)))(slabs)


NAT_Z, NAT_SMALL, NAT_REST = 3072, 4096, 4112


def _repack_w(g_win):
    n, d, cols = g_win.shape
    tr = 256

    def body(g_ref, wm_ref, ws_ref):
        nat = jnp.concatenate([g_ref[j] for j in range(n)], axis=1)
        wm_ref[...] = jnp.concatenate([nat[:, NAT_Z:NAT_SMALL], nat[:, NAT_REST:], nat[:, :NAT_Z]], axis=1)
        ws_ref[...] = jnp.concatenate([nat[:, NAT_SMALL:NAT_REST], jnp.zeros((tr, LANES - 16), nat.dtype)], axis=1)

    return _pc(body, name="repack_w", grid=(d // tr,),
               out_shape=(jax.ShapeDtypeStruct((d, n * cols - 16), g_win.dtype), jax.ShapeDtypeStruct((d, LANES), g_win.dtype)),
               in_specs=[pl.BlockSpec((n, tr, cols), lambda i: (0, i, 0))],
               out_specs=(pl.BlockSpec((tr, n * cols - 16), lambda i: (i, 0)), pl.BlockSpec((tr, LANES), lambda i: (i, 0))),
               compiler_params=_cp(("arbitrary",)))(g_win)


def _grad_slabs(gwa, gwb, gws, cols):
    d = gwa.shape[0]
    tr = 256

    def body(a_ref, b_ref, s_ref, o_ref):
        a = a_ref[...]
        nat = jnp.concatenate([b_ref[...], a[:, :NAT_SMALL - NAT_Z], s_ref[:, 0:16], a[:, NAT_SMALL - NAT_Z:]], axis=1)
        for j in range(N_DEV):
            o_ref[j] = nat[:, j * cols:(j + 1) * cols].astype(BF)

    return _pc(body, name="grad_slabs", grid=(d // tr,), out_shape=jax.ShapeDtypeStruct((N_DEV, d, cols), BF),
               in_specs=[pl.BlockSpec((tr, gwa.shape[1]), lambda i: (i, 0)), pl.BlockSpec((tr, gwb.shape[1]), lambda i: (i, 0)),
                         pl.BlockSpec((tr, LANES), lambda i: (i, 0))],
               out_specs=pl.BlockSpec((N_DEV, tr, cols), lambda i: (0, i, 0)), compiler_params=_cp(("arbitrary",)))(gwa, gwb, gws)


def _inproj(x, ng, sc1, sh, wmain, wsmall):
    t, d = x.shape
    n = wmain.shape[1]
    tm, tn, rb = min(t, 1024), 1024, 128

    def body(x_ref, ng_ref, sc_ref, sh_ref, wm_ref, ws_ref, p_ref, ps_ref, h_ref, hs_ref):
        @pl.when(pl.program_id(1) == 0)
        def _():
            def rows(i, carry):
                r0 = pl.multiple_of(i * rb, rb)
                xv = x_ref[pl.ds(r0, rb), :]
                r = lax.rsqrt(jnp.mean(xv * xv, axis=1, keepdims=True) + EPS)
                hb = (((xv * r) * ng_ref[...]) * sc_ref[...] + sh_ref[...]).astype(BF)
                hs_ref[pl.ds(r0, rb), :] = hb
                h_ref[pl.ds(r0, rb), :] = hb
                return carry
            lax.fori_loop(0, tm // rb, rows, 0)
            ps_ref[...] = jnp.dot(hs_ref[...], ws_ref[...], preferred_element_type=f32)
        p_ref[...] = jnp.dot(hs_ref[...], wm_ref[...], preferred_element_type=f32).astype(BF)

    vec = pl.BlockSpec((1, d), lambda i, j: (0, 0))
    return _pc(body, name="inproj", grid=(t // tm, n // tn),
               out_shape=(jax.ShapeDtypeStruct((t, n), BF), jax.ShapeDtypeStruct((t, LANES), f32), jax.ShapeDtypeStruct((t, d), BF)),
               in_specs=[pl.BlockSpec((tm, d), lambda i, j: (i, 0)), vec, vec, vec,
                         pl.BlockSpec((d, tn), lambda i, j: (0, j)), pl.BlockSpec((d, LANES), lambda i, j: (0, 0))],
               out_specs=(pl.BlockSpec((tm, tn), lambda i, j: (i, j)), pl.BlockSpec((tm, LANES), lambda i, j: (i, 0)),
                          pl.BlockSpec((tm, d), lambda i, j: (i, 0))),
               scratch_shapes=[pltpu.VMEM((tm, d), BF)], compiler_params=_cp(("arbitrary", "arbitrary")))(x, ng, sc1, sh, wmain, wsmall)


P_Z, P_SGU, P_SGV, P_SGG, P_CVA, P_CVB, P_CVG, P_Q = 0, 1024, 1536, 2048, 2560, 3072, 3584, 4096


def _dn_specs(t, tb, rev):
    nt = t // tb
    ti = (lambda i: nt - 1 - i) if rev else (lambda i: i)
    qb = P_Q // 1024
    hb = tb // QK_HALO
    specs = []
    for s in range(3):
        specs.append(pl.BlockSpec((tb, 1024), lambda i, s=s: (ti(i), qb + s)))
    for s in range(3):
        specs.append(pl.BlockSpec((QK_HALO, 1024), lambda i, s=s: (jnp.maximum(ti(i) * hb - 1, 0), qb + s)))
    specs.append(pl.BlockSpec((DN_CONV, 3072), lambda i: (0, 0)))
    specs.append(pl.BlockSpec((tb, LANES), lambda i: (ti(i), 0)))
    specs.append(pl.BlockSpec((tb // CHUNK, 16, CHUNK), lambda i: (ti(i), 0, 0)))
    specs.append(pl.BlockSpec((1, LANES), lambda i: (0, 0)))
    specs.append(pl.BlockSpec((1, LANES), lambda i: (0, 0)))
    specs.append(pl.BlockSpec((16, CHUNK), lambda i: (0, 0)))
    specs.append(pl.BlockSpec((16, CHUNK), lambda i: (0, 0)))
    return specs, ti


def _dn_conv(first, tiles, halos, cw_ref, xs_ref, cpre_ref, tb):
    for s in range(3):
        c0 = s * 1024
        xs_ref[0:QK_HALO, c0:c0 + 1024] = jnp.where(first, 0.0, halos[s][...].astype(f32))
        xs_ref[QK_HALO:QK_HALO + tb, c0:c0 + 1024] = tiles[s][...].astype(f32)
    for sub in range(3072 // 256):
        cs = slice(sub * 256, (sub + 1) * 256)
        acc = cw_ref[0:1, cs] * xs_ref[pl.ds(QK_HALO - 3, tb), cs]
        for j in range(1, DN_CONV):
            acc = acc + cw_ref[j:j + 1, cs] * xs_ref[pl.ds(QK_HALO - 3 + j, tb), cs]
        cpre_ref[:, cs] = acc


def _dn_head(cpre_ref, r0, h, beta_all, gc_all, gcrow_ref, lane, tri_incl, tri_strict):
    rows = pl.ds(r0, CHUNK)
    cq = cpre_ref[rows, pl.ds(HD * h, HD)]
    ck = cpre_ref[rows, pl.ds(1024 + HD * h, HD)]
    cv = cpre_ref[rows, pl.ds(2048 + HD * h, HD)]
    sq, sk, v = _silu(cq), _silu(ck), _silu(cv)
    rq = lax.rsqrt(jnp.sum(sq * sq, axis=1, keepdims=True) + EPS)
    rk = lax.rsqrt(jnp.sum(sk * sk, axis=1, keepdims=True) + EPS)
    q, k = sq * rq, sk * rk
    beta = _lanecol(beta_all, lane, h)
    gcc = _lanecol(gc_all, lane, 8 + h)
    gcr = gcrow_ref[pl.ds(8 + h, 1), :]
    diff = gcc - gcr
    dmat = jnp.where(tri_incl, jnp.exp(jnp.where(tri_incl, diff, 0.0)), 0.0)
    gam = jnp.exp(gcc)
    glast = gcr[:, CHUNK - 1:CHUNK]
    gam_last = jnp.exp(glast)
    kscale = jnp.exp(glast - gcc)
    qs = q * (HD ** -0.5)
    kb = k * beta
    vb = v * beta
    kbg = kb * gam
    kk = _dot_nt(kb, k)
    lmat = jnp.where(tri_strict, kk * dmat, 0.0)
    pmat = _dot_nt(qs, k) * dmat
    return dict(cq=cq, ck=ck, cv=cv, sq=sq, sk=sk, v=v, rq=rq, rk=rk, q=q, k=k, beta=beta, gcc=gcc, dmat=dmat, gam=gam,
                gam_last=gam_last, kscale=kscale, qs=qs, kb=kb, vb=vb, kbg=kbg, lmat=lmat, pmat=pmat)


def _dn_gates(ps_ref, arow_ref, dtb_ref, pst_ref, acol_ref, dtc_ref):
    ps = ps_ref[...]
    beta_all = _sigmoid(ps)
    g_all = arow_ref[...] * _softplus(ps + dtb_ref[...])
    return ps, beta_all, g_all


def _dn_fwd(p, ps, pst, convw, arow, dtb, acol, dtc, comm=()):
    t = p.shape[0]
    nh = 1024 // HD
    tb = min(t, 256)
    nc = tb // CHUNK
    specs, _ = _dn_specs(t, tb, False)

    def compute(q_ref, k_ref, v_ref, hq_ref, hk_ref, hv_ref, cw_ref, ps_ref, pst_ref, arow_ref, dtb_ref, acol_ref, dtc_ref,
             o_ref, ss_ref, ti_ref, xs_ref, cpre_ref, st_ref, gcrow_ref, ball_ref, gall_ref):
        i = pl.program_id(0)

        @pl.when(i == 0)
        def _():
            st_ref[...] = jnp.zeros_like(st_ref)

        _dn_conv(i == 0, (q_ref, k_ref, v_ref), (hq_ref, hk_ref, hv_ref), cw_ref, xs_ref, cpre_ref, tb)
        _, beta_all, g_all = _dn_gates(ps_ref, arow_ref, dtb_ref, pst_ref, acol_ref, dtc_ref)
        ball_ref[...] = beta_all
        gall_ref[...] = g_all
        ri = lax.broadcasted_iota(jnp.int32, (CHUNK, CHUNK), 0)
        ci = lax.broadcasted_iota(jnp.int32, (CHUNK, CHUNK), 1)
        tri_incl, tri_strict = ri >= ci, ri > ci
        lower = tri_incl.astype(f32)
        upper = (ri <= ci).astype(f32)
        lane = lax.broadcasted_iota(jnp.int32, (CHUNK, LANES), 1)

        def chunk(c, carry):
            r0 = pl.multiple_of(c * CHUNK, CHUNK)
            gc_all = _dot_hi(lower, gall_ref[pl.ds(r0, CHUNK), :])
            g_rows = acol_ref[...] * _softplus(pst_ref[c] + dtc_ref[...])
            gcrow_ref[...] = _dot_hi(g_rows, upper)
            beta_c = ball_ref[pl.ds(r0, CHUNK), :]
            hs = range(nh)
            mm = lambda a, b: jnp.dot(a, b, preferred_element_type=f32)
            hd = [_dn_head(cpre_ref, r0, h, beta_c, gc_all, gcrow_ref, lane, tri_incl, tri_strict) for h in hs]
            tinv = _tri_inv([d["lmat"] for d in hd])
            tib = [x.astype(BF) for x in tinv]
            u = [mm(tib[h], hd[h]["vb"].astype(BF)) for h in hs]
            w = [mm(tib[h], hd[h]["kbg"].astype(BF)) for h in hs]
            s = [st_ref[h] for h in hs]
            sb = [x.astype(BF) for x in s]
            vn = [u[h] - mm(w[h].astype(BF), sb[h]) for h in hs]
            vnb = [x.astype(BF) for x in vn]
            o = [mm((hd[h]["qs"] * hd[h]["gam"]).astype(BF), sb[h]) + mm(hd[h]["pmat"].astype(BF), vnb[h]) for h in hs]
            snew = [s[h] * hd[h]["gam_last"] + _dot_tn(hd[h]["k"] * hd[h]["kscale"], vnb[h]) for h in hs]
            for h in hs:
                o_ref[pl.ds(r0, CHUNK), pl.ds(HD * h, HD)] = o[h]
                ss_ref[c, h] = sb[h]
                ti_ref[c, h] = tinv[h]
                st_ref[h] = snew[h]
            return carry

        lax.fori_loop(0, nc, chunk, 0)

    return _with_comm(
        compute, comm, "dn_fwd", t // tb, (p, p, p, p, p, p, convw, ps, pst, arow, dtb, acol, dtc), specs,
        (jax.ShapeDtypeStruct((t, 1024), f32), jax.ShapeDtypeStruct((t // CHUNK, nh, HD, HD), BF),
         jax.ShapeDtypeStruct((t // CHUNK, nh, CHUNK, CHUNK), f32)),
        (pl.BlockSpec((tb, 1024), lambda i: (i, 0)), pl.BlockSpec((nc, nh, HD, HD), lambda i: (i, 0, 0, 0)),
         pl.BlockSpec((nc, nh, CHUNK, CHUNK), lambda i: (i, 0, 0, 0))),
        [pltpu.VMEM((tb + QK_HALO, 3072), f32), pltpu.VMEM((tb, 3072), f32), pltpu.VMEM((nh, HD, HD), f32),
         pltpu.VMEM((16, CHUNK), f32), pltpu.VMEM((tb, LANES), f32), pltpu.VMEM((tb, LANES), f32)])


def _dn_bwd(p, ps, pst, convw, arow, dtb, acol, dtc, do, ss, tinv_all, comm=()):
    t = p.shape[0]
    nh = 1024 // HD
    tb = min(t, 256)
    nc = tb // CHUNK
    nt = t // tb
    specs, ti = _dn_specs(t, tb, True)
    specs = specs + [pl.BlockSpec((tb, 1024), lambda i: (ti(i), 0)), pl.BlockSpec((nc, nh, HD, HD), lambda i: (ti(i), 0, 0, 0)),
                     pl.BlockSpec((nc, nh, CHUNK, CHUNK), lambda i: (ti(i), 0, 0, 0))]

    def compute(q_ref, k_ref, v_ref, hq_ref, hk_ref, hv_ref, cw_ref, ps_ref, pst_ref, arow_ref, dtb_ref, acol_ref, dtc_ref,
             do_ref, ss_ref, ti_ref,
             dp_ref, dps_ref, dcw_ref, dal_ref, ddt_ref,
             xs_ref, cpre_ref, dst_ref, gcrow_ref, ball_ref, gall_ref, dcs_ref, dball_ref, dgcall_ref):
        i = pl.program_id(0)

        @pl.when(i == 0)
        def _():
            dst_ref[...] = jnp.zeros_like(dst_ref)
            dcs_ref[...] = jnp.zeros_like(dcs_ref)
            dcw_ref[...] = jnp.zeros_like(dcw_ref)
            dal_ref[...] = jnp.zeros_like(dal_ref)
            ddt_ref[...] = jnp.zeros_like(ddt_ref)

        _dn_conv(i == nt - 1, (q_ref, k_ref, v_ref), (hq_ref, hk_ref, hv_ref), cw_ref, xs_ref, cpre_ref, tb)
        ps, beta_all, g_all = _dn_gates(ps_ref, arow_ref, dtb_ref, pst_ref, acol_ref, dtc_ref)
        ball_ref[...] = beta_all
        gall_ref[...] = g_all
        ri = lax.broadcasted_iota(jnp.int32, (CHUNK, CHUNK), 0)
        ci = lax.broadcasted_iota(jnp.int32, (CHUNK, CHUNK), 1)
        tri_incl, tri_strict = ri >= ci, ri > ci
        lower = tri_incl.astype(f32)
        upper = (ri <= ci).astype(f32)
        lane = lax.broadcasted_iota(jnp.int32, (CHUNK, LANES), 1)
        row_last = lax.broadcasted_iota(jnp.int32, (CHUNK, 1), 0) == CHUNK - 1

        def chunk(cc, carry):
            c = nc - 1 - cc
            r0 = pl.multiple_of(c * CHUNK, CHUNK)
            rows = pl.ds(r0, CHUNK)
            gc_all = _dot_hi(lower, gall_ref[rows, :])
            g_rows = acol_ref[...] * _softplus(pst_ref[c] + dtc_ref[...])
            gcrow_ref[...] = _dot_hi(g_rows, upper)
            beta_c = ball_ref[rows, :]
            hs = range(nh)
            bf = lambda a: a.astype(BF)
            mm = lambda a, b: jnp.dot(a, b, preferred_element_type=f32)
            mnt = lambda a, b: lax.dot_general(a, b, (((1,), (1,)), ((), ())), preferred_element_type=f32)
            mtn = lambda a, b: lax.dot_general(a, b, (((0,), (0,)), ((), ())), preferred_element_type=f32)
            rsum = lambda a: jnp.sum(a, axis=1, keepdims=True)
            hd = [_dn_head(cpre_ref, r0, h, beta_c, gc_all, gcrow_ref, lane, tri_incl, tri_strict) for h in hs]
            tib = [bf(ti_ref[c, h]) for h in hs]
            sb = [ss_ref[c, h] for h in hs]
            vbb = [bf(d["vb"]) for d in hd]
            kgb = [bf(d["kbg"]) for d in hd]
            u = [mm(tib[h], vbb[h]) for h in hs]
            w = [mm(tib[h], kgb[h]) for h in hs]
            wb = [bf(x) for x in w]
            vn = [u[h] - mm(wb[h], sb[h]) for h in hs]
            vnb = [bf(x) for x in vn]
            qgb = [bf(d["qs"] * d["gam"]) for d in hd]
            kt = [d["k"] * d["kscale"] for d in hd]
            ktb = [bf(x) for x in kt]
            pmb = [bf(d["pmat"]) for d in hd]
            dob = [bf(do_ref[rows, pl.ds(HD * h, HD)]) for h in hs]
            dsn = [dst_ref[h] for h in hs]
            dsnb = [bf(x) for x in dsn]
            dvn = [mtn(pmb[h], dob[h]) + mm(ktb[h], dsnb[h]) for h in hs]
            dvnb = [bf(x) for x in dvn]
            dpm = [jnp.where(tri_incl, mnt(dob[h], vnb[h]), 0.0) for h in hs]
            dqg = [mnt(dob[h], sb[h]) for h in hs]
            dkt = [mnt(vnb[h], dsnb[h]) for h in hs]
            dw = [-mnt(dvnb[h], sb[h]) for h in hs]
            dwb = [bf(x) for x in dw]
            dgl = [jnp.sum(rsum(sb[h].astype(f32) * dsn[h]), axis=0, keepdims=True) for h in hs]
            dsnew = [mtn(qgb[h], dob[h]) + hd[h]["gam_last"] * dsn[h] - mtn(wb[h], dvnb[h]) for h in hs]
            for h in hs:
                dst_ref[h] = dsnew[h]
            dtm = [mnt(dvnb[h], vbb[h]) + mnt(dwb[h], kgb[h]) for h in hs]
            dvb = [mtn(tib[h], dvnb[h]) for h in hs]
            dkbg = [mtn(tib[h], dwb[h]) for h in hs]
            x1 = [bf(mtn(tib[h], bf(dtm[h]))) for h in hs]
            dl = [jnp.where(tri_strict, -mnt(x1[h], tib[h]), 0.0) for h in hs]
            mmat = [dl[h] * hd[h]["lmat"] + dpm[h] * hd[h]["pmat"] for h in hs]
            dkkb = [bf(dl[h] * hd[h]["dmat"]) for h in hs]
            dqkb = [bf(dpm[h] * hd[h]["dmat"]) for h in hs]
            kb16 = [bf(d["k"]) for d in hd]
            dkb = [mm(dkkb[h], kb16[h]) + dkbg[h] * hd[h]["gam"] for h in hs]
            dk = [mtn(dkkb[h], bf(hd[h]["kb"])) + mtn(dqkb[h], bf(hd[h]["qs"])) + dkt[h] * hd[h]["kscale"] + dkb[h] * hd[h]["beta"]
                  for h in hs]
            dq = [(mm(dqkb[h], kb16[h]) + dqg[h] * hd[h]["gam"]) * (HD ** -0.5) for h in hs]
            dbeta = [rsum(dkb[h] * hd[h]["k"] + dvb[h] * hd[h]["v"]) for h in hs]
            dgam = [rsum(dkbg[h] * hd[h]["kb"] + dqg[h] * hd[h]["qs"]) for h in hs]
            ktdk = [rsum(dkt[h] * kt[h]) for h in hs]
            csum = [rsum(jnp.where(ri == ci, jnp.sum(mmat[h], axis=0, keepdims=True), 0.0)) for h in hs]
            dbeta_all = jnp.zeros((CHUNK, LANES), f32)
            dgc_all = jnp.zeros((CHUNK, LANES), f32)
            for h in hs:
                d = hd[h]
                extra = jnp.sum(ktdk[h], axis=0, keepdims=True) + dgl[h] * d["gam_last"]
                dgc = rsum(mmat[h]) - csum[h] + dgam[h] * d["gam"] - ktdk[h] + jnp.where(row_last, extra, 0.0)
                dbeta_all = jnp.where(lane == h, dbeta[h], dbeta_all)
                dgc_all = jnp.where(lane == 8 + h, dgc, dgc_all)
                dsq = d["rq"] * (dq[h] - d["q"] * rsum(dq[h] * d["q"]))
                dsk = d["rk"] * (dk[h] - d["k"] * rsum(dk[h] * d["k"]))
                dcs_ref[rows, pl.ds(HD * h, HD)] = dsq * _dsilu(d["cq"])
                dcs_ref[rows, pl.ds(1024 + HD * h, HD)] = dsk * _dsilu(d["ck"])
                dcs_ref[rows, pl.ds(2048 + HD * h, HD)] = (dvb[h] * d["beta"]) * _dsilu(d["cv"])
            dball_ref[rows, :] = dbeta_all
            dgcall_ref[rows, :] = _dot_hi(upper, dgc_all)
            return carry

        lax.fori_loop(0, nc, chunk, 0)

        dbeta_t = dball_ref[...]
        dg_t = dgcall_ref[...]
        sg = _sigmoid(ps + dtb_ref[...])
        da = dg_t * arow_ref[...] * sg
        dps_ref[...] = dbeta_t * beta_all * (1.0 - beta_all) + da
        ddt_ref[...] += jnp.sum(da, axis=0, keepdims=True)
        dal_ref[...] += jnp.sum(dg_t * g_all, axis=0, keepdims=True)

        for sub in range(3072 // 256):
            cs = slice(sub * 256, (sub + 1) * 256)
            dc = dcs_ref[0:tb, cs]
            acc = cw_ref[DN_CONV - 1:DN_CONV, cs] * dc
            for j in range(DN_CONV - 1):
                acc = acc + cw_ref[j:j + 1, cs] * dcs_ref[pl.ds(DN_CONV - 1 - j, tb), cs]
            dp_ref[:, cs] = acc.astype(BF)
            for j in range(DN_CONV):
                dcw_ref[j:j + 1, cs] += jnp.sum(dc * xs_ref[pl.ds(QK_HALO - 3 + j, tb), cs], axis=0, keepdims=True)
        dcs_ref[tb:tb + QK_HALO, :] = dcs_ref[0:QK_HALO, :]

    vecl = pl.BlockSpec((1, LANES), lambda i: (0, 0))
    return _with_comm(
        compute, comm, "dn_bwd", nt, (p, p, p, p, p, p, convw, ps, pst, arow, dtb, acol, dtc, do, ss, tinv_all), specs,
        (jax.ShapeDtypeStruct((t, 3072), BF), jax.ShapeDtypeStruct((t, LANES), f32), jax.ShapeDtypeStruct((DN_CONV, 3072), f32),
         jax.ShapeDtypeStruct((1, LANES), f32), jax.ShapeDtypeStruct((1, LANES), f32)),
        (pl.BlockSpec((tb, 3072), lambda i: (ti(i), 0)), pl.BlockSpec((tb, LANES), lambda i: (ti(i), 0)),
         pl.BlockSpec((DN_CONV, 3072), lambda i: (0, 0)), vecl, vecl),
        [pltpu.VMEM((tb + QK_HALO, 3072), f32), pltpu.VMEM((tb, 3072), f32), pltpu.VMEM((nh, HD, HD), f32),
         pltpu.VMEM((16, CHUNK), f32), pltpu.VMEM((tb, LANES), f32), pltpu.VMEM((tb, LANES), f32),
         pltpu.VMEM((tb + QK_HALO, 3072), f32), pltpu.VMEM((tb, LANES), f32), pltpu.VMEM((tb, LANES), f32)])


def _ln_fwd(x, g, b):
    mu = jnp.mean(x, axis=1, keepdims=True)
    xc = x - mu
    rstd = lax.rsqrt(jnp.mean(xc * xc, axis=1, keepdims=True) + LN_EPS)
    xhat = xc * rstd
    return xhat * g + b, xhat, rstd


def _ln_bwd(dy, xhat, rstd, g):
    dxh = dy * g
    return rstd * (dxh - jnp.mean(dxh, axis=1, keepdims=True) - xhat * jnp.mean(dxh * xhat, axis=1, keepdims=True))


def _br_specs(t, tb, rev):
    nt = t // tb
    ti = (lambda i: nt - 1 - i) if rev else (lambda i: i)
    hb = tb // CV_HALO
    col = lambda off: off // 512
    specs = [pl.BlockSpec((tb, 1024), lambda i: (ti(i), 0)),
             pl.BlockSpec((tb, 1024), lambda i: (ti(i), 0))]
    for off in (P_SGU, P_SGV, P_SGG, P_CVA, P_CVB, P_CVG):
        specs.append(pl.BlockSpec((tb, 512), lambda i, off=off: (ti(i), col(off))))
    for off in (P_CVA, P_CVB):
        specs.append(pl.BlockSpec((CV_HALO, 512), lambda i, off=off: (jnp.maximum(ti(i) * hb - 1, 0), col(off))))
    v512 = pl.BlockSpec((1, 512), lambda i: (0, 0))
    specs += [pl.BlockSpec((1, HD), lambda i: (0, 0)), v512, v512,
              pl.BlockSpec((SG_GROUPS, SG_BLOCK, SG_BLOCK), lambda i: (0, 0, 0)),
              pl.BlockSpec((SG_BLOCK, 512), lambda i: (0, 0)),
              pl.BlockSpec((CV_HALO, 512), lambda i: (0, 0)),
              v512, v512, v512]
    return specs, ti


def _sg_mask():
    r = lax.broadcasted_iota(jnp.int32, (SG_BLOCK, SG_BLOCK), 0) // CHUNK
    c = lax.broadcasted_iota(jnp.int32, (SG_BLOCK, SG_BLOCK), 1) // CHUNK
    return r >= c


def _cv_glu(first, a_ref, b_ref, ha_ref, hb_ref, gs_ref, tb):
    ha = jnp.where(first, 0.0, ha_ref[...].astype(f32))
    gs_ref[0:CV_HALO, :] = ha * _sigmoid(hb_ref[...].astype(f32))
    gs_ref[CV_HALO:CV_HALO + tb, :] = a_ref[...].astype(f32) * _sigmoid(b_ref[...].astype(f32))


SUBLANES = 8
CV_SHIFT_ROWS = CV_HALO - SUBLANES


def _cv_shift(src_ref, sh_ref, tb, cs):
    for b in range(1, SUBLANES):
        sh_ref[b - 1, 0:tb + CV_SHIFT_ROWS, cs] = src_ref[pl.ds(b, tb + CV_SHIFT_ROWS), cs]


def _cv_tap(src_ref, sh_ref, off, tb, cs):
    a, b = divmod(off, SUBLANES)
    if b == 0:
        return src_ref[pl.ds(SUBLANES * a, tb), cs]
    return sh_ref[b - 1, pl.ds(SUBLANES * a, tb), cs]


def _cv_conv(gs_ref, gsh_ref, cw_ref, cb_ref, tb, cs):
    acc = cb_ref[:, cs] + cw_ref[0:1, cs] * _cv_tap(gs_ref, gsh_ref, CV_HALO - (CV_K - 1), tb, cs)
    for j in range(1, CV_K):
        acc = acc + cw_ref[j:j + 1, cs] * _cv_tap(gs_ref, gsh_ref, CV_HALO - (CV_K - 1) + j, tb, cs)
    return acc


def _branch_fwd(o, p, dng, slg, slb, sgw, sgbias, cvw, cvb, clg, clb):
    t = o.shape[0]
    tb = min(t, 256)
    specs, _ = _br_specs(t, tb, False)

    def body(o_ref, z_ref, su_ref, sv_ref, sgt_ref, ca_ref, cb_ref, cg_ref, ha_ref, hb_ref,
             dng_ref, slg_ref, slb_ref, sgw_ref, sgb_ref, cvw_ref, cvb_ref, clg_ref, clb_ref, y_ref, gs_ref, vln_ref, gsh_ref):
        i = pl.program_id(0)
        for h in range(1024 // HD):
            cs = slice(h * HD, (h + 1) * HD)
            oh = o_ref[:, cs]
            r = lax.rsqrt(jnp.mean(oh * oh, axis=1, keepdims=True) + EPS)
            y_ref[:, cs] = ((oh * r) * dng_ref[...] * _silu(z_ref[:, cs].astype(f32))).astype(BF)
        mask = _sg_mask()
        for g in range(SG_GROUPS):
            cs = slice(g * HD, (g + 1) * HD)
            vg = _gelu(sv_ref[:, cs].astype(f32))
            vln, _, _ = _ln_fwd(vg, slg_ref[:, cs], slb_ref[:, cs])
            vln_ref[...] = vln
            ws = jnp.where(mask, sgw_ref[g], 0.0)
            for nb in range(tb // SG_BLOCK):
                rs = slice(nb * SG_BLOCK, (nb + 1) * SG_BLOCK)
                mixed = _dot(ws, vln_ref[rs, :]) + sgb_ref[:, cs]
                u = _gelu(su_ref[rs, cs].astype(f32))
                y_ref[rs, 1024 + g * HD:1024 + (g + 1) * HD] = (u * mixed * _silu(sgt_ref[rs, cs].astype(f32))).astype(BF)
        _cv_glu(i == 0, ca_ref, cb_ref, ha_ref, hb_ref, gs_ref, tb)
        for g in range(CV_GROUPS):
            cs = slice(g * HD, (g + 1) * HD)
            _cv_shift(gs_ref, gsh_ref, tb, cs)
            dw = _cv_conv(gs_ref, gsh_ref, cvw_ref, cvb_ref, tb, cs)
            ln, _, _ = _ln_fwd(dw, clg_ref[:, cs], clb_ref[:, cs])
            y_ref[:, 1536 + g * HD:1536 + (g + 1) * HD] = (_silu(ln) * _silu(cg_ref[:, cs].astype(f32))).astype(BF)

    return _pc(body, name="branch_fwd", grid=(t // tb,), out_shape=jax.ShapeDtypeStruct((t, 2048), BF),
               in_specs=specs, out_specs=pl.BlockSpec((tb, 2048), lambda i: (i, 0)),
               scratch_shapes=[pltpu.VMEM((tb + CV_HALO, 512), f32), pltpu.VMEM((tb, HD), f32),
                               pltpu.VMEM((SUBLANES - 1, tb + CV_SHIFT_ROWS, 512), f32)],
               compiler_params=_cp(("arbitrary",)))(o, p, p, p, p, p, p, p, p, p, dng, slg, slb, sgw, sgbias, cvw, cvb, clg, clb)


def _branch_bwd(o, p, dng, slg, slb, sgw, sgbias, cvw, cvb, clg, clb, dy):
    t = o.shape[0]
    tb = min(t, 256)
    nt = t // tb
    specs, ti = _br_specs(t, tb, True)
    specs = specs + [pl.BlockSpec((tb, 2048), lambda i: (ti(i), 0))]

    def body(o_ref, z_ref, su_ref, sv_ref, sgt_ref, ca_ref, cb_ref, cg_ref, ha_ref, hb_ref,
             dng_ref, slg_ref, slb_ref, sgw_ref, sgb_ref, cvw_ref, cvb_ref, clg_ref, clb_ref, dy_ref,
             dp_ref, do_ref, ddng_ref, dslg_ref, dslb_ref, dsgw_ref, dsgb_ref, dcvw_ref, dcvb_ref, dclg_ref, dclb_ref,
             gs_ref, vln_ref, xh_ref, dvl_ref, ddw_ref, dbias_ref, gsh_ref, dsh_ref):
        i = pl.program_id(0)

        @pl.when(i == 0)
        def _():
            for r in (ddng_ref, dslg_ref, dslb_ref, dsgw_ref, dsgb_ref, dcvw_ref, dcvb_ref, dclg_ref, dclb_ref, ddw_ref, dbias_ref):
                r[...] = jnp.zeros_like(r)

        for h in range(1024 // HD):
            cs = slice(h * HD, (h + 1) * HD)
            oh = o_ref[:, cs]
            zz = z_ref[:, cs].astype(f32)
            dyh = dy_ref[:, cs].astype(f32)
            r = lax.rsqrt(jnp.mean(oh * oh, axis=1, keepdims=True) + EPS)
            nrm = oh * r
            sz = _silu(zz)
            dn = dyh * dng_ref[...] * sz
            ddng_ref[...] += jnp.sum(dyh * nrm * sz, axis=0, keepdims=True)
            dp_ref[:, cs] = (dyh * nrm * dng_ref[...] * _dsilu(zz)).astype(BF)
            do_ref[:, cs] = r * (dn - nrm * jnp.mean(dn * nrm, axis=1, keepdims=True))
        mask = _sg_mask()
        for g in range(SG_GROUPS):
            cs = slice(g * HD, (g + 1) * HD)
            sv = sv_ref[:, cs].astype(f32)
            vln, xhat, rstd = _ln_fwd(_gelu(sv), slg_ref[:, cs], slb_ref[:, cs])
            vln_ref[...] = vln
            xh_ref[...] = xhat
            ws = jnp.where(mask, sgw_ref[g], 0.0)
            dws = jnp.zeros((SG_BLOCK, SG_BLOCK), f32)
            for nb in range(tb // SG_BLOCK):
                rs = slice(nb * SG_BLOCK, (nb + 1) * SG_BLOCK)
                dyb = dy_ref[rs, 1024 + g * HD:1024 + (g + 1) * HD].astype(f32)
                su = su_ref[rs, cs].astype(f32)
                gt = sgt_ref[rs, cs].astype(f32)
                mixed = _dot(ws, vln_ref[rs, :]) + sgb_ref[:, cs]
                u = _gelu(su)
                sgt = _silu(gt)
                dmixed = dyb * u * sgt
                dp_ref[rs, P_SGU + g * HD:P_SGU + (g + 1) * HD] = (dyb * mixed * sgt * _dgelu(su)).astype(BF)
                dp_ref[rs, P_SGG + g * HD:P_SGG + (g + 1) * HD] = (dyb * u * mixed * _dsilu(gt)).astype(BF)
                dvl_ref[rs, :] = _dot_tn(ws, dmixed)
                dws = dws + _dot_nt(dmixed, vln_ref[rs, :])
                dbias_ref[:, cs] += dmixed
            dsgw_ref[g] += jnp.where(mask, dws, 0.0)
            dvl = dvl_ref[...]
            xhat = xh_ref[...]
            dslg_ref[:, cs] += jnp.sum(dvl * xhat, axis=0, keepdims=True)
            dslb_ref[:, cs] += jnp.sum(dvl, axis=0, keepdims=True)
            dvg = _ln_bwd(dvl, xhat, rstd, slg_ref[:, cs])
            dp_ref[:, P_SGV + g * HD:P_SGV + (g + 1) * HD] = (dvg * _dgelu(sv)).astype(BF)
        _cv_glu(i == nt - 1, ca_ref, cb_ref, ha_ref, hb_ref, gs_ref, tb)
        for g in range(CV_GROUPS):
            cs = slice(g * HD, (g + 1) * HD)
            dyc = dy_ref[:, 1536 + g * HD:1536 + (g + 1) * HD].astype(f32)
            cg = cg_ref[:, cs].astype(f32)
            _cv_shift(gs_ref, gsh_ref, tb, cs)
            dw = _cv_conv(gs_ref, gsh_ref, cvw_ref, cvb_ref, tb, cs)
            ln, xhat, rstd = _ln_fwd(dw, clg_ref[:, cs], clb_ref[:, cs])
            dln = dyc * _silu(cg) * _dsilu(ln)
            dp_ref[:, P_CVG + g * HD:P_CVG + (g + 1) * HD] = (dyc * _silu(ln) * _dsilu(cg)).astype(BF)
            dclg_ref[:, cs] += jnp.sum(dln * xhat, axis=0, keepdims=True)
            dclb_ref[:, cs] += jnp.sum(dln, axis=0, keepdims=True)
            ddw = _ln_bwd(dln, xhat, rstd, clg_ref[:, cs])
            dcvb_ref[:, cs] += jnp.sum(ddw, axis=0, keepdims=True)
            ddw_ref[0:tb, cs] = ddw
            _cv_shift(ddw_ref, dsh_ref, tb, cs)
            dglu = cvw_ref[CV_K - 1:CV_K, cs] * ddw
            for j in range(CV_K - 1):
                dglu = dglu + cvw_ref[j:j + 1, cs] * _cv_tap(ddw_ref, dsh_ref, CV_K - 1 - j, tb, cs)
            for j in range(CV_K):
                tap = _cv_tap(gs_ref, gsh_ref, CV_HALO - (CV_K - 1) + j, tb, cs)
                dcvw_ref[j:j + 1, cs] += jnp.sum(ddw * tap, axis=0, keepdims=True)
            a = ca_ref[:, cs].astype(f32)
            sb = _sigmoid(cb_ref[:, cs].astype(f32))
            dp_ref[:, P_CVA + g * HD:P_CVA + (g + 1) * HD] = (dglu * sb).astype(BF)
            dp_ref[:, P_CVB + g * HD:P_CVB + (g + 1) * HD] = (dglu * a * sb * (1.0 - sb)).astype(BF)
        ddw_ref[tb:tb + CV_HALO, :] = ddw_ref[0:CV_HALO, :]

        @pl.when(i == nt - 1)
        def _():
            lane = lax.broadcasted_iota(jnp.int32, (SG_BLOCK, LANES), 1)
            acc = jnp.zeros((SG_BLOCK, LANES), f32)
            for g in range(SG_GROUPS):
                acc = jnp.where(lane == g, jnp.sum(dbias_ref[:, g * HD:(g + 1) * HD], axis=1, keepdims=True), acc)
            dsgb_ref[...] = acc

    v512 = pl.BlockSpec((1, 512), lambda i: (0, 0))
    return _pc(body, name="branch_bwd", grid=(nt,),
               out_shape=(jax.ShapeDtypeStruct((t, 4096), BF), jax.ShapeDtypeStruct((t, 1024), f32),
                          jax.ShapeDtypeStruct((1, HD), f32), jax.ShapeDtypeStruct((1, 512), f32), jax.ShapeDtypeStruct((1, 512), f32),
                          jax.ShapeDtypeStruct((SG_GROUPS, SG_BLOCK, SG_BLOCK), f32), jax.ShapeDtypeStruct((SG_BLOCK, LANES), f32),
                          jax.ShapeDtypeStruct((CV_HALO, 512), f32), jax.ShapeDtypeStruct((1, 512), f32),
                          jax.ShapeDtypeStruct((1, 512), f32), jax.ShapeDtypeStruct((1, 512), f32)),
               in_specs=specs,
               out_specs=(pl.BlockSpec((tb, 4096), lambda i: (ti(i), 0)), pl.BlockSpec((tb, 1024), lambda i: (ti(i), 0)),
                          pl.BlockSpec((1, HD), lambda i: (0, 0)), v512, v512,
                          pl.BlockSpec((SG_GROUPS, SG_BLOCK, SG_BLOCK), lambda i: (0, 0, 0)),
                          pl.BlockSpec((SG_BLOCK, LANES), lambda i: (0, 0)), pl.BlockSpec((CV_HALO, 512), lambda i: (0, 0)),
                          v512, v512, v512),
               scratch_shapes=[pltpu.VMEM((tb + CV_HALO, 512), f32), pltpu.VMEM((tb, HD), f32), pltpu.VMEM((tb, HD), f32),
                               pltpu.VMEM((tb, HD), f32), pltpu.VMEM((tb + CV_HALO, 512), f32), pltpu.VMEM((SG_BLOCK, 512), f32),
                               pltpu.VMEM((SUBLANES - 1, tb + CV_SHIFT_ROWS, 512), f32),
                               pltpu.VMEM((SUBLANES - 1, tb + CV_SHIFT_ROWS, 512), f32)],
               compiler_params=_cp(("arbitrary",)))(o, p, p, p, p, p, p, p, p, p, dng, slg, slb, sgw, sgbias, cvw, cvb, clg, clb, dy)


def _outproj(ycat, w, x, gate):
    t, d = x.shape
    tm = min(t, 512)

    def body(y_ref, w_ref, x_ref, g_ref, o_ref):
        o_ref[...] = x_ref[...] + g_ref[...] * jnp.dot(y_ref[...], w_ref[...], preferred_element_type=f32)

    return _pc(body, name="outproj", grid=(t // tm,), out_shape=jax.ShapeDtypeStruct((t, d), f32),
               in_specs=[pl.BlockSpec((tm, ycat.shape[1]), lambda i: (i, 0)), pl.BlockSpec(w.shape, lambda i: (0, 0)),
                         pl.BlockSpec((tm, d), lambda i: (i, 0)), pl.BlockSpec((1, d), lambda i: (0, 0))],
               out_specs=pl.BlockSpec((tm, d), lambda i: (i, 0)), compiler_params=_cp(("arbitrary",)))(ycat, w, x, gate)


def _outproj_bwd(dxo, w, gate):
    t, d = dxo.shape
    tm = min(t, 512)

    def body(d_ref, w_ref, g_ref, o_ref):
        o_ref[...] = _dot_nt(d_ref[...] * g_ref[...], w_ref[...]).astype(BF)

    return _pc(body, name="outproj_bwd", grid=(t // tm,), out_shape=jax.ShapeDtypeStruct((t, w.shape[0]), BF),
               in_specs=[pl.BlockSpec((tm, d), lambda i: (i, 0)), pl.BlockSpec(w.shape, lambda i: (0, 0)),
                         pl.BlockSpec((1, d), lambda i: (0, 0))],
               out_specs=pl.BlockSpec((tm, w.shape[0]), lambda i: (i, 0)), compiler_params=_cp(("arbitrary",)))(dxo, w, gate)


def _tn_acc(a, b, name):
    kk, m = a.shape
    n = b.shape[1]
    tk, tn = min(kk, 512), min(n, 1024)

    def body(a_ref, b_ref, o_ref):
        @pl.when(pl.program_id(1) == 0)
        def _():
            o_ref[...] = jnp.zeros_like(o_ref)
        o_ref[...] += _dot_tn(a_ref[...], b_ref[...])

    return _pc(body, name=name, grid=(n // tn, kk // tk), out_shape=jax.ShapeDtypeStruct((m, n), f32),
               in_specs=[pl.BlockSpec((tk, m), lambda j, k: (k, 0)), pl.BlockSpec((tk, tn), lambda j, k: (k, j))],
               out_specs=pl.BlockSpec((m, tn), lambda j, k: (0, j)), compiler_params=_cp(("arbitrary", "arbitrary")))(a, b)


def _wout_grad(gmat, w, gate):
    m, n = gmat.shape
    tr = m // N_DEV

    def body(g_ref, w_ref, gt_ref, o_ref, dg_ref):
        @pl.when(pl.program_id(0) == 0)
        def _():
            dg_ref[...] = jnp.zeros_like(dg_ref)
        gm = g_ref[...]
        o_ref[0] = (gm * gt_ref[...]).astype(BF)
        dg_ref[...] += jnp.sum(gm * w_ref[...].astype(f32), axis=0, keepdims=True)

    return _pc(body, name="wout_grad", grid=(N_DEV,),
               out_shape=(jax.ShapeDtypeStruct((N_DEV, tr, n), BF), jax.ShapeDtypeStruct((1, n), f32)),
               in_specs=[pl.BlockSpec((tr, n), lambda i: (i, 0)), pl.BlockSpec((tr, n), lambda i: (i, 0)),
                         pl.BlockSpec((1, n), lambda i: (0, 0))],
               out_specs=(pl.BlockSpec((1, tr, n), lambda i: (i, 0, 0)), pl.BlockSpec((1, n), lambda i: (0, 0))),
               compiler_params=_cp(("arbitrary",)))(gmat, w, gate)


def _inproj_bwd(dpa, dpb, dps, wmain, wsmall, x, ng, sc1, dxo):
    t, d = x.shape
    na, nb = dpa.shape[1] // 1024, dpb.shape[1] // 1024
    nk = na + nb
    tm, rb = min(t, 512), 128

    def body(a_ref, b_ref, s_ref, wm_ref, ws_ref, x_ref, ng_ref, sc_ref, dxo_ref, dx_ref, dsh_ref, dsc_ref, dng_ref, acc_ref):
        i, k = pl.program_id(0), pl.program_id(1)

        @pl.when((i == 0) & (k == 0))
        def _():
            dsh_ref[...] = jnp.zeros_like(dsh_ref)
            dsc_ref[...] = jnp.zeros_like(dsc_ref)
            dng_ref[...] = jnp.zeros_like(dng_ref)

        @pl.when(k == 0)
        def _():
            acc_ref[...] = _dot_nt(s_ref[...], ws_ref[...])

        @pl.when(k < na)
        def _():
            acc_ref[...] += _dot_nt(a_ref[...], wm_ref[...])

        @pl.when(k >= na)
        def _():
            acc_ref[...] += _dot_nt(b_ref[...], wm_ref[...])

        @pl.when(k == nk - 1)
        def _():
            def rows(j, carry):
                r0 = pl.multiple_of(j * rb, rb)
                rr = pl.ds(r0, rb)
                xv = x_ref[rr, :]
                dh = acc_ref[rr, :]
                r = lax.rsqrt(jnp.mean(xv * xv, axis=1, keepdims=True) + EPS)
                xn = xv * r
                dsh_ref[...] += jnp.sum(dh, axis=0, keepdims=True)
                dsc_ref[...] += jnp.sum(dh * (xn * ng_ref[...]), axis=0, keepdims=True)
                dng_ref[...] += jnp.sum(dh * sc_ref[...] * xn, axis=0, keepdims=True)
                dxn = dh * (ng_ref[...] * sc_ref[...])
                dx_ref[rr, :] = r * (dxn - xn * jnp.mean(dxn * xn, axis=1, keepdims=True)) + dxo_ref[rr, :]
                return carry
            lax.fori_loop(0, tm // rb, rows, 0)

    vec = pl.BlockSpec((1, d), lambda i, k: (0, 0))
    row = pl.BlockSpec((tm, d), lambda i, k: (i, 0))
    shp = jax.ShapeDtypeStruct((1, d), f32)
    return _pc(body, name="inproj_bwd", grid=(t // tm, nk), out_shape=(jax.ShapeDtypeStruct((t, d), f32), shp, shp, shp),
               in_specs=[pl.BlockSpec((tm, 1024), lambda i, k: (i, jnp.minimum(k, na - 1))),
                         pl.BlockSpec((tm, 1024), lambda i, k: (i, jnp.clip(k - na, 0, nb - 1))),
                         pl.BlockSpec((tm, LANES), lambda i, k: (i, 0)),
                         pl.BlockSpec((d, 1024), lambda i, k: (0, k)), pl.BlockSpec((d, LANES), lambda i, k: (0, 0)),
                         row, vec, vec, row],
               out_specs=(row, vec, vec, vec), scratch_shapes=[pltpu.VMEM((tm, d), f32)],
               compiler_params=_cp(("arbitrary", "arbitrary")))(dpa, dpb, dps, wmain, wsmall, x, ng, sc1, dxo)


def _loss_head(x, fg, tgt):
    t, d = x.shape
    tm = min(t, 512)

    def body(x_ref, g_ref, t_ref, dx_ref, l_ref, dg_ref):
        @pl.when(pl.program_id(0) == 0)
        def _():
            l_ref[...] = jnp.zeros_like(l_ref)
            dg_ref[...] = jnp.zeros_like(dg_ref)
        xv = x_ref[...]
        r = lax.rsqrt(jnp.mean(xv * xv, axis=1, keepdims=True) + EPS)
        xn = xv * r
        err = xn * g_ref[...] - t_ref[...]
        l_ref[...] += 0.5 * jnp.sum(jnp.mean(err * err, axis=1, keepdims=True), axis=0, keepdims=True)
        dy = err * (1.0 / d)
        dg_ref[...] += jnp.sum(dy * xn, axis=0, keepdims=True)
        dxn = dy * g_ref[...]
        dx_ref[...] = r * (dxn - xn * jnp.mean(dxn * xn, axis=1, keepdims=True))

    row = pl.BlockSpec((tm, d), lambda i: (i, 0))
    vec = pl.BlockSpec((1, d), lambda i: (0, 0))
    return _pc(body, name="loss_head", grid=(t // tm,),
               out_shape=(jax.ShapeDtypeStruct((t, d), f32), jax.ShapeDtypeStruct((1, 1), f32), jax.ShapeDtypeStruct((1, d), f32)),
               in_specs=[row, vec, row], out_specs=(row, pl.BlockSpec((1, 1), lambda i: (0, 0)), vec),
               compiler_params=_cp(("arbitrary",)))(x, fg, tgt)


def _pack(arrs, rows_mult=8):
    flat = jnp.concatenate([a.reshape(-1).astype(f32) for a in arrs])
    n = flat.shape[0]
    per = rows_mult * LANES
    pad = (-n) % per
    if pad:
        flat = jnp.concatenate([flat, jnp.zeros((pad,), f32)])
    return flat.reshape(-1, LANES)


def _unpack(packed, shapes, lead=()):
    flat = packed.reshape(lead + (-1,))
    out, off = [], 0
    for s in shapes:
        n = 1
        for v in s:
            n *= v
        out.append(flat[..., off:off + n].reshape(lead + tuple(s)))
        off += n
    return out


def _pad_lanes(v, at):
    return jnp.pad(v.astype(f32), (at, LANES - at - v.shape[0])).reshape(1, LANES)


def kernel(x, c, norm_g, w_ada, b_ada, w_in, conv_qkv, a_log, dt_bias, dn_norm_g, sg_ln_g, sg_ln_b, sg_w, sg_b, cv_w, cv_b, cv_ln_g, cv_ln_b, w_out, final_g, loss_target, m_norm_g, m_w_ada, m_b_ada, m_w_in, m_conv_qkv, m_a_log, m_dt_bias, m_dn_norm_g, m_sg_ln_g, m_sg_ln_b, m_sg_w, m_sg_b, m_cv_w, m_cv_b, m_cv_ln_g, m_cv_ln_b, m_w_out, m_final_g, v_norm_g, v_w_ada, v_b_ada, v_w_in, v_conv_qkv, v_a_log, v_dt_bias, v_dn_norm_g, v_sg_ln_g, v_sg_ln_b, v_sg_w, v_sg_b, v_cv_w, v_cv_b, v_cv_ln_g, v_cv_ln_b, v_w_out, v_final_g):
    nl, d = norm_g.shape
    t = x.shape[1]
    nh = a_log.shape[1]
    xi, yi, ci = lax.axis_index("x"), lax.axis_index("y"), lax.axis_index("c")
    me = 4 * xi + 2 * yi + ci
    x0 = x[0]
    tgt = loss_target[0]
    ada_cols = w_ada.shape[2]
    in_cols = w_in.shape[2]
    cq_cols = conv_qkv.shape[2]
    cvw_cols = cv_w.shape[2]

    start_shapes = [c.shape, conv_qkv.shape, cv_w.shape]
    start_pack = _pack([c, conv_qkv, cv_w])
    wi_b, wo_b = w_in.astype(BF), w_out.astype(BF)
    g_small, g_win0, g_wout0 = _exchange([start_pack, wi_b[0], wo_b[0]], ["gather"] * 3, "gather_start")
    c_all, cq_g, cvw_g = _unpack(g_small, start_shapes, lead=(N_DEV,))
    c_all = c_all.reshape(N_DEV, d)
    conv_full = jnp.moveaxis(cq_g, 0, 2).reshape(nl, DN_CONV, N_DEV * cq_cols)
    cvw_full = jnp.moveaxis(cvw_g, 0, 2).reshape(nl, CV_K, N_DEV * cvw_cols)

    def repack(g_win, g_wout):
        w_main_l, w_small_l = _repack_w(g_win)
        return w_main_l, w_small_l, g_wout.reshape(N_DEV * w_out.shape[1], w_out.shape[2])

    weights = [repack(g_win0, g_wout0)]

    b_my = lax.dynamic_slice_in_dim(b_ada, me * ada_cols, ada_cols, axis=1).reshape(nl, 1, ada_cols)
    mod_part = _ada_fwd(c_all, w_ada, b_my)
    (mod_g,) = _exchange([mod_part], ["gather"], "gather_mod")
    mod_mine = lax.dynamic_index_in_dim(mod_g, me, axis=2, keepdims=False)
    mod = jnp.moveaxis(mod_mine, 0, 1).reshape(nl, N_DEV * ada_cols)
    shift, scale, gate = mod[:, :d], mod[:, d:2 * d], mod[:, 2 * d:]

    arow = [_pad_lanes(-jnp.exp(a_log[l]), 8) for l in range(nl)]
    dtb = [_pad_lanes(dt_bias[l], 8) for l in range(nl)]
    saved = []
    xl = x0
    for l in range(nl):
        sc1 = (1.0 + scale[l]).reshape(1, d)
        w_main_l, w_small_l, w_out_l = weights[l]
        p, ps, hb = _inproj(xl, norm_g[l].reshape(1, d), sc1, shift[l].reshape(1, d), w_main_l, w_small_l)
        pst = ps[:, :16].reshape(t // CHUNK, CHUNK, 16).transpose(0, 2, 1)
        acol = jnp.broadcast_to(arow[l][0, :16].reshape(16, 1), (16, CHUNK))
        dtc = jnp.broadcast_to(dtb[l][0, :16].reshape(16, 1), (16, CHUNK))
        if l + 1 < nl:
            o, ss, tinv, g_win_n, g_wout_n = _dn_fwd(p, ps, pst, conv_full[l], arow[l], dtb[l], acol, dtc,
                                                     comm=((wi_b[l + 1], wo_b[l + 1]), ("gather", "gather")))
            weights.append(repack(g_win_n, g_wout_n))
        else:
            o, ss, tinv = _dn_fwd(p, ps, pst, conv_full[l], arow[l], dtb[l], acol, dtc)
        sgbias = jnp.repeat(sg_b[l].T, HD, axis=1)
        cvw_pad = jnp.pad(cvw_full[l], ((0, CV_HALO - CV_K), (0, 0)))
        br_par = (dn_norm_g[l].reshape(1, HD), sg_ln_g[l].reshape(1, -1), sg_ln_b[l].reshape(1, -1), sg_w[l], sgbias, cvw_pad,
                  cv_b[l].reshape(1, -1), cv_ln_g[l].reshape(1, -1), cv_ln_b[l].reshape(1, -1))
        ycat = _branch_fwd(o, p, *br_par)
        xn = _outproj(ycat, w_out_l, xl, gate[l].reshape(1, d))
        saved.append((xl, p, ps, pst, hb, o, ss, tinv, ycat, sc1, acol, dtc, br_par))
        xl = xn

    dx, loss_part, dfinal_g = _loss_head(xl, final_g.reshape(1, d), tgt)
    loss = lax.psum(loss_part[0, 0], ("x", "y", "c"))

    small_grads = [None] * nl
    dmods = [None] * nl
    win_r = [None] * nl
    wout_r = [None] * nl
    slabs = None
    for l in reversed(range(nl)):
        xl, p, ps, pst, hb, o, ss, tinv, ycat, sc1, acol, dtc, br_par = saved[l]
        w_main_l, w_small_l, w_out_l = weights[l]
        gate_l = gate[l].reshape(1, d)
        dycat = _outproj_bwd(dx, w_out_l, gate_l)
        gmat = _tn_acc(ycat, dx, "wout_tn")
        wout_slab, dgate = _wout_grad(gmat, w_out_l, gate_l)
        dpa, do, ddng, dslg, dslb, dsgw, dsgb, dcvw, dcvb, dclg, dclb = _branch_bwd(o, p, *br_par, dycat)
        if slabs is not None:
            dpb, dps, dconvw, dal, ddt, win_r[l + 1], wout_r[l + 1] = _dn_bwd(
                p, ps, pst, conv_full[l], arow[l], dtb[l], acol, dtc, do, ss, tinv, comm=(slabs, ("scatter", "scatter")))
        else:
            dpb, dps, dconvw, dal, ddt = _dn_bwd(p, ps, pst, conv_full[l], arow[l], dtb[l], acol, dtc, do, ss, tinv)
        dx, dshift, dscale, dng = _inproj_bwd(dpa, dpb, dps, w_main_l, w_small_l, xl, norm_g[l].reshape(1, d), sc1, dx)
        gwa = _tn_acc(hb, dpa, "win_tn_a")
        gwb = _tn_acc(hb, dpb, "win_tn_b")
        gws = _tn_acc(hb, dps, "win_tn_s")
        slabs = (_grad_slabs(gwa, gwb, gws, in_cols), wout_slab)
        dmods[l] = jnp.concatenate([dshift[0], dscale[0], dgate[0]])
        small_grads[l] = (dng[0], dal[0, 8:8 + nh], ddt[0, 8:8 + nh], ddng[0], dslg[0], dslb[0], dsgw, dsgb[:, :SG_GROUPS].T,
                          dcvb[0], dclg[0], dclb[0], dconvw, dcvw[:CV_K])

    stack = lambda j: jnp.stack([small_grads[l][j] for l in range(nl)])
    dmod = jnp.stack(dmods)
    rep_names = ["b_ada", "norm_g", "a_log", "dt_bias", "dn_norm_g", "sg_ln_g", "sg_ln_b", "sg_w", "sg_b", "cv_b", "cv_ln_g",
                 "cv_ln_b", "final_g"]
    rep_grads = [dmod] + [stack(j) for j in range(11)] + [dfinal_g[0]]
    full_grads = [stack(11), stack(12)]
    grad_shapes = [g.shape for g in rep_grads + full_grads]
    gpack = _pack(rep_grads + full_grads, 1024)
    gpack_g, win_r[0], wout_r[0] = _exchange([gpack, slabs[0], slabs[1]], ["gather", "scatter", "scatter"], "exchange_grads")

    gsum = _sum_slabs(gpack_g, "sum_small")
    gl = _unpack(gsum, grad_shapes)
    g_rep = dict(zip(rep_names, gl[:len(rep_names)]))
    g_conv = lax.dynamic_slice_in_dim(gl[-2], me * cq_cols, cq_cols, axis=2)
    g_cvw = lax.dynamic_slice_in_dim(gl[-1], me * cvw_cols, cvw_cols, axis=2)

    g_win, d_win, nm_win, nv_win = _adam_layers(win_r, w_in, m_w_in, v_w_in, "adam_w_in", 128)
    g_wo, d_wo, nm_wo, nv_wo = _adam_layers(wout_r, w_out, m_w_out, v_w_out, "adam_w_out", 64)
    dmod_all = _unpack(gpack_g, grad_shapes, lead=(N_DEV,))[0]
    dmod_my = jnp.moveaxis(lax.dynamic_slice_in_dim(dmod_all, me * ada_cols, ada_cols, axis=2), 0, 1)
    g_wa, d_wa, nm_wa, nv_wa = _ada_bwd_adam(c_all, dmod_my, w_ada, m_w_ada, v_w_ada)

    small_w = dict(b_ada=b_ada, norm_g=norm_g, a_log=a_log, dt_bias=dt_bias, dn_norm_g=dn_norm_g, sg_ln_g=sg_ln_g, sg_ln_b=sg_ln_b,
                   sg_w=sg_w, sg_b=sg_b, cv_b=cv_b, cv_ln_g=cv_ln_g, cv_ln_b=cv_ln_b, final_g=final_g, conv_qkv=conv_qkv, cv_w=cv_w)
    small_m = dict(b_ada=m_b_ada, norm_g=m_norm_g, a_log=m_a_log, dt_bias=m_dt_bias, dn_norm_g=m_dn_norm_g, sg_ln_g=m_sg_ln_g,
                   sg_ln_b=m_sg_ln_b, sg_w=m_sg_w, sg_b=m_sg_b, cv_b=m_cv_b, cv_ln_g=m_cv_ln_g, cv_ln_b=m_cv_ln_b,
                   final_g=m_final_g, conv_qkv=m_conv_qkv, cv_w=m_cv_w)
    small_v = dict(b_ada=v_b_ada, norm_g=v_norm_g, a_log=v_a_log, dt_bias=v_dt_bias, dn_norm_g=v_dn_norm_g, sg_ln_g=v_sg_ln_g,
                   sg_ln_b=v_sg_ln_b, sg_w=v_sg_w, sg_b=v_sg_b, cv_b=v_cv_b, cv_ln_g=v_cv_ln_g, cv_ln_b=v_cv_ln_b,
                   final_g=v_final_g, conv_qkv=v_conv_qkv, cv_w=v_cv_w)
    small_g = dict(g_rep, conv_qkv=g_conv, cv_w=g_cvw)
    names = rep_names + ["conv_qkv", "cv_w"]
    shapes = [small_w[n].shape for n in names]
    gp = _pack([small_g[n] for n in names], 1024)
    sg_, sd_, sm_, sv_ = _adam_slabs(gp.reshape((1,) + gp.shape), _pack([small_w[n] for n in names], 1024),
                                     _pack([small_m[n] for n in names], 1024), _pack([small_v[n] for n in names], 1024),
                                     "adam_small", 1024)
    sgrad = dict(zip(names, _unpack(sg_, shapes)))
    sdelta = dict(zip(names, _unpack(sd_, shapes)))
    snm = dict(zip(names, _unpack(sm_, shapes)))
    snv = dict(zip(names, _unpack(sv_, shapes)))
    for dct, big in ((sgrad, (g_wa, g_win, g_wo)), (sdelta, (d_wa, d_win, d_wo)), (snm, (nm_wa, nm_win, nm_wo)), (snv, (nv_wa, nv_win, nv_wo))):
        dct["w_ada"] = big[0]
        dct["w_in"] = big[1].reshape(w_in.shape)
        dct["w_out"] = big[2].reshape(w_out.shape)

    order = ["norm_g", "w_ada", "b_ada", "w_in", "conv_qkv", "a_log", "dt_bias", "dn_norm_g", "sg_ln_g", "sg_ln_b", "sg_w", "sg_b",
             "cv_w", "cv_b", "cv_ln_g", "cv_ln_b", "w_out", "final_g"]
    outs = [loss, dx.reshape(x.shape)]
    for dct in (sgrad, sdelta, snm, snv):
        outs += [dct[n] for n in order]
    return tuple(outs)
```

```python
import functools

import jax
import jax.numpy as jnp
from jax import lax
from jax.experimental import pallas as pl
from jax.experimental.pallas import tpu as pltpu

f32 = jnp.float32
BF = jnp.bfloat16
N_DEV = 8
LANES = 128
CHUNK = 64
HD = 128
DN_CONV = 4
SG_BLOCK = 128
SG_GROUPS = 4
CV_GROUPS = 4
CV_K = 31
CV_HALO = 32
QK_HALO = 16
EPS = 1e-6
LN_EPS = 1e-5
ADAM_LR, ADAM_B1, ADAM_B2, ADAM_EPS, ADAM_WD, ADAM_STEP = 0.001, 0.9, 0.999, 1e-08, 0.01, 10
VMEM_LIMIT = 56 * 1024 * 1024
HI = lax.Precision.HIGHEST
MESH = pl.DeviceIdType.MESH


def _cp(sem=None, vmem=VMEM_LIMIT):
    return pltpu.CompilerParams(dimension_semantics=sem, vmem_limit_bytes=vmem)


def _pc(body, **kw):
    return pl.pallas_call(body, **kw)


def _dot(a, b):
    return jnp.dot(a.astype(BF), b.astype(BF), preferred_element_type=f32)


def _dot_nt(a, b):
    return lax.dot_general(a.astype(BF), b.astype(BF), (((1,), (1,)), ((), ())), preferred_element_type=f32)


def _dot_tn(a, b):
    return lax.dot_general(a.astype(BF), b.astype(BF), (((0,), (0,)), ((), ())), preferred_element_type=f32)


def _dot_hi(a, b):
    return jnp.dot(a, b, preferred_element_type=f32, precision=HI)


def _sigmoid(x):
    return 1.0 / (1.0 + jnp.exp(-x))


def _silu(x):
    return x * _sigmoid(x)


def _dsilu(x):
    s = _sigmoid(x)
    return s * (1.0 + x * (1.0 - s))


def _softplus(x):
    return jnp.maximum(x, 0.0) + jnp.log(1.0 + jnp.exp(-jnp.abs(x)))


def _gelu(x):
    return 0.5 * x * (1.0 + lax.erf(x * 0.7071067811865476))


def _dgelu(x):
    return 0.5 * (1.0 + lax.erf(x * 0.7071067811865476)) + x * jnp.exp(-0.5 * x * x) * 0.3989422804014327


def _lanecol(x, lane_iota, j):
    return jnp.sum(jnp.where(lane_iota == j, x, 0.0), axis=1, keepdims=True)


def _split(a):
    hi = a.astype(BF)
    return hi, (a - hi.astype(f32)).astype(BF)


def _dot3(a, b):
    mm = lambda u, v: jnp.dot(u, v, preferred_element_type=f32)
    return mm(a[0], b[0]) + (mm(a[0], b[1]) + mm(a[1], b[0]))


INV_BASE = 8


def _tri_inv(l_mats):
    n = l_mats[0].shape[0]
    r = lax.broadcasted_iota(jnp.int32, (n, n), 0)
    c = lax.broadcasted_iota(jnp.int32, (n, n), 1)
    same = lambda size: (r // size) == (c // size)
    diag = [jnp.where(same(INV_BASE), l, 0.0) for l in l_mats]
    xs = [(r == c).astype(f32) - l for l in diag]
    sp = [_split(l) for l in diag]
    size = 2
    while size < INV_BASE:
        ps = [_dot3(s, s) for s in sp]
        sp = [_split(p) for p in ps]
        sx = [_split(x) for x in xs]
        xs = [x + _dot3(a, b) for x, a, b in zip(xs, sx, sp)]
        size *= 2
    blk = INV_BASE
    while blk < n:
        join = same(2 * blk) & jnp.logical_not(same(blk))
        lo = [_split(jnp.where(join, l, 0.0)) for l in l_mats]
        sx = [_split(x) for x in xs]
        ys = [_split(_dot3(a, b)) for a, b in zip(lo, sx)]
        xs = [x - _dot3(a, b) for x, a, b in zip(xs, sx, ys)]
        blk *= 2
    return xs


def _me():
    x, y, c = lax.axis_index("x"), lax.axis_index("y"), lax.axis_index("c")
    return x, y, c, 4 * x + 2 * y + c


def _exchange(arrs, kinds, name):
    n = len(arrs)

    def body(*refs):
        copies = _comm_copies(refs[:n], refs[n:2 * n], kinds, *refs[2 * n:])
        _comm_start(copies)
        _comm_wait(copies)

    any_spec = pl.BlockSpec(memory_space=pl.ANY)
    return _pc(body, name=name, out_shape=_comm_out_shapes(arrs, kinds), in_specs=[any_spec] * n, out_specs=tuple([any_spec] * n),
               scratch_shapes=_comm_sems(n))(*arrs)


def _gather_two_level(arrs, name):
    n = len(arrs)

    def body(*refs):
        ins, outs = refs[:n], refs[n:2 * n]
        send, recv, loc = refs[2 * n:]
        x, y, c, me = _me()
        flips = ((1, 0), (0, 1), (1, 1))
        chips = [(jnp.bitwise_xor(x, fx), jnp.bitwise_xor(y, fy)) for fx, fy in flips]
        slot = lambda a, px, py, pc: outs[a].at[4 * px + 2 * py + pc]

        def copy(a, k, src, block, to):
            return pltpu.make_async_remote_copy(src_ref=src, dst_ref=slot(a, *block), send_sem=send.at[a, k], recv_sem=recv.at[a, k],
                                                device_id=to, device_id_type=MESH)

        mine, first, passed = [], [], []
        for a in range(n):
            mine.append(pltpu.make_async_copy(ins[a], slot(a, x, y, c), loc.at[a]))
            first.append(copy(a, 0, ins[a], (x, y, c), (x, y, 1 - c)))
            for j, (px, py) in enumerate(chips):
                first.append(copy(a, 1 + j, ins[a], (x, y, c), (px, py, c)))
        for cp in mine + first:
            cp.start()
        for a in range(n):
            for j, (px, py) in enumerate(chips):
                copy(a, 1 + j, ins[a], (px, py, c), (x, y, c)).wait_recv()
                fwd = copy(a, 4 + j, slot(a, px, py, c), (px, py, c), (x, y, 1 - c))
                fwd.start()
                passed.append(fwd)
        for a in range(n):
            copy(a, 0, ins[a], (x, y, 1 - c), (x, y, c)).wait_recv()
            for j, (px, py) in enumerate(chips):
                copy(a, 4 + j, ins[a], (px, py, 1 - c), (x, y, c)).wait_recv()
        for cp in first + passed:
            cp.wait_send()
        for cp in mine:
            cp.wait()

    any_spec = pl.BlockSpec(memory_space=pl.ANY)
    return _pc(body, name=name, out_shape=_comm_out_shapes(arrs, ["gather"] * n), in_specs=[any_spec] * n,
               out_specs=tuple([any_spec] * n), scratch_shapes=_comm_sems(n))(*arrs)


def _comm_out_shapes(arrs, kinds):
    return tuple(jax.ShapeDtypeStruct(a.shape if k == "scatter" else (N_DEV,) + a.shape, a.dtype) for a, k in zip(arrs, kinds))


def _comm_sems(n):
    return [pltpu.SemaphoreType.DMA((n, N_DEV - 1)), pltpu.SemaphoreType.DMA((n, N_DEV - 1)), pltpu.SemaphoreType.DMA((n,))]


def _comm_copies(ins, outs, kinds, send, recv, loc):
    x, y, c, me = _me()
    local, remote = [], []
    for a in range(len(ins)):
        own = ins[a].at[me] if kinds[a] == "scatter" else ins[a]
        local.append(pltpu.make_async_copy(own, outs[a].at[me], loc.at[a]))
        for k in range(1, N_DEV):
            px = jnp.bitwise_xor(x, (k >> 2) & 1)
            py = jnp.bitwise_xor(y, (k >> 1) & 1)
            pc = jnp.bitwise_xor(c, k & 1)
            src = ins[a].at[4 * px + 2 * py + pc] if kinds[a] == "scatter" else ins[a]
            remote.append(pltpu.make_async_remote_copy(src_ref=src, dst_ref=outs[a].at[me], send_sem=send.at[a, k - 1],
                                                       recv_sem=recv.at[a, k - 1], device_id=(px, py, pc), device_id_type=MESH))
    return local, remote


def _comm_start(copies):
    for cp in copies[0] + copies[1]:
        cp.start()


def _comm_wait(copies):
    for cp in copies[1] + copies[0]:
        cp.wait()


def _with_comm(compute, comm, name, grid, args, in_specs, out_shape, out_specs, scratch):
    arrs, kinds = comm if comm else ((), ())
    n, n_in, n_out, n_scr = len(arrs), len(args), len(out_shape), len(scratch)

    def at_step(which):
        ok = pl.program_id(0) == (0 if which == "first" else grid[0] - 1)
        for ax in range(1, len(grid)):
            ok = ok & (pl.program_id(ax) == (0 if which == "first" else grid[ax] - 1))
        return ok

    def body(*refs):
        ins, cin = refs[:n_in], refs[n_in:n_in + n]
        outs, cout = refs[n_in + n:n_in + n + n_out], refs[n_in + n + n_out:n_in + 2 * n + n_out]
        scr, sems = refs[n_in + 2 * n + n_out:n_in + 2 * n + n_out + n_scr], refs[n_in + 2 * n + n_out + n_scr:]
        if n:
            @pl.when(at_step("first"))
            def _():
                _comm_start(_comm_copies(cin, cout, kinds, *sems))
        compute(*ins, *outs, *scr)
        if n:
            @pl.when(at_step("last"))
            def _():
                _comm_wait(_comm_copies(cin, cout, kinds, *sems))

    any_spec = pl.BlockSpec(memory_space=pl.ANY)
    return _pc(body, name=name + ("_comm" if n else ""), grid=tuple(grid),
               out_shape=tuple(out_shape) + (_comm_out_shapes(arrs, kinds) if n else ()),
               in_specs=list(in_specs) + [any_spec] * n, out_specs=tuple(out_specs) + (any_spec,) * n,
               scratch_shapes=list(scratch) + (_comm_sems(n) if n else []),
               compiler_params=_cp(("arbitrary",) * len(grid)))(*args, *arrs)


def _ada_fwd(c_all, w_ada, b_my):
    nl, d, cols = w_ada.shape

    def body(c_ref, w_ref, b_ref, o_ref):
        ca = _silu(c_ref[...])
        o_ref[0] = _dot(ca, w_ref[0]) + b_ref[0]

    return _pc(body, name="ada_fwd", grid=(nl,), out_shape=jax.ShapeDtypeStruct((nl, N_DEV, cols), f32),
               in_specs=[pl.BlockSpec((N_DEV, d), lambda l: (0, 0)), pl.BlockSpec((1, d, cols), lambda l: (l, 0, 0)),
                         pl.BlockSpec((1, 1, cols), lambda l: (l, 0, 0))],
               out_specs=pl.BlockSpec((1, N_DEV, cols), lambda l: (l, 0, 0)), compiler_params=_cp(("arbitrary",)))(c_all, w_ada, b_my)


def _adam_math(w, g, m, v):
    m = ADAM_B1 * m + (1.0 - ADAM_B1) * g
    v = ADAM_B2 * v + (1.0 - ADAM_B2) * (g * g)
    m_hat = m / (1.0 - ADAM_B1 ** ADAM_STEP)
    v_hat = v / (1.0 - ADAM_B2 ** ADAM_STEP)
    delta = -ADAM_LR * (m_hat / (jnp.sqrt(v_hat) + ADAM_EPS) + ADAM_WD * w)
    return delta, m, v


def _ada_bwd_adam(c_all, dmod_my, w, m, v):
    nl, d, cols = w.shape
    tk = 512

    def body(c_ref, dm_ref, w_ref, m_ref, v_ref, g_ref, dl_ref, nm_ref, nv_ref):
        ca = _silu(c_ref[...])
        g = _dot_tn(ca, dm_ref[0])
        dl, nm, nv = _adam_math(w_ref[0], g, m_ref[0], v_ref[0])
        g_ref[0] = g
        dl_ref[0] = dl
        nm_ref[0] = nm
        nv_ref[0] = nv

    wspec = pl.BlockSpec((1, tk, cols), lambda l, k: (l, k, 0))
    shp = jax.ShapeDtypeStruct(w.shape, f32)
    return _pc(body, name="ada_bwd_adam", grid=(nl, d // tk), out_shape=(shp, shp, shp, shp),
               in_specs=[pl.BlockSpec((N_DEV, tk), lambda l, k: (0, k)), pl.BlockSpec((1, N_DEV, cols), lambda l, k: (l, 0, 0)),
                         wspec, wspec, wspec],
               out_specs=(wspec, wspec, wspec, wspec), compiler_params=_cp(("arbitrary", "arbitrary")))(c_all, dmod_my, w, m, v)


def _adam_slabs(slabs, w, m, v, name, tr):
    n, rows, cols = slabs.shape
    tr = min(tr, rows)

    def body(s_ref, w_ref, m_ref, v_ref, g_ref, dl_ref, nm_ref, nv_ref):
        g = s_ref[0].astype(f32)
        for j in range(1, n):
            g = g + s_ref[j].astype(f32)
        dl, nm, nv = _adam_math(w_ref[...], g, m_ref[...], v_ref[...])
        g_ref[...] = g
        dl_ref[...] = dl
        nm_ref[...] = nm
        nv_ref[...] = nv

    spec = pl.BlockSpec((tr, cols), lambda i: (i, 0))
    shp = jax.ShapeDtypeStruct((rows, cols), f32)
    return _pc(body, name=name, grid=(rows // tr,), out_shape=(shp, shp, shp, shp),
               in_specs=[pl.BlockSpec((n, tr, cols), lambda i: (0, i, 0)), spec, spec, spec],
               out_specs=(spec, spec, spec, spec), compiler_params=_cp(("arbitrary",)))(slabs, w, m, v)


def _adam_layers(slabs, w, m, v, name, tr):
    nl, rows, cols = w.shape
    n = slabs[0].shape[0]
    tr = min(tr, rows)

    def body(*refs):
        s_refs, (w_ref, m_ref, v_ref, g_ref, dl_ref, nm_ref, nv_ref) = refs[:nl], refs[nl:]
        for l in range(nl):
            @pl.when(pl.program_id(0) == l)
            def _(l=l):
                g = s_refs[l][0].astype(f32)
                for j in range(1, n):
                    g = g + s_refs[l][j].astype(f32)
                dl, nm, nv = _adam_math(w_ref[0], g, m_ref[0], v_ref[0])
                g_ref[0] = g
                dl_ref[0] = dl
                nm_ref[0] = nm
                nv_ref[0] = nv

    spec = pl.BlockSpec((1, tr, cols), lambda l, i: (l, i, 0))
    s_specs = [pl.BlockSpec((n, tr, cols), lambda l, i, j=j: (0, jnp.where(l == j, i, 0), 0)) for j in range(nl)]
    shp = jax.ShapeDtypeStruct(w.shape, f32)
    return _pc(body, name=name, grid=(nl, rows // tr), out_shape=(shp, shp, shp, shp),
               in_specs=s_specs + [spec, spec, spec], out_specs=(spec, spec, spec, spec),
               compiler_params=_cp(("arbitrary", "arbitrary")))(*slabs, w, m, v)


def _sum_slabs(slabs, name):
    n, rows, cols = slabs.shape
    tr = min(rows, 1024)

    def body(s_ref, o_ref):
        g = s_ref[0]
        for j in range(1, n):
            g = g + s_ref[j]
        o_ref[...] = g

    return _pc(body, name=name, grid=(rows // tr,), out_shape=jax.ShapeDtypeStruct((rows, cols), f32),
               in_specs=[pl.BlockSpec((n, tr, cols), lambda i: (0, i, 0))], out_specs=pl.BlockSpec((tr, cols), lambda i: (i, 0)),
               compiler_params=_cp(("arbitrary",)))(slabs)


NAT_Z, NAT_SMALL, NAT_REST = 3072, 4096, 4112


def _repack_w(g_win):
    n, d, cols = g_win.shape
    tr = 256

    def body(g_ref, wm_ref, ws_ref):
        nat = jnp.concatenate([g_ref[j] for j in range(n)], axis=1)
        wm_ref[...] = jnp.concatenate([nat[:, NAT_Z:NAT_SMALL], nat[:, NAT_REST:], nat[:, :NAT_Z]], axis=1)
        ws_ref[...] = jnp.concatenate([nat[:, NAT_SMALL:NAT_REST], jnp.zeros((tr, LANES - 16), nat.dtype)], axis=1)

    return _pc(body, name="repack_w", grid=(d // tr,),
               out_shape=(jax.ShapeDtypeStruct((d, n * cols - 16), g_win.dtype), jax.ShapeDtypeStruct((d, LANES), g_win.dtype)),
               in_specs=[pl.BlockSpec((n, tr, cols), lambda i: (0, i, 0))],
               out_specs=(pl.BlockSpec((tr, n * cols - 16), lambda i: (i, 0)), pl.BlockSpec((tr, LANES), lambda i: (i, 0))),
               compiler_params=_cp(("arbitrary",)))(g_win)


def _grad_slabs(gwa, gwb, gws, cols):
    d = gwa.shape[0]
    tr = 256

    def body(a_ref, b_ref, s_ref, o_ref):
        a = a_ref[...]
        nat = jnp.concatenate([b_ref[...], a[:, :NAT_SMALL - NAT_Z], s_ref[:, 0:16], a[:, NAT_SMALL - NAT_Z:]], axis=1)
        for j in range(N_DEV):
            o_ref[j] = nat[:, j * cols:(j + 1) * cols].astype(BF)

    return _pc(body, name="grad_slabs", grid=(d // tr,), out_shape=jax.ShapeDtypeStruct((N_DEV, d, cols), BF),
               in_specs=[pl.BlockSpec((tr, gwa.shape[1]), lambda i: (i, 0)), pl.BlockSpec((tr, gwb.shape[1]), lambda i: (i, 0)),
                         pl.BlockSpec((tr, LANES), lambda i: (i, 0))],
               out_specs=pl.BlockSpec((N_DEV, tr, cols), lambda i: (0, i, 0)), compiler_params=_cp(("arbitrary",)))(gwa, gwb, gws)


def _inproj(x, ng, sc1, sh, wmain, wsmall):
    t, d = x.shape
    n = wmain.shape[1]
    tm, tn, rb = min(t, 1024), 1024, 128

    def body(x_ref, ng_ref, sc_ref, sh_ref, wm_ref, ws_ref, p_ref, ps_ref, h_ref, hs_ref):
        @pl.when(pl.program_id(1) == 0)
        def _():
            def rows(i, carry):
                r0 = pl.multiple_of(i * rb, rb)
                xv = x_ref[pl.ds(r0, rb), :]
                r = lax.rsqrt(jnp.mean(xv * xv, axis=1, keepdims=True) + EPS)
                hb = (((xv * r) * ng_ref[...]) * sc_ref[...] + sh_ref[...]).astype(BF)
                hs_ref[pl.ds(r0, rb), :] = hb
                h_ref[pl.ds(r0, rb), :] = hb
                return carry
            lax.fori_loop(0, tm // rb, rows, 0)
            ps_ref[...] = jnp.dot(hs_ref[...], ws_ref[...], preferred_element_type=f32)
        p_ref[...] = jnp.dot(hs_ref[...], wm_ref[...], preferred_element_type=f32).astype(BF)

    vec = pl.BlockSpec((1, d), lambda i, j: (0, 0))
    return _pc(body, name="inproj", grid=(t // tm, n // tn),
               out_shape=(jax.ShapeDtypeStruct((t, n), BF), jax.ShapeDtypeStruct((t, LANES), f32), jax.ShapeDtypeStruct((t, d), BF)),
               in_specs=[pl.BlockSpec((tm, d), lambda i, j: (i, 0)), vec, vec, vec,
                         pl.BlockSpec((d, tn), lambda i, j: (0, j)), pl.BlockSpec((d, LANES), lambda i, j: (0, 0))],
               out_specs=(pl.BlockSpec((tm, tn), lambda i, j: (i, j)), pl.BlockSpec((tm, LANES), lambda i, j: (i, 0)),
                          pl.BlockSpec((tm, d), lambda i, j: (i, 0))),
               scratch_shapes=[pltpu.VMEM((tm, d), BF)], compiler_params=_cp(("arbitrary", "arbitrary")))(x, ng, sc1, sh, wmain, wsmall)


P_Z, P_SGU, P_SGV, P_SGG, P_CVA, P_CVB, P_CVG, P_Q = 0, 1024, 1536, 2048, 2560, 3072, 3584, 4096


def _dn_specs(t, tb, rev):
    nt = t // tb
    ti = (lambda i: nt - 1 - i) if rev else (lambda i: i)
    qb = P_Q // 1024
    hb = tb // QK_HALO
    specs = []
    for s in range(3):
        specs.append(pl.BlockSpec((tb, 1024), lambda i, s=s: (ti(i), qb + s)))
    for s in range(3):
        specs.append(pl.BlockSpec((QK_HALO, 1024), lambda i, s=s: (jnp.maximum(ti(i) * hb - 1, 0), qb + s)))
    specs.append(pl.BlockSpec((DN_CONV, 3072), lambda i: (0, 0)))
    specs.append(pl.BlockSpec((tb, LANES), lambda i: (ti(i), 0)))
    specs.append(pl.BlockSpec((tb // CHUNK, 16, CHUNK), lambda i: (ti(i), 0, 0)))
    specs.append(pl.BlockSpec((1, LANES), lambda i: (0, 0)))
    specs.append(pl.BlockSpec((1, LANES), lambda i: (0, 0)))
    specs.append(pl.BlockSpec((16, CHUNK), lambda i: (0, 0)))
    specs.append(pl.BlockSpec((16, CHUNK), lambda i: (0, 0)))
    return specs, ti


def _dn_conv(first, tiles, halos, cw_ref, xs_ref, cpre_ref, tb):
    for s in range(3):
        c0 = s * 1024
        xs_ref[0:QK_HALO, c0:c0 + 1024] = jnp.where(first, 0.0, halos[s][...].astype(f32))
        xs_ref[QK_HALO:QK_HALO + tb, c0:c0 + 1024] = tiles[s][...].astype(f32)
    for sub in range(3072 // 256):
        cs = slice(sub * 256, (sub + 1) * 256)
        acc = cw_ref[0:1, cs] * xs_ref[pl.ds(QK_HALO - 3, tb), cs]
        for j in range(1, DN_CONV):
            acc = acc + cw_ref[j:j + 1, cs] * xs_ref[pl.ds(QK_HALO - 3 + j, tb), cs]
        cpre_ref[:, cs] = acc


def _dn_head(cpre_ref, r0, h, beta_all, gc_all, gcrow_ref, lane, tri_incl, tri_strict):
    rows = pl.ds(r0, CHUNK)
    cq = cpre_ref[rows, pl.ds(HD * h, HD)]
    ck = cpre_ref[rows, pl.ds(1024 + HD * h, HD)]
    cv = cpre_ref[rows, pl.ds(2048 + HD * h, HD)]
    sq, sk, v = _silu(cq), _silu(ck), _silu(cv)
    rq = lax.rsqrt(jnp.sum(sq * sq, axis=1, keepdims=True) + EPS)
    rk = lax.rsqrt(jnp.sum(sk * sk, axis=1, keepdims=True) + EPS)
    q, k = sq * rq, sk * rk
    beta = _lanecol(beta_all, lane, h)
    gcc = _lanecol(gc_all, lane, 8 + h)
    gcr = gcrow_ref[pl.ds(8 + h, 1), :]
    diff = gcc - gcr
    dmat = jnp.where(tri_incl, jnp.exp(jnp.where(tri_incl, diff, 0.0)), 0.0)
    gam = jnp.exp(gcc)
    glast = gcr[:, CHUNK - 1:CHUNK]
    gam_last = jnp.exp(glast)
    kscale = jnp.exp(glast - gcc)
    qs = q * (HD ** -0.5)
    kb = k * beta
    vb = v * beta
    kbg = kb * gam
    kk = _dot_nt(kb, k)
    lmat = jnp.where(tri_strict, kk * dmat, 0.0)
    pmat = _dot_nt(qs, k) * dmat
    return dict(cq=cq, ck=ck, cv=cv, sq=sq, sk=sk, v=v, rq=rq, rk=rk, q=q, k=k, beta=beta, gcc=gcc, dmat=dmat, gam=gam,
                gam_last=gam_last, kscale=kscale, qs=qs, kb=kb, vb=vb, kbg=kbg, lmat=lmat, pmat=pmat)


def _dn_gates(ps_ref, arow_ref, dtb_ref, pst_ref, acol_ref, dtc_ref):
    ps = ps_ref[...]
    beta_all = _sigmoid(ps)
    g_all = arow_ref[...] * _softplus(ps + dtb_ref[...])
    return ps, beta_all, g_all


def _dn_fwd(p, ps, pst, convw, arow, dtb, acol, dtc, comm=()):
    t = p.shape[0]
    nh = 1024 // HD
    tb = min(t, 256)
    nc = tb // CHUNK
    specs, _ = _dn_specs(t, tb, False)

    def compute(q_ref, k_ref, v_ref, hq_ref, hk_ref, hv_ref, cw_ref, ps_ref, pst_ref, arow_ref, dtb_ref, acol_ref, dtc_ref,
             o_ref, ss_ref, ti_ref, xs_ref, cpre_ref, st_ref, gcrow_ref, ball_ref, gall_ref):
        i = pl.program_id(0)

        @pl.when(i == 0)
        def _():
            st_ref[...] = jnp.zeros_like(st_ref)

        _dn_conv(i == 0, (q_ref, k_ref, v_ref), (hq_ref, hk_ref, hv_ref), cw_ref, xs_ref, cpre_ref, tb)
        _, beta_all, g_all = _dn_gates(ps_ref, arow_ref, dtb_ref, pst_ref, acol_ref, dtc_ref)
        ball_ref[...] = beta_all
        gall_ref[...] = g_all
        ri = lax.broadcasted_iota(jnp.int32, (CHUNK, CHUNK), 0)
        ci = lax.broadcasted_iota(jnp.int32, (CHUNK, CHUNK), 1)
        tri_incl, tri_strict = ri >= ci, ri > ci
        lower = tri_incl.astype(f32)
        upper = (ri <= ci).astype(f32)
        lane = lax.broadcasted_iota(jnp.int32, (CHUNK, LANES), 1)

        def chunk(c, carry):
            r0 = pl.multiple_of(c * CHUNK, CHUNK)
            gc_all = _dot_hi(lower, gall_ref[pl.ds(r0, CHUNK), :])
            g_rows = acol_ref[...] * _softplus(pst_ref[c] + dtc_ref[...])
            gcrow_ref[...] = _dot_hi(g_rows, upper)
            beta_c = ball_ref[pl.ds(r0, CHUNK), :]
            hs = range(nh)
            mm = lambda a, b: jnp.dot(a, b, preferred_element_type=f32)
            hd = [_dn_head(cpre_ref, r0, h, beta_c, gc_all, gcrow_ref, lane, tri_incl, tri_strict) for h in hs]
            tinv = _tri_inv([d["lmat"] for d in hd])
            tib = [x.astype(BF) for x in tinv]
            u = [mm(tib[h], hd[h]["vb"].astype(BF)) for h in hs]
            w = [mm(tib[h], hd[h]["kbg"].astype(BF)) for h in hs]
            s = [st_ref[h] for h in hs]
            sb = [x.astype(BF) for x in s]
            vn = [u[h] - mm(w[h].astype(BF), sb[h]) for h in hs]
            vnb = [x.astype(BF) for x in vn]
            o = [mm((hd[h]["qs"] * hd[h]["gam"]).astype(BF), sb[h]) + mm(hd[h]["pmat"].astype(BF), vnb[h]) for h in hs]
            snew = [s[h] * hd[h]["gam_last"] + _dot_tn(hd[h]["k"] * hd[h]["kscale"], vnb[h]) for h in hs]
            for h in hs:
                o_ref[pl.ds(r0, CHUNK), pl.ds(HD * h, HD)] = o[h]
                ss_ref[c, h] = sb[h]
                ti_ref[c, h] = tinv[h]
                st_ref[h] = snew[h]
            return carry

        lax.fori_loop(0, nc, chunk, 0)

    return _with_comm(
        compute, comm, "dn_fwd", (t // tb,), (p, p, p, p, p, p, convw, ps, pst, arow, dtb, acol, dtc), specs,
        (jax.ShapeDtypeStruct((t, 1024), f32), jax.ShapeDtypeStruct((t // CHUNK, nh, HD, HD), BF),
         jax.ShapeDtypeStruct((t // CHUNK, nh, CHUNK, CHUNK), f32)),
        (pl.BlockSpec((tb, 1024), lambda i: (i, 0)), pl.BlockSpec((nc, nh, HD, HD), lambda i: (i, 0, 0, 0)),
         pl.BlockSpec((nc, nh, CHUNK, CHUNK), lambda i: (i, 0, 0, 0))),
        [pltpu.VMEM((tb + QK_HALO, 3072), f32), pltpu.VMEM((tb, 3072), f32), pltpu.VMEM((nh, HD, HD), f32),
         pltpu.VMEM((16, CHUNK), f32), pltpu.VMEM((tb, LANES), f32), pltpu.VMEM((tb, LANES), f32)])


def _dn_bwd(p, ps, pst, convw, arow, dtb, acol, dtc, do, ss, tinv_all, comm=()):
    t = p.shape[0]
    nh = 1024 // HD
    tb = min(t, 256)
    nc = tb // CHUNK
    nt = t // tb
    specs, ti = _dn_specs(t, tb, True)
    specs = specs + [pl.BlockSpec((tb, 1024), lambda i: (ti(i), 0)), pl.BlockSpec((nc, nh, HD, HD), lambda i: (ti(i), 0, 0, 0)),
                     pl.BlockSpec((nc, nh, CHUNK, CHUNK), lambda i: (ti(i), 0, 0, 0))]

    def compute(q_ref, k_ref, v_ref, hq_ref, hk_ref, hv_ref, cw_ref, ps_ref, pst_ref, arow_ref, dtb_ref, acol_ref, dtc_ref,
             do_ref, ss_ref, ti_ref,
             dp_ref, dps_ref, dcw_ref, dal_ref, ddt_ref,
             xs_ref, cpre_ref, dst_ref, gcrow_ref, ball_ref, gall_ref, dcs_ref, dball_ref, dgcall_ref):
        i = pl.program_id(0)

        @pl.when(i == 0)
        def _():
            dst_ref[...] = jnp.zeros_like(dst_ref)
            dcs_ref[...] = jnp.zeros_like(dcs_ref)
            dcw_ref[...] = jnp.zeros_like(dcw_ref)
            dal_ref[...] = jnp.zeros_like(dal_ref)
            ddt_ref[...] = jnp.zeros_like(ddt_ref)

        _dn_conv(i == nt - 1, (q_ref, k_ref, v_ref), (hq_ref, hk_ref, hv_ref), cw_ref, xs_ref, cpre_ref, tb)
        ps, beta_all, g_all = _dn_gates(ps_ref, arow_ref, dtb_ref, pst_ref, acol_ref, dtc_ref)
        ball_ref[...] = beta_all
        gall_ref[...] = g_all
        ri = lax.broadcasted_iota(jnp.int32, (CHUNK, CHUNK), 0)
        ci = lax.broadcasted_iota(jnp.int32, (CHUNK, CHUNK), 1)
        tri_incl, tri_strict = ri >= ci, ri > ci
        lower = tri_incl.astype(f32)
        upper = (ri <= ci).astype(f32)
        lane = lax.broadcasted_iota(jnp.int32, (CHUNK, LANES), 1)
        row_last = lax.broadcasted_iota(jnp.int32, (CHUNK, 1), 0) == CHUNK - 1

        def chunk(cc, carry):
            c = nc - 1 - cc
            r0 = pl.multiple_of(c * CHUNK, CHUNK)
            rows = pl.ds(r0, CHUNK)
            gc_all = _dot_hi(lower, gall_ref[rows, :])
            g_rows = acol_ref[...] * _softplus(pst_ref[c] + dtc_ref[...])
            gcrow_ref[...] = _dot_hi(g_rows, upper)
            beta_c = ball_ref[rows, :]
            hs = range(nh)
            bf = lambda a: a.astype(BF)
            mm = lambda a, b: jnp.dot(a, b, preferred_element_type=f32)
            mnt = lambda a, b: lax.dot_general(a, b, (((1,), (1,)), ((), ())), preferred_element_type=f32)
            mtn = lambda a, b: lax.dot_general(a, b, (((0,), (0,)), ((), ())), preferred_element_type=f32)
            rsum = lambda a: jnp.sum(a, axis=1, keepdims=True)
            hd = [_dn_head(cpre_ref, r0, h, beta_c, gc_all, gcrow_ref, lane, tri_incl, tri_strict) for h in hs]
            tib = [bf(ti_ref[c, h]) for h in hs]
            sb = [ss_ref[c, h] for h in hs]
            vbb = [bf(d["vb"]) for d in hd]
            kgb = [bf(d["kbg"]) for d in hd]
            u = [mm(tib[h], vbb[h]) for h in hs]
            w = [mm(tib[h], kgb[h]) for h in hs]
            wb = [bf(x) for x in w]
            vn = [u[h] - mm(wb[h], sb[h]) for h in hs]
            vnb = [bf(x) for x in vn]
            qgb = [bf(d["qs"] * d["gam"]) for d in hd]
            kt = [d["k"] * d["kscale"] for d in hd]
            ktb = [bf(x) for x in kt]
            pmb = [bf(d["pmat"]) for d in hd]
            dob = [bf(do_ref[rows, pl.ds(HD * h, HD)]) for h in hs]
            dsn = [dst_ref[h] for h in hs]
            dsnb = [bf(x) for x in dsn]
            dvn = [mtn(pmb[h], dob[h]) + mm(ktb[h], dsnb[h]) for h in hs]
            dvnb = [bf(x) for x in dvn]
            dpm = [jnp.where(tri_incl, mnt(dob[h], vnb[h]), 0.0) for h in hs]
            dqg = [mnt(dob[h], sb[h]) for h in hs]
            dkt = [mnt(vnb[h], dsnb[h]) for h in hs]
            dw = [-mnt(dvnb[h], sb[h]) for h in hs]
            dwb = [bf(x) for x in dw]
            dgl = [jnp.sum(rsum(sb[h].astype(f32) * dsn[h]), axis=0, keepdims=True) for h in hs]
            dsnew = [mtn(qgb[h], dob[h]) + hd[h]["gam_last"] * dsn[h] - mtn(wb[h], dvnb[h]) for h in hs]
            for h in hs:
                dst_ref[h] = dsnew[h]
            dtm = [mnt(dvnb[h], vbb[h]) + mnt(dwb[h], kgb[h]) for h in hs]
            dvb = [mtn(tib[h], dvnb[h]) for h in hs]
            dkbg = [mtn(tib[h], dwb[h]) for h in hs]
            x1 = [bf(mtn(tib[h], bf(dtm[h]))) for h in hs]
            dl = [jnp.where(tri_strict, -mnt(x1[h], tib[h]), 0.0) for h in hs]
            mmat = [dl[h] * hd[h]["lmat"] + dpm[h] * hd[h]["pmat"] for h in hs]
            dkkb = [bf(dl[h] * hd[h]["dmat"]) for h in hs]
            dqkb = [bf(dpm[h] * hd[h]["dmat"]) for h in hs]
            kb16 = [bf(d["k"]) for d in hd]
            dkb = [mm(dkkb[h], kb16[h]) + dkbg[h] * hd[h]["gam"] for h in hs]
            dk = [mtn(dkkb[h], bf(hd[h]["kb"])) + mtn(dqkb[h], bf(hd[h]["qs"])) + dkt[h] * hd[h]["kscale"] + dkb[h] * hd[h]["beta"]
                  for h in hs]
            dq = [(mm(dqkb[h], kb16[h]) + dqg[h] * hd[h]["gam"]) * (HD ** -0.5) for h in hs]
            dbeta = [rsum(dkb[h] * hd[h]["k"] + dvb[h] * hd[h]["v"]) for h in hs]
            dgam = [rsum(dkbg[h] * hd[h]["kb"] + dqg[h] * hd[h]["qs"]) for h in hs]
            ktdk = [rsum(dkt[h] * kt[h]) for h in hs]
            csum = [rsum(jnp.where(ri == ci, jnp.sum(mmat[h], axis=0, keepdims=True), 0.0)) for h in hs]
            dbeta_all = jnp.zeros((CHUNK, LANES), f32)
            dgc_all = jnp.zeros((CHUNK, LANES), f32)
            for h in hs:
                d = hd[h]
                extra = jnp.sum(ktdk[h], axis=0, keepdims=True) + dgl[h] * d["gam_last"]
                dgc = rsum(mmat[h]) - csum[h] + dgam[h] * d["gam"] - ktdk[h] + jnp.where(row_last, extra, 0.0)
                dbeta_all = jnp.where(lane == h, dbeta[h], dbeta_all)
                dgc_all = jnp.where(lane == 8 + h, dgc, dgc_all)
                dsq = d["rq"] * (dq[h] - d["q"] * rsum(dq[h] * d["q"]))
                dsk = d["rk"] * (dk[h] - d["k"] * rsum(dk[h] * d["k"]))
                dcs_ref[rows, pl.ds(HD * h, HD)] = dsq * _dsilu(d["cq"])
                dcs_ref[rows, pl.ds(1024 + HD * h, HD)] = dsk * _dsilu(d["ck"])
                dcs_ref[rows, pl.ds(2048 + HD * h, HD)] = (dvb[h] * d["beta"]) * _dsilu(d["cv"])
            dball_ref[rows, :] = dbeta_all
            dgcall_ref[rows, :] = _dot_hi(upper, dgc_all)
            return carry

        lax.fori_loop(0, nc, chunk, 0)

        dbeta_t = dball_ref[...]
        dg_t = dgcall_ref[...]
        sg = _sigmoid(ps + dtb_ref[...])
        da = dg_t * arow_ref[...] * sg
        dps_ref[...] = dbeta_t * beta_all * (1.0 - beta_all) + da
        ddt_ref[...] += jnp.sum(da, axis=0, keepdims=True)
        dal_ref[...] += jnp.sum(dg_t * g_all, axis=0, keepdims=True)

        for sub in range(3072 // 256):
            cs = slice(sub * 256, (sub + 1) * 256)
            dc = dcs_ref[0:tb, cs]
            acc = cw_ref[DN_CONV - 1:DN_CONV, cs] * dc
            for j in range(DN_CONV - 1):
                acc = acc + cw_ref[j:j + 1, cs] * dcs_ref[pl.ds(DN_CONV - 1 - j, tb), cs]
            dp_ref[:, cs] = acc.astype(BF)
            for j in range(DN_CONV):
                dcw_ref[j:j + 1, cs] += jnp.sum(dc * xs_ref[pl.ds(QK_HALO - 3 + j, tb), cs], axis=0, keepdims=True)
        dcs_ref[tb:tb + QK_HALO, :] = dcs_ref[0:QK_HALO, :]

    vecl = pl.BlockSpec((1, LANES), lambda i: (0, 0))
    return _with_comm(
        compute, comm, "dn_bwd", (nt,), (p, p, p, p, p, p, convw, ps, pst, arow, dtb, acol, dtc, do, ss, tinv_all), specs,
        (jax.ShapeDtypeStruct((t, 3072), BF), jax.ShapeDtypeStruct((t, LANES), f32), jax.ShapeDtypeStruct((DN_CONV, 3072), f32),
         jax.ShapeDtypeStruct((1, LANES), f32), jax.ShapeDtypeStruct((1, LANES), f32)),
        (pl.BlockSpec((tb, 3072), lambda i: (ti(i), 0)), pl.BlockSpec((tb, LANES), lambda i: (ti(i), 0)),
         pl.BlockSpec((DN_CONV, 3072), lambda i: (0, 0)), vecl, vecl),
        [pltpu.VMEM((tb + QK_HALO, 3072), f32), pltpu.VMEM((tb, 3072), f32), pltpu.VMEM((nh, HD, HD), f32),
         pltpu.VMEM((16, CHUNK), f32), pltpu.VMEM((tb, LANES), f32), pltpu.VMEM((tb, LANES), f32),
         pltpu.VMEM((tb + QK_HALO, 3072), f32), pltpu.VMEM((tb, LANES), f32), pltpu.VMEM((tb, LANES), f32)])


def _ln_fwd(x, g, b):
    mu = jnp.mean(x, axis=1, keepdims=True)
    xc = x - mu
    rstd = lax.rsqrt(jnp.mean(xc * xc, axis=1, keepdims=True) + LN_EPS)
    xhat = xc * rstd
    return xhat * g + b, xhat, rstd


def _ln_bwd(dy, xhat, rstd, g):
    dxh = dy * g
    return rstd * (dxh - jnp.mean(dxh, axis=1, keepdims=True) - xhat * jnp.mean(dxh * xhat, axis=1, keepdims=True))


def _br_specs(t, tb, rev):
    nt = t // tb
    ti = (lambda i: nt - 1 - i) if rev else (lambda i: i)
    hb = tb // CV_HALO
    col = lambda off: off // 512
    specs = [pl.BlockSpec((tb, 1024), lambda i: (ti(i), 0)),
             pl.BlockSpec((tb, 1024), lambda i: (ti(i), 0))]
    for off in (P_SGU, P_SGV, P_SGG, P_CVA, P_CVB, P_CVG):
        specs.append(pl.BlockSpec((tb, 512), lambda i, off=off: (ti(i), col(off))))
    for off in (P_CVA, P_CVB):
        specs.append(pl.BlockSpec((CV_HALO, 512), lambda i, off=off: (jnp.maximum(ti(i) * hb - 1, 0), col(off))))
    v512 = pl.BlockSpec((1, 512), lambda i: (0, 0))
    specs += [pl.BlockSpec((1, HD), lambda i: (0, 0)), v512, v512,
              pl.BlockSpec((SG_GROUPS, SG_BLOCK, SG_BLOCK), lambda i: (0, 0, 0)),
              pl.BlockSpec((SG_BLOCK, 512), lambda i: (0, 0)),
              pl.BlockSpec((CV_HALO, 512), lambda i: (0, 0)),
              v512, v512, v512]
    return specs, ti


def _sg_mask():
    r = lax.broadcasted_iota(jnp.int32, (SG_BLOCK, SG_BLOCK), 0) // CHUNK
    c = lax.broadcasted_iota(jnp.int32, (SG_BLOCK, SG_BLOCK), 1) // CHUNK
    return r >= c


def _cv_glu(first, a_ref, b_ref, ha_ref, hb_ref, gs_ref, tb):
    ha = jnp.where(first, 0.0, ha_ref[...].astype(f32))
    gs_ref[0:CV_HALO, :] = ha * _sigmoid(hb_ref[...].astype(f32))
    gs_ref[CV_HALO:CV_HALO + tb, :] = a_ref[...].astype(f32) * _sigmoid(b_ref[...].astype(f32))


SUBLANES = 8
CV_SHIFT_ROWS = CV_HALO - SUBLANES


def _cv_shift(src_ref, sh_ref, tb, cs):
    for b in range(1, SUBLANES):
        sh_ref[b - 1, 0:tb + CV_SHIFT_ROWS, cs] = src_ref[pl.ds(b, tb + CV_SHIFT_ROWS), cs]


def _cv_tap(src_ref, sh_ref, off, tb, cs):
    a, b = divmod(off, SUBLANES)
    if b == 0:
        return src_ref[pl.ds(SUBLANES * a, tb), cs]
    return sh_ref[b - 1, pl.ds(SUBLANES * a, tb), cs]


def _cv_conv(gs_ref, gsh_ref, cw_ref, cb_ref, tb, cs):
    acc = cb_ref[:, cs] + cw_ref[0:1, cs] * _cv_tap(gs_ref, gsh_ref, CV_HALO - (CV_K - 1), tb, cs)
    for j in range(1, CV_K):
        acc = acc + cw_ref[j:j + 1, cs] * _cv_tap(gs_ref, gsh_ref, CV_HALO - (CV_K - 1) + j, tb, cs)
    return acc


def _branch_fwd(o, p, dng, slg, slb, sgw, sgbias, cvw, cvb, clg, clb):
    t = o.shape[0]
    tb = min(t, 256)
    specs, _ = _br_specs(t, tb, False)

    def body(o_ref, z_ref, su_ref, sv_ref, sgt_ref, ca_ref, cb_ref, cg_ref, ha_ref, hb_ref,
             dng_ref, slg_ref, slb_ref, sgw_ref, sgb_ref, cvw_ref, cvb_ref, clg_ref, clb_ref, y_ref, gs_ref, vln_ref, gsh_ref):
        i = pl.program_id(0)
        for h in range(1024 // HD):
            cs = slice(h * HD, (h + 1) * HD)
            oh = o_ref[:, cs]
            r = lax.rsqrt(jnp.mean(oh * oh, axis=1, keepdims=True) + EPS)
            y_ref[:, cs] = ((oh * r) * dng_ref[...] * _silu(z_ref[:, cs].astype(f32))).astype(BF)
        mask = _sg_mask()
        for g in range(SG_GROUPS):
            cs = slice(g * HD, (g + 1) * HD)
            vg = _gelu(sv_ref[:, cs].astype(f32))
            vln, _, _ = _ln_fwd(vg, slg_ref[:, cs], slb_ref[:, cs])
            vln_ref[...] = vln
            ws = jnp.where(mask, sgw_ref[g], 0.0)
            for nb in range(tb // SG_BLOCK):
                rs = slice(nb * SG_BLOCK, (nb + 1) * SG_BLOCK)
                mixed = _dot(ws, vln_ref[rs, :]) + sgb_ref[:, cs]
                u = _gelu(su_ref[rs, cs].astype(f32))
                y_ref[rs, 1024 + g * HD:1024 + (g + 1) * HD] = (u * mixed * _silu(sgt_ref[rs, cs].astype(f32))).astype(BF)
        _cv_glu(i == 0, ca_ref, cb_ref, ha_ref, hb_ref, gs_ref, tb)
        for g in range(CV_GROUPS):
            cs = slice(g * HD, (g + 1) * HD)
            _cv_shift(gs_ref, gsh_ref, tb, cs)
            dw = _cv_conv(gs_ref, gsh_ref, cvw_ref, cvb_ref, tb, cs)
            ln, _, _ = _ln_fwd(dw, clg_ref[:, cs], clb_ref[:, cs])
            y_ref[:, 1536 + g * HD:1536 + (g + 1) * HD] = (_silu(ln) * _silu(cg_ref[:, cs].astype(f32))).astype(BF)

    return _pc(body, name="branch_fwd", grid=(t // tb,), out_shape=jax.ShapeDtypeStruct((t, 2048), BF),
               in_specs=specs, out_specs=pl.BlockSpec((tb, 2048), lambda i: (i, 0)),
               scratch_shapes=[pltpu.VMEM((tb + CV_HALO, 512), f32), pltpu.VMEM((tb, HD), f32),
                               pltpu.VMEM((SUBLANES - 1, tb + CV_SHIFT_ROWS, 512), f32)],
               compiler_params=_cp(("arbitrary",)))(o, p, p, p, p, p, p, p, p, p, dng, slg, slb, sgw, sgbias, cvw, cvb, clg, clb)


def _branch_bwd(o, p, dng, slg, slb, sgw, sgbias, cvw, cvb, clg, clb, dy):
    t = o.shape[0]
    tb = min(t, 256)
    nt = t // tb
    specs, ti = _br_specs(t, tb, True)
    specs = specs + [pl.BlockSpec((tb, 2048), lambda i: (ti(i), 0))]

    def body(o_ref, z_ref, su_ref, sv_ref, sgt_ref, ca_ref, cb_ref, cg_ref, ha_ref, hb_ref,
             dng_ref, slg_ref, slb_ref, sgw_ref, sgb_ref, cvw_ref, cvb_ref, clg_ref, clb_ref, dy_ref,
             dp_ref, do_ref, ddng_ref, dslg_ref, dslb_ref, dsgw_ref, dsgb_ref, dcvw_ref, dcvb_ref, dclg_ref, dclb_ref,
             gs_ref, vln_ref, xh_ref, dvl_ref, ddw_ref, dbias_ref, gsh_ref, dsh_ref):
        i = pl.program_id(0)

        @pl.when(i == 0)
        def _():
            for r in (ddng_ref, dslg_ref, dslb_ref, dsgw_ref, dsgb_ref, dcvw_ref, dcvb_ref, dclg_ref, dclb_ref, ddw_ref, dbias_ref):
                r[...] = jnp.zeros_like(r)

        for h in range(1024 // HD):
            cs = slice(h * HD, (h + 1) * HD)
            oh = o_ref[:, cs]
            zz = z_ref[:, cs].astype(f32)
            dyh = dy_ref[:, cs].astype(f32)
            r = lax.rsqrt(jnp.mean(oh * oh, axis=1, keepdims=True) + EPS)
            nrm = oh * r
            sz = _silu(zz)
            dn = dyh * dng_ref[...] * sz
            ddng_ref[...] += jnp.sum(dyh * nrm * sz, axis=0, keepdims=True)
            dp_ref[:, cs] = (dyh * nrm * dng_ref[...] * _dsilu(zz)).astype(BF)
            do_ref[:, cs] = r * (dn - nrm * jnp.mean(dn * nrm, axis=1, keepdims=True))
        mask = _sg_mask()
        for g in range(SG_GROUPS):
            cs = slice(g * HD, (g + 1) * HD)
            sv = sv_ref[:, cs].astype(f32)
            vln, xhat, rstd = _ln_fwd(_gelu(sv), slg_ref[:, cs], slb_ref[:, cs])
            vln_ref[...] = vln
            xh_ref[...] = xhat
            ws = jnp.where(mask, sgw_ref[g], 0.0)
            dws = jnp.zeros((SG_BLOCK, SG_BLOCK), f32)
            for nb in range(tb // SG_BLOCK):
                rs = slice(nb * SG_BLOCK, (nb + 1) * SG_BLOCK)
                dyb = dy_ref[rs, 1024 + g * HD:1024 + (g + 1) * HD].astype(f32)
                su = su_ref[rs, cs].astype(f32)
                gt = sgt_ref[rs, cs].astype(f32)
                mixed = _dot(ws, vln_ref[rs, :]) + sgb_ref[:, cs]
                u = _gelu(su)
                sgt = _silu(gt)
                dmixed = dyb * u * sgt
                dp_ref[rs, P_SGU + g * HD:P_SGU + (g + 1) * HD] = (dyb * mixed * sgt * _dgelu(su)).astype(BF)
                dp_ref[rs, P_SGG + g * HD:P_SGG + (g + 1) * HD] = (dyb * u * mixed * _dsilu(gt)).astype(BF)
                dvl_ref[rs, :] = _dot_tn(ws, dmixed)
                dws = dws + _dot_nt(dmixed, vln_ref[rs, :])
                dbias_ref[:, cs] += dmixed
            dsgw_ref[g] += jnp.where(mask, dws, 0.0)
            dvl = dvl_ref[...]
            xhat = xh_ref[...]
            dslg_ref[:, cs] += jnp.sum(dvl * xhat, axis=0, keepdims=True)
            dslb_ref[:, cs] += jnp.sum(dvl, axis=0, keepdims=True)
            dvg = _ln_bwd(dvl, xhat, rstd, slg_ref[:, cs])
            dp_ref[:, P_SGV + g * HD:P_SGV + (g + 1) * HD] = (dvg * _dgelu(sv)).astype(BF)
        _cv_glu(i == nt - 1, ca_ref, cb_ref, ha_ref, hb_ref, gs_ref, tb)
        for g in range(CV_GROUPS):
            cs = slice(g * HD, (g + 1) * HD)
            dyc = dy_ref[:, 1536 + g * HD:1536 + (g + 1) * HD].astype(f32)
            cg = cg_ref[:, cs].astype(f32)
            _cv_shift(gs_ref, gsh_ref, tb, cs)
            dw = _cv_conv(gs_ref, gsh_ref, cvw_ref, cvb_ref, tb, cs)
            ln, xhat, rstd = _ln_fwd(dw, clg_ref[:, cs], clb_ref[:, cs])
            dln = dyc * _silu(cg) * _dsilu(ln)
            dp_ref[:, P_CVG + g * HD:P_CVG + (g + 1) * HD] = (dyc * _silu(ln) * _dsilu(cg)).astype(BF)
            dclg_ref[:, cs] += jnp.sum(dln * xhat, axis=0, keepdims=True)
            dclb_ref[:, cs] += jnp.sum(dln, axis=0, keepdims=True)
            ddw = _ln_bwd(dln, xhat, rstd, clg_ref[:, cs])
            dcvb_ref[:, cs] += jnp.sum(ddw, axis=0, keepdims=True)
            ddw_ref[0:tb, cs] = ddw
            _cv_shift(ddw_ref, dsh_ref, tb, cs)
            dglu = cvw_ref[CV_K - 1:CV_K, cs] * ddw
            for j in range(CV_K - 1):
                dglu = dglu + cvw_ref[j:j + 1, cs] * _cv_tap(ddw_ref, dsh_ref, CV_K - 1 - j, tb, cs)
            for j in range(CV_K):
                tap = _cv_tap(gs_ref, gsh_ref, CV_HALO - (CV_K - 1) + j, tb, cs)
                dcvw_ref[j:j + 1, cs] += jnp.sum(ddw * tap, axis=0, keepdims=True)
            a = ca_ref[:, cs].astype(f32)
            sb = _sigmoid(cb_ref[:, cs].astype(f32))
            dp_ref[:, P_CVA + g * HD:P_CVA + (g + 1) * HD] = (dglu * sb).astype(BF)
            dp_ref[:, P_CVB + g * HD:P_CVB + (g + 1) * HD] = (dglu * a * sb * (1.0 - sb)).astype(BF)
        ddw_ref[tb:tb + CV_HALO, :] = ddw_ref[0:CV_HALO, :]

        @pl.when(i == nt - 1)
        def _():
            lane = lax.broadcasted_iota(jnp.int32, (SG_BLOCK, LANES), 1)
            acc = jnp.zeros((SG_BLOCK, LANES), f32)
            for g in range(SG_GROUPS):
                acc = jnp.where(lane == g, jnp.sum(dbias_ref[:, g * HD:(g + 1) * HD], axis=1, keepdims=True), acc)
            dsgb_ref[...] = acc

    v512 = pl.BlockSpec((1, 512), lambda i: (0, 0))
    return _pc(body, name="branch_bwd", grid=(nt,),
               out_shape=(jax.ShapeDtypeStruct((t, 4096), BF), jax.ShapeDtypeStruct((t, 1024), f32),
                          jax.ShapeDtypeStruct((1, HD), f32), jax.ShapeDtypeStruct((1, 512), f32), jax.ShapeDtypeStruct((1, 512), f32),
                          jax.ShapeDtypeStruct((SG_GROUPS, SG_BLOCK, SG_BLOCK), f32), jax.ShapeDtypeStruct((SG_BLOCK, LANES), f32),
                          jax.ShapeDtypeStruct((CV_HALO, 512), f32), jax.ShapeDtypeStruct((1, 512), f32),
                          jax.ShapeDtypeStruct((1, 512), f32), jax.ShapeDtypeStruct((1, 512), f32)),
               in_specs=specs,
               out_specs=(pl.BlockSpec((tb, 4096), lambda i: (ti(i), 0)), pl.BlockSpec((tb, 1024), lambda i: (ti(i), 0)),
                          pl.BlockSpec((1, HD), lambda i: (0, 0)), v512, v512,
                          pl.BlockSpec((SG_GROUPS, SG_BLOCK, SG_BLOCK), lambda i: (0, 0, 0)),
                          pl.BlockSpec((SG_BLOCK, LANES), lambda i: (0, 0)), pl.BlockSpec((CV_HALO, 512), lambda i: (0, 0)),
                          v512, v512, v512),
               scratch_shapes=[pltpu.VMEM((tb + CV_HALO, 512), f32), pltpu.VMEM((tb, HD), f32), pltpu.VMEM((tb, HD), f32),
                               pltpu.VMEM((tb, HD), f32), pltpu.VMEM((tb + CV_HALO, 512), f32), pltpu.VMEM((SG_BLOCK, 512), f32),
                               pltpu.VMEM((SUBLANES - 1, tb + CV_SHIFT_ROWS, 512), f32),
                               pltpu.VMEM((SUBLANES - 1, tb + CV_SHIFT_ROWS, 512), f32)],
               compiler_params=_cp(("arbitrary",)))(o, p, p, p, p, p, p, p, p, p, dng, slg, slb, sgw, sgbias, cvw, cvb, clg, clb, dy)


def _outproj(ycat, w, x, gate):
    t, d = x.shape
    tm = min(t, 512)

    def body(y_ref, w_ref, x_ref, g_ref, o_ref):
        o_ref[...] = x_ref[...] + g_ref[...] * jnp.dot(y_ref[...], w_ref[...], preferred_element_type=f32)

    return _pc(body, name="outproj", grid=(t // tm,), out_shape=jax.ShapeDtypeStruct((t, d), f32),
               in_specs=[pl.BlockSpec((tm, ycat.shape[1]), lambda i: (i, 0)), pl.BlockSpec(w.shape, lambda i: (0, 0)),
                         pl.BlockSpec((tm, d), lambda i: (i, 0)), pl.BlockSpec((1, d), lambda i: (0, 0))],
               out_specs=pl.BlockSpec((tm, d), lambda i: (i, 0)), compiler_params=_cp(("arbitrary",)))(ycat, w, x, gate)


def _outproj_bwd(dxo, w, gate):
    t, d = dxo.shape
    tm = min(t, 512)

    def body(d_ref, w_ref, g_ref, o_ref):
        o_ref[...] = _dot_nt(d_ref[...] * g_ref[...], w_ref[...]).astype(BF)

    return _pc(body, name="outproj_bwd", grid=(t // tm,), out_shape=jax.ShapeDtypeStruct((t, w.shape[0]), BF),
               in_specs=[pl.BlockSpec((tm, d), lambda i: (i, 0)), pl.BlockSpec(w.shape, lambda i: (0, 0)),
                         pl.BlockSpec((1, d), lambda i: (0, 0))],
               out_specs=pl.BlockSpec((tm, w.shape[0]), lambda i: (i, 0)), compiler_params=_cp(("arbitrary",)))(dxo, w, gate)


def _tn_acc(a, b, name):
    kk, m = a.shape
    n = b.shape[1]
    tk, tn = min(kk, 512), min(n, 1024)

    def body(a_ref, b_ref, o_ref):
        @pl.when(pl.program_id(1) == 0)
        def _():
            o_ref[...] = jnp.zeros_like(o_ref)
        o_ref[...] += _dot_tn(a_ref[...], b_ref[...])

    return _pc(body, name=name, grid=(n // tn, kk // tk), out_shape=jax.ShapeDtypeStruct((m, n), f32),
               in_specs=[pl.BlockSpec((tk, m), lambda j, k: (k, 0)), pl.BlockSpec((tk, tn), lambda j, k: (k, j))],
               out_specs=pl.BlockSpec((m, tn), lambda j, k: (0, j)), compiler_params=_cp(("arbitrary", "arbitrary")))(a, b)


def _wout_grad(gmat, w, gate):
    m, n = gmat.shape
    tr = m // N_DEV

    def body(g_ref, w_ref, gt_ref, o_ref, dg_ref):
        @pl.when(pl.program_id(0) == 0)
        def _():
            dg_ref[...] = jnp.zeros_like(dg_ref)
        gm = g_ref[...]
        o_ref[0] = (gm * gt_ref[...]).astype(BF)
        dg_ref[...] += jnp.sum(gm * w_ref[...].astype(f32), axis=0, keepdims=True)

    return _pc(body, name="wout_grad", grid=(N_DEV,),
               out_shape=(jax.ShapeDtypeStruct((N_DEV, tr, n), BF), jax.ShapeDtypeStruct((1, n), f32)),
               in_specs=[pl.BlockSpec((tr, n), lambda i: (i, 0)), pl.BlockSpec((tr, n), lambda i: (i, 0)),
                         pl.BlockSpec((1, n), lambda i: (0, 0))],
               out_specs=(pl.BlockSpec((1, tr, n), lambda i: (i, 0, 0)), pl.BlockSpec((1, n), lambda i: (0, 0))),
               compiler_params=_cp(("arbitrary",)))(gmat, w, gate)


def _inproj_bwd(dpa, dpb, dps, wmain, wsmall, x, ng, sc1, dxo, comm=()):
    t, d = x.shape
    na, nb = dpa.shape[1] // 1024, dpb.shape[1] // 1024
    nk = na + nb
    tm, rb = min(t, 512), 128

    def body(a_ref, b_ref, s_ref, wm_ref, ws_ref, x_ref, ng_ref, sc_ref, dxo_ref, dx_ref, dsh_ref, dsc_ref, dng_ref, acc_ref):
        i, k = pl.program_id(0), pl.program_id(1)

        @pl.when((i == 0) & (k == 0))
        def _():
            dsh_ref[...] = jnp.zeros_like(dsh_ref)
            dsc_ref[...] = jnp.zeros_like(dsc_ref)
            dng_ref[...] = jnp.zeros_like(dng_ref)

        @pl.when(k == 0)
        def _():
            acc_ref[...] = _dot_nt(s_ref[...], ws_ref[...])

        @pl.when(k < na)
        def _():
            acc_ref[...] += _dot_nt(a_ref[...], wm_ref[...])

        @pl.when(k >= na)
        def _():
            acc_ref[...] += _dot_nt(b_ref[...], wm_ref[...])

        @pl.when(k == nk - 1)
        def _():
            def rows(j, carry):
                r0 = pl.multiple_of(j * rb, rb)
                rr = pl.ds(r0, rb)
                xv = x_ref[rr, :]
                dh = acc_ref[rr, :]
                r = lax.rsqrt(jnp.mean(xv * xv, axis=1, keepdims=True) + EPS)
                xn = xv * r
                dsh_ref[...] += jnp.sum(dh, axis=0, keepdims=True)
                dsc_ref[...] += jnp.sum(dh * (xn * ng_ref[...]), axis=0, keepdims=True)
                dng_ref[...] += jnp.sum(dh * sc_ref[...] * xn, axis=0, keepdims=True)
                dxn = dh * (ng_ref[...] * sc_ref[...])
                dx_ref[rr, :] = r * (dxn - xn * jnp.mean(dxn * xn, axis=1, keepdims=True)) + dxo_ref[rr, :]
                return carry
            lax.fori_loop(0, tm // rb, rows, 0)

    vec = pl.BlockSpec((1, d), lambda i, k: (0, 0))
    row = pl.BlockSpec((tm, d), lambda i, k: (i, 0))
    shp = jax.ShapeDtypeStruct((1, d), f32)
    return _with_comm(
        body, comm, "inproj_bwd", (t // tm, nk), (dpa, dpb, dps, wmain, wsmall, x, ng, sc1, dxo),
        [pl.BlockSpec((tm, 1024), lambda i, k: (i, jnp.minimum(k, na - 1))),
         pl.BlockSpec((tm, 1024), lambda i, k: (i, jnp.clip(k - na, 0, nb - 1))),
         pl.BlockSpec((tm, LANES), lambda i, k: (i, 0)),
         pl.BlockSpec((d, 1024), lambda i, k: (0, k)), pl.BlockSpec((d, LANES), lambda i, k: (0, 0)), row, vec, vec, row],
        (jax.ShapeDtypeStruct((t, d), f32), shp, shp, shp), (row, vec, vec, vec), [pltpu.VMEM((tm, d), f32)])


def _loss_head(x, fg, tgt):
    t, d = x.shape
    tm = min(t, 512)

    def body(x_ref, g_ref, t_ref, dx_ref, l_ref, dg_ref):
        @pl.when(pl.program_id(0) == 0)
        def _():
            l_ref[...] = jnp.zeros_like(l_ref)
            dg_ref[...] = jnp.zeros_like(dg_ref)
        xv = x_ref[...]
        r = lax.rsqrt(jnp.mean(xv * xv, axis=1, keepdims=True) + EPS)
        xn = xv * r
        err = xn * g_ref[...] - t_ref[...]
        l_ref[...] += 0.5 * jnp.sum(jnp.mean(err * err, axis=1, keepdims=True), axis=0, keepdims=True)
        dy = err * (1.0 / d)
        dg_ref[...] += jnp.sum(dy * xn, axis=0, keepdims=True)
        dxn = dy * g_ref[...]
        dx_ref[...] = r * (dxn - xn * jnp.mean(dxn * xn, axis=1, keepdims=True))

    row = pl.BlockSpec((tm, d), lambda i: (i, 0))
    vec = pl.BlockSpec((1, d), lambda i: (0, 0))
    return _pc(body, name="loss_head", grid=(t // tm,),
               out_shape=(jax.ShapeDtypeStruct((t, d), f32), jax.ShapeDtypeStruct((1, 1), f32), jax.ShapeDtypeStruct((1, d), f32)),
               in_specs=[row, vec, row], out_specs=(row, pl.BlockSpec((1, 1), lambda i: (0, 0)), vec),
               compiler_params=_cp(("arbitrary",)))(x, fg, tgt)


def _pack(arrs, rows_mult=8):
    flat = jnp.concatenate([a.reshape(-1).astype(f32) for a in arrs])
    n = flat.shape[0]
    per = rows_mult * LANES
    pad = (-n) % per
    if pad:
        flat = jnp.concatenate([flat, jnp.zeros((pad,), f32)])
    return flat.reshape(-1, LANES)


def _unpack(packed, shapes, lead=()):
    flat = packed.reshape(lead + (-1,))
    out, off = [], 0
    for s in shapes:
        n = 1
        for v in s:
            n *= v
        out.append(flat[..., off:off + n].reshape(lead + tuple(s)))
        off += n
    return out


def _pad_lanes(v, at):
    return jnp.pad(v.astype(f32), (at, LANES - at - v.shape[0])).reshape(1, LANES)


def kernel(x, c, norm_g, w_ada, b_ada, w_in, conv_qkv, a_log, dt_bias, dn_norm_g, sg_ln_g, sg_ln_b, sg_w, sg_b, cv_w, cv_b, cv_ln_g, cv_ln_b, w_out, final_g, loss_target, m_norm_g, m_w_ada, m_b_ada, m_w_in, m_conv_qkv, m_a_log, m_dt_bias, m_dn_norm_g, m_sg_ln_g, m_sg_ln_b, m_sg_w, m_sg_b, m_cv_w, m_cv_b, m_cv_ln_g, m_cv_ln_b, m_w_out, m_final_g, v_norm_g, v_w_ada, v_b_ada, v_w_in, v_conv_qkv, v_a_log, v_dt_bias, v_dn_norm_g, v_sg_ln_g, v_sg_ln_b, v_sg_w, v_sg_b, v_cv_w, v_cv_b, v_cv_ln_g, v_cv_ln_b, v_w_out, v_final_g):
    nl, d = norm_g.shape
    t = x.shape[1]
    nh = a_log.shape[1]
    xi, yi, ci = lax.axis_index("x"), lax.axis_index("y"), lax.axis_index("c")
    me = 4 * xi + 2 * yi + ci
    x0 = x[0]
    tgt = loss_target[0]
    ada_cols = w_ada.shape[2]
    in_cols = w_in.shape[2]
    cq_cols = conv_qkv.shape[2]
    cvw_cols = cv_w.shape[2]

    start_shapes = [c.shape, conv_qkv.shape, cv_w.shape]
    start_pack = _pack([c, conv_qkv, cv_w])
    wi_b, wo_b = w_in.astype(BF), w_out.astype(BF)
    g_small, g_win0 = _gather_two_level([start_pack, wi_b[0]], "gather_start")
    c_all, cq_g, cvw_g = _unpack(g_small, start_shapes, lead=(N_DEV,))
    c_all = c_all.reshape(N_DEV, d)
    conv_full = jnp.moveaxis(cq_g, 0, 2).reshape(nl, DN_CONV, N_DEV * cq_cols)
    cvw_full = jnp.moveaxis(cvw_g, 0, 2).reshape(nl, CV_K, N_DEV * cvw_cols)
    w_packed = [_repack_w(g_win0)]
    w_out_full = []

    b_my = lax.dynamic_slice_in_dim(b_ada, me * ada_cols, ada_cols, axis=1).reshape(nl, 1, ada_cols)
    mod_part = _ada_fwd(c_all, w_ada, b_my)
    (mod_g,) = _exchange([mod_part], ["gather"], "gather_mod")
    mod_mine = lax.dynamic_index_in_dim(mod_g, me, axis=2, keepdims=False)
    mod = jnp.moveaxis(mod_mine, 0, 1).reshape(nl, N_DEV * ada_cols)
    shift, scale, gate = mod[:, :d], mod[:, d:2 * d], mod[:, 2 * d:]

    arow = [_pad_lanes(-jnp.exp(a_log[l]), 8) for l in range(nl)]
    dtb = [_pad_lanes(dt_bias[l], 8) for l in range(nl)]
    saved = []
    xl = x0
    for l in range(nl):
        sc1 = (1.0 + scale[l]).reshape(1, d)
        w_main_l, w_small_l = w_packed[l]
        p, ps, hb = _inproj(xl, norm_g[l].reshape(1, d), sc1, shift[l].reshape(1, d), w_main_l, w_small_l)
        pst = ps[:, :16].reshape(t // CHUNK, CHUNK, 16).transpose(0, 2, 1)
        acol = jnp.broadcast_to(arow[l][0, :16].reshape(16, 1), (16, CHUNK))
        dtc = jnp.broadcast_to(dtb[l][0, :16].reshape(16, 1), (16, CHUNK))
        ahead = (wo_b[l], wi_b[l + 1]) if l + 1 < nl else (wo_b[l],)
        o, ss, tinv, g_wout, *g_win_n = _dn_fwd(p, ps, pst, conv_full[l], arow[l], dtb[l], acol, dtc,
                                                comm=(ahead, ("gather",) * len(ahead)))
        w_out_l = g_wout.reshape(N_DEV * w_out.shape[1], w_out.shape[2])
        w_out_full.append(w_out_l)
        if g_win_n:
            w_packed.append(_repack_w(g_win_n[0]))
        sgbias = jnp.repeat(sg_b[l].T, HD, axis=1)
        cvw_pad = jnp.pad(cvw_full[l], ((0, CV_HALO - CV_K), (0, 0)))
        br_par = (dn_norm_g[l].reshape(1, HD), sg_ln_g[l].reshape(1, -1), sg_ln_b[l].reshape(1, -1), sg_w[l], sgbias, cvw_pad,
                  cv_b[l].reshape(1, -1), cv_ln_g[l].reshape(1, -1), cv_ln_b[l].reshape(1, -1))
        ycat = _branch_fwd(o, p, *br_par)
        xn = _outproj(ycat, w_out_l, xl, gate[l].reshape(1, d))
        saved.append((xl, p, ps, pst, hb, o, ss, tinv, ycat, sc1, acol, dtc, br_par))
        xl = xn

    dx, loss_part, dfinal_g = _loss_head(xl, final_g.reshape(1, d), tgt)
    loss = lax.psum(loss_part[0, 0], ("x", "y", "c"))

    small_grads = [None] * nl
    dmods = [None] * nl
    win_r = [None] * nl
    wout_r = [None] * nl
    rep_names = ["b_ada", "norm_g", "a_log", "dt_bias", "dn_norm_g", "sg_ln_g", "sg_ln_b", "sg_w", "sg_b", "cv_b", "cv_ln_g",
                 "cv_ln_b", "final_g"]
    win_slab = None
    for l in reversed(range(nl)):
        xl, p, ps, pst, hb, o, ss, tinv, ycat, sc1, acol, dtc, br_par = saved[l]
        w_main_l, w_small_l = w_packed[l]
        w_out_l = w_out_full[l]
        gate_l = gate[l].reshape(1, d)
        dycat = _outproj_bwd(dx, w_out_l, gate_l)
        gmat = _tn_acc(ycat, dx, "wout_tn")
        wout_slab, dgate = _wout_grad(gmat, w_out_l, gate_l)
        dpa, do, ddng, dslg, dslb, dsgw, dsgb, dcvw, dcvb, dclg, dclb = _branch_bwd(o, p, *br_par, dycat)
        behind = (wout_slab,) if win_slab is None else (wout_slab, win_slab)
        dpb, dps, dconvw, dal, ddt, wout_r[l], *win_got = _dn_bwd(
            p, ps, pst, conv_full[l], arow[l], dtb[l], acol, dtc, do, ss, tinv, comm=(behind, ("scatter",) * len(behind)))
        if win_got:
            win_r[l + 1] = win_got[0]
        gwa = _tn_acc(hb, dpa, "win_tn_a")
        gwb = _tn_acc(hb, dpb, "win_tn_b")
        gws = _tn_acc(hb, dps, "win_tn_s")
        win_slab = _grad_slabs(gwa, gwb, gws, in_cols)
        small_grads[l] = [None, dal[0, 8:8 + nh], ddt[0, 8:8 + nh], ddng[0], dslg[0], dslb[0], dsgw, dsgb[:, :SG_GROUPS].T,
                          dcvb[0], dclg[0], dclb[0], dconvw, dcvw[:CV_K]]
        ib_args = (dpa, dpb, dps, w_main_l, w_small_l, xl, norm_g[l].reshape(1, d), sc1, dx)
        if l > 0:
            dx, dshift, dscale, dng = _inproj_bwd(*ib_args)
            dmods[l] = jnp.concatenate([dshift[0], dscale[0], dgate[0]])
            small_grads[l][0] = dng[0]
        else:
            zeros_d = jnp.zeros((d,), f32)
            dmods[0] = jnp.concatenate([zeros_d, zeros_d, dgate[0]])
            small_grads[0][0] = zeros_d
            stack = lambda j: jnp.stack([small_grads[i][j] for i in range(nl)])
            rep_grads = [jnp.stack(dmods)] + [stack(j) for j in range(11)] + [dfinal_g[0]]
            full_grads = [stack(11), stack(12)]
            grad_shapes = [g.shape for g in rep_grads + full_grads]
            gpack = _pack(rep_grads + full_grads, 1024)
            dx, dshift, dscale, dng, win_r[0], gpack_g = _inproj_bwd(*ib_args, comm=((win_slab, gpack), ("scatter", "gather")))
            late_shapes = [(d,), (d,), (d,)]
            (late_g,) = _exchange([_pack([dshift[0], dscale[0], dng[0]])], ["gather"], "exchange_late")

    gsum = _sum_slabs(gpack_g, "sum_small")
    gl = _unpack(gsum, grad_shapes)
    late = _unpack(_sum_slabs(late_g, "sum_late"), late_shapes)
    gl[0] = jnp.concatenate([jnp.concatenate([late[0], late[1], gl[0][0, 2 * d:]])[None], gl[0][1:]], axis=0)
    gl[1] = jnp.concatenate([late[2][None], gl[1][1:]], axis=0)
    g_rep = dict(zip(rep_names, gl[:len(rep_names)]))
    g_conv = lax.dynamic_slice_in_dim(gl[-2], me * cq_cols, cq_cols, axis=2)
    g_cvw = lax.dynamic_slice_in_dim(gl[-1], me * cvw_cols, cvw_cols, axis=2)

    g_win, d_win, nm_win, nv_win = _adam_layers(win_r, w_in, m_w_in, v_w_in, "adam_w_in", 128)
    g_wo, d_wo, nm_wo, nv_wo = _adam_layers(wout_r, w_out, m_w_out, v_w_out, "adam_w_out", 64)
    dmod_all = _unpack(gpack_g, grad_shapes, lead=(N_DEV,))[0]
    late_all = _unpack(late_g, late_shapes, lead=(N_DEV,))
    dmod_all = jnp.concatenate([jnp.concatenate([late_all[0], late_all[1], dmod_all[:, 0, 2 * d:]], axis=1)[:, None],
                                dmod_all[:, 1:]], axis=1)
    dmod_my = jnp.moveaxis(lax.dynamic_slice_in_dim(dmod_all, me * ada_cols, ada_cols, axis=2), 0, 1)
    g_wa, d_wa, nm_wa, nv_wa = _ada_bwd_adam(c_all, dmod_my, w_ada, m_w_ada, v_w_ada)

    small_w = dict(b_ada=b_ada, norm_g=norm_g, a_log=a_log, dt_bias=dt_bias, dn_norm_g=dn_norm_g, sg_ln_g=sg_ln_g, sg_ln_b=sg_ln_b,
                   sg_w=sg_w, sg_b=sg_b, cv_b=cv_b, cv_ln_g=cv_ln_g, cv_ln_b=cv_ln_b, final_g=final_g, conv_qkv=conv_qkv, cv_w=cv_w)
    small_m = dict(b_ada=m_b_ada, norm_g=m_norm_g, a_log=m_a_log, dt_bias=m_dt_bias, dn_norm_g=m_dn_norm_g, sg_ln_g=m_sg_ln_g,
                   sg_ln_b=m_sg_ln_b, sg_w=m_sg_w, sg_b=m_sg_b, cv_b=m_cv_b, cv_ln_g=m_cv_ln_g, cv_ln_b=m_cv_ln_b,
                   final_g=m_final_g, conv_qkv=m_conv_qkv, cv_w=m_cv_w)
    small_v = dict(b_ada=v_b_ada, norm_g=v_norm_g, a_log=v_a_log, dt_bias=v_dt_bias, dn_norm_g=v_dn_norm_g, sg_ln_g=v_sg_ln_g,
                   sg_ln_b=v_sg_ln_b, sg_w=v_sg_w, sg_b=v_sg_b, cv_b=v_cv_b, cv_ln_g=v_cv_ln_g, cv_ln_b=v_cv_ln_b,
                   final_g=v_final_g, conv_qkv=v_conv_qkv, cv_w=v_cv_w)
    small_g = dict(g_rep, conv_qkv=g_conv, cv_w=g_cvw)
    names = rep_names + ["conv_qkv", "cv_w"]
    shapes = [small_w[n].shape for n in names]
    gp = _pack([small_g[n] for n in names], 1024)
    sg_, sd_, sm_, sv_ = _adam_slabs(gp.reshape((1,) + gp.shape), _pack([small_w[n] for n in names], 1024),
                                     _pack([small_m[n] for n in names], 1024), _pack([small_v[n] for n in names], 1024),
                                     "adam_small", 1024)
    sgrad = dict(zip(names, _unpack(sg_, shapes)))
    sdelta = dict(zip(names, _unpack(sd_, shapes)))
    snm = dict(zip(names, _unpack(sm_, shapes)))
    snv = dict(zip(names, _unpack(sv_, shapes)))
    for dct, big in ((sgrad, (g_wa, g_win, g_wo)), (sdelta, (d_wa, d_win, d_wo)), (snm, (nm_wa, nm_win, nm_wo)), (snv, (nv_wa, nv_win, nv_wo))):
        dct["w_ada"] = big[0]
        dct["w_in"] = big[1].reshape(w_in.shape)
        dct["w_out"] = big[2].reshape(w_out.shape)

    order = ["norm_g", "w_ada", "b_ada", "w_in", "conv_qkv", "a_log", "dt_bias", "dn_norm_g", "sg_ln_g", "sg_ln_b", "sg_w", "sg_b",
             "cv_w", "cv_b", "cv_ln_g", "cv_ln_b", "w_out", "final_g"]
    outs = [loss, dx.reshape(x.shape)]
    for dct in (sgrad, sdelta, snm, snv):
        outs += [dct[n] for n in order]
    return tuple(outs)
```

```python
import functools

import jax
import jax.numpy as jnp
from jax import lax
from jax.experimental import pallas as pl
from jax.experimental.pallas import tpu as pltpu

f32 = jnp.float32
BF = jnp.bfloat16
N_DEV = 8
LANES = 128
CHUNK = 64
HD = 128
DN_CONV = 4
SG_BLOCK = 128
SG_GROUPS = 4
CV_GROUPS = 4
CV_K = 31
CV_HALO = 32
QK_HALO = 16
EPS = 1e-6
LN_EPS = 1e-5
ADAM_LR, ADAM_B1, ADAM_B2, ADAM_EPS, ADAM_WD, ADAM_STEP = 0.001, 0.9, 0.999, 1e-08, 0.01, 10
VMEM_LIMIT = 56 * 1024 * 1024
HI = lax.Precision.HIGHEST
MESH = pl.DeviceIdType.MESH


def _cp(sem=None, vmem=VMEM_LIMIT):
    return pltpu.CompilerParams(dimension_semantics=sem, vmem_limit_bytes=vmem)


def _pc(body, **kw):
    return pl.pallas_call(body, **kw)


def _dot(a, b):
    return jnp.dot(a.astype(BF), b.astype(BF), preferred_element_type=f32)


def _dot_nt(a, b):
    return lax.dot_general(a.astype(BF), b.astype(BF), (((1,), (1,)), ((), ())), preferred_element_type=f32)


def _dot_tn(a, b):
    return lax.dot_general(a.astype(BF), b.astype(BF), (((0,), (0,)), ((), ())), preferred_element_type=f32)


def _dot_hi(a, b):
    return jnp.dot(a, b, preferred_element_type=f32, precision=HI)


def _sigmoid(x):
    return 1.0 / (1.0 + jnp.exp(-x))


def _silu(x):
    return x * _sigmoid(x)


def _dsilu(x):
    s = _sigmoid(x)
    return s * (1.0 + x * (1.0 - s))


def _softplus(x):
    return jnp.maximum(x, 0.0) + jnp.log(1.0 + jnp.exp(-jnp.abs(x)))


def _gelu(x):
    return 0.5 * x * (1.0 + lax.erf(x * 0.7071067811865476))


def _dgelu(x):
    return 0.5 * (1.0 + lax.erf(x * 0.7071067811865476)) + x * jnp.exp(-0.5 * x * x) * 0.3989422804014327


def _lanecol(x, lane_iota, j):
    return jnp.sum(jnp.where(lane_iota == j, x, 0.0), axis=1, keepdims=True)


def _split(a):
    hi = a.astype(BF)
    return hi, (a - hi.astype(f32)).astype(BF)


def _dot3(a, b):
    mm = lambda u, v: jnp.dot(u, v, preferred_element_type=f32)
    return mm(a[0], b[0]) + (mm(a[0], b[1]) + mm(a[1], b[0]))


INV_BASE = 8
DN_GROUP = 4


def _tri_inv(l_mats):
    n = l_mats[0].shape[0]
    r = lax.broadcasted_iota(jnp.int32, (n, n), 0)
    c = lax.broadcasted_iota(jnp.int32, (n, n), 1)
    same = lambda size: (r // size) == (c // size)
    diag = [jnp.where(same(INV_BASE), l, 0.0) for l in l_mats]
    xs = [(r == c).astype(f32) - l for l in diag]
    sp = [_split(l) for l in diag]
    size = 2
    while size < INV_BASE:
        ps = [_dot3(s, s) for s in sp]
        sp = [_split(p) for p in ps]
        sx = [_split(x) for x in xs]
        xs = [x + _dot3(a, b) for x, a, b in zip(xs, sx, sp)]
        size *= 2
    blk = INV_BASE
    while blk < n:
        join = same(2 * blk) & jnp.logical_not(same(blk))
        lo = [_split(jnp.where(join, l, 0.0)) for l in l_mats]
        sx = [_split(x) for x in xs]
        ys = [_split(_dot3(a, b)) for a, b in zip(lo, sx)]
        xs = [x - _dot3(a, b) for x, a, b in zip(xs, sx, ys)]
        blk *= 2
    return xs


def _me():
    x, y, c = lax.axis_index("x"), lax.axis_index("y"), lax.axis_index("c")
    return x, y, c, 4 * x + 2 * y + c


def _exchange(arrs, kinds, name):
    n = len(arrs)

    def body(*refs):
        copies = _comm_copies(refs[:n], refs[n:2 * n], kinds, *refs[2 * n:])
        _comm_start(copies)
        _comm_wait(copies)

    any_spec = pl.BlockSpec(memory_space=pl.ANY)
    return _pc(body, name=name, out_shape=_comm_out_shapes(arrs, kinds), in_specs=[any_spec] * n, out_specs=tuple([any_spec] * n),
               scratch_shapes=_comm_sems(n))(*arrs)


def _gather_two_level(arrs, name):
    n = len(arrs)

    def body(*refs):
        ins, outs = refs[:n], refs[n:2 * n]
        send, recv, loc = refs[2 * n:]
        x, y, c, me = _me()
        flips = ((1, 0), (0, 1), (1, 1))
        chips = [(jnp.bitwise_xor(x, fx), jnp.bitwise_xor(y, fy)) for fx, fy in flips]
        slot = lambda a, px, py, pc: outs[a].at[4 * px + 2 * py + pc]

        def copy(a, k, src, block, to):
            return pltpu.make_async_remote_copy(src_ref=src, dst_ref=slot(a, *block), send_sem=send.at[a, k], recv_sem=recv.at[a, k],
                                                device_id=to, device_id_type=MESH)

        mine, first, passed = [], [], []
        for a in range(n):
            mine.append(pltpu.make_async_copy(ins[a], slot(a, x, y, c), loc.at[a]))
            first.append(copy(a, 0, ins[a], (x, y, c), (x, y, 1 - c)))
            for j, (px, py) in enumerate(chips):
                first.append(copy(a, 1 + j, ins[a], (x, y, c), (px, py, c)))
        for cp in mine + first:
            cp.start()
        for a in range(n):
            for j, (px, py) in enumerate(chips):
                copy(a, 1 + j, ins[a], (px, py, c), (x, y, c)).wait_recv()
                fwd = copy(a, 4 + j, slot(a, px, py, c), (px, py, c), (x, y, 1 - c))
                fwd.start()
                passed.append(fwd)
        for a in range(n):
            copy(a, 0, ins[a], (x, y, 1 - c), (x, y, c)).wait_recv()
            for j, (px, py) in enumerate(chips):
                copy(a, 4 + j, ins[a], (px, py, 1 - c), (x, y, c)).wait_recv()
        for cp in first + passed:
            cp.wait_send()
        for cp in mine:
            cp.wait()

    any_spec = pl.BlockSpec(memory_space=pl.ANY)
    return _pc(body, name=name, out_shape=_comm_out_shapes(arrs, ["gather"] * n), in_specs=[any_spec] * n,
               out_specs=tuple([any_spec] * n), scratch_shapes=_comm_sems(n))(*arrs)


def _comm_out_shapes(arrs, kinds):
    return tuple(jax.ShapeDtypeStruct(a.shape if k == "scatter" else (N_DEV,) + a.shape, a.dtype) for a, k in zip(arrs, kinds))


def _comm_sems(n):
    return [pltpu.SemaphoreType.DMA((n, N_DEV - 1)), pltpu.SemaphoreType.DMA((n, N_DEV - 1)), pltpu.SemaphoreType.DMA((n,))]


def _comm_copies(ins, outs, kinds, send, recv, loc):
    x, y, c, me = _me()
    local, remote = [], []
    for a in range(len(ins)):
        own = ins[a].at[me] if kinds[a] == "scatter" else ins[a]
        local.append(pltpu.make_async_copy(own, outs[a].at[me], loc.at[a]))
        for k in range(1, N_DEV):
            px = jnp.bitwise_xor(x, (k >> 2) & 1)
            py = jnp.bitwise_xor(y, (k >> 1) & 1)
            pc = jnp.bitwise_xor(c, k & 1)
            src = ins[a].at[4 * px + 2 * py + pc] if kinds[a] == "scatter" else ins[a]
            remote.append(pltpu.make_async_remote_copy(src_ref=src, dst_ref=outs[a].at[me], send_sem=send.at[a, k - 1],
                                                       recv_sem=recv.at[a, k - 1], device_id=(px, py, pc), device_id_type=MESH))
    return local, remote


def _comm_start(copies):
    for cp in copies[0] + copies[1]:
        cp.start()


def _comm_wait(copies):
    for cp in copies[1] + copies[0]:
        cp.wait()


def _with_comm(compute, comm, name, grid, args, in_specs, out_shape, out_specs, scratch):
    arrs, kinds = comm if comm else ((), ())
    n, n_in, n_out, n_scr = len(arrs), len(args), len(out_shape), len(scratch)

    def at_step(which):
        ok = pl.program_id(0) == (0 if which == "first" else grid[0] - 1)
        for ax in range(1, len(grid)):
            ok = ok & (pl.program_id(ax) == (0 if which == "first" else grid[ax] - 1))
        return ok

    def body(*refs):
        ins, cin = refs[:n_in], refs[n_in:n_in + n]
        outs, cout = refs[n_in + n:n_in + n + n_out], refs[n_in + n + n_out:n_in + 2 * n + n_out]
        scr, sems = refs[n_in + 2 * n + n_out:n_in + 2 * n + n_out + n_scr], refs[n_in + 2 * n + n_out + n_scr:]
        if n:
            @pl.when(at_step("first"))
            def _():
                _comm_start(_comm_copies(cin, cout, kinds, *sems))
        compute(*ins, *outs, *scr)
        if n:
            @pl.when(at_step("last"))
            def _():
                _comm_wait(_comm_copies(cin, cout, kinds, *sems))

    any_spec = pl.BlockSpec(memory_space=pl.ANY)
    return _pc(body, name=name + ("_comm" if n else ""), grid=tuple(grid),
               out_shape=tuple(out_shape) + (_comm_out_shapes(arrs, kinds) if n else ()),
               in_specs=list(in_specs) + [any_spec] * n, out_specs=tuple(out_specs) + (any_spec,) * n,
               scratch_shapes=list(scratch) + (_comm_sems(n) if n else []),
               compiler_params=_cp(("arbitrary",) * len(grid)))(*args, *arrs)


def _ada_fwd(c_all, w_ada, b_my):
    nl, d, cols = w_ada.shape

    def body(c_ref, w_ref, b_ref, o_ref):
        ca = _silu(c_ref[...])
        o_ref[0] = _dot(ca, w_ref[0]) + b_ref[0]

    return _pc(body, name="ada_fwd", grid=(nl,), out_shape=jax.ShapeDtypeStruct((nl, N_DEV, cols), f32),
               in_specs=[pl.BlockSpec((N_DEV, d), lambda l: (0, 0)), pl.BlockSpec((1, d, cols), lambda l: (l, 0, 0)),
                         pl.BlockSpec((1, 1, cols), lambda l: (l, 0, 0))],
               out_specs=pl.BlockSpec((1, N_DEV, cols), lambda l: (l, 0, 0)), compiler_params=_cp(("arbitrary",)))(c_all, w_ada, b_my)


def _adam_math(w, g, m, v):
    m = ADAM_B1 * m + (1.0 - ADAM_B1) * g
    v = ADAM_B2 * v + (1.0 - ADAM_B2) * (g * g)
    m_hat = m / (1.0 - ADAM_B1 ** ADAM_STEP)
    v_hat = v / (1.0 - ADAM_B2 ** ADAM_STEP)
    delta = -ADAM_LR * (m_hat / (jnp.sqrt(v_hat) + ADAM_EPS) + ADAM_WD * w)
    return delta, m, v


def _ada_bwd_adam(c_all, dmod_my, w, m, v):
    nl, d, cols = w.shape
    tk = 512

    def body(c_ref, dm_ref, w_ref, m_ref, v_ref, g_ref, dl_ref, nm_ref, nv_ref):
        ca = _silu(c_ref[...])
        g = _dot_tn(ca, dm_ref[0])
        dl, nm, nv = _adam_math(w_ref[0], g, m_ref[0], v_ref[0])
        g_ref[0] = g
        dl_ref[0] = dl
        nm_ref[0] = nm
        nv_ref[0] = nv

    wspec = pl.BlockSpec((1, tk, cols), lambda l, k: (l, k, 0))
    shp = jax.ShapeDtypeStruct(w.shape, f32)
    return _pc(body, name="ada_bwd_adam", grid=(nl, d // tk), out_shape=(shp, shp, shp, shp),
               in_specs=[pl.BlockSpec((N_DEV, tk), lambda l, k: (0, k)), pl.BlockSpec((1, N_DEV, cols), lambda l, k: (l, 0, 0)),
                         wspec, wspec, wspec],
               out_specs=(wspec, wspec, wspec, wspec), compiler_params=_cp(("arbitrary", "arbitrary")))(c_all, dmod_my, w, m, v)


def _adam_slabs(slabs, w, m, v, name, tr):
    n, rows, cols = slabs.shape
    tr = min(tr, rows)

    def body(s_ref, w_ref, m_ref, v_ref, g_ref, dl_ref, nm_ref, nv_ref):
        g = s_ref[0].astype(f32)
        for j in range(1, n):
            g = g + s_ref[j].astype(f32)
        dl, nm, nv = _adam_math(w_ref[...], g, m_ref[...], v_ref[...])
        g_ref[...] = g
        dl_ref[...] = dl
        nm_ref[...] = nm
        nv_ref[...] = nv

    spec = pl.BlockSpec((tr, cols), lambda i: (i, 0))
    shp = jax.ShapeDtypeStruct((rows, cols), f32)
    return _pc(body, name=name, grid=(rows // tr,), out_shape=(shp, shp, shp, shp),
               in_specs=[pl.BlockSpec((n, tr, cols), lambda i: (0, i, 0)), spec, spec, spec],
               out_specs=(spec, spec, spec, spec), compiler_params=_cp(("arbitrary",)))(slabs, w, m, v)


def _adam_layers(slabs, w, m, v, name, tr):
    nl, rows, cols = w.shape
    n = slabs[0].shape[0]
    tr = min(tr, rows)

    def body(*refs):
        s_refs, (w_ref, m_ref, v_ref, g_ref, dl_ref, nm_ref, nv_ref) = refs[:nl], refs[nl:]
        for l in range(nl):
            @pl.when(pl.program_id(0) == l)
            def _(l=l):
                g = s_refs[l][0].astype(f32)
                for j in range(1, n):
                    g = g + s_refs[l][j].astype(f32)
                dl, nm, nv = _adam_math(w_ref[0], g, m_ref[0], v_ref[0])
                g_ref[0] = g
                dl_ref[0] = dl
                nm_ref[0] = nm
                nv_ref[0] = nv

    spec = pl.BlockSpec((1, tr, cols), lambda l, i: (l, i, 0))
    s_specs = [pl.BlockSpec((n, tr, cols), lambda l, i, j=j: (0, jnp.where(l == j, i, 0), 0)) for j in range(nl)]
    shp = jax.ShapeDtypeStruct(w.shape, f32)
    return _pc(body, name=name, grid=(nl, rows // tr), out_shape=(shp, shp, shp, shp),
               in_specs=s_specs + [spec, spec, spec], out_specs=(spec, spec, spec, spec),
               compiler_params=_cp(("arbitrary", "arbitrary")))(*slabs, w, m, v)


def _sum_slabs(slabs, name):
    n, rows, cols = slabs.shape
    tr = min(rows, 1024)

    def body(s_ref, o_ref):
        g = s_ref[0]
        for j in range(1, n):
            g = g + s_ref[j]
        o_ref[...] = g

    return _pc(body, name=name, grid=(rows // tr,), out_shape=jax.ShapeDtypeStruct((rows, cols), f32),
               in_specs=[pl.BlockSpec((n, tr, cols), lambda i: (0, i, 0))], out_specs=pl.BlockSpec((tr, cols), lambda i: (i, 0)),
               compiler_params=_cp(("arbitrary",)))(slabs)


NAT_Z, NAT_SMALL, NAT_REST = 3072, 4096, 4112


def _repack_w(g_win):
    n, d, cols = g_win.shape
    tr = 256

    def body(g_ref, wm_ref, ws_ref):
        nat = jnp.concatenate([g_ref[j] for j in range(n)], axis=1)
        wm_ref[...] = jnp.concatenate([nat[:, NAT_Z:NAT_SMALL], nat[:, NAT_REST:], nat[:, :NAT_Z]], axis=1)
        ws_ref[...] = jnp.concatenate([nat[:, NAT_SMALL:NAT_REST], jnp.zeros((tr, LANES - 16), nat.dtype)], axis=1)

    return _pc(body, name="repack_w", grid=(d // tr,),
               out_shape=(jax.ShapeDtypeStruct((d, n * cols - 16), g_win.dtype), jax.ShapeDtypeStruct((d, LANES), g_win.dtype)),
               in_specs=[pl.BlockSpec((n, tr, cols), lambda i: (0, i, 0))],
               out_specs=(pl.BlockSpec((tr, n * cols - 16), lambda i: (i, 0)), pl.BlockSpec((tr, LANES), lambda i: (i, 0))),
               compiler_params=_cp(("arbitrary",)))(g_win)


def _grad_slabs(gwa, gwb, gws, cols):
    d = gwa.shape[0]
    tr = 256

    def body(a_ref, b_ref, s_ref, o_ref):
        a = a_ref[...]
        nat = jnp.concatenate([b_ref[...], a[:, :NAT_SMALL - NAT_Z], s_ref[:, 0:16], a[:, NAT_SMALL - NAT_Z:]], axis=1)
        for j in range(N_DEV):
            o_ref[j] = nat[:, j * cols:(j + 1) * cols].astype(BF)

    return _pc(body, name="grad_slabs", grid=(d // tr,), out_shape=jax.ShapeDtypeStruct((N_DEV, d, cols), BF),
               in_specs=[pl.BlockSpec((tr, gwa.shape[1]), lambda i: (i, 0)), pl.BlockSpec((tr, gwb.shape[1]), lambda i: (i, 0)),
                         pl.BlockSpec((tr, LANES), lambda i: (i, 0))],
               out_specs=pl.BlockSpec((N_DEV, tr, cols), lambda i: (0, i, 0)), compiler_params=_cp(("arbitrary",)))(gwa, gwb, gws)


def _inproj(x, ng, sc1, sh, wmain, wsmall):
    t, d = x.shape
    n = wmain.shape[1]
    tm, tn, rb = min(t, 1024), 1024, 128

    def body(x_ref, ng_ref, sc_ref, sh_ref, wm_ref, ws_ref, p_ref, ps_ref, h_ref, hs_ref):
        @pl.when(pl.program_id(1) == 0)
        def _():
            def rows(i, carry):
                r0 = pl.multiple_of(i * rb, rb)
                xv = x_ref[pl.ds(r0, rb), :]
                r = lax.rsqrt(jnp.mean(xv * xv, axis=1, keepdims=True) + EPS)
                hb = (((xv * r) * ng_ref[...]) * sc_ref[...] + sh_ref[...]).astype(BF)
                hs_ref[pl.ds(r0, rb), :] = hb
                h_ref[pl.ds(r0, rb), :] = hb
                return carry
            lax.fori_loop(0, tm // rb, rows, 0)
            ps_ref[...] = jnp.dot(hs_ref[...], ws_ref[...], preferred_element_type=f32)
        p_ref[...] = jnp.dot(hs_ref[...], wm_ref[...], preferred_element_type=f32).astype(BF)

    vec = pl.BlockSpec((1, d), lambda i, j: (0, 0))
    return _pc(body, name="inproj", grid=(t // tm, n // tn),
               out_shape=(jax.ShapeDtypeStruct((t, n), BF), jax.ShapeDtypeStruct((t, LANES), f32), jax.ShapeDtypeStruct((t, d), BF)),
               in_specs=[pl.BlockSpec((tm, d), lambda i, j: (i, 0)), vec, vec, vec,
                         pl.BlockSpec((d, tn), lambda i, j: (0, j)), pl.BlockSpec((d, LANES), lambda i, j: (0, 0))],
               out_specs=(pl.BlockSpec((tm, tn), lambda i, j: (i, j)), pl.BlockSpec((tm, LANES), lambda i, j: (i, 0)),
                          pl.BlockSpec((tm, d), lambda i, j: (i, 0))),
               scratch_shapes=[pltpu.VMEM((tm, d), BF)], compiler_params=_cp(("arbitrary", "arbitrary")))(x, ng, sc1, sh, wmain, wsmall)


P_Z, P_SGU, P_SGV, P_SGG, P_CVA, P_CVB, P_CVG, P_Q = 0, 1024, 1536, 2048, 2560, 3072, 3584, 4096


def _dn_specs(t, tb, rev):
    nt = t // tb
    ti = (lambda i: nt - 1 - i) if rev else (lambda i: i)
    qb = P_Q // 1024
    hb = tb // QK_HALO
    specs = []
    for s in range(3):
        specs.append(pl.BlockSpec((tb, 1024), lambda i, s=s: (ti(i), qb + s)))
    for s in range(3):
        specs.append(pl.BlockSpec((QK_HALO, 1024), lambda i, s=s: (jnp.maximum(ti(i) * hb - 1, 0), qb + s)))
    specs.append(pl.BlockSpec((DN_CONV, 3072), lambda i: (0, 0)))
    specs.append(pl.BlockSpec((tb, LANES), lambda i: (ti(i), 0)))
    specs.append(pl.BlockSpec((tb // CHUNK, 16, CHUNK), lambda i: (ti(i), 0, 0)))
    specs.append(pl.BlockSpec((1, LANES), lambda i: (0, 0)))
    specs.append(pl.BlockSpec((1, LANES), lambda i: (0, 0)))
    specs.append(pl.BlockSpec((16, CHUNK), lambda i: (0, 0)))
    specs.append(pl.BlockSpec((16, CHUNK), lambda i: (0, 0)))
    return specs, ti


def _dn_conv(first, tiles, halos, cw_ref, xs_ref, cpre_ref, tb):
    for s in range(3):
        c0 = s * 1024
        xs_ref[0:QK_HALO, c0:c0 + 1024] = jnp.where(first, 0.0, halos[s][...].astype(f32))
        xs_ref[QK_HALO:QK_HALO + tb, c0:c0 + 1024] = tiles[s][...].astype(f32)
    for sub in range(3072 // 256):
        cs = slice(sub * 256, (sub + 1) * 256)
        acc = cw_ref[0:1, cs] * xs_ref[pl.ds(QK_HALO - 3, tb), cs]
        for j in range(1, DN_CONV):
            acc = acc + cw_ref[j:j + 1, cs] * xs_ref[pl.ds(QK_HALO - 3 + j, tb), cs]
        cpre_ref[:, cs] = acc


def _dn_head(cpre_ref, r0, h, beta_all, gc_all, gcrow_ref, lane, tri_incl, tri_strict):
    rows = pl.ds(r0, CHUNK)
    cq = cpre_ref[rows, pl.ds(HD * h, HD)]
    ck = cpre_ref[rows, pl.ds(1024 + HD * h, HD)]
    cv = cpre_ref[rows, pl.ds(2048 + HD * h, HD)]
    sq, sk, v = _silu(cq), _silu(ck), _silu(cv)
    rq = lax.rsqrt(jnp.sum(sq * sq, axis=1, keepdims=True) + EPS)
    rk = lax.rsqrt(jnp.sum(sk * sk, axis=1, keepdims=True) + EPS)
    q, k = sq * rq, sk * rk
    beta = _lanecol(beta_all, lane, h)
    gcc = _lanecol(gc_all, lane, 8 + h)
    gcr = gcrow_ref[pl.ds(8 + h, 1), :]
    diff = gcc - gcr
    dmat = jnp.where(tri_incl, jnp.exp(jnp.where(tri_incl, diff, 0.0)), 0.0)
    gam = jnp.exp(gcc)
    glast = gcr[:, CHUNK - 1:CHUNK]
    gam_last = jnp.exp(glast)
    kscale = jnp.exp(glast - gcc)
    qs = q * (HD ** -0.5)
    kb = k * beta
    vb = v * beta
    kbg = kb * gam
    kk = _dot_nt(kb, k)
    lmat = jnp.where(tri_strict, kk * dmat, 0.0)
    pmat = _dot_nt(qs, k) * dmat
    return dict(cq=cq, ck=ck, cv=cv, sq=sq, sk=sk, v=v, rq=rq, rk=rk, q=q, k=k, beta=beta, gcc=gcc, dmat=dmat, gam=gam,
                gam_last=gam_last, kscale=kscale, qs=qs, kb=kb, vb=vb, kbg=kbg, lmat=lmat, pmat=pmat)


def _dn_gates(ps_ref, arow_ref, dtb_ref, pst_ref, acol_ref, dtc_ref):
    ps = ps_ref[...]
    beta_all = _sigmoid(ps)
    g_all = arow_ref[...] * _softplus(ps + dtb_ref[...])
    return ps, beta_all, g_all


def _dn_fwd(p, ps, pst, convw, arow, dtb, acol, dtc, comm=()):
    t = p.shape[0]
    nh = 1024 // HD
    tb = min(t, 256)
    nc = tb // CHUNK
    specs, _ = _dn_specs(t, tb, False)

    def compute(q_ref, k_ref, v_ref, hq_ref, hk_ref, hv_ref, cw_ref, ps_ref, pst_ref, arow_ref, dtb_ref, acol_ref, dtc_ref,
             o_ref, ss_ref, ti_ref, xs_ref, cpre_ref, st_ref, gcrow_ref, ball_ref, gall_ref):
        i = pl.program_id(0)

        @pl.when(i == 0)
        def _():
            st_ref[...] = jnp.zeros_like(st_ref)

        _dn_conv(i == 0, (q_ref, k_ref, v_ref), (hq_ref, hk_ref, hv_ref), cw_ref, xs_ref, cpre_ref, tb)
        _, beta_all, g_all = _dn_gates(ps_ref, arow_ref, dtb_ref, pst_ref, acol_ref, dtc_ref)
        ball_ref[...] = beta_all
        gall_ref[...] = g_all
        ri = lax.broadcasted_iota(jnp.int32, (CHUNK, CHUNK), 0)
        ci = lax.broadcasted_iota(jnp.int32, (CHUNK, CHUNK), 1)
        tri_incl, tri_strict = ri >= ci, ri > ci
        lower = tri_incl.astype(f32)
        upper = (ri <= ci).astype(f32)
        lane = lax.broadcasted_iota(jnp.int32, (CHUNK, LANES), 1)

        def chunk_group(cg, carry):
            hs = range(nh)
            mm = lambda a, b: jnp.dot(a, b, preferred_element_type=f32)
            cs = [cg * DN_GROUP + j for j in range(DN_GROUP)]
            r0s = [pl.multiple_of(c * CHUNK, CHUNK) for c in cs]
            hd = []
            for j in range(DN_GROUP):
                gc_all = _dot_hi(lower, gall_ref[pl.ds(r0s[j], CHUNK), :])
                g_rows = acol_ref[...] * _softplus(pst_ref[cs[j]] + dtc_ref[...])
                gcrow_ref[j] = _dot_hi(g_rows, upper)
                beta_c = ball_ref[pl.ds(r0s[j], CHUNK), :]
                hd += [_dn_head(cpre_ref, r0s[j], h, beta_c, gc_all, gcrow_ref.at[j], lane, tri_incl, tri_strict) for h in hs]
            tinv = _tri_inv([d["lmat"] for d in hd])
            tib = [x.astype(BF) for x in tinv]
            u = [mm(tib[i], hd[i]["vb"].astype(BF)) for i in range(len(hd))]
            w = [mm(tib[i], hd[i]["kbg"].astype(BF)).astype(BF) for i in range(len(hd))]
            qg = [(d["qs"] * d["gam"]).astype(BF) for d in hd]
            kt = [(d["k"] * d["kscale"]).astype(BF) for d in hd]
            pm = [d["pmat"].astype(BF) for d in hd]
            for j in range(DN_GROUP):
                ix = [j * nh + h for h in hs]
                s = [st_ref[h] for h in hs]
                sb = [x.astype(BF) for x in s]
                vnb = [(u[ix[h]] - mm(w[ix[h]], sb[h])).astype(BF) for h in hs]
                o = [mm(qg[ix[h]], sb[h]) + mm(pm[ix[h]], vnb[h]) for h in hs]
                snew = [s[h] * hd[ix[h]]["gam_last"] + lax.dot_general(kt[ix[h]], vnb[h], (((0,), (0,)), ((), ())),
                                                                      preferred_element_type=f32) for h in hs]
                for h in hs:
                    o_ref[pl.ds(r0s[j], CHUNK), pl.ds(HD * h, HD)] = o[h]
                    ss_ref[cs[j], h] = sb[h]
                    ti_ref[cs[j], h] = tinv[ix[h]]
                    st_ref[h] = snew[h]
            return carry

        lax.fori_loop(0, nc // DN_GROUP, chunk_group, 0)

    return _with_comm(
        compute, comm, "dn_fwd", (t // tb,), (p, p, p, p, p, p, convw, ps, pst, arow, dtb, acol, dtc), specs,
        (jax.ShapeDtypeStruct((t, 1024), f32), jax.ShapeDtypeStruct((t // CHUNK, nh, HD, HD), BF),
         jax.ShapeDtypeStruct((t // CHUNK, nh, CHUNK, CHUNK), f32)),
        (pl.BlockSpec((tb, 1024), lambda i: (i, 0)), pl.BlockSpec((nc, nh, HD, HD), lambda i: (i, 0, 0, 0)),
         pl.BlockSpec((nc, nh, CHUNK, CHUNK), lambda i: (i, 0, 0, 0))),
        [pltpu.VMEM((tb + QK_HALO, 3072), f32), pltpu.VMEM((tb, 3072), f32), pltpu.VMEM((nh, HD, HD), f32),
         pltpu.VMEM((DN_GROUP, 16, CHUNK), f32), pltpu.VMEM((tb, LANES), f32), pltpu.VMEM((tb, LANES), f32)])


def _dn_bwd(p, ps, pst, convw, arow, dtb, acol, dtc, do, ss, tinv_all, comm=()):
    t = p.shape[0]
    nh = 1024 // HD
    tb = min(t, 256)
    nc = tb // CHUNK
    nt = t // tb
    specs, ti = _dn_specs(t, tb, True)
    specs = specs + [pl.BlockSpec((tb, 1024), lambda i: (ti(i), 0)), pl.BlockSpec((nc, nh, HD, HD), lambda i: (ti(i), 0, 0, 0)),
                     pl.BlockSpec((nc, nh, CHUNK, CHUNK), lambda i: (ti(i), 0, 0, 0))]

    def compute(q_ref, k_ref, v_ref, hq_ref, hk_ref, hv_ref, cw_ref, ps_ref, pst_ref, arow_ref, dtb_ref, acol_ref, dtc_ref,
             do_ref, ss_ref, ti_ref,
             dp_ref, dps_ref, dcw_ref, dal_ref, ddt_ref,
             xs_ref, cpre_ref, dst_ref, gcrow_ref, ball_ref, gall_ref, dcs_ref, dball_ref, dgcall_ref):
        i = pl.program_id(0)

        @pl.when(i == 0)
        def _():
            dst_ref[...] = jnp.zeros_like(dst_ref)
            dcs_ref[...] = jnp.zeros_like(dcs_ref)
            dcw_ref[...] = jnp.zeros_like(dcw_ref)
            dal_ref[...] = jnp.zeros_like(dal_ref)
            ddt_ref[...] = jnp.zeros_like(ddt_ref)

        _dn_conv(i == nt - 1, (q_ref, k_ref, v_ref), (hq_ref, hk_ref, hv_ref), cw_ref, xs_ref, cpre_ref, tb)
        ps, beta_all, g_all = _dn_gates(ps_ref, arow_ref, dtb_ref, pst_ref, acol_ref, dtc_ref)
        ball_ref[...] = beta_all
        gall_ref[...] = g_all
        ri = lax.broadcasted_iota(jnp.int32, (CHUNK, CHUNK), 0)
        ci = lax.broadcasted_iota(jnp.int32, (CHUNK, CHUNK), 1)
        tri_incl, tri_strict = ri >= ci, ri > ci
        lower = tri_incl.astype(f32)
        upper = (ri <= ci).astype(f32)
        lane = lax.broadcasted_iota(jnp.int32, (CHUNK, LANES), 1)
        row_last = lax.broadcasted_iota(jnp.int32, (CHUNK, 1), 0) == CHUNK - 1

        def chunk_group(cg, carry):
            hs = range(nh)
            bf = lambda a: a.astype(BF)
            mm = lambda a, b: jnp.dot(a, b, preferred_element_type=f32)
            mnt = lambda a, b: lax.dot_general(a, b, (((1,), (1,)), ((), ())), preferred_element_type=f32)
            mtn = lambda a, b: lax.dot_general(a, b, (((0,), (0,)), ((), ())), preferred_element_type=f32)
            rsum = lambda a: jnp.sum(a, axis=1, keepdims=True)
            cs = [nc - 1 - (cg * DN_GROUP + j) for j in range(DN_GROUP)]
            r0s = [pl.multiple_of(c * CHUNK, CHUNK) for c in cs]
            nn = range(DN_GROUP * nh)
            jof = [i // nh for i in nn]
            hof = [i % nh for i in nn]
            hd = []
            for j in range(DN_GROUP):
                rows = pl.ds(r0s[j], CHUNK)
                gc_all = _dot_hi(lower, gall_ref[rows, :])
                g_rows = acol_ref[...] * _softplus(pst_ref[cs[j]] + dtc_ref[...])
                gcrow_ref[j] = _dot_hi(g_rows, upper)
                beta_c = ball_ref[rows, :]
                hd += [_dn_head(cpre_ref, r0s[j], h, beta_c, gc_all, gcrow_ref.at[j], lane, tri_incl, tri_strict) for h in hs]
            tib = [bf(ti_ref[cs[jof[i]], hof[i]]) for i in nn]
            sb = [ss_ref[cs[jof[i]], hof[i]] for i in nn]
            vbb = [bf(d["vb"]) for d in hd]
            kgb = [bf(d["kbg"]) for d in hd]
            u = [mm(tib[i], vbb[i]) for i in nn]
            wb = [bf(mm(tib[i], kgb[i])) for i in nn]
            vnb = [bf(u[i] - mm(wb[i], sb[i])) for i in nn]
            qgb = [bf(d["qs"] * d["gam"]) for d in hd]
            kt = [d["k"] * d["kscale"] for d in hd]
            ktb = [bf(x) for x in kt]
            pmb = [bf(d["pmat"]) for d in hd]
            dob = [bf(do_ref[pl.ds(r0s[jof[i]], CHUNK), pl.ds(HD * hof[i], HD)]) for i in nn]
            dvn0 = [mtn(pmb[i], dob[i]) for i in nn]
            dsn0 = [mtn(qgb[i], dob[i]) for i in nn]
            dpm = [jnp.where(tri_incl, mnt(dob[i], vnb[i]), 0.0) for i in nn]
            dqg = [mnt(dob[i], sb[i]) for i in nn]
            dvnb, dkt, dgl = [], [], []
            for j in range(DN_GROUP):
                ix = [j * nh + h for h in hs]
                dsn = [dst_ref[h] for h in hs]
                dsnb = [bf(x) for x in dsn]
                dvn_j = [bf(dvn0[ix[h]] + mm(ktb[ix[h]], dsnb[h])) for h in hs]
                dsnew = [dsn0[ix[h]] + hd[ix[h]]["gam_last"] * dsn[h] - mtn(wb[ix[h]], dvn_j[h]) for h in hs]
                for h in hs:
                    dst_ref[h] = dsnew[h]
                dkt += [mnt(vnb[ix[h]], dsnb[h]) for h in hs]
                dgl += [jnp.sum(rsum(sb[ix[h]].astype(f32) * dsn[h]), axis=0, keepdims=True) for h in hs]
                dvnb += dvn_j
            dwb = [bf(-mnt(dvnb[i], sb[i])) for i in nn]
            dtm = [mnt(dvnb[i], vbb[i]) + mnt(dwb[i], kgb[i]) for i in nn]
            dvb = [mtn(tib[i], dvnb[i]) for i in nn]
            dkbg = [mtn(tib[i], dwb[i]) for i in nn]
            x1 = [bf(mtn(tib[i], bf(dtm[i]))) for i in nn]
            dl = [jnp.where(tri_strict, -mnt(x1[i], tib[i]), 0.0) for i in nn]
            mmat = [dl[i] * hd[i]["lmat"] + dpm[i] * hd[i]["pmat"] for i in nn]
            dkkb = [bf(dl[i] * hd[i]["dmat"]) for i in nn]
            dqkb = [bf(dpm[i] * hd[i]["dmat"]) for i in nn]
            kb16 = [bf(d["k"]) for d in hd]
            dkb = [mm(dkkb[i], kb16[i]) + dkbg[i] * hd[i]["gam"] for i in nn]
            dk = [mtn(dkkb[i], bf(hd[i]["kb"])) + mtn(dqkb[i], bf(hd[i]["qs"])) + dkt[i] * hd[i]["kscale"] + dkb[i] * hd[i]["beta"]
                  for i in nn]
            dq = [(mm(dqkb[i], kb16[i]) + dqg[i] * hd[i]["gam"]) * (HD ** -0.5) for i in nn]
            dbeta = [rsum(dkb[i] * hd[i]["k"] + dvb[i] * hd[i]["v"]) for i in nn]
            dgam = [rsum(dkbg[i] * hd[i]["kb"] + dqg[i] * hd[i]["qs"]) for i in nn]
            ktdk = [rsum(dkt[i] * kt[i]) for i in nn]
            csum = [rsum(jnp.where(ri == ci, jnp.sum(mmat[i], axis=0, keepdims=True), 0.0)) for i in nn]
            for j in range(DN_GROUP):
                rows = pl.ds(r0s[j], CHUNK)
                dbeta_all = jnp.zeros((CHUNK, LANES), f32)
                dgc_all = jnp.zeros((CHUNK, LANES), f32)
                for h in hs:
                    i = j * nh + h
                    d = hd[i]
                    extra = jnp.sum(ktdk[i], axis=0, keepdims=True) + dgl[i] * d["gam_last"]
                    dgc = rsum(mmat[i]) - csum[i] + dgam[i] * d["gam"] - ktdk[i] + jnp.where(row_last, extra, 0.0)
                    dbeta_all = jnp.where(lane == h, dbeta[i], dbeta_all)
                    dgc_all = jnp.where(lane == 8 + h, dgc, dgc_all)
                    dsq = d["rq"] * (dq[i] - d["q"] * rsum(dq[i] * d["q"]))
                    dsk = d["rk"] * (dk[i] - d["k"] * rsum(dk[i] * d["k"]))
                    dcs_ref[rows, pl.ds(HD * h, HD)] = dsq * _dsilu(d["cq"])
                    dcs_ref[rows, pl.ds(1024 + HD * h, HD)] = dsk * _dsilu(d["ck"])
                    dcs_ref[rows, pl.ds(2048 + HD * h, HD)] = (dvb[i] * d["beta"]) * _dsilu(d["cv"])
                dball_ref[rows, :] = dbeta_all
                dgcall_ref[rows, :] = _dot_hi(upper, dgc_all)
            return carry

        lax.fori_loop(0, nc // DN_GROUP, chunk_group, 0)

        dbeta_t = dball_ref[...]
        dg_t = dgcall_ref[...]
        sg = _sigmoid(ps + dtb_ref[...])
        da = dg_t * arow_ref[...] * sg
        dps_ref[...] = dbeta_t * beta_all * (1.0 - beta_all) + da
        ddt_ref[...] += jnp.sum(da, axis=0, keepdims=True)
        dal_ref[...] += jnp.sum(dg_t * g_all, axis=0, keepdims=True)

        for sub in range(3072 // 256):
            cs = slice(sub * 256, (sub + 1) * 256)
            dc = dcs_ref[0:tb, cs]
            acc = cw_ref[DN_CONV - 1:DN_CONV, cs] * dc
            for j in range(DN_CONV - 1):
                acc = acc + cw_ref[j:j + 1, cs] * dcs_ref[pl.ds(DN_CONV - 1 - j, tb), cs]
            dp_ref[:, cs] = acc.astype(BF)
            for j in range(DN_CONV):
                dcw_ref[j:j + 1, cs] += jnp.sum(dc * xs_ref[pl.ds(QK_HALO - 3 + j, tb), cs], axis=0, keepdims=True)
        dcs_ref[tb:tb + QK_HALO, :] = dcs_ref[0:QK_HALO, :]

    vecl = pl.BlockSpec((1, LANES), lambda i: (0, 0))
    return _with_comm(
        compute, comm, "dn_bwd", (nt,), (p, p, p, p, p, p, convw, ps, pst, arow, dtb, acol, dtc, do, ss, tinv_all), specs,
        (jax.ShapeDtypeStruct((t, 3072), BF), jax.ShapeDtypeStruct((t, LANES), f32), jax.ShapeDtypeStruct((DN_CONV, 3072), f32),
         jax.ShapeDtypeStruct((1, LANES), f32), jax.ShapeDtypeStruct((1, LANES), f32)),
        (pl.BlockSpec((tb, 3072), lambda i: (ti(i), 0)), pl.BlockSpec((tb, LANES), lambda i: (ti(i), 0)),
         pl.BlockSpec((DN_CONV, 3072), lambda i: (0, 0)), vecl, vecl),
        [pltpu.VMEM((tb + QK_HALO, 3072), f32), pltpu.VMEM((tb, 3072), f32), pltpu.VMEM((nh, HD, HD), f32),
         pltpu.VMEM((DN_GROUP, 16, CHUNK), f32), pltpu.VMEM((tb, LANES), f32), pltpu.VMEM((tb, LANES), f32),
         pltpu.VMEM((tb + QK_HALO, 3072), f32), pltpu.VMEM((tb, LANES), f32), pltpu.VMEM((tb, LANES), f32)])


def _ln_fwd(x, g, b):
    mu = jnp.mean(x, axis=1, keepdims=True)
    xc = x - mu
    rstd = lax.rsqrt(jnp.mean(xc * xc, axis=1, keepdims=True) + LN_EPS)
    xhat = xc * rstd
    return xhat * g + b, xhat, rstd


def _ln_bwd(dy, xhat, rstd, g):
    dxh = dy * g
    return rstd * (dxh - jnp.mean(dxh, axis=1, keepdims=True) - xhat * jnp.mean(dxh * xhat, axis=1, keepdims=True))


def _br_specs(t, tb, rev):
    nt = t // tb
    ti = (lambda i: nt - 1 - i) if rev else (lambda i: i)
    hb = tb // CV_HALO
    col = lambda off: off // 512
    specs = [pl.BlockSpec((tb, 1024), lambda i: (ti(i), 0)),
             pl.BlockSpec((tb, 1024), lambda i: (ti(i), 0))]
    for off in (P_SGU, P_SGV, P_SGG, P_CVA, P_CVB, P_CVG):
        specs.append(pl.BlockSpec((tb, 512), lambda i, off=off: (ti(i), col(off))))
    for off in (P_CVA, P_CVB):
        specs.append(pl.BlockSpec((CV_HALO, 512), lambda i, off=off: (jnp.maximum(ti(i) * hb - 1, 0), col(off))))
    v512 = pl.BlockSpec((1, 512), lambda i: (0, 0))
    specs += [pl.BlockSpec((1, HD), lambda i: (0, 0)), v512, v512,
              pl.BlockSpec((SG_GROUPS, SG_BLOCK, SG_BLOCK), lambda i: (0, 0, 0)),
              pl.BlockSpec((SG_BLOCK, 512), lambda i: (0, 0)),
              pl.BlockSpec((CV_HALO, 512), lambda i: (0, 0)),
              v512, v512, v512]
    return specs, ti


def _sg_mask():
    r = lax.broadcasted_iota(jnp.int32, (SG_BLOCK, SG_BLOCK), 0) // CHUNK
    c = lax.broadcasted_iota(jnp.int32, (SG_BLOCK, SG_BLOCK), 1) // CHUNK
    return r >= c


def _cv_glu(first, a_ref, b_ref, ha_ref, hb_ref, gs_ref, tb):
    ha = jnp.where(first, 0.0, ha_ref[...].astype(f32))
    gs_ref[0:CV_HALO, :] = ha * _sigmoid(hb_ref[...].astype(f32))
    gs_ref[CV_HALO:CV_HALO + tb, :] = a_ref[...].astype(f32) * _sigmoid(b_ref[...].astype(f32))


SUBLANES = 8
CV_SHIFT_ROWS = CV_HALO - SUBLANES


def _cv_shift(src_ref, sh_ref, tb, cs):
    for b in range(1, SUBLANES):
        sh_ref[b - 1, 0:tb + CV_SHIFT_ROWS, cs] = src_ref[pl.ds(b, tb + CV_SHIFT_ROWS), cs]


def _cv_tap(src_ref, sh_ref, off, tb, cs):
    a, b = divmod(off, SUBLANES)
    if b == 0:
        return src_ref[pl.ds(SUBLANES * a, tb), cs]
    return sh_ref[b - 1, pl.ds(SUBLANES * a, tb), cs]


def _cv_conv(gs_ref, gsh_ref, cw_ref, cb_ref, tb, cs):
    acc = cb_ref[:, cs] + cw_ref[0:1, cs] * _cv_tap(gs_ref, gsh_ref, CV_HALO - (CV_K - 1), tb, cs)
    for j in range(1, CV_K):
        acc = acc + cw_ref[j:j + 1, cs] * _cv_tap(gs_ref, gsh_ref, CV_HALO - (CV_K - 1) + j, tb, cs)
    return acc


def _branch_fwd(o, p, dng, slg, slb, sgw, sgbias, cvw, cvb, clg, clb):
    t = o.shape[0]
    tb = min(t, 256)
    specs, _ = _br_specs(t, tb, False)

    def body(o_ref, z_ref, su_ref, sv_ref, sgt_ref, ca_ref, cb_ref, cg_ref, ha_ref, hb_ref,
             dng_ref, slg_ref, slb_ref, sgw_ref, sgb_ref, cvw_ref, cvb_ref, clg_ref, clb_ref, y_ref, gs_ref, vln_ref, gsh_ref):
        i = pl.program_id(0)
        for h in range(1024 // HD):
            cs = slice(h * HD, (h + 1) * HD)
            oh = o_ref[:, cs]
            r = lax.rsqrt(jnp.mean(oh * oh, axis=1, keepdims=True) + EPS)
            y_ref[:, cs] = ((oh * r) * dng_ref[...] * _silu(z_ref[:, cs].astype(f32))).astype(BF)
        mask = _sg_mask()
        for g in range(SG_GROUPS):
            cs = slice(g * HD, (g + 1) * HD)
            vg = _gelu(sv_ref[:, cs].astype(f32))
            vln, _, _ = _ln_fwd(vg, slg_ref[:, cs], slb_ref[:, cs])
            vln_ref[...] = vln
            ws = jnp.where(mask, sgw_ref[g], 0.0)
            for nb in range(tb // SG_BLOCK):
                rs = slice(nb * SG_BLOCK, (nb + 1) * SG_BLOCK)
                mixed = _dot(ws, vln_ref[rs, :]) + sgb_ref[:, cs]
                u = _gelu(su_ref[rs, cs].astype(f32))
                y_ref[rs, 1024 + g * HD:1024 + (g + 1) * HD] = (u * mixed * _silu(sgt_ref[rs, cs].astype(f32))).astype(BF)
        _cv_glu(i == 0, ca_ref, cb_ref, ha_ref, hb_ref, gs_ref, tb)
        for g in range(CV_GROUPS):
            cs = slice(g * HD, (g + 1) * HD)
            _cv_shift(gs_ref, gsh_ref, tb, cs)
            dw = _cv_conv(gs_ref, gsh_ref, cvw_ref, cvb_ref, tb, cs)
            ln, _, _ = _ln_fwd(dw, clg_ref[:, cs], clb_ref[:, cs])
            y_ref[:, 1536 + g * HD:1536 + (g + 1) * HD] = (_silu(ln) * _silu(cg_ref[:, cs].astype(f32))).astype(BF)

    return _pc(body, name="branch_fwd", grid=(t // tb,), out_shape=jax.ShapeDtypeStruct((t, 2048), BF),
               in_specs=specs, out_specs=pl.BlockSpec((tb, 2048), lambda i: (i, 0)),
               scratch_shapes=[pltpu.VMEM((tb + CV_HALO, 512), f32), pltpu.VMEM((tb, HD), f32),
                               pltpu.VMEM((SUBLANES - 1, tb + CV_SHIFT_ROWS, 512), f32)],
               compiler_params=_cp(("arbitrary",)))(o, p, p, p, p, p, p, p, p, p, dng, slg, slb, sgw, sgbias, cvw, cvb, clg, clb)


def _branch_bwd(o, p, dng, slg, slb, sgw, sgbias, cvw, cvb, clg, clb, dy):
    t = o.shape[0]
    tb = min(t, 256)
    nt = t // tb
    specs, ti = _br_specs(t, tb, True)
    specs = specs + [pl.BlockSpec((tb, 2048), lambda i: (ti(i), 0))]

    def body(o_ref, z_ref, su_ref, sv_ref, sgt_ref, ca_ref, cb_ref, cg_ref, ha_ref, hb_ref,
             dng_ref, slg_ref, slb_ref, sgw_ref, sgb_ref, cvw_ref, cvb_ref, clg_ref, clb_ref, dy_ref,
             dp_ref, do_ref, ddng_ref, dslg_ref, dslb_ref, dsgw_ref, dsgb_ref, dcvw_ref, dcvb_ref, dclg_ref, dclb_ref,
             gs_ref, vln_ref, xh_ref, dvl_ref, ddw_ref, dbias_ref, gsh_ref, dsh_ref):
        i = pl.program_id(0)

        @pl.when(i == 0)
        def _():
            for r in (ddng_ref, dslg_ref, dslb_ref, dsgw_ref, dsgb_ref, dcvw_ref, dcvb_ref, dclg_ref, dclb_ref, ddw_ref, dbias_ref):
                r[...] = jnp.zeros_like(r)

        for h in range(1024 // HD):
            cs = slice(h * HD, (h + 1) * HD)
            oh = o_ref[:, cs]
            zz = z_ref[:, cs].astype(f32)
            dyh = dy_ref[:, cs].astype(f32)
            r = lax.rsqrt(jnp.mean(oh * oh, axis=1, keepdims=True) + EPS)
            nrm = oh * r
            sz = _silu(zz)
            dn = dyh * dng_ref[...] * sz
            ddng_ref[...] += jnp.sum(dyh * nrm * sz, axis=0, keepdims=True)
            dp_ref[:, cs] = (dyh * nrm * dng_ref[...] * _dsilu(zz)).astype(BF)
            do_ref[:, cs] = r * (dn - nrm * jnp.mean(dn * nrm, axis=1, keepdims=True))
        mask = _sg_mask()
        for g in range(SG_GROUPS):
            cs = slice(g * HD, (g + 1) * HD)
            sv = sv_ref[:, cs].astype(f32)
            vln, xhat, rstd = _ln_fwd(_gelu(sv), slg_ref[:, cs], slb_ref[:, cs])
            vln_ref[...] = vln
            xh_ref[...] = xhat
            ws = jnp.where(mask, sgw_ref[g], 0.0)
            dws = jnp.zeros((SG_BLOCK, SG_BLOCK), f32)
            for nb in range(tb // SG_BLOCK):
                rs = slice(nb * SG_BLOCK, (nb + 1) * SG_BLOCK)
                dyb = dy_ref[rs, 1024 + g * HD:1024 + (g + 1) * HD].astype(f32)
                su = su_ref[rs, cs].astype(f32)
                gt = sgt_ref[rs, cs].astype(f32)
                mixed = _dot(ws, vln_ref[rs, :]) + sgb_ref[:, cs]
                u = _gelu(su)
                sgt = _silu(gt)
                dmixed = dyb * u * sgt
                dp_ref[rs, P_SGU + g * HD:P_SGU + (g + 1) * HD] = (dyb * mixed * sgt * _dgelu(su)).astype(BF)
                dp_ref[rs, P_SGG + g * HD:P_SGG + (g + 1) * HD] = (dyb * u * mixed * _dsilu(gt)).astype(BF)
                dvl_ref[rs, :] = _dot_tn(ws, dmixed)
                dws = dws + _dot_nt(dmixed, vln_ref[rs, :])
                dbias_ref[:, cs] += dmixed
            dsgw_ref[g] += jnp.where(mask, dws, 0.0)
            dvl = dvl_ref[...]
            xhat = xh_ref[...]
            dslg_ref[:, cs] += jnp.sum(dvl * xhat, axis=0, keepdims=True)
            dslb_ref[:, cs] += jnp.sum(dvl, axis=0, keepdims=True)
            dvg = _ln_bwd(dvl, xhat, rstd, slg_ref[:, cs])
            dp_ref[:, P_SGV + g * HD:P_SGV + (g + 1) * HD] = (dvg * _dgelu(sv)).astype(BF)
        _cv_glu(i == nt - 1, ca_ref, cb_ref, ha_ref, hb_ref, gs_ref, tb)
        for g in range(CV_GROUPS):
            cs = slice(g * HD, (g + 1) * HD)
            dyc = dy_ref[:, 1536 + g * HD:1536 + (g + 1) * HD].astype(f32)
            cg = cg_ref[:, cs].astype(f32)
            _cv_shift(gs_ref, gsh_ref, tb, cs)
            dw = _cv_conv(gs_ref, gsh_ref, cvw_ref, cvb_ref, tb, cs)
            ln, xhat, rstd = _ln_fwd(dw, clg_ref[:, cs], clb_ref[:, cs])
            dln = dyc * _silu(cg) * _dsilu(ln)
            dp_ref[:, P_CVG + g * HD:P_CVG + (g + 1) * HD] = (dyc * _silu(ln) * _dsilu(cg)).astype(BF)
            dclg_ref[:, cs] += jnp.sum(dln * xhat, axis=0, keepdims=True)
            dclb_ref[:, cs] += jnp.sum(dln, axis=0, keepdims=True)
            ddw = _ln_bwd(dln, xhat, rstd, clg_ref[:, cs])
            dcvb_ref[:, cs] += jnp.sum(ddw, axis=0, keepdims=True)
            ddw_ref[0:tb, cs] = ddw
            _cv_shift(ddw_ref, dsh_ref, tb, cs)
            dglu = cvw_ref[CV_K - 1:CV_K, cs] * ddw
            for j in range(CV_K - 1):
                dglu = dglu + cvw_ref[j:j + 1, cs] * _cv_tap(ddw_ref, dsh_ref, CV_K - 1 - j, tb, cs)
            for j in range(CV_K):
                tap = _cv_tap(gs_ref, gsh_ref, CV_HALO - (CV_K - 1) + j, tb, cs)
                dcvw_ref[j:j + 1, cs] += jnp.sum(ddw * tap, axis=0, keepdims=True)
            a = ca_ref[:, cs].astype(f32)
            sb = _sigmoid(cb_ref[:, cs].astype(f32))
            dp_ref[:, P_CVA + g * HD:P_CVA + (g + 1) * HD] = (dglu * sb).astype(BF)
            dp_ref[:, P_CVB + g * HD:P_CVB + (g + 1) * HD] = (dglu * a * sb * (1.0 - sb)).astype(BF)
        ddw_ref[tb:tb + CV_HALO, :] = ddw_ref[0:CV_HALO, :]

        @pl.when(i == nt - 1)
        def _():
            lane = lax.broadcasted_iota(jnp.int32, (SG_BLOCK, LANES), 1)
            acc = jnp.zeros((SG_BLOCK, LANES), f32)
            for g in range(SG_GROUPS):
                acc = jnp.where(lane == g, jnp.sum(dbias_ref[:, g * HD:(g + 1) * HD], axis=1, keepdims=True), acc)
            dsgb_ref[...] = acc

    v512 = pl.BlockSpec((1, 512), lambda i: (0, 0))
    return _pc(body, name="branch_bwd", grid=(nt,),
               out_shape=(jax.ShapeDtypeStruct((t, 4096), BF), jax.ShapeDtypeStruct((t, 1024), f32),
                          jax.ShapeDtypeStruct((1, HD), f32), jax.ShapeDtypeStruct((1, 512), f32), jax.ShapeDtypeStruct((1, 512), f32),
                          jax.ShapeDtypeStruct((SG_GROUPS, SG_BLOCK, SG_BLOCK), f32), jax.ShapeDtypeStruct((SG_BLOCK, LANES), f32),
                          jax.ShapeDtypeStruct((CV_HALO, 512), f32), jax.ShapeDtypeStruct((1, 512), f32),
                          jax.ShapeDtypeStruct((1, 512), f32), jax.ShapeDtypeStruct((1, 512), f32)),
               in_specs=specs,
               out_specs=(pl.BlockSpec((tb, 4096), lambda i: (ti(i), 0)), pl.BlockSpec((tb, 1024), lambda i: (ti(i), 0)),
                          pl.BlockSpec((1, HD), lambda i: (0, 0)), v512, v512,
                          pl.BlockSpec((SG_GROUPS, SG_BLOCK, SG_BLOCK), lambda i: (0, 0, 0)),
                          pl.BlockSpec((SG_BLOCK, LANES), lambda i: (0, 0)), pl.BlockSpec((CV_HALO, 512), lambda i: (0, 0)),
                          v512, v512, v512),
               scratch_shapes=[pltpu.VMEM((tb + CV_HALO, 512), f32), pltpu.VMEM((tb, HD), f32), pltpu.VMEM((tb, HD), f32),
                               pltpu.VMEM((tb, HD), f32), pltpu.VMEM((tb + CV_HALO, 512), f32), pltpu.VMEM((SG_BLOCK, 512), f32),
                               pltpu.VMEM((SUBLANES - 1, tb + CV_SHIFT_ROWS, 512), f32),
                               pltpu.VMEM((SUBLANES - 1, tb + CV_SHIFT_ROWS, 512), f32)],
               compiler_params=_cp(("arbitrary",)))(o, p, p, p, p, p, p, p, p, p, dng, slg, slb, sgw, sgbias, cvw, cvb, clg, clb, dy)


def _outproj(ycat, w, x, gate):
    t, d = x.shape
    tm = min(t, 512)

    def body(y_ref, w_ref, x_ref, g_ref, o_ref):
        o_ref[...] = x_ref[...] + g_ref[...] * jnp.dot(y_ref[...], w_ref[...], preferred_element_type=f32)

    return _pc(body, name="outproj", grid=(t // tm,), out_shape=jax.ShapeDtypeStruct((t, d), f32),
               in_specs=[pl.BlockSpec((tm, ycat.shape[1]), lambda i: (i, 0)), pl.BlockSpec(w.shape, lambda i: (0, 0)),
                         pl.BlockSpec((tm, d), lambda i: (i, 0)), pl.BlockSpec((1, d), lambda i: (0, 0))],
               out_specs=pl.BlockSpec((tm, d), lambda i: (i, 0)), compiler_params=_cp(("arbitrary",)))(ycat, w, x, gate)


def _outproj_bwd(dxo, w, gate):
    t, d = dxo.shape
    tm = min(t, 512)

    def body(d_ref, w_ref, g_ref, o_ref):
        o_ref[...] = _dot_nt(d_ref[...] * g_ref[...], w_ref[...]).astype(BF)

    return _pc(body, name="outproj_bwd", grid=(t // tm,), out_shape=jax.ShapeDtypeStruct((t, w.shape[0]), BF),
               in_specs=[pl.BlockSpec((tm, d), lambda i: (i, 0)), pl.BlockSpec(w.shape, lambda i: (0, 0)),
                         pl.BlockSpec((1, d), lambda i: (0, 0))],
               out_specs=pl.BlockSpec((tm, w.shape[0]), lambda i: (i, 0)), compiler_params=_cp(("arbitrary",)))(dxo, w, gate)


def _tn_acc(a, b, name):
    kk, m = a.shape
    n = b.shape[1]
    tk, tn = min(kk, 1024), min(n, 1024)

    def body(a_ref, b_ref, o_ref):
        @pl.when(pl.program_id(1) == 0)
        def _():
            o_ref[...] = jnp.zeros_like(o_ref)
        o_ref[...] += _dot_tn(a_ref[...], b_ref[...])

    return _pc(body, name=name, grid=(n // tn, kk // tk), out_shape=jax.ShapeDtypeStruct((m, n), f32),
               in_specs=[pl.BlockSpec((tk, m), lambda j, k: (k, 0)), pl.BlockSpec((tk, tn), lambda j, k: (k, j))],
               out_specs=pl.BlockSpec((m, tn), lambda j, k: (0, j)), compiler_params=_cp(("arbitrary", "arbitrary")))(a, b)


def _wout_grad(gmat, w, gate):
    m, n = gmat.shape
    tr = m // N_DEV

    def body(g_ref, w_ref, gt_ref, o_ref, dg_ref):
        @pl.when(pl.program_id(0) == 0)
        def _():
            dg_ref[...] = jnp.zeros_like(dg_ref)
        gm = g_ref[...]
        o_ref[0] = (gm * gt_ref[...]).astype(BF)
        dg_ref[...] += jnp.sum(gm * w_ref[...].astype(f32), axis=0, keepdims=True)

    return _pc(body, name="wout_grad", grid=(N_DEV,),
               out_shape=(jax.ShapeDtypeStruct((N_DEV, tr, n), BF), jax.ShapeDtypeStruct((1, n), f32)),
               in_specs=[pl.BlockSpec((tr, n), lambda i: (i, 0)), pl.BlockSpec((tr, n), lambda i: (i, 0)),
                         pl.BlockSpec((1, n), lambda i: (0, 0))],
               out_specs=(pl.BlockSpec((1, tr, n), lambda i: (i, 0, 0)), pl.BlockSpec((1, n), lambda i: (0, 0))),
               compiler_params=_cp(("arbitrary",)))(gmat, w, gate)


def _inproj_bwd(dpa, dpb, dps, wmain, wsmall, x, ng, sc1, dxo, comm=()):
    t, d = x.shape
    na, nb = dpa.shape[1] // 1024, dpb.shape[1] // 1024
    nk = na + nb
    tm, rb = min(t, 512), 128

    def body(a_ref, b_ref, s_ref, wm_ref, ws_ref, x_ref, ng_ref, sc_ref, dxo_ref, dx_ref, dsh_ref, dsc_ref, dng_ref, acc_ref):
        i, k = pl.program_id(0), pl.program_id(1)

        @pl.when((i == 0) & (k == 0))
        def _():
            dsh_ref[...] = jnp.zeros_like(dsh_ref)
            dsc_ref[...] = jnp.zeros_like(dsc_ref)
            dng_ref[...] = jnp.zeros_like(dng_ref)

        @pl.when(k == 0)
        def _():
            acc_ref[...] = _dot_nt(s_ref[...], ws_ref[...])

        @pl.when(k < na)
        def _():
            acc_ref[...] += _dot_nt(a_ref[...], wm_ref[...])

        @pl.when(k >= na)
        def _():
            acc_ref[...] += _dot_nt(b_ref[...], wm_ref[...])

        @pl.when(k == nk - 1)
        def _():
            def rows(j, carry):
                r0 = pl.multiple_of(j * rb, rb)
                rr = pl.ds(r0, rb)
                xv = x_ref[rr, :]
                dh = acc_ref[rr, :]
                r = lax.rsqrt(jnp.mean(xv * xv, axis=1, keepdims=True) + EPS)
                xn = xv * r
                dsh_ref[...] += jnp.sum(dh, axis=0, keepdims=True)
                dsc_ref[...] += jnp.sum(dh * (xn * ng_ref[...]), axis=0, keepdims=True)
                dng_ref[...] += jnp.sum(dh * sc_ref[...] * xn, axis=0, keepdims=True)
                dxn = dh * (ng_ref[...] * sc_ref[...])
                dx_ref[rr, :] = r * (dxn - xn * jnp.mean(dxn * xn, axis=1, keepdims=True)) + dxo_ref[rr, :]
                return carry
            lax.fori_loop(0, tm // rb, rows, 0)

    vec = pl.BlockSpec((1, d), lambda i, k: (0, 0))
    row = pl.BlockSpec((tm, d), lambda i, k: (i, 0))
    shp = jax.ShapeDtypeStruct((1, d), f32)
    return _with_comm(
        body, comm, "inproj_bwd", (t // tm, nk), (dpa, dpb, dps, wmain, wsmall, x, ng, sc1, dxo),
        [pl.BlockSpec((tm, 1024), lambda i, k: (i, jnp.minimum(k, na - 1))),
         pl.BlockSpec((tm, 1024), lambda i, k: (i, jnp.clip(k - na, 0, nb - 1))),
         pl.BlockSpec((tm, LANES), lambda i, k: (i, 0)),
         pl.BlockSpec((d, 1024), lambda i, k: (0, k)), pl.BlockSpec((d, LANES), lambda i, k: (0, 0)), row, vec, vec, row],
        (jax.ShapeDtypeStruct((t, d), f32), shp, shp, shp), (row, vec, vec, vec), [pltpu.VMEM((tm, d), f32)])


def _loss_head(x, fg, tgt):
    t, d = x.shape
    tm = min(t, 512)

    def body(x_ref, g_ref, t_ref, dx_ref, l_ref, dg_ref):
        @pl.when(pl.program_id(0) == 0)
        def _():
            l_ref[...] = jnp.zeros_like(l_ref)
            dg_ref[...] = jnp.zeros_like(dg_ref)
        xv = x_ref[...]
        r = lax.rsqrt(jnp.mean(xv * xv, axis=1, keepdims=True) + EPS)
        xn = xv * r
        err = xn * g_ref[...] - t_ref[...]
        l_ref[...] += 0.5 * jnp.sum(jnp.mean(err * err, axis=1, keepdims=True), axis=0, keepdims=True)
        dy = err * (1.0 / d)
        dg_ref[...] += jnp.sum(dy * xn, axis=0, keepdims=True)
        dxn = dy * g_ref[...]
        dx_ref[...] = r * (dxn - xn * jnp.mean(dxn * xn, axis=1, keepdims=True))

    row = pl.BlockSpec((tm, d), lambda i: (i, 0))
    vec = pl.BlockSpec((1, d), lambda i: (0, 0))
    return _pc(body, name="loss_head", grid=(t // tm,),
               out_shape=(jax.ShapeDtypeStruct((t, d), f32), jax.ShapeDtypeStruct((1, 1), f32), jax.ShapeDtypeStruct((1, d), f32)),
               in_specs=[row, vec, row], out_specs=(row, pl.BlockSpec((1, 1), lambda i: (0, 0)), vec),
               compiler_params=_cp(("arbitrary",)))(x, fg, tgt)


def _pack(arrs, rows_mult=8):
    flat = jnp.concatenate([a.reshape(-1).astype(f32) for a in arrs])
    n = flat.shape[0]
    per = rows_mult * LANES
    pad = (-n) % per
    if pad:
        flat = jnp.concatenate([flat, jnp.zeros((pad,), f32)])
    return flat.reshape(-1, LANES)


def _unpack(packed, shapes, lead=()):
    flat = packed.reshape(lead + (-1,))
    out, off = [], 0
    for s in shapes:
        n = 1
        for v in s:
            n *= v
        out.append(flat[..., off:off + n].reshape(lead + tuple(s)))
        off += n
    return out


def _pad_lanes(v, at):
    return jnp.pad(v.astype(f32), (at, LANES - at - v.shape[0])).reshape(1, LANES)


def kernel(x, c, norm_g, w_ada, b_ada, w_in, conv_qkv, a_log, dt_bias, dn_norm_g, sg_ln_g, sg_ln_b, sg_w, sg_b, cv_w, cv_b, cv_ln_g, cv_ln_b, w_out, final_g, loss_target, m_norm_g, m_w_ada, m_b_ada, m_w_in, m_conv_qkv, m_a_log, m_dt_bias, m_dn_norm_g, m_sg_ln_g, m_sg_ln_b, m_sg_w, m_sg_b, m_cv_w, m_cv_b, m_cv_ln_g, m_cv_ln_b, m_w_out, m_final_g, v_norm_g, v_w_ada, v_b_ada, v_w_in, v_conv_qkv, v_a_log, v_dt_bias, v_dn_norm_g, v_sg_ln_g, v_sg_ln_b, v_sg_w, v_sg_b, v_cv_w, v_cv_b, v_cv_ln_g, v_cv_ln_b, v_w_out, v_final_g):
    nl, d = norm_g.shape
    t = x.shape[1]
    nh = a_log.shape[1]
    xi, yi, ci = lax.axis_index("x"), lax.axis_index("y"), lax.axis_index("c")
    me = 4 * xi + 2 * yi + ci
    x0 = x[0]
    tgt = loss_target[0]
    ada_cols = w_ada.shape[2]
    in_cols = w_in.shape[2]
    cq_cols = conv_qkv.shape[2]
    cvw_cols = cv_w.shape[2]

    start_shapes = [c.shape, conv_qkv.shape, cv_w.shape]
    start_pack = _pack([c, conv_qkv, cv_w])
    wi_b, wo_b = w_in.astype(BF), w_out.astype(BF)
    g_small, g_win0 = _gather_two_level([start_pack, wi_b[0]], "gather_start")
    c_all, cq_g, cvw_g = _unpack(g_small, start_shapes, lead=(N_DEV,))
    c_all = c_all.reshape(N_DEV, d)
    conv_full = jnp.moveaxis(cq_g, 0, 2).reshape(nl, DN_CONV, N_DEV * cq_cols)
    cvw_full = jnp.moveaxis(cvw_g, 0, 2).reshape(nl, CV_K, N_DEV * cvw_cols)
    w_packed = [_repack_w(g_win0)]
    w_out_full = []

    b_my = lax.dynamic_slice_in_dim(b_ada, me * ada_cols, ada_cols, axis=1).reshape(nl, 1, ada_cols)
    mod_part = _ada_fwd(c_all, w_ada, b_my)
    (mod_g,) = _exchange([mod_part], ["gather"], "gather_mod")
    mod_mine = lax.dynamic_index_in_dim(mod_g, me, axis=2, keepdims=False)
    mod = jnp.moveaxis(mod_mine, 0, 1).reshape(nl, N_DEV * ada_cols)
    shift, scale, gate = mod[:, :d], mod[:, d:2 * d], mod[:, 2 * d:]

    arow = [_pad_lanes(-jnp.exp(a_log[l]), 8) for l in range(nl)]
    dtb = [_pad_lanes(dt_bias[l], 8) for l in range(nl)]
    saved = []
    xl = x0
    for l in range(nl):
        sc1 = (1.0 + scale[l]).reshape(1, d)
        w_main_l, w_small_l = w_packed[l]
        p, ps, hb = _inproj(xl, norm_g[l].reshape(1, d), sc1, shift[l].reshape(1, d), w_main_l, w_small_l)
        pst = ps[:, :16].reshape(t // CHUNK, CHUNK, 16).transpose(0, 2, 1)
        acol = jnp.broadcast_to(arow[l][0, :16].reshape(16, 1), (16, CHUNK))
        dtc = jnp.broadcast_to(dtb[l][0, :16].reshape(16, 1), (16, CHUNK))
        ahead = (wo_b[l], wi_b[l + 1]) if l + 1 < nl else (wo_b[l],)
        o, ss, tinv, g_wout, *g_win_n = _dn_fwd(p, ps, pst, conv_full[l], arow[l], dtb[l], acol, dtc,
                                                comm=(ahead, ("gather",) * len(ahead)))
        w_out_l = g_wout.reshape(N_DEV * w_out.shape[1], w_out.shape[2])
        w_out_full.append(w_out_l)
        if g_win_n:
            w_packed.append(_repack_w(g_win_n[0]))
        sgbias = jnp.repeat(sg_b[l].T, HD, axis=1)
        cvw_pad = jnp.pad(cvw_full[l], ((0, CV_HALO - CV_K), (0, 0)))
        br_par = (dn_norm_g[l].reshape(1, HD), sg_ln_g[l].reshape(1, -1), sg_ln_b[l].reshape(1, -1), sg_w[l], sgbias, cvw_pad,
                  cv_b[l].reshape(1, -1), cv_ln_g[l].reshape(1, -1), cv_ln_b[l].reshape(1, -1))
        ycat = _branch_fwd(o, p, *br_par)
        xn = _outproj(ycat, w_out_l, xl, gate[l].reshape(1, d))
        saved.append((xl, p, ps, pst, hb, o, ss, tinv, ycat, sc1, acol, dtc, br_par))
        xl = xn

    dx, loss_part, dfinal_g = _loss_head(xl, final_g.reshape(1, d), tgt)
    loss = lax.psum(loss_part[0, 0], ("x", "y", "c"))

    small_grads = [None] * nl
    dmods = [None] * nl
    win_r = [None] * nl
    wout_r = [None] * nl
    rep_names = ["b_ada", "norm_g", "a_log", "dt_bias", "dn_norm_g", "sg_ln_g", "sg_ln_b", "sg_w", "sg_b", "cv_b", "cv_ln_g",
                 "cv_ln_b", "final_g"]
    win_slab = None
    for l in reversed(range(nl)):
        xl, p, ps, pst, hb, o, ss, tinv, ycat, sc1, acol, dtc, br_par = saved[l]
        w_main_l, w_small_l = w_packed[l]
        w_out_l = w_out_full[l]
        gate_l = gate[l].reshape(1, d)
        dycat = _outproj_bwd(dx, w_out_l, gate_l)
        gmat = _tn_acc(ycat, dx, "wout_tn")
        wout_slab, dgate = _wout_grad(gmat, w_out_l, gate_l)
        dpa, do, ddng, dslg, dslb, dsgw, dsgb, dcvw, dcvb, dclg, dclb = _branch_bwd(o, p, *br_par, dycat)
        behind = (wout_slab,) if win_slab is None else (wout_slab, win_slab)
        dpb, dps, dconvw, dal, ddt, wout_r[l], *win_got = _dn_bwd(
            p, ps, pst, conv_full[l], arow[l], dtb[l], acol, dtc, do, ss, tinv, comm=(behind, ("scatter",) * len(behind)))
        if win_got:
            win_r[l + 1] = win_got[0]
        gwa = _tn_acc(hb, dpa, "win_tn_a")
        gwb = _tn_acc(hb, dpb, "win_tn_b")
        gws = _tn_acc(hb, dps, "win_tn_s")
        win_slab = _grad_slabs(gwa, gwb, gws, in_cols)
        small_grads[l] = [None, dal[0, 8:8 + nh], ddt[0, 8:8 + nh], ddng[0], dslg[0], dslb[0], dsgw, dsgb[:, :SG_GROUPS].T,
                          dcvb[0], dclg[0], dclb[0], dconvw, dcvw[:CV_K]]
        ib_args = (dpa, dpb, dps, w_main_l, w_small_l, xl, norm_g[l].reshape(1, d), sc1, dx)
        if l > 0:
            dx, dshift, dscale, dng = _inproj_bwd(*ib_args)
            dmods[l] = jnp.concatenate([dshift[0], dscale[0], dgate[0]])
            small_grads[l][0] = dng[0]
        else:
            zeros_d = jnp.zeros((d,), f32)
            dmods[0] = jnp.concatenate([zeros_d, zeros_d, dgate[0]])
            small_grads[0][0] = zeros_d
            stack = lambda j: jnp.stack([small_grads[i][j] for i in range(nl)])
            rep_grads = [jnp.stack(dmods)] + [stack(j) for j in range(11)] + [dfinal_g[0]]
            full_grads = [stack(11), stack(12)]
            grad_shapes = [g.shape for g in rep_grads + full_grads]
            gpack = _pack(rep_grads + full_grads, 1024)
            dx, dshift, dscale, dng, win_r[0], gpack_g = _inproj_bwd(*ib_args, comm=((win_slab, gpack), ("scatter", "gather")))
            late_shapes = [(d,), (d,), (d,)]
            (late_g,) = _exchange([_pack([dshift[0], dscale[0], dng[0]])], ["gather"], "exchange_late")

    gsum = _sum_slabs(gpack_g, "sum_small")
    gl = _unpack(gsum, grad_shapes)
    late = _unpack(_sum_slabs(late_g, "sum_late"), late_shapes)
    gl[0] = jnp.concatenate([jnp.concatenate([late[0], late[1], gl[0][0, 2 * d:]])[None], gl[0][1:]], axis=0)
    gl[1] = jnp.concatenate([late[2][None], gl[1][1:]], axis=0)
    g_rep = dict(zip(rep_names, gl[:len(rep_names)]))
    g_conv = lax.dynamic_slice_in_dim(gl[-2], me * cq_cols, cq_cols, axis=2)
    g_cvw = lax.dynamic_slice_in_dim(gl[-1], me * cvw_cols, cvw_cols, axis=2)

    g_win, d_win, nm_win, nv_win = _adam_layers(win_r, w_in, m_w_in, v_w_in, "adam_w_in", 128)
    g_wo, d_wo, nm_wo, nv_wo = _adam_layers(wout_r, w_out, m_w_out, v_w_out, "adam_w_out", 64)
    dmod_all = _unpack(gpack_g, grad_shapes, lead=(N_DEV,))[0]
    late_all = _unpack(late_g, late_shapes, lead=(N_DEV,))
    dmod_all = jnp.concatenate([jnp.concatenate([late_all[0], late_all[1], dmod_all[:, 0, 2 * d:]], axis=1)[:, None],
                                dmod_all[:, 1:]], axis=1)
    dmod_my = jnp.moveaxis(lax.dynamic_slice_in_dim(dmod_all, me * ada_cols, ada_cols, axis=2), 0, 1)
    g_wa, d_wa, nm_wa, nv_wa = _ada_bwd_adam(c_all, dmod_my, w_ada, m_w_ada, v_w_ada)

    small_w = dict(b_ada=b_ada, norm_g=norm_g, a_log=a_log, dt_bias=dt_bias, dn_norm_g=dn_norm_g, sg_ln_g=sg_ln_g, sg_ln_b=sg_ln_b,
                   sg_w=sg_w, sg_b=sg_b, cv_b=cv_b, cv_ln_g=cv_ln_g, cv_ln_b=cv_ln_b, final_g=final_g, conv_qkv=conv_qkv, cv_w=cv_w)
    small_m = dict(b_ada=m_b_ada, norm_g=m_norm_g, a_log=m_a_log, dt_bias=m_dt_bias, dn_norm_g=m_dn_norm_g, sg_ln_g=m_sg_ln_g,
                   sg_ln_b=m_sg_ln_b, sg_w=m_sg_w, sg_b=m_sg_b, cv_b=m_cv_b, cv_ln_g=m_cv_ln_g, cv_ln_b=m_cv_ln_b,
                   final_g=m_final_g, conv_qkv=m_conv_qkv, cv_w=m_cv_w)
    small_v = dict(b_ada=v_b_ada, norm_g=v_norm_g, a_log=v_a_log, dt_bias=v_dt_bias, dn_norm_g=v_dn_norm_g, sg_ln_g=v_sg_ln_g,
                   sg_ln_b=v_sg_ln_b, sg_w=v_sg_w, sg_b=v_sg_b, cv_b=v_cv_b, cv_ln_g=v_cv_ln_g, cv_ln_b=v_cv_ln_b,
                   final_g=v_final_g, conv_qkv=v_conv_qkv, cv_w=v_cv_w)
    small_g = dict(g_rep, conv_qkv=g_conv, cv_w=g_cvw)
    names = rep_names + ["conv_qkv", "cv_w"]
    shapes = [small_w[n].shape for n in names]
    gp = _pack([small_g[n] for n in names], 1024)
    sg_, sd_, sm_, sv_ = _adam_slabs(gp.reshape((1,) + gp.shape), _pack([small_w[n] for n in names], 1024),
                                     _pack([small_m[n] for n in names], 1024), _pack([small_v[n] for n in names], 1024),
                                     "adam_small", 1024)
    sgrad = dict(zip(names, _unpack(sg_, shapes)))
    sdelta = dict(zip(names, _unpack(sd_, shapes)))
    snm = dict(zip(names, _unpack(sm_, shapes)))
    snv = dict(zip(names, _unpack(sv_, shapes)))
    for dct, big in ((sgrad, (g_wa, g_win, g_wo)), (sdelta, (d_wa, d_win, d_wo)), (snm, (nm_wa, nm_win, nm_wo)), (snv, (nv_wa, nv_win, nv_wo))):
        dct["w_ada"] = big[0]
        dct["w_in"] = big[1].reshape(w_in.shape)
        dct["w_out"] = big[2].reshape(w_out.shape)

    order = ["norm_g", "w_ada", "b_ada", "w_in", "conv_qkv", "a_log", "dt_bias", "dn_norm_g", "sg_ln_g", "sg_ln_b", "sg_w", "sg_b",
             "cv_w", "cv_b", "cv_ln_g", "cv_ln_b", "w_out", "final_g"]
    outs = [loss, dx.reshape(x.shape)]
    for dct in (sgrad, sdelta, snm, snv):
        outs += [dct[n] for n in order]
    return tuple(outs)
```

```python
import functools

import jax
import jax.numpy as jnp
from jax import lax
from jax.experimental import pallas as pl
from jax.experimental.pallas import tpu as pltpu

f32 = jnp.float32
BF = jnp.bfloat16
N_DEV = 8
LANES = 128
CHUNK = 64
HD = 128
DN_CONV = 4
SG_BLOCK = 128
SG_GROUPS = 4
CV_GROUPS = 4
CV_K = 31
CV_HALO = 32
QK_HALO = 16
EPS = 1e-6
LN_EPS = 1e-5
ADAM_LR, ADAM_B1, ADAM_B2, ADAM_EPS, ADAM_WD, ADAM_STEP = 0.001, 0.9, 0.999, 1e-08, 0.01, 10
VMEM_LIMIT = 56 * 1024 * 1024
HI = lax.Precision.HIGHEST
MESH = pl.DeviceIdType.MESH


def _cp(sem=None, vmem=VMEM_LIMIT):
    return pltpu.CompilerParams(dimension_semantics=sem, vmem_limit_bytes=vmem)


def _pc(body, **kw):
    return pl.pallas_call(body, **kw)


def _dot(a, b):
    return jnp.dot(a.astype(BF), b.astype(BF), preferred_element_type=f32)


def _dot_nt(a, b):
    return lax.dot_general(a.astype(BF), b.astype(BF), (((1,), (1,)), ((), ())), preferred_element_type=f32)


def _dot_tn(a, b):
    return lax.dot_general(a.astype(BF), b.astype(BF), (((0,), (0,)), ((), ())), preferred_element_type=f32)


def _dot_hi(a, b):
    return jnp.dot(a, b, preferred_element_type=f32, precision=HI)


def _sigmoid(x):
    return 1.0 / (1.0 + jnp.exp(-x))


def _silu(x):
    return x * _sigmoid(x)


def _dsilu(x):
    s = _sigmoid(x)
    return s * (1.0 + x * (1.0 - s))


def _softplus(x):
    return jnp.maximum(x, 0.0) + jnp.log(1.0 + jnp.exp(-jnp.abs(x)))


def _gelu(x):
    return 0.5 * x * (1.0 + lax.erf(x * 0.7071067811865476))


def _dgelu(x):
    return 0.5 * (1.0 + lax.erf(x * 0.7071067811865476)) + x * jnp.exp(-0.5 * x * x) * 0.3989422804014327


def _lanecol(x, lane_iota, j):
    return jnp.sum(jnp.where(lane_iota == j, x, 0.0), axis=1, keepdims=True)


def _split(a):
    hi = a.astype(BF)
    return hi, (a - hi.astype(f32)).astype(BF)


def _dot3(a, b):
    mm = lambda u, v: jnp.dot(u, v, preferred_element_type=f32)
    return mm(a[0], b[0]) + (mm(a[0], b[1]) + mm(a[1], b[0]))


INV_BASE = 8
DN_GROUP = 4


def _tri_inv(l_mats):
    n = l_mats[0].shape[0]
    r = lax.broadcasted_iota(jnp.int32, (n, n), 0)
    c = lax.broadcasted_iota(jnp.int32, (n, n), 1)
    same = lambda size: (r // size) == (c // size)
    diag = [jnp.where(same(INV_BASE), l, 0.0) for l in l_mats]
    xs = [(r == c).astype(f32) - l for l in diag]
    sp = [_split(l) for l in diag]
    size = 2
    while size < INV_BASE:
        ps = [_dot3(s, s) for s in sp]
        sp = [_split(p) for p in ps]
        sx = [_split(x) for x in xs]
        xs = [x + _dot3(a, b) for x, a, b in zip(xs, sx, sp)]
        size *= 2
    blk = INV_BASE
    while blk < n:
        join = same(2 * blk) & jnp.logical_not(same(blk))
        lo = [_split(jnp.where(join, l, 0.0)) for l in l_mats]
        sx = [_split(x) for x in xs]
        ys = [_split(_dot3(a, b)) for a, b in zip(lo, sx)]
        xs = [x - _dot3(a, b) for x, a, b in zip(xs, sx, ys)]
        blk *= 2
    return xs


def _me():
    x, y, c = lax.axis_index("x"), lax.axis_index("y"), lax.axis_index("c")
    return x, y, c, 4 * x + 2 * y + c


def _exchange(arrs, kinds, name):
    n = len(arrs)

    def body(*refs):
        for phase in ("first", "middle", "last"):
            _comm_phase(refs[:n], refs[n:2 * n], kinds, *refs[2 * n:], phase)

    any_spec = pl.BlockSpec(memory_space=pl.ANY)
    return _pc(body, name=name, out_shape=_comm_out_shapes(arrs, kinds), in_specs=[any_spec] * n, out_specs=tuple([any_spec] * n),
               scratch_shapes=_comm_sems(n))(*arrs)


def _comm_out_shapes(arrs, kinds):
    return tuple(jax.ShapeDtypeStruct(a.shape if k == "scatter" else (N_DEV,) + a.shape, a.dtype) for a, k in zip(arrs, kinds))


def _comm_sems(n):
    return [pltpu.SemaphoreType.DMA((n, N_DEV - 1)), pltpu.SemaphoreType.DMA((n, N_DEV - 1)), pltpu.SemaphoreType.DMA((n,))]


def _comm_phase(ins, outs, kinds, send, recv, loc, phase):
    x, y, c, me = _me()
    flips = ((1, 0), (0, 1), (1, 1))
    chips = [(jnp.bitwise_xor(x, fx), jnp.bitwise_xor(y, fy)) for fx, fy in flips]
    for a in range(len(ins)):
        slot = lambda px, py, pc, a=a: outs[a].at[4 * px + 2 * py + pc]

        def copy(k, src, block, to, a=a, slot=slot):
            return pltpu.make_async_remote_copy(src_ref=src, dst_ref=slot(*block), send_sem=send.at[a, k], recv_sem=recv.at[a, k],
                                                device_id=to, device_id_type=MESH)

        if kinds[a] == "gather2":
            mine = pltpu.make_async_copy(ins[a], slot(x, y, c), loc.at[a])
            first = [copy(0, ins[a], (x, y, c), (x, y, 1 - c))] + [copy(1 + j, ins[a], (x, y, c), (px, py, c))
                                                                  for j, (px, py) in enumerate(chips)]
            passed = [copy(4 + j, slot(px, py, c), (px, py, c), (x, y, 1 - c)) for j, (px, py) in enumerate(chips)]
            if phase == "first":
                for cp in [mine] + first:
                    cp.start()
            elif phase == "middle":
                for j, (px, py) in enumerate(chips):
                    copy(1 + j, ins[a], (px, py, c), (x, y, c)).wait_recv()
                    passed[j].start()
            else:
                copy(0, ins[a], (x, y, 1 - c), (x, y, c)).wait_recv()
                for j, (px, py) in enumerate(chips):
                    copy(4 + j, ins[a], (px, py, 1 - c), (x, y, c)).wait_recv()
                for cp in first + passed:
                    cp.wait_send()
                mine.wait()
        else:
            own = ins[a].at[me] if kinds[a] == "scatter" else ins[a]
            copies = [pltpu.make_async_copy(own, outs[a].at[me], loc.at[a])]
            for k in range(1, N_DEV):
                px = jnp.bitwise_xor(x, (k >> 2) & 1)
                py = jnp.bitwise_xor(y, (k >> 1) & 1)
                pc = jnp.bitwise_xor(c, k & 1)
                src = ins[a].at[4 * px + 2 * py + pc] if kinds[a] == "scatter" else ins[a]
                copies.append(pltpu.make_async_remote_copy(src_ref=src, dst_ref=outs[a].at[me], send_sem=send.at[a, k - 1],
                                                           recv_sem=recv.at[a, k - 1], device_id=(px, py, pc), device_id_type=MESH))
            if phase == "first":
                for cp in copies:
                    cp.start()
            elif phase == "last":
                for cp in copies:
                    cp.wait()


def _with_comm(compute, comm, name, grid, args, in_specs, out_shape, out_specs, scratch):
    arrs, kinds = comm if comm else ((), ())
    n, n_in, n_out, n_scr = len(arrs), len(args), len(out_shape), len(scratch)
    marks = {"first": [0] * len(grid), "middle": [(3 * grid[0]) // 4] + [0] * (len(grid) - 1), "last": [g - 1 for g in grid]}

    def at_step(which):
        ok = pl.program_id(0) == marks[which][0]
        for ax in range(1, len(grid)):
            ok = ok & (pl.program_id(ax) == marks[which][ax])
        return ok

    def body(*refs):
        ins, cin = refs[:n_in], refs[n_in:n_in + n]
        outs, cout = refs[n_in + n:n_in + n + n_out], refs[n_in + n + n_out:n_in + 2 * n + n_out]
        scr, sems = refs[n_in + 2 * n + n_out:n_in + 2 * n + n_out + n_scr], refs[n_in + 2 * n + n_out + n_scr:]
        for phase in ("first", "middle") if n else ():
            @pl.when(at_step(phase))
            def _(phase=phase):
                _comm_phase(cin, cout, kinds, *sems, phase)
        compute(*ins, *outs, *scr)
        if n:
            @pl.when(at_step("last"))
            def _():
                _comm_phase(cin, cout, kinds, *sems, "last")

    any_spec = pl.BlockSpec(memory_space=pl.ANY)
    return _pc(body, name=name + ("_comm" if n else ""), grid=tuple(grid),
               out_shape=tuple(out_shape) + (_comm_out_shapes(arrs, kinds) if n else ()),
               in_specs=list(in_specs) + [any_spec] * n, out_specs=tuple(out_specs) + (any_spec,) * n,
               scratch_shapes=list(scratch) + (_comm_sems(n) if n else []),
               compiler_params=_cp(("arbitrary",) * len(grid)))(*args, *arrs)


def _ada_fwd(c_all, w_ada, b_my):
    nl, d, cols = w_ada.shape

    def body(c_ref, w_ref, b_ref, o_ref):
        ca = _silu(c_ref[...])
        o_ref[0] = _dot(ca, w_ref[0]) + b_ref[0]

    return _pc(body, name="ada_fwd", grid=(nl,), out_shape=jax.ShapeDtypeStruct((nl, N_DEV, cols), f32),
               in_specs=[pl.BlockSpec((N_DEV, d), lambda l: (0, 0)), pl.BlockSpec((1, d, cols), lambda l: (l, 0, 0)),
                         pl.BlockSpec((1, 1, cols), lambda l: (l, 0, 0))],
               out_specs=pl.BlockSpec((1, N_DEV, cols), lambda l: (l, 0, 0)), compiler_params=_cp(("arbitrary",)))(c_all, w_ada, b_my)


def _adam_math(w, g, m, v):
    m = ADAM_B1 * m + (1.0 - ADAM_B1) * g
    v = ADAM_B2 * v + (1.0 - ADAM_B2) * (g * g)
    m_hat = m / (1.0 - ADAM_B1 ** ADAM_STEP)
    v_hat = v / (1.0 - ADAM_B2 ** ADAM_STEP)
    delta = -ADAM_LR * (m_hat / (jnp.sqrt(v_hat) + ADAM_EPS) + ADAM_WD * w)
    return delta, m, v


def _ada_bwd_adam(c_all, dmod_my, w, m, v):
    nl, d, cols = w.shape
    tk = 512

    def body(c_ref, dm_ref, w_ref, m_ref, v_ref, g_ref, dl_ref, nm_ref, nv_ref):
        ca = _silu(c_ref[...])
        g = _dot_tn(ca, dm_ref[0])
        dl, nm, nv = _adam_math(w_ref[0], g, m_ref[0], v_ref[0])
        g_ref[0] = g
        dl_ref[0] = dl
        nm_ref[0] = nm
        nv_ref[0] = nv

    wspec = pl.BlockSpec((1, tk, cols), lambda l, k: (l, k, 0))
    shp = jax.ShapeDtypeStruct(w.shape, f32)
    return _pc(body, name="ada_bwd_adam", grid=(nl, d // tk), out_shape=(shp, shp, shp, shp),
               in_specs=[pl.BlockSpec((N_DEV, tk), lambda l, k: (0, k)), pl.BlockSpec((1, N_DEV, cols), lambda l, k: (l, 0, 0)),
                         wspec, wspec, wspec],
               out_specs=(wspec, wspec, wspec, wspec), compiler_params=_cp(("arbitrary", "arbitrary")))(c_all, dmod_my, w, m, v)


def _adam_slabs(slabs, w, m, v, name, tr):
    n, rows, cols = slabs.shape
    tr = min(tr, rows)

    def body(s_ref, w_ref, m_ref, v_ref, g_ref, dl_ref, nm_ref, nv_ref):
        g = s_ref[0].astype(f32)
        for j in range(1, n):
            g = g + s_ref[j].astype(f32)
        dl, nm, nv = _adam_math(w_ref[...], g, m_ref[...], v_ref[...])
        g_ref[...] = g
        dl_ref[...] = dl
        nm_ref[...] = nm
        nv_ref[...] = nv

    spec = pl.BlockSpec((tr, cols), lambda i: (i, 0))
    shp = jax.ShapeDtypeStruct((rows, cols), f32)
    return _pc(body, name=name, grid=(rows // tr,), out_shape=(shp, shp, shp, shp),
               in_specs=[pl.BlockSpec((n, tr, cols), lambda i: (0, i, 0)), spec, spec, spec],
               out_specs=(spec, spec, spec, spec), compiler_params=_cp(("arbitrary",)))(slabs, w, m, v)


def _adam_layers(slabs, w, m, v, name, tr):
    nl, rows, cols = w.shape
    n = slabs[0].shape[0]
    tr = min(tr, rows)

    def body(*refs):
        s_refs, (w_ref, m_ref, v_ref, g_ref, dl_ref, nm_ref, nv_ref) = refs[:nl], refs[nl:]
        for l in range(nl):
            @pl.when(pl.program_id(0) == l)
            def _(l=l):
                g = s_refs[l][0].astype(f32)
                for j in range(1, n):
                    g = g + s_refs[l][j].astype(f32)
                dl, nm, nv = _adam_math(w_ref[0], g, m_ref[0], v_ref[0])
                g_ref[0] = g
                dl_ref[0] = dl
                nm_ref[0] = nm
                nv_ref[0] = nv

    spec = pl.BlockSpec((1, tr, cols), lambda l, i: (l, i, 0))
    s_specs = [pl.BlockSpec((n, tr, cols), lambda l, i, j=j: (0, jnp.where(l == j, i, 0), 0)) for j in range(nl)]
    shp = jax.ShapeDtypeStruct(w.shape, f32)
    return _pc(body, name=name, grid=(nl, rows // tr), out_shape=(shp, shp, shp, shp),
               in_specs=s_specs + [spec, spec, spec], out_specs=(spec, spec, spec, spec),
               compiler_params=_cp(("arbitrary", "arbitrary")))(*slabs, w, m, v)


def _sum_slabs(slabs, name):
    n, rows, cols = slabs.shape
    tr = min(rows, 1024)

    def body(s_ref, o_ref):
        g = s_ref[0]
        for j in range(1, n):
            g = g + s_ref[j]
        o_ref[...] = g

    return _pc(body, name=name, grid=(rows // tr,), out_shape=jax.ShapeDtypeStruct((rows, cols), f32),
               in_specs=[pl.BlockSpec((n, tr, cols), lambda i: (0, i, 0))], out_specs=pl.BlockSpec((tr, cols), lambda i: (i, 0)),
               compiler_params=_cp(("arbitrary",)))(slabs)


NAT_Z, NAT_SMALL, NAT_REST = 3072, 4096, 4112


def _repack_w(g_win):
    n, d, cols = g_win.shape
    tr = 256

    def body(g_ref, wm_ref, ws_ref):
        nat = jnp.concatenate([g_ref[j] for j in range(n)], axis=1)
        wm_ref[...] = jnp.concatenate([nat[:, NAT_Z:NAT_SMALL], nat[:, NAT_REST:], nat[:, :NAT_Z]], axis=1)
        ws_ref[...] = jnp.concatenate([nat[:, NAT_SMALL:NAT_REST], jnp.zeros((tr, LANES - 16), nat.dtype)], axis=1)

    return _pc(body, name="repack_w", grid=(d // tr,),
               out_shape=(jax.ShapeDtypeStruct((d, n * cols - 16), g_win.dtype), jax.ShapeDtypeStruct((d, LANES), g_win.dtype)),
               in_specs=[pl.BlockSpec((n, tr, cols), lambda i: (0, i, 0))],
               out_specs=(pl.BlockSpec((tr, n * cols - 16), lambda i: (i, 0)), pl.BlockSpec((tr, LANES), lambda i: (i, 0))),
               compiler_params=_cp(("arbitrary",)))(g_win)


def _grad_slabs(gwa, gwb, gws, cols):
    d = gwa.shape[0]
    tr = 256

    def body(a_ref, b_ref, s_ref, o_ref):
        a = a_ref[...]
        nat = jnp.concatenate([b_ref[...], a[:, :NAT_SMALL - NAT_Z], s_ref[:, 0:16], a[:, NAT_SMALL - NAT_Z:]], axis=1)
        for j in range(N_DEV):
            o_ref[j] = nat[:, j * cols:(j + 1) * cols].astype(BF)

    return _pc(body, name="grad_slabs", grid=(d // tr,), out_shape=jax.ShapeDtypeStruct((N_DEV, d, cols), BF),
               in_specs=[pl.BlockSpec((tr, gwa.shape[1]), lambda i: (i, 0)), pl.BlockSpec((tr, gwb.shape[1]), lambda i: (i, 0)),
                         pl.BlockSpec((tr, LANES), lambda i: (i, 0))],
               out_specs=pl.BlockSpec((N_DEV, tr, cols), lambda i: (0, i, 0)), compiler_params=_cp(("arbitrary",)))(gwa, gwb, gws)


def _inproj(x, ng, sc1, sh, wmain, wsmall):
    t, d = x.shape
    n = wmain.shape[1]
    tm, tn, rb = min(t, 1024), 1024, 128

    def body(x_ref, ng_ref, sc_ref, sh_ref, wm_ref, ws_ref, p_ref, ps_ref, h_ref, hs_ref):
        @pl.when(pl.program_id(1) == 0)
        def _():
            def rows(i, carry):
                r0 = pl.multiple_of(i * rb, rb)
                xv = x_ref[pl.ds(r0, rb), :]
                r = lax.rsqrt(jnp.mean(xv * xv, axis=1, keepdims=True) + EPS)
                hb = (((xv * r) * ng_ref[...]) * sc_ref[...] + sh_ref[...]).astype(BF)
                hs_ref[pl.ds(r0, rb), :] = hb
                h_ref[pl.ds(r0, rb), :] = hb
                return carry
            lax.fori_loop(0, tm // rb, rows, 0)
            ps_ref[...] = jnp.dot(hs_ref[...], ws_ref[...], preferred_element_type=f32)
        p_ref[...] = jnp.dot(hs_ref[...], wm_ref[...], preferred_element_type=f32).astype(BF)

    vec = pl.BlockSpec((1, d), lambda i, j: (0, 0))
    return _pc(body, name="inproj", grid=(t // tm, n // tn),
               out_shape=(jax.ShapeDtypeStruct((t, n), BF), jax.ShapeDtypeStruct((t, LANES), f32), jax.ShapeDtypeStruct((t, d), BF)),
               in_specs=[pl.BlockSpec((tm, d), lambda i, j: (i, 0)), vec, vec, vec,
                         pl.BlockSpec((d, tn), lambda i, j: (0, j)), pl.BlockSpec((d, LANES), lambda i, j: (0, 0))],
               out_specs=(pl.BlockSpec((tm, tn), lambda i, j: (i, j)), pl.BlockSpec((tm, LANES), lambda i, j: (i, 0)),
                          pl.BlockSpec((tm, d), lambda i, j: (i, 0))),
               scratch_shapes=[pltpu.VMEM((tm, d), BF)], compiler_params=_cp(("arbitrary", "arbitrary")))(x, ng, sc1, sh, wmain, wsmall)


P_Z, P_SGU, P_SGV, P_SGG, P_CVA, P_CVB, P_CVG, P_Q = 0, 1024, 1536, 2048, 2560, 3072, 3584, 4096


def _dn_specs(t, tb, rev):
    nt = t // tb
    ti = (lambda i: nt - 1 - i) if rev else (lambda i: i)
    qb = P_Q // 1024
    hb = tb // QK_HALO
    specs = []
    for s in range(3):
        specs.append(pl.BlockSpec((tb, 1024), lambda i, s=s: (ti(i), qb + s)))
    for s in range(3):
        specs.append(pl.BlockSpec((QK_HALO, 1024), lambda i, s=s: (jnp.maximum(ti(i) * hb - 1, 0), qb + s)))
    specs.append(pl.BlockSpec((DN_CONV, 3072), lambda i: (0, 0)))
    specs.append(pl.BlockSpec((tb, LANES), lambda i: (ti(i), 0)))
    specs.append(pl.BlockSpec((tb // CHUNK, 16, CHUNK), lambda i: (ti(i), 0, 0)))
    specs.append(pl.BlockSpec((1, LANES), lambda i: (0, 0)))
    specs.append(pl.BlockSpec((1, LANES), lambda i: (0, 0)))
    specs.append(pl.BlockSpec((16, CHUNK), lambda i: (0, 0)))
    specs.append(pl.BlockSpec((16, CHUNK), lambda i: (0, 0)))
    return specs, ti


def _dn_conv(first, tiles, halos, cw_ref, xs_ref, cpre_ref, tb):
    for s in range(3):
        c0 = s * 1024
        xs_ref[0:QK_HALO, c0:c0 + 1024] = jnp.where(first, 0.0, halos[s][...].astype(f32))
        xs_ref[QK_HALO:QK_HALO + tb, c0:c0 + 1024] = tiles[s][...].astype(f32)
    for sub in range(3072 // 256):
        cs = slice(sub * 256, (sub + 1) * 256)
        acc = cw_ref[0:1, cs] * xs_ref[pl.ds(QK_HALO - 3, tb), cs]
        for j in range(1, DN_CONV):
            acc = acc + cw_ref[j:j + 1, cs] * xs_ref[pl.ds(QK_HALO - 3 + j, tb), cs]
        cpre_ref[:, cs] = acc


def _dn_head(cpre_ref, r0, h, beta_all, gc_all, gcrow_ref, lane, tri_incl, tri_strict):
    rows = pl.ds(r0, CHUNK)
    cq = cpre_ref[rows, pl.ds(HD * h, HD)]
    ck = cpre_ref[rows, pl.ds(1024 + HD * h, HD)]
    cv = cpre_ref[rows, pl.ds(2048 + HD * h, HD)]
    sq, sk, v = _silu(cq), _silu(ck), _silu(cv)
    rq = lax.rsqrt(jnp.sum(sq * sq, axis=1, keepdims=True) + EPS)
    rk = lax.rsqrt(jnp.sum(sk * sk, axis=1, keepdims=True) + EPS)
    q, k = sq * rq, sk * rk
    beta = _lanecol(beta_all, lane, h)
    gcc = _lanecol(gc_all, lane, 8 + h)
    gcr = gcrow_ref[pl.ds(8 + h, 1), :]
    diff = gcc - gcr
    dmat = jnp.where(tri_incl, jnp.exp(jnp.where(tri_incl, diff, 0.0)), 0.0)
    gam = jnp.exp(gcc)
    glast = gcr[:, CHUNK - 1:CHUNK]
    gam_last = jnp.exp(glast)
    kscale = jnp.exp(glast - gcc)
    qs = q * (HD ** -0.5)
    kb = k * beta
    vb = v * beta
    kbg = kb * gam
    kk = _dot_nt(kb, k)
    lmat = jnp.where(tri_strict, kk * dmat, 0.0)
    pmat = _dot_nt(qs, k) * dmat
    return dict(cq=cq, ck=ck, cv=cv, sq=sq, sk=sk, v=v, rq=rq, rk=rk, q=q, k=k, beta=beta, gcc=gcc, dmat=dmat, gam=gam,
                gam_last=gam_last, kscale=kscale, qs=qs, kb=kb, vb=vb, kbg=kbg, lmat=lmat, pmat=pmat)


def _dn_gates(ps_ref, arow_ref, dtb_ref, pst_ref, acol_ref, dtc_ref):
    ps = ps_ref[...]
    beta_all = _sigmoid(ps)
    g_all = arow_ref[...] * _softplus(ps + dtb_ref[...])
    return ps, beta_all, g_all


def _dn_fwd(p, ps, pst, convw, arow, dtb, acol, dtc, comm=()):
    t = p.shape[0]
    nh = 1024 // HD
    tb = min(t, 256)
    nc = tb // CHUNK
    specs, _ = _dn_specs(t, tb, False)

    def compute(q_ref, k_ref, v_ref, hq_ref, hk_ref, hv_ref, cw_ref, ps_ref, pst_ref, arow_ref, dtb_ref, acol_ref, dtc_ref,
             o_ref, ss_ref, ti_ref, xs_ref, cpre_ref, st_ref, gcrow_ref, ball_ref, gall_ref):
        i = pl.program_id(0)

        @pl.when(i == 0)
        def _():
            st_ref[...] = jnp.zeros_like(st_ref)

        _dn_conv(i == 0, (q_ref, k_ref, v_ref), (hq_ref, hk_ref, hv_ref), cw_ref, xs_ref, cpre_ref, tb)
        _, beta_all, g_all = _dn_gates(ps_ref, arow_ref, dtb_ref, pst_ref, acol_ref, dtc_ref)
        ball_ref[...] = beta_all
        gall_ref[...] = g_all
        ri = lax.broadcasted_iota(jnp.int32, (CHUNK, CHUNK), 0)
        ci = lax.broadcasted_iota(jnp.int32, (CHUNK, CHUNK), 1)
        tri_incl, tri_strict = ri >= ci, ri > ci
        lower = tri_incl.astype(f32)
        upper = (ri <= ci).astype(f32)
        lane = lax.broadcasted_iota(jnp.int32, (CHUNK, LANES), 1)

        def chunk_group(cg, carry):
            hs = range(nh)
            mm = lambda a, b: jnp.dot(a, b, preferred_element_type=f32)
            cs = [cg * DN_GROUP + j for j in range(DN_GROUP)]
            r0s = [pl.multiple_of(c * CHUNK, CHUNK) for c in cs]
            hd = []
            for j in range(DN_GROUP):
                gc_all = _dot_hi(lower, gall_ref[pl.ds(r0s[j], CHUNK), :])
                g_rows = acol_ref[...] * _softplus(pst_ref[cs[j]] + dtc_ref[...])
                gcrow_ref[j] = _dot_hi(g_rows, upper)
                beta_c = ball_ref[pl.ds(r0s[j], CHUNK), :]
                hd += [_dn_head(cpre_ref, r0s[j], h, beta_c, gc_all, gcrow_ref.at[j], lane, tri_incl, tri_strict) for h in hs]
            tinv = _tri_inv([d["lmat"] for d in hd])
            tib = [x.astype(BF) for x in tinv]
            u = [mm(tib[i], hd[i]["vb"].astype(BF)) for i in range(len(hd))]
            w = [mm(tib[i], hd[i]["kbg"].astype(BF)).astype(BF) for i in range(len(hd))]
            qg = [(d["qs"] * d["gam"]).astype(BF) for d in hd]
            kt = [(d["k"] * d["kscale"]).astype(BF) for d in hd]
            pm = [d["pmat"].astype(BF) for d in hd]
            for j in range(DN_GROUP):
                ix = [j * nh + h for h in hs]
                s = [st_ref[h] for h in hs]
                sb = [x.astype(BF) for x in s]
                vnb = [(u[ix[h]] - mm(w[ix[h]], sb[h])).astype(BF) for h in hs]
                o = [mm(qg[ix[h]], sb[h]) + mm(pm[ix[h]], vnb[h]) for h in hs]
                snew = [s[h] * hd[ix[h]]["gam_last"] + lax.dot_general(kt[ix[h]], vnb[h], (((0,), (0,)), ((), ())),
                                                                      preferred_element_type=f32) for h in hs]
                for h in hs:
                    o_ref[pl.ds(r0s[j], CHUNK), pl.ds(HD * h, HD)] = o[h]
                    ss_ref[cs[j], h] = sb[h]
                    ti_ref[cs[j], h] = tinv[ix[h]]
                    st_ref[h] = snew[h]
            return carry

        lax.fori_loop(0, nc // DN_GROUP, chunk_group, 0)

    return _with_comm(
        compute, comm, "dn_fwd", (t // tb,), (p, p, p, p, p, p, convw, ps, pst, arow, dtb, acol, dtc), specs,
        (jax.ShapeDtypeStruct((t, 1024), f32), jax.ShapeDtypeStruct((t // CHUNK, nh, HD, HD), BF),
         jax.ShapeDtypeStruct((t // CHUNK, nh, CHUNK, CHUNK), f32)),
        (pl.BlockSpec((tb, 1024), lambda i: (i, 0)), pl.BlockSpec((nc, nh, HD, HD), lambda i: (i, 0, 0, 0)),
         pl.BlockSpec((nc, nh, CHUNK, CHUNK), lambda i: (i, 0, 0, 0))),
        [pltpu.VMEM((tb + QK_HALO, 3072), f32), pltpu.VMEM((tb, 3072), f32), pltpu.VMEM((nh, HD, HD), f32),
         pltpu.VMEM((DN_GROUP, 16, CHUNK), f32), pltpu.VMEM((tb, LANES), f32), pltpu.VMEM((tb, LANES), f32)])


def _dn_bwd(p, ps, pst, convw, arow, dtb, acol, dtc, do, ss, tinv_all, comm=()):
    t = p.shape[0]
    nh = 1024 // HD
    tb = min(t, 256)
    nc = tb // CHUNK
    nt = t // tb
    specs, ti = _dn_specs(t, tb, True)
    specs = specs + [pl.BlockSpec((tb, 1024), lambda i: (ti(i), 0)), pl.BlockSpec((nc, nh, HD, HD), lambda i: (ti(i), 0, 0, 0)),
                     pl.BlockSpec((nc, nh, CHUNK, CHUNK), lambda i: (ti(i), 0, 0, 0))]

    def compute(q_ref, k_ref, v_ref, hq_ref, hk_ref, hv_ref, cw_ref, ps_ref, pst_ref, arow_ref, dtb_ref, acol_ref, dtc_ref,
             do_ref, ss_ref, ti_ref,
             dp_ref, dps_ref, dcw_ref, dal_ref, ddt_ref,
             xs_ref, cpre_ref, dst_ref, gcrow_ref, ball_ref, gall_ref, dcs_ref, dball_ref, dgcall_ref):
        i = pl.program_id(0)

        @pl.when(i == 0)
        def _():
            dst_ref[...] = jnp.zeros_like(dst_ref)
            dcs_ref[...] = jnp.zeros_like(dcs_ref)
            dcw_ref[...] = jnp.zeros_like(dcw_ref)
            dal_ref[...] = jnp.zeros_like(dal_ref)
            ddt_ref[...] = jnp.zeros_like(ddt_ref)

        _dn_conv(i == nt - 1, (q_ref, k_ref, v_ref), (hq_ref, hk_ref, hv_ref), cw_ref, xs_ref, cpre_ref, tb)
        ps, beta_all, g_all = _dn_gates(ps_ref, arow_ref, dtb_ref, pst_ref, acol_ref, dtc_ref)
        ball_ref[...] = beta_all
        gall_ref[...] = g_all
        ri = lax.broadcasted_iota(jnp.int32, (CHUNK, CHUNK), 0)
        ci = lax.broadcasted_iota(jnp.int32, (CHUNK, CHUNK), 1)
        tri_incl, tri_strict = ri >= ci, ri > ci
        lower = tri_incl.astype(f32)
        upper = (ri <= ci).astype(f32)
        lane = lax.broadcasted_iota(jnp.int32, (CHUNK, LANES), 1)
        row_last = lax.broadcasted_iota(jnp.int32, (CHUNK, 1), 0) == CHUNK - 1

        def chunk_group(cg, carry):
            hs = range(nh)
            bf = lambda a: a.astype(BF)
            mm = lambda a, b: jnp.dot(a, b, preferred_element_type=f32)
            mnt = lambda a, b: lax.dot_general(a, b, (((1,), (1,)), ((), ())), preferred_element_type=f32)
            mtn = lambda a, b: lax.dot_general(a, b, (((0,), (0,)), ((), ())), preferred_element_type=f32)
            rsum = lambda a: jnp.sum(a, axis=1, keepdims=True)
            cs = [nc - 1 - (cg * DN_GROUP + j) for j in range(DN_GROUP)]
            r0s = [pl.multiple_of(c * CHUNK, CHUNK) for c in cs]
            nn = range(DN_GROUP * nh)
            jof = [i // nh for i in nn]
            hof = [i % nh for i in nn]
            hd = []
            for j in range(DN_GROUP):
                rows = pl.ds(r0s[j], CHUNK)
                gc_all = _dot_hi(lower, gall_ref[rows, :])
                g_rows = acol_ref[...] * _softplus(pst_ref[cs[j]] + dtc_ref[...])
                gcrow_ref[j] = _dot_hi(g_rows, upper)
                beta_c = ball_ref[rows, :]
                hd += [_dn_head(cpre_ref, r0s[j], h, beta_c, gc_all, gcrow_ref.at[j], lane, tri_incl, tri_strict) for h in hs]
            tib = [bf(ti_ref[cs[jof[i]], hof[i]]) for i in nn]
            sb = [ss_ref[cs[jof[i]], hof[i]] for i in nn]
            vbb = [bf(d["vb"]) for d in hd]
            kgb = [bf(d["kbg"]) for d in hd]
            u = [mm(tib[i], vbb[i]) for i in nn]
            wb = [bf(mm(tib[i], kgb[i])) for i in nn]
            vnb = [bf(u[i] - mm(wb[i], sb[i])) for i in nn]
            qgb = [bf(d["qs"] * d["gam"]) for d in hd]
            kt = [d["k"] * d["kscale"] for d in hd]
            ktb = [bf(x) for x in kt]
            pmb = [bf(d["pmat"]) for d in hd]
            dob = [bf(do_ref[pl.ds(r0s[jof[i]], CHUNK), pl.ds(HD * hof[i], HD)]) for i in nn]
            dvn0 = [mtn(pmb[i], dob[i]) for i in nn]
            dsn0 = [mtn(qgb[i], dob[i]) for i in nn]
            dpm = [jnp.where(tri_incl, mnt(dob[i], vnb[i]), 0.0) for i in nn]
            dqg = [mnt(dob[i], sb[i]) for i in nn]
            dvnb, dkt, dgl = [], [], []
            for j in range(DN_GROUP):
                ix = [j * nh + h for h in hs]
                dsn = [dst_ref[h] for h in hs]
                dsnb = [bf(x) for x in dsn]
                dvn_j = [bf(dvn0[ix[h]] + mm(ktb[ix[h]], dsnb[h])) for h in hs]
                dsnew = [dsn0[ix[h]] + hd[ix[h]]["gam_last"] * dsn[h] - mtn(wb[ix[h]], dvn_j[h]) for h in hs]
                for h in hs:
                    dst_ref[h] = dsnew[h]
                dkt += [mnt(vnb[ix[h]], dsnb[h]) for h in hs]
                dgl += [jnp.sum(rsum(sb[ix[h]].astype(f32) * dsn[h]), axis=0, keepdims=True) for h in hs]
                dvnb += dvn_j
            dwb = [bf(-mnt(dvnb[i], sb[i])) for i in nn]
            dtm = [mnt(dvnb[i], vbb[i]) + mnt(dwb[i], kgb[i]) for i in nn]
            dvb = [mtn(tib[i], dvnb[i]) for i in nn]
            dkbg = [mtn(tib[i], dwb[i]) for i in nn]
            x1 = [bf(mtn(tib[i], bf(dtm[i]))) for i in nn]
            dl = [jnp.where(tri_strict, -mnt(x1[i], tib[i]), 0.0) for i in nn]
            mmat = [dl[i] * hd[i]["lmat"] + dpm[i] * hd[i]["pmat"] for i in nn]
            dkkb = [bf(dl[i] * hd[i]["dmat"]) for i in nn]
            dqkb = [bf(dpm[i] * hd[i]["dmat"]) for i in nn]
            kb16 = [bf(d["k"]) for d in hd]
            dkb = [mm(dkkb[i], kb16[i]) + dkbg[i] * hd[i]["gam"] for i in nn]
            dk = [mtn(dkkb[i], bf(hd[i]["kb"])) + mtn(dqkb[i], bf(hd[i]["qs"])) + dkt[i] * hd[i]["kscale"] + dkb[i] * hd[i]["beta"]
                  for i in nn]
            dq = [(mm(dqkb[i], kb16[i]) + dqg[i] * hd[i]["gam"]) * (HD ** -0.5) for i in nn]
            dbeta = [rsum(dkb[i] * hd[i]["k"] + dvb[i] * hd[i]["v"]) for i in nn]
            dgam = [rsum(dkbg[i] * hd[i]["kb"] + dqg[i] * hd[i]["qs"]) for i in nn]
            ktdk = [rsum(dkt[i] * kt[i]) for i in nn]
            csum = [rsum(jnp.where(ri == ci, jnp.sum(mmat[i], axis=0, keepdims=True), 0.0)) for i in nn]
            for j in range(DN_GROUP):
                rows = pl.ds(r0s[j], CHUNK)
                dbeta_all = jnp.zeros((CHUNK, LANES), f32)
                dgc_all = jnp.zeros((CHUNK, LANES), f32)
                for h in hs:
                    i = j * nh + h
                    d = hd[i]
                    extra = jnp.sum(ktdk[i], axis=0, keepdims=True) + dgl[i] * d["gam_last"]
                    dgc = rsum(mmat[i]) - csum[i] + dgam[i] * d["gam"] - ktdk[i] + jnp.where(row_last, extra, 0.0)
                    dbeta_all = jnp.where(lane == h, dbeta[i], dbeta_all)
                    dgc_all = jnp.where(lane == 8 + h, dgc, dgc_all)
                    dsq = d["rq"] * (dq[i] - d["q"] * rsum(dq[i] * d["q"]))
                    dsk = d["rk"] * (dk[i] - d["k"] * rsum(dk[i] * d["k"]))
                    dcs_ref[rows, pl.ds(HD * h, HD)] = dsq * _dsilu(d["cq"])
                    dcs_ref[rows, pl.ds(1024 + HD * h, HD)] = dsk * _dsilu(d["ck"])
                    dcs_ref[rows, pl.ds(2048 + HD * h, HD)] = (dvb[i] * d["beta"]) * _dsilu(d["cv"])
                dball_ref[rows, :] = dbeta_all
                dgcall_ref[rows, :] = _dot_hi(upper, dgc_all)
            return carry

        lax.fori_loop(0, nc // DN_GROUP, chunk_group, 0)

        dbeta_t = dball_ref[...]
        dg_t = dgcall_ref[...]
        sg = _sigmoid(ps + dtb_ref[...])
        da = dg_t * arow_ref[...] * sg
        dps_ref[...] = dbeta_t * beta_all * (1.0 - beta_all) + da
        ddt_ref[...] += jnp.sum(da, axis=0, keepdims=True)
        dal_ref[...] += jnp.sum(dg_t * g_all, axis=0, keepdims=True)

        for sub in range(3072 // 256):
            cs = slice(sub * 256, (sub + 1) * 256)
            dc = dcs_ref[0:tb, cs]
            acc = cw_ref[DN_CONV - 1:DN_CONV, cs] * dc
            for j in range(DN_CONV - 1):
                acc = acc + cw_ref[j:j + 1, cs] * dcs_ref[pl.ds(DN_CONV - 1 - j, tb), cs]
            dp_ref[:, cs] = acc.astype(BF)
            for j in range(DN_CONV):
                dcw_ref[j:j + 1, cs] += jnp.sum(dc * xs_ref[pl.ds(QK_HALO - 3 + j, tb), cs], axis=0, keepdims=True)
        dcs_ref[tb:tb + QK_HALO, :] = dcs_ref[0:QK_HALO, :]

    vecl = pl.BlockSpec((1, LANES), lambda i: (0, 0))
    return _with_comm(
        compute, comm, "dn_bwd", (nt,), (p, p, p, p, p, p, convw, ps, pst, arow, dtb, acol, dtc, do, ss, tinv_all), specs,
        (jax.ShapeDtypeStruct((t, 3072), BF), jax.ShapeDtypeStruct((t, LANES), f32), jax.ShapeDtypeStruct((DN_CONV, 3072), f32),
         jax.ShapeDtypeStruct((1, LANES), f32), jax.ShapeDtypeStruct((1, LANES), f32)),
        (pl.BlockSpec((tb, 3072), lambda i: (ti(i), 0)), pl.BlockSpec((tb, LANES), lambda i: (ti(i), 0)),
         pl.BlockSpec((DN_CONV, 3072), lambda i: (0, 0)), vecl, vecl),
        [pltpu.VMEM((tb + QK_HALO, 3072), f32), pltpu.VMEM((tb, 3072), f32), pltpu.VMEM((nh, HD, HD), f32),
         pltpu.VMEM((DN_GROUP, 16, CHUNK), f32), pltpu.VMEM((tb, LANES), f32), pltpu.VMEM((tb, LANES), f32),
         pltpu.VMEM((tb + QK_HALO, 3072), f32), pltpu.VMEM((tb, LANES), f32), pltpu.VMEM((tb, LANES), f32)])


def _ln_fwd(x, g, b):
    mu = jnp.mean(x, axis=1, keepdims=True)
    xc = x - mu
    rstd = lax.rsqrt(jnp.mean(xc * xc, axis=1, keepdims=True) + LN_EPS)
    xhat = xc * rstd
    return xhat * g + b, xhat, rstd


def _ln_bwd(dy, xhat, rstd, g):
    dxh = dy * g
    return rstd * (dxh - jnp.mean(dxh, axis=1, keepdims=True) - xhat * jnp.mean(dxh * xhat, axis=1, keepdims=True))


def _br_specs(t, tb, rev):
    nt = t // tb
    ti = (lambda i: nt - 1 - i) if rev else (lambda i: i)
    hb = tb // CV_HALO
    col = lambda off: off // 512
    specs = [pl.BlockSpec((tb, 1024), lambda i: (ti(i), 0)),
             pl.BlockSpec((tb, 1024), lambda i: (ti(i), 0))]
    for off in (P_SGU, P_SGV, P_SGG, P_CVA, P_CVB, P_CVG):
        specs.append(pl.BlockSpec((tb, 512), lambda i, off=off: (ti(i), col(off))))
    for off in (P_CVA, P_CVB):
        specs.append(pl.BlockSpec((CV_HALO, 512), lambda i, off=off: (jnp.maximum(ti(i) * hb - 1, 0), col(off))))
    v512 = pl.BlockSpec((1, 512), lambda i: (0, 0))
    specs += [pl.BlockSpec((1, HD), lambda i: (0, 0)), v512, v512,
              pl.BlockSpec((SG_GROUPS, SG_BLOCK, SG_BLOCK), lambda i: (0, 0, 0)),
              pl.BlockSpec((SG_BLOCK, 512), lambda i: (0, 0)),
              pl.BlockSpec((CV_HALO, 512), lambda i: (0, 0)),
              v512, v512, v512]
    return specs, ti


def _sg_mask():
    r = lax.broadcasted_iota(jnp.int32, (SG_BLOCK, SG_BLOCK), 0) // CHUNK
    c = lax.broadcasted_iota(jnp.int32, (SG_BLOCK, SG_BLOCK), 1) // CHUNK
    return r >= c


def _cv_glu(first, a_ref, b_ref, ha_ref, hb_ref, gs_ref, tb):
    ha = jnp.where(first, 0.0, ha_ref[...].astype(f32))
    gs_ref[0:CV_HALO, :] = ha * _sigmoid(hb_ref[...].astype(f32))
    gs_ref[CV_HALO:CV_HALO + tb, :] = a_ref[...].astype(f32) * _sigmoid(b_ref[...].astype(f32))


SUBLANES = 8
CV_SHIFT_ROWS = CV_HALO - SUBLANES


def _cv_shift(src_ref, sh_ref, tb, cs):
    for b in range(1, SUBLANES):
        sh_ref[b - 1, 0:tb + CV_SHIFT_ROWS, cs] = src_ref[pl.ds(b, tb + CV_SHIFT_ROWS), cs]


def _cv_tap(src_ref, sh_ref, off, tb, cs):
    a, b = divmod(off, SUBLANES)
    if b == 0:
        return src_ref[pl.ds(SUBLANES * a, tb), cs]
    return sh_ref[b - 1, pl.ds(SUBLANES * a, tb), cs]


def _cv_conv(gs_ref, gsh_ref, cw_ref, cb_ref, tb, cs):
    acc = cb_ref[:, cs] + cw_ref[0:1, cs] * _cv_tap(gs_ref, gsh_ref, CV_HALO - (CV_K - 1), tb, cs)
    for j in range(1, CV_K):
        acc = acc + cw_ref[j:j + 1, cs] * _cv_tap(gs_ref, gsh_ref, CV_HALO - (CV_K - 1) + j, tb, cs)
    return acc


def _branch_fwd(o, p, dng, slg, slb, sgw, sgbias, cvw, cvb, clg, clb):
    t = o.shape[0]
    tb = min(t, 256)
    specs, _ = _br_specs(t, tb, False)

    def body(o_ref, z_ref, su_ref, sv_ref, sgt_ref, ca_ref, cb_ref, cg_ref, ha_ref, hb_ref,
             dng_ref, slg_ref, slb_ref, sgw_ref, sgb_ref, cvw_ref, cvb_ref, clg_ref, clb_ref, y_ref, gs_ref, vln_ref, gsh_ref):
        i = pl.program_id(0)
        for h in range(1024 // HD):
            cs = slice(h * HD, (h + 1) * HD)
            oh = o_ref[:, cs]
            r = lax.rsqrt(jnp.mean(oh * oh, axis=1, keepdims=True) + EPS)
            y_ref[:, cs] = ((oh * r) * dng_ref[...] * _silu(z_ref[:, cs].astype(f32))).astype(BF)
        mask = _sg_mask()
        for g in range(SG_GROUPS):
            cs = slice(g * HD, (g + 1) * HD)
            vg = _gelu(sv_ref[:, cs].astype(f32))
            vln, _, _ = _ln_fwd(vg, slg_ref[:, cs], slb_ref[:, cs])
            vln_ref[...] = vln
            ws = jnp.where(mask, sgw_ref[g], 0.0)
            for nb in range(tb // SG_BLOCK):
                rs = slice(nb * SG_BLOCK, (nb + 1) * SG_BLOCK)
                mixed = _dot(ws, vln_ref[rs, :]) + sgb_ref[:, cs]
                u = _gelu(su_ref[rs, cs].astype(f32))
                y_ref[rs, 1024 + g * HD:1024 + (g + 1) * HD] = (u * mixed * _silu(sgt_ref[rs, cs].astype(f32))).astype(BF)
        _cv_glu(i == 0, ca_ref, cb_ref, ha_ref, hb_ref, gs_ref, tb)
        for g in range(CV_GROUPS):
            cs = slice(g * HD, (g + 1) * HD)
            _cv_shift(gs_ref, gsh_ref, tb, cs)
            dw = _cv_conv(gs_ref, gsh_ref, cvw_ref, cvb_ref, tb, cs)
            ln, _, _ = _ln_fwd(dw, clg_ref[:, cs], clb_ref[:, cs])
            y_ref[:, 1536 + g * HD:1536 + (g + 1) * HD] = (_silu(ln) * _silu(cg_ref[:, cs].astype(f32))).astype(BF)

    return _pc(body, name="branch_fwd", grid=(t // tb,), out_shape=jax.ShapeDtypeStruct((t, 2048), BF),
               in_specs=specs, out_specs=pl.BlockSpec((tb, 2048), lambda i: (i, 0)),
               scratch_shapes=[pltpu.VMEM((tb + CV_HALO, 512), f32), pltpu.VMEM((tb, HD), f32),
                               pltpu.VMEM((SUBLANES - 1, tb + CV_SHIFT_ROWS, 512), f32)],
               compiler_params=_cp(("arbitrary",)))(o, p, p, p, p, p, p, p, p, p, dng, slg, slb, sgw, sgbias, cvw, cvb, clg, clb)


def _branch_bwd(o, p, dng, slg, slb, sgw, sgbias, cvw, cvb, clg, clb, dy):
    t = o.shape[0]
    tb = min(t, 256)
    nt = t // tb
    specs, ti = _br_specs(t, tb, True)
    specs = specs + [pl.BlockSpec((tb, 2048), lambda i: (ti(i), 0))]

    def body(o_ref, z_ref, su_ref, sv_ref, sgt_ref, ca_ref, cb_ref, cg_ref, ha_ref, hb_ref,
             dng_ref, slg_ref, slb_ref, sgw_ref, sgb_ref, cvw_ref, cvb_ref, clg_ref, clb_ref, dy_ref,
             dp_ref, do_ref, ddng_ref, dslg_ref, dslb_ref, dsgw_ref, dsgb_ref, dcvw_ref, dcvb_ref, dclg_ref, dclb_ref,
             gs_ref, vln_ref, xh_ref, dvl_ref, ddw_ref, dbias_ref, gsh_ref, dsh_ref):
        i = pl.program_id(0)

        @pl.when(i == 0)
        def _():
            for r in (ddng_ref, dslg_ref, dslb_ref, dsgw_ref, dsgb_ref, dcvw_ref, dcvb_ref, dclg_ref, dclb_ref, ddw_ref, dbias_ref):
                r[...] = jnp.zeros_like(r)

        for h in range(1024 // HD):
            cs = slice(h * HD, (h + 1) * HD)
            oh = o_ref[:, cs]
            zz = z_ref[:, cs].astype(f32)
            dyh = dy_ref[:, cs].astype(f32)
            r = lax.rsqrt(jnp.mean(oh * oh, axis=1, keepdims=True) + EPS)
            nrm = oh * r
            sz = _silu(zz)
            dn = dyh * dng_ref[...] * sz
            ddng_ref[...] += jnp.sum(dyh * nrm * sz, axis=0, keepdims=True)
            dp_ref[:, cs] = (dyh * nrm * dng_ref[...] * _dsilu(zz)).astype(BF)
            do_ref[:, cs] = r * (dn - nrm * jnp.mean(dn * nrm, axis=1, keepdims=True))
        mask = _sg_mask()
        for g in range(SG_GROUPS):
            cs = slice(g * HD, (g + 1) * HD)
            sv = sv_ref[:, cs].astype(f32)
            vln, xhat, rstd = _ln_fwd(_gelu(sv), slg_ref[:, cs], slb_ref[:, cs])
            vln_ref[...] = vln
            xh_ref[...] = xhat
            ws = jnp.where(mask, sgw_ref[g], 0.0)
            dws = jnp.zeros((SG_BLOCK, SG_BLOCK), f32)
            for nb in range(tb // SG_BLOCK):
                rs = slice(nb * SG_BLOCK, (nb + 1) * SG_BLOCK)
                dyb = dy_ref[rs, 1024 + g * HD:1024 + (g + 1) * HD].astype(f32)
                su = su_ref[rs, cs].astype(f32)
                gt = sgt_ref[rs, cs].astype(f32)
                mixed = _dot(ws, vln_ref[rs, :]) + sgb_ref[:, cs]
                u = _gelu(su)
                sgt = _silu(gt)
                dmixed = dyb * u * sgt
                dp_ref[rs, P_SGU + g * HD:P_SGU + (g + 1) * HD] = (dyb * mixed * sgt * _dgelu(su)).astype(BF)
                dp_ref[rs, P_SGG + g * HD:P_SGG + (g + 1) * HD] = (dyb * u * mixed * _dsilu(gt)).astype(BF)
                dvl_ref[rs, :] = _dot_tn(ws, dmixed)
                dws = dws + _dot_nt(dmixed, vln_ref[rs, :])
                dbias_ref[:, cs] += dmixed
            dsgw_ref[g] += jnp.where(mask, dws, 0.0)
            dvl = dvl_ref[...]
            xhat = xh_ref[...]
            dslg_ref[:, cs] += jnp.sum(dvl * xhat, axis=0, keepdims=True)
            dslb_ref[:, cs] += jnp.sum(dvl, axis=0, keepdims=True)
            dvg = _ln_bwd(dvl, xhat, rstd, slg_ref[:, cs])
            dp_ref[:, P_SGV + g * HD:P_SGV + (g + 1) * HD] = (dvg * _dgelu(sv)).astype(BF)
        _cv_glu(i == nt - 1, ca_ref, cb_ref, ha_ref, hb_ref, gs_ref, tb)
        for g in range(CV_GROUPS):
            cs = slice(g * HD, (g + 1) * HD)
            dyc = dy_ref[:, 1536 + g * HD:1536 + (g + 1) * HD].astype(f32)
            cg = cg_ref[:, cs].astype(f32)
            _cv_shift(gs_ref, gsh_ref, tb, cs)
            dw = _cv_conv(gs_ref, gsh_ref, cvw_ref, cvb_ref, tb, cs)
            ln, xhat, rstd = _ln_fwd(dw, clg_ref[:, cs], clb_ref[:, cs])
            dln = dyc * _silu(cg) * _dsilu(ln)
            dp_ref[:, P_CVG + g * HD:P_CVG + (g + 1) * HD] = (dyc * _silu(ln) * _dsilu(cg)).astype(BF)
            dclg_ref[:, cs] += jnp.sum(dln * xhat, axis=0, keepdims=True)
            dclb_ref[:, cs] += jnp.sum(dln, axis=0, keepdims=True)
            ddw = _ln_bwd(dln, xhat, rstd, clg_ref[:, cs])
            dcvb_ref[:, cs] += jnp.sum(ddw, axis=0, keepdims=True)
            ddw_ref[0:tb, cs] = ddw
            _cv_shift(ddw_ref, dsh_ref, tb, cs)
            dglu = cvw_ref[CV_K - 1:CV_K, cs] * ddw
            for j in range(CV_K - 1):
                dglu = dglu + cvw_ref[j:j + 1, cs] * _cv_tap(ddw_ref, dsh_ref, CV_K - 1 - j, tb, cs)
            for j in range(CV_K):
                tap = _cv_tap(gs_ref, gsh_ref, CV_HALO - (CV_K - 1) + j, tb, cs)
                dcvw_ref[j:j + 1, cs] += jnp.sum(ddw * tap, axis=0, keepdims=True)
            a = ca_ref[:, cs].astype(f32)
            sb = _sigmoid(cb_ref[:, cs].astype(f32))
            dp_ref[:, P_CVA + g * HD:P_CVA + (g + 1) * HD] = (dglu * sb).astype(BF)
            dp_ref[:, P_CVB + g * HD:P_CVB + (g + 1) * HD] = (dglu * a * sb * (1.0 - sb)).astype(BF)
        ddw_ref[tb:tb + CV_HALO, :] = ddw_ref[0:CV_HALO, :]

        @pl.when(i == nt - 1)
        def _():
            lane = lax.broadcasted_iota(jnp.int32, (SG_BLOCK, LANES), 1)
            acc = jnp.zeros((SG_BLOCK, LANES), f32)
            for g in range(SG_GROUPS):
                acc = jnp.where(lane == g, jnp.sum(dbias_ref[:, g * HD:(g + 1) * HD], axis=1, keepdims=True), acc)
            dsgb_ref[...] = acc

    v512 = pl.BlockSpec((1, 512), lambda i: (0, 0))
    return _pc(body, name="branch_bwd", grid=(nt,),
               out_shape=(jax.ShapeDtypeStruct((t, 4096), BF), jax.ShapeDtypeStruct((t, 1024), f32),
                          jax.ShapeDtypeStruct((1, HD), f32), jax.ShapeDtypeStruct((1, 512), f32), jax.ShapeDtypeStruct((1, 512), f32),
                          jax.ShapeDtypeStruct((SG_GROUPS, SG_BLOCK, SG_BLOCK), f32), jax.ShapeDtypeStruct((SG_BLOCK, LANES), f32),
                          jax.ShapeDtypeStruct((CV_HALO, 512), f32), jax.ShapeDtypeStruct((1, 512), f32),
                          jax.ShapeDtypeStruct((1, 512), f32), jax.ShapeDtypeStruct((1, 512), f32)),
               in_specs=specs,
               out_specs=(pl.BlockSpec((tb, 4096), lambda i: (ti(i), 0)), pl.BlockSpec((tb, 1024), lambda i: (ti(i), 0)),
                          pl.BlockSpec((1, HD), lambda i: (0, 0)), v512, v512,
                          pl.BlockSpec((SG_GROUPS, SG_BLOCK, SG_BLOCK), lambda i: (0, 0, 0)),
                          pl.BlockSpec((SG_BLOCK, LANES), lambda i: (0, 0)), pl.BlockSpec((CV_HALO, 512), lambda i: (0, 0)),
                          v512, v512, v512),
               scratch_shapes=[pltpu.VMEM((tb + CV_HALO, 512), f32), pltpu.VMEM((tb, HD), f32), pltpu.VMEM((tb, HD), f32),
                               pltpu.VMEM((tb, HD), f32), pltpu.VMEM((tb + CV_HALO, 512), f32), pltpu.VMEM((SG_BLOCK, 512), f32),
                               pltpu.VMEM((SUBLANES - 1, tb + CV_SHIFT_ROWS, 512), f32),
                               pltpu.VMEM((SUBLANES - 1, tb + CV_SHIFT_ROWS, 512), f32)],
               compiler_params=_cp(("arbitrary",)))(o, p, p, p, p, p, p, p, p, p, dng, slg, slb, sgw, sgbias, cvw, cvb, clg, clb, dy)


def _outproj(ycat, w, x, gate):
    t, d = x.shape
    tm = min(t, 512)

    def body(y_ref, w_ref, x_ref, g_ref, o_ref):
        o_ref[...] = x_ref[...] + g_ref[...] * jnp.dot(y_ref[...], w_ref[...], preferred_element_type=f32)

    return _pc(body, name="outproj", grid=(t // tm,), out_shape=jax.ShapeDtypeStruct((t, d), f32),
               in_specs=[pl.BlockSpec((tm, ycat.shape[1]), lambda i: (i, 0)), pl.BlockSpec(w.shape, lambda i: (0, 0)),
                         pl.BlockSpec((tm, d), lambda i: (i, 0)), pl.BlockSpec((1, d), lambda i: (0, 0))],
               out_specs=pl.BlockSpec((tm, d), lambda i: (i, 0)), compiler_params=_cp(("arbitrary",)))(ycat, w, x, gate)


def _outproj_bwd(dxo, w, gate):
    t, d = dxo.shape
    tm = min(t, 512)

    def body(d_ref, w_ref, g_ref, o_ref):
        o_ref[...] = _dot_nt(d_ref[...] * g_ref[...], w_ref[...]).astype(BF)

    return _pc(body, name="outproj_bwd", grid=(t // tm,), out_shape=jax.ShapeDtypeStruct((t, w.shape[0]), BF),
               in_specs=[pl.BlockSpec((tm, d), lambda i: (i, 0)), pl.BlockSpec(w.shape, lambda i: (0, 0)),
                         pl.BlockSpec((1, d), lambda i: (0, 0))],
               out_specs=pl.BlockSpec((tm, w.shape[0]), lambda i: (i, 0)), compiler_params=_cp(("arbitrary",)))(dxo, w, gate)


def _tn_acc(a, b, name):
    kk, m = a.shape
    n = b.shape[1]
    tk, tn = min(kk, 1024), min(n, 1024)

    def body(a_ref, b_ref, o_ref):
        @pl.when(pl.program_id(1) == 0)
        def _():
            o_ref[...] = jnp.zeros_like(o_ref)
        o_ref[...] += _dot_tn(a_ref[...], b_ref[...])

    return _pc(body, name=name, grid=(n // tn, kk // tk), out_shape=jax.ShapeDtypeStruct((m, n), f32),
               in_specs=[pl.BlockSpec((tk, m), lambda j, k: (k, 0)), pl.BlockSpec((tk, tn), lambda j, k: (k, j))],
               out_specs=pl.BlockSpec((m, tn), lambda j, k: (0, j)), compiler_params=_cp(("arbitrary", "arbitrary")))(a, b)


def _wout_grad(gmat, w, gate):
    m, n = gmat.shape
    tr = m // N_DEV

    def body(g_ref, w_ref, gt_ref, o_ref, dg_ref):
        @pl.when(pl.program_id(0) == 0)
        def _():
            dg_ref[...] = jnp.zeros_like(dg_ref)
        gm = g_ref[...]
        o_ref[0] = (gm * gt_ref[...]).astype(BF)
        dg_ref[...] += jnp.sum(gm * w_ref[...].astype(f32), axis=0, keepdims=True)

    return _pc(body, name="wout_grad", grid=(N_DEV,),
               out_shape=(jax.ShapeDtypeStruct((N_DEV, tr, n), BF), jax.ShapeDtypeStruct((1, n), f32)),
               in_specs=[pl.BlockSpec((tr, n), lambda i: (i, 0)), pl.BlockSpec((tr, n), lambda i: (i, 0)),
                         pl.BlockSpec((1, n), lambda i: (0, 0))],
               out_specs=(pl.BlockSpec((1, tr, n), lambda i: (i, 0, 0)), pl.BlockSpec((1, n), lambda i: (0, 0))),
               compiler_params=_cp(("arbitrary",)))(gmat, w, gate)


def _inproj_bwd(dpa, dpb, dps, wmain, wsmall, x, ng, sc1, dxo, comm=()):
    t, d = x.shape
    na, nb = dpa.shape[1] // 1024, dpb.shape[1] // 1024
    nk = na + nb
    tm, rb = min(t, 512), 128

    def body(a_ref, b_ref, s_ref, wm_ref, ws_ref, x_ref, ng_ref, sc_ref, dxo_ref, dx_ref, dsh_ref, dsc_ref, dng_ref, acc_ref):
        i, k = pl.program_id(0), pl.program_id(1)

        @pl.when((i == 0) & (k == 0))
        def _():
            dsh_ref[...] = jnp.zeros_like(dsh_ref)
            dsc_ref[...] = jnp.zeros_like(dsc_ref)
            dng_ref[...] = jnp.zeros_like(dng_ref)

        @pl.when(k == 0)
        def _():
            acc_ref[...] = _dot_nt(s_ref[...], ws_ref[...])

        @pl.when(k < na)
        def _():
            acc_ref[...] += _dot_nt(a_ref[...], wm_ref[...])

        @pl.when(k >= na)
        def _():
            acc_ref[...] += _dot_nt(b_ref[...], wm_ref[...])

        @pl.when(k == nk - 1)
        def _():
            def rows(j, carry):
                r0 = pl.multiple_of(j * rb, rb)
                rr = pl.ds(r0, rb)
                xv = x_ref[rr, :]
                dh = acc_ref[rr, :]
                r = lax.rsqrt(jnp.mean(xv * xv, axis=1, keepdims=True) + EPS)
                xn = xv * r
                dsh_ref[...] += jnp.sum(dh, axis=0, keepdims=True)
                dsc_ref[...] += jnp.sum(dh * (xn * ng_ref[...]), axis=0, keepdims=True)
                dng_ref[...] += jnp.sum(dh * sc_ref[...] * xn, axis=0, keepdims=True)
                dxn = dh * (ng_ref[...] * sc_ref[...])
                dx_ref[rr, :] = r * (dxn - xn * jnp.mean(dxn * xn, axis=1, keepdims=True)) + dxo_ref[rr, :]
                return carry
            lax.fori_loop(0, tm // rb, rows, 0)

    vec = pl.BlockSpec((1, d), lambda i, k: (0, 0))
    row = pl.BlockSpec((tm, d), lambda i, k: (i, 0))
    shp = jax.ShapeDtypeStruct((1, d), f32)
    return _with_comm(
        body, comm, "inproj_bwd", (t // tm, nk), (dpa, dpb, dps, wmain, wsmall, x, ng, sc1, dxo),
        [pl.BlockSpec((tm, 1024), lambda i, k: (i, jnp.minimum(k, na - 1))),
         pl.BlockSpec((tm, 1024), lambda i, k: (i, jnp.clip(k - na, 0, nb - 1))),
         pl.BlockSpec((tm, LANES), lambda i, k: (i, 0)),
         pl.BlockSpec((d, 1024), lambda i, k: (0, k)), pl.BlockSpec((d, LANES), lambda i, k: (0, 0)), row, vec, vec, row],
        (jax.ShapeDtypeStruct((t, d), f32), shp, shp, shp), (row, vec, vec, vec), [pltpu.VMEM((tm, d), f32)])


def _loss_head(x, fg, tgt):
    t, d = x.shape
    tm = min(t, 512)

    def body(x_ref, g_ref, t_ref, dx_ref, l_ref, dg_ref):
        @pl.when(pl.program_id(0) == 0)
        def _():
            l_ref[...] = jnp.zeros_like(l_ref)
            dg_ref[...] = jnp.zeros_like(dg_ref)
        xv = x_ref[...]
        r = lax.rsqrt(jnp.mean(xv * xv, axis=1, keepdims=True) + EPS)
        xn = xv * r
        err = xn * g_ref[...] - t_ref[...]
        l_ref[...] += 0.5 * jnp.sum(jnp.mean(err * err, axis=1, keepdims=True), axis=0, keepdims=True)
        dy = err * (1.0 / d)
        dg_ref[...] += jnp.sum(dy * xn, axis=0, keepdims=True)
        dxn = dy * g_ref[...]
        dx_ref[...] = r * (dxn - xn * jnp.mean(dxn * xn, axis=1, keepdims=True))

    row = pl.BlockSpec((tm, d), lambda i: (i, 0))
    vec = pl.BlockSpec((1, d), lambda i: (0, 0))
    return _pc(body, name="loss_head", grid=(t // tm,),
               out_shape=(jax.ShapeDtypeStruct((t, d), f32), jax.ShapeDtypeStruct((1, 1), f32), jax.ShapeDtypeStruct((1, d), f32)),
               in_specs=[row, vec, row], out_specs=(row, pl.BlockSpec((1, 1), lambda i: (0, 0)), vec),
               compiler_params=_cp(("arbitrary",)))(x, fg, tgt)


def _pack(arrs, rows_mult=8):
    flat = jnp.concatenate([a.reshape(-1).astype(f32) for a in arrs])
    n = flat.shape[0]
    per = rows_mult * LANES
    pad = (-n) % per
    if pad:
        flat = jnp.concatenate([flat, jnp.zeros((pad,), f32)])
    return flat.reshape(-1, LANES)


def _unpack(packed, shapes, lead=()):
    flat = packed.reshape(lead + (-1,))
    out, off = [], 0
    for s in shapes:
        n = 1
        for v in s:
            n *= v
        out.append(flat[..., off:off + n].reshape(lead + tuple(s)))
        off += n
    return out


def _pad_lanes(v, at):
    return jnp.pad(v.astype(f32), (at, LANES - at - v.shape[0])).reshape(1, LANES)


def kernel(x, c, norm_g, w_ada, b_ada, w_in, conv_qkv, a_log, dt_bias, dn_norm_g, sg_ln_g, sg_ln_b, sg_w, sg_b, cv_w, cv_b, cv_ln_g, cv_ln_b, w_out, final_g, loss_target, m_norm_g, m_w_ada, m_b_ada, m_w_in, m_conv_qkv, m_a_log, m_dt_bias, m_dn_norm_g, m_sg_ln_g, m_sg_ln_b, m_sg_w, m_sg_b, m_cv_w, m_cv_b, m_cv_ln_g, m_cv_ln_b, m_w_out, m_final_g, v_norm_g, v_w_ada, v_b_ada, v_w_in, v_conv_qkv, v_a_log, v_dt_bias, v_dn_norm_g, v_sg_ln_g, v_sg_ln_b, v_sg_w, v_sg_b, v_cv_w, v_cv_b, v_cv_ln_g, v_cv_ln_b, v_w_out, v_final_g):
    nl, d = norm_g.shape
    t = x.shape[1]
    nh = a_log.shape[1]
    xi, yi, ci = lax.axis_index("x"), lax.axis_index("y"), lax.axis_index("c")
    me = 4 * xi + 2 * yi + ci
    x0 = x[0]
    tgt = loss_target[0]
    ada_cols = w_ada.shape[2]
    in_cols = w_in.shape[2]
    cq_cols = conv_qkv.shape[2]
    cvw_cols = cv_w.shape[2]

    start_shapes = [c.shape, conv_qkv.shape, cv_w.shape]
    start_pack = _pack([c, conv_qkv, cv_w])
    wi_b, wo_b = w_in.astype(BF), w_out.astype(BF)
    g_small, g_win0 = _exchange([start_pack, wi_b[0]], ["gather2", "gather2"], "gather_start")
    c_all, cq_g, cvw_g = _unpack(g_small, start_shapes, lead=(N_DEV,))
    c_all = c_all.reshape(N_DEV, d)
    conv_full = jnp.moveaxis(cq_g, 0, 2).reshape(nl, DN_CONV, N_DEV * cq_cols)
    cvw_full = jnp.moveaxis(cvw_g, 0, 2).reshape(nl, CV_K, N_DEV * cvw_cols)
    w_packed = [_repack_w(g_win0)]
    w_out_full = []

    b_my = lax.dynamic_slice_in_dim(b_ada, me * ada_cols, ada_cols, axis=1).reshape(nl, 1, ada_cols)
    mod_part = _ada_fwd(c_all, w_ada, b_my)
    (mod_g,) = _exchange([mod_part], ["gather"], "gather_mod")
    mod_mine = lax.dynamic_index_in_dim(mod_g, me, axis=2, keepdims=False)
    mod = jnp.moveaxis(mod_mine, 0, 1).reshape(nl, N_DEV * ada_cols)
    shift, scale, gate = mod[:, :d], mod[:, d:2 * d], mod[:, 2 * d:]

    arow = [_pad_lanes(-jnp.exp(a_log[l]), 8) for l in range(nl)]
    dtb = [_pad_lanes(dt_bias[l], 8) for l in range(nl)]
    saved = []
    xl = x0
    for l in range(nl):
        sc1 = (1.0 + scale[l]).reshape(1, d)
        w_main_l, w_small_l = w_packed[l]
        p, ps, hb = _inproj(xl, norm_g[l].reshape(1, d), sc1, shift[l].reshape(1, d), w_main_l, w_small_l)
        pst = ps[:, :16].reshape(t // CHUNK, CHUNK, 16).transpose(0, 2, 1)
        acol = jnp.broadcast_to(arow[l][0, :16].reshape(16, 1), (16, CHUNK))
        dtc = jnp.broadcast_to(dtb[l][0, :16].reshape(16, 1), (16, CHUNK))
        ahead = (wo_b[l], wi_b[l + 1]) if l + 1 < nl else (wo_b[l],)
        o, ss, tinv, g_wout, *g_win_n = _dn_fwd(p, ps, pst, conv_full[l], arow[l], dtb[l], acol, dtc,
                                                comm=(ahead, ("gather2",) * len(ahead)))
        w_out_l = g_wout.reshape(N_DEV * w_out.shape[1], w_out.shape[2])
        w_out_full.append(w_out_l)
        if g_win_n:
            w_packed.append(_repack_w(g_win_n[0]))
        sgbias = jnp.repeat(sg_b[l].T, HD, axis=1)
        cvw_pad = jnp.pad(cvw_full[l], ((0, CV_HALO - CV_K), (0, 0)))
        br_par = (dn_norm_g[l].reshape(1, HD), sg_ln_g[l].reshape(1, -1), sg_ln_b[l].reshape(1, -1), sg_w[l], sgbias, cvw_pad,
                  cv_b[l].reshape(1, -1), cv_ln_g[l].reshape(1, -1), cv_ln_b[l].reshape(1, -1))
        ycat = _branch_fwd(o, p, *br_par)
        xn = _outproj(ycat, w_out_l, xl, gate[l].reshape(1, d))
        saved.append((xl, p, ps, pst, hb, o, ss, tinv, ycat, sc1, acol, dtc, br_par))
        xl = xn

    dx, loss_part, dfinal_g = _loss_head(xl, final_g.reshape(1, d), tgt)
    loss = lax.psum(loss_part[0, 0], ("x", "y", "c"))

    small_grads = [None] * nl
    dmods = [None] * nl
    win_r = [None] * nl
    wout_r = [None] * nl
    rep_names = ["b_ada", "norm_g", "a_log", "dt_bias", "dn_norm_g", "sg_ln_g", "sg_ln_b", "sg_w", "sg_b", "cv_b", "cv_ln_g",
                 "cv_ln_b", "final_g"]
    win_slab = None
    for l in reversed(range(nl)):
        xl, p, ps, pst, hb, o, ss, tinv, ycat, sc1, acol, dtc, br_par = saved[l]
        w_main_l, w_small_l = w_packed[l]
        w_out_l = w_out_full[l]
        gate_l = gate[l].reshape(1, d)
        dycat = _outproj_bwd(dx, w_out_l, gate_l)
        gmat = _tn_acc(ycat, dx, "wout_tn")
        wout_slab, dgate = _wout_grad(gmat, w_out_l, gate_l)
        dpa, do, ddng, dslg, dslb, dsgw, dsgb, dcvw, dcvb, dclg, dclb = _branch_bwd(o, p, *br_par, dycat)
        behind = (wout_slab,) if win_slab is None else (wout_slab, win_slab)
        dpb, dps, dconvw, dal, ddt, wout_r[l], *win_got = _dn_bwd(
            p, ps, pst, conv_full[l], arow[l], dtb[l], acol, dtc, do, ss, tinv, comm=(behind, ("scatter",) * len(behind)))
        if win_got:
            win_r[l + 1] = win_got[0]
        gwa = _tn_acc(hb, dpa, "win_tn_a")
        gwb = _tn_acc(hb, dpb, "win_tn_b")
        gws = _tn_acc(hb, dps, "win_tn_s")
        win_slab = _grad_slabs(gwa, gwb, gws, in_cols)
        small_grads[l] = [None, dal[0, 8:8 + nh], ddt[0, 8:8 + nh], ddng[0], dslg[0], dslb[0], dsgw, dsgb[:, :SG_GROUPS].T,
                          dcvb[0], dclg[0], dclb[0], dconvw, dcvw[:CV_K]]
        ib_args = (dpa, dpb, dps, w_main_l, w_small_l, xl, norm_g[l].reshape(1, d), sc1, dx)
        if l > 0:
            dx, dshift, dscale, dng = _inproj_bwd(*ib_args)
            dmods[l] = jnp.concatenate([dshift[0], dscale[0], dgate[0]])
            small_grads[l][0] = dng[0]
        else:
            zeros_d = jnp.zeros((d,), f32)
            dmods[0] = jnp.concatenate([zeros_d, zeros_d, dgate[0]])
            small_grads[0][0] = zeros_d
            stack = lambda j: jnp.stack([small_grads[i][j] for i in range(nl)])
            rep_grads = [jnp.stack(dmods)] + [stack(j) for j in range(11)] + [dfinal_g[0]]
            full_grads = [stack(11), stack(12)]
            grad_shapes = [g.shape for g in rep_grads + full_grads]
            gpack = _pack(rep_grads + full_grads, 1024)
            dx, dshift, dscale, dng, win_r[0], gpack_g = _inproj_bwd(*ib_args, comm=((win_slab, gpack), ("scatter", "gather2")))
            late_shapes = [(d,), (d,), (d,)]
            (late_g,) = _exchange([_pack([dshift[0], dscale[0], dng[0]])], ["gather"], "exchange_late")

    gsum = _sum_slabs(gpack_g, "sum_small")
    gl = _unpack(gsum, grad_shapes)
    late = _unpack(_sum_slabs(late_g, "sum_late"), late_shapes)
    gl[0] = jnp.concatenate([jnp.concatenate([late[0], late[1], gl[0][0, 2 * d:]])[None], gl[0][1:]], axis=0)
    gl[1] = jnp.concatenate([late[2][None], gl[1][1:]], axis=0)
    g_rep = dict(zip(rep_names, gl[:len(rep_names)]))
    g_conv = lax.dynamic_slice_in_dim(gl[-2], me * cq_cols, cq_cols, axis=2)
    g_cvw = lax.dynamic_slice_in_dim(gl[-1], me * cvw_cols, cvw_cols, axis=2)

    g_win, d_win, nm_win, nv_win = _adam_layers(win_r, w_in, m_w_in, v_w_in, "adam_w_in", 128)
    g_wo, d_wo, nm_wo, nv_wo = _adam_layers(wout_r, w_out, m_w_out, v_w_out, "adam_w_out", 64)
    dmod_all = _unpack(gpack_g, grad_shapes, lead=(N_DEV,))[0]
    late_all = _unpack(late_g, late_shapes, lead=(N_DEV,))
    dmod_all = jnp.concatenate([jnp.concatenate([late_all[0], late_all[1], dmod_all[:, 0, 2 * d:]], axis=1)[:, None],
                                dmod_all[:, 1:]], axis=1)
    dmod_my = jnp.moveaxis(lax.dynamic_slice_in_dim(dmod_all, me * ada_cols, ada_cols, axis=2), 0, 1)
    g_wa, d_wa, nm_wa, nv_wa = _ada_bwd_adam(c_all, dmod_my, w_ada, m_w_ada, v_w_ada)

    small_w = dict(b_ada=b_ada, norm_g=norm_g, a_log=a_log, dt_bias=dt_bias, dn_norm_g=dn_norm_g, sg_ln_g=sg_ln_g, sg_ln_b=sg_ln_b,
                   sg_w=sg_w, sg_b=sg_b, cv_b=cv_b, cv_ln_g=cv_ln_g, cv_ln_b=cv_ln_b, final_g=final_g, conv_qkv=conv_qkv, cv_w=cv_w)
    small_m = dict(b_ada=m_b_ada, norm_g=m_norm_g, a_log=m_a_log, dt_bias=m_dt_bias, dn_norm_g=m_dn_norm_g, sg_ln_g=m_sg_ln_g,
                   sg_ln_b=m_sg_ln_b, sg_w=m_sg_w, sg_b=m_sg_b, cv_b=m_cv_b, cv_ln_g=m_cv_ln_g, cv_ln_b=m_cv_ln_b,
                   final_g=m_final_g, conv_qkv=m_conv_qkv, cv_w=m_cv_w)
    small_v = dict(b_ada=v_b_ada, norm_g=v_norm_g, a_log=v_a_log, dt_bias=v_dt_bias, dn_norm_g=v_dn_norm_g, sg_ln_g=v_sg_ln_g,
                   sg_ln_b=v_sg_ln_b, sg_w=v_sg_w, sg_b=v_sg_b, cv_b=v_cv_b, cv_ln_g=v_cv_ln_g, cv_ln_b=v_cv_ln_b,
                   final_g=v_final_g, conv_qkv=v_conv_qkv, cv_w=v_cv_w)
    small_g = dict(g_rep, conv_qkv=g_conv, cv_w=g_cvw)
    names = rep_names + ["conv_qkv", "cv_w"]
    shapes = [small_w[n].shape for n in names]
    gp = _pack([small_g[n] for n in names], 1024)
    sg_, sd_, sm_, sv_ = _adam_slabs(gp.reshape((1,) + gp.shape), _pack([small_w[n] for n in names], 1024),
                                     _pack([small_m[n] for n in names], 1024), _pack([small_v[n] for n in names], 1024),
                                     "adam_small", 1024)
    sgrad = dict(zip(names, _unpack(sg_, shapes)))
    sdelta = dict(zip(names, _unpack(sd_, shapes)))
    snm = dict(zip(names, _unpack(sm_, shapes)))
    snv = dict(zip(names, _unpack(sv_, shapes)))
    for dct, big in ((sgrad, (g_wa, g_win, g_wo)), (sdelta, (d_wa, d_win, d_wo)), (snm, (nm_wa, nm_win, nm_wo)), (snv, (nv_wa, nv_win, nv_wo))):
        dct["w_ada"] = big[0]
        dct["w_in"] = big[1].reshape(w_in.shape)
        dct["w_out"] = big[2].reshape(w_out.shape)

    order = ["norm_g", "w_ada", "b_ada", "w_in", "conv_qkv", "a_log", "dt_bias", "dn_norm_g", "sg_ln_g", "sg_ln_b", "sg_w", "sg_b",
             "cv_w", "cv_b", "cv_ln_g", "cv_ln_b", "w_out", "final_g"]
    outs = [loss, dx.reshape(x.shape)]
    for dct in (sgrad, sdelta, snm, snv):
        outs += [dct[n] for n in order]
    return tuple(outs)
```

```python
import functools

import jax
import jax.numpy as jnp
from jax import lax
from jax.experimental import pallas as pl
from jax.experimental.pallas import tpu as pltpu

f32 = jnp.float32
BF = jnp.bfloat16
N_DEV = 8
LANES = 128
CHUNK = 64
HD = 128
DN_CONV = 4
SG_BLOCK = 128
SG_GROUPS = 4
CV_GROUPS = 4
CV_K = 31
CV_HALO = 32
QK_HALO = 16
EPS = 1e-6
LN_EPS = 1e-5
ADAM_LR, ADAM_B1, ADAM_B2, ADAM_EPS, ADAM_WD, ADAM_STEP = 0.001, 0.9, 0.999, 1e-08, 0.01, 10
VMEM_LIMIT = 56 * 1024 * 1024
HI = lax.Precision.HIGHEST
MESH = pl.DeviceIdType.MESH


def _cp(sem=None, vmem=VMEM_LIMIT):
    return pltpu.CompilerParams(dimension_semantics=sem, vmem_limit_bytes=vmem)


def _pc(body, **kw):
    return pl.pallas_call(body, **kw)


def _dot(a, b):
    return jnp.dot(a.astype(BF), b.astype(BF), preferred_element_type=f32)


def _dot_nt(a, b):
    return lax.dot_general(a.astype(BF), b.astype(BF), (((1,), (1,)), ((), ())), preferred_element_type=f32)


def _dot_tn(a, b):
    return lax.dot_general(a.astype(BF), b.astype(BF), (((0,), (0,)), ((), ())), preferred_element_type=f32)


def _dot_hi(a, b):
    return jnp.dot(a, b, preferred_element_type=f32, precision=HI)


def _sigmoid(x):
    return 1.0 / (1.0 + jnp.exp(-x))


def _silu(x):
    return x * _sigmoid(x)


def _dsilu(x):
    s = _sigmoid(x)
    return s * (1.0 + x * (1.0 - s))


def _softplus(x):
    return jnp.maximum(x, 0.0) + jnp.log(1.0 + jnp.exp(-jnp.abs(x)))


def _gelu(x):
    return 0.5 * x * (1.0 + lax.erf(x * 0.7071067811865476))


def _dgelu(x):
    return 0.5 * (1.0 + lax.erf(x * 0.7071067811865476)) + x * jnp.exp(-0.5 * x * x) * 0.3989422804014327


def _lanecol(x, lane_iota, j):
    return jnp.sum(jnp.where(lane_iota == j, x, 0.0), axis=1, keepdims=True)


def _split(a):
    hi = a.astype(BF)
    return hi, (a - hi.astype(f32)).astype(BF)


def _dot3(a, b):
    mm = lambda u, v: jnp.dot(u, v, preferred_element_type=f32)
    return mm(a[0], b[0]) + (mm(a[0], b[1]) + mm(a[1], b[0]))


INV_BASE = 8
DN_GROUP = 4
DN_GROUP_BWD = 4


def _tri_inv(l_mats):
    n = l_mats[0].shape[0]
    r = lax.broadcasted_iota(jnp.int32, (n, n), 0)
    c = lax.broadcasted_iota(jnp.int32, (n, n), 1)
    same = lambda size: (r // size) == (c // size)
    diag = [jnp.where(same(INV_BASE), l, 0.0) for l in l_mats]
    xs = [(r == c).astype(f32) - l for l in diag]
    sp = [_split(l) for l in diag]
    size = 2
    while size < INV_BASE:
        ps = [_dot3(s, s) for s in sp]
        sp = [_split(p) for p in ps]
        sx = [_split(x) for x in xs]
        xs = [x + _dot3(a, b) for x, a, b in zip(xs, sx, sp)]
        size *= 2
    blk = INV_BASE
    while blk < n:
        join = same(2 * blk) & jnp.logical_not(same(blk))
        lo = [_split(jnp.where(join, l, 0.0)) for l in l_mats]
        sx = [_split(x) for x in xs]
        ys = [_split(_dot3(a, b)) for a, b in zip(lo, sx)]
        xs = [x - _dot3(a, b) for x, a, b in zip(xs, sx, ys)]
        blk *= 2
    return xs


def _me():
    x, y, c = lax.axis_index("x"), lax.axis_index("y"), lax.axis_index("c")
    return x, y, c, 4 * x + 2 * y + c


def _exchange(arrs, kinds, name):
    n = len(arrs)

    def body(*refs):
        for phase in ("first", "middle", "last"):
            _comm_phase(refs[:n], refs[n:2 * n], kinds, *refs[2 * n:], phase)

    any_spec = pl.BlockSpec(memory_space=pl.ANY)
    return _pc(body, name=name, out_shape=_comm_out_shapes(arrs, kinds), in_specs=[any_spec] * n, out_specs=tuple([any_spec] * n),
               scratch_shapes=_comm_sems(n))(*arrs)


def _comm_out_shapes(arrs, kinds):
    return tuple(jax.ShapeDtypeStruct(a.shape if k == "scatter" else (N_DEV,) + a.shape, a.dtype) for a, k in zip(arrs, kinds))


def _comm_sems(n):
    return [pltpu.SemaphoreType.DMA((n, N_DEV - 1)), pltpu.SemaphoreType.DMA((n, N_DEV - 1)), pltpu.SemaphoreType.DMA((n,))]


def _comm_phase(ins, outs, kinds, send, recv, loc, phase):
    x, y, c, me = _me()
    flips = ((1, 0), (0, 1), (1, 1))
    chips = [(jnp.bitwise_xor(x, fx), jnp.bitwise_xor(y, fy)) for fx, fy in flips]
    for a in range(len(ins)):
        slot = lambda px, py, pc, a=a: outs[a].at[4 * px + 2 * py + pc]

        def copy(k, src, block, to, a=a, slot=slot):
            return pltpu.make_async_remote_copy(src_ref=src, dst_ref=slot(*block), send_sem=send.at[a, k], recv_sem=recv.at[a, k],
                                                device_id=to, device_id_type=MESH)

        if kinds[a] == "gather2":
            mine = pltpu.make_async_copy(ins[a], slot(x, y, c), loc.at[a])
            first = [copy(0, ins[a], (x, y, c), (x, y, 1 - c))] + [copy(1 + j, ins[a], (x, y, c), (px, py, c))
                                                                  for j, (px, py) in enumerate(chips)]
            passed = [copy(4 + j, slot(px, py, c), (px, py, c), (x, y, 1 - c)) for j, (px, py) in enumerate(chips)]
            if phase == "first":
                for cp in [mine] + first:
                    cp.start()
            elif phase == "middle":
                for j, (px, py) in enumerate(chips):
                    copy(1 + j, ins[a], (px, py, c), (x, y, c)).wait_recv()
                    passed[j].start()
            else:
                copy(0, ins[a], (x, y, 1 - c), (x, y, c)).wait_recv()
                for j, (px, py) in enumerate(chips):
                    copy(4 + j, ins[a], (px, py, 1 - c), (x, y, c)).wait_recv()
                for cp in first + passed:
                    cp.wait_send()
                mine.wait()
        else:
            own = ins[a].at[me] if kinds[a] == "scatter" else ins[a]
            copies = [pltpu.make_async_copy(own, outs[a].at[me], loc.at[a])]
            for k in range(1, N_DEV):
                px = jnp.bitwise_xor(x, (k >> 2) & 1)
                py = jnp.bitwise_xor(y, (k >> 1) & 1)
                pc = jnp.bitwise_xor(c, k & 1)
                src = ins[a].at[4 * px + 2 * py + pc] if kinds[a] == "scatter" else ins[a]
                copies.append(pltpu.make_async_remote_copy(src_ref=src, dst_ref=outs[a].at[me], send_sem=send.at[a, k - 1],
                                                           recv_sem=recv.at[a, k - 1], device_id=(px, py, pc), device_id_type=MESH))
            if phase == "first":
                for cp in copies:
                    cp.start()
            elif phase == "last":
                for cp in copies:
                    cp.wait()


def _with_comm(compute, comm, name, grid, args, in_specs, out_shape, out_specs, scratch):
    arrs, kinds = comm if comm else ((), ())
    n, n_in, n_out, n_scr = len(arrs), len(args), len(out_shape), len(scratch)
    marks = {"first": [0] * len(grid), "middle": [(3 * grid[0]) // 4] + [0] * (len(grid) - 1), "last": [g - 1 for g in grid]}

    def at_step(which):
        ok = pl.program_id(0) == marks[which][0]
        for ax in range(1, len(grid)):
            ok = ok & (pl.program_id(ax) == marks[which][ax])
        return ok

    def body(*refs):
        ins, cin = refs[:n_in], refs[n_in:n_in + n]
        outs, cout = refs[n_in + n:n_in + n + n_out], refs[n_in + n + n_out:n_in + 2 * n + n_out]
        scr, sems = refs[n_in + 2 * n + n_out:n_in + 2 * n + n_out + n_scr], refs[n_in + 2 * n + n_out + n_scr:]
        for phase in ("first", "middle") if n else ():
            @pl.when(at_step(phase))
            def _(phase=phase):
                _comm_phase(cin, cout, kinds, *sems, phase)
        compute(*ins, *outs, *scr)
        if n:
            @pl.when(at_step("last"))
            def _():
                _comm_phase(cin, cout, kinds, *sems, "last")

    any_spec = pl.BlockSpec(memory_space=pl.ANY)
    return _pc(body, name=name + ("_comm" if n else ""), grid=tuple(grid),
               out_shape=tuple(out_shape) + (_comm_out_shapes(arrs, kinds) if n else ()),
               in_specs=list(in_specs) + [any_spec] * n, out_specs=tuple(out_specs) + (any_spec,) * n,
               scratch_shapes=list(scratch) + (_comm_sems(n) if n else []),
               compiler_params=_cp(("arbitrary",) * len(grid)))(*args, *arrs)


def _ada_fwd(c_all, w_ada, b_my):
    nl, d, cols = w_ada.shape

    def body(c_ref, w_ref, b_ref, o_ref):
        ca = _silu(c_ref[...])
        o_ref[0] = _dot(ca, w_ref[0]) + b_ref[0]

    return _pc(body, name="ada_fwd", grid=(nl,), out_shape=jax.ShapeDtypeStruct((nl, N_DEV, cols), f32),
               in_specs=[pl.BlockSpec((N_DEV, d), lambda l: (0, 0)), pl.BlockSpec((1, d, cols), lambda l: (l, 0, 0)),
                         pl.BlockSpec((1, 1, cols), lambda l: (l, 0, 0))],
               out_specs=pl.BlockSpec((1, N_DEV, cols), lambda l: (l, 0, 0)), compiler_params=_cp(("arbitrary",)))(c_all, w_ada, b_my)


def _adam_math(w, g, m, v):
    m = ADAM_B1 * m + (1.0 - ADAM_B1) * g
    v = ADAM_B2 * v + (1.0 - ADAM_B2) * (g * g)
    m_hat = m / (1.0 - ADAM_B1 ** ADAM_STEP)
    v_hat = v / (1.0 - ADAM_B2 ** ADAM_STEP)
    delta = -ADAM_LR * (m_hat / (jnp.sqrt(v_hat) + ADAM_EPS) + ADAM_WD * w)
    return delta, m, v


def _ada_bwd_adam(c_all, dmod_my, w, m, v):
    nl, d, cols = w.shape
    tk = 512

    def body(c_ref, dm_ref, w_ref, m_ref, v_ref, g_ref, dl_ref, nm_ref, nv_ref):
        ca = _silu(c_ref[...])
        g = _dot_tn(ca, dm_ref[0])
        dl, nm, nv = _adam_math(w_ref[0], g, m_ref[0], v_ref[0])
        g_ref[0] = g
        dl_ref[0] = dl
        nm_ref[0] = nm
        nv_ref[0] = nv

    wspec = pl.BlockSpec((1, tk, cols), lambda l, k: (l, k, 0))
    shp = jax.ShapeDtypeStruct(w.shape, f32)
    return _pc(body, name="ada_bwd_adam", grid=(nl, d // tk), out_shape=(shp, shp, shp, shp),
               in_specs=[pl.BlockSpec((N_DEV, tk), lambda l, k: (0, k)), pl.BlockSpec((1, N_DEV, cols), lambda l, k: (l, 0, 0)),
                         wspec, wspec, wspec],
               out_specs=(wspec, wspec, wspec, wspec), compiler_params=_cp(("arbitrary", "arbitrary")))(c_all, dmod_my, w, m, v)


def _adam_slabs(slabs, w, m, v, name, tr):
    n, rows, cols = slabs.shape
    tr = min(tr, rows)

    def body(s_ref, w_ref, m_ref, v_ref, g_ref, dl_ref, nm_ref, nv_ref):
        g = s_ref[0].astype(f32)
        for j in range(1, n):
            g = g + s_ref[j].astype(f32)
        dl, nm, nv = _adam_math(w_ref[...], g, m_ref[...], v_ref[...])
        g_ref[...] = g
        dl_ref[...] = dl
        nm_ref[...] = nm
        nv_ref[...] = nv

    spec = pl.BlockSpec((tr, cols), lambda i: (i, 0))
    shp = jax.ShapeDtypeStruct((rows, cols), f32)
    return _pc(body, name=name, grid=(rows // tr,), out_shape=(shp, shp, shp, shp),
               in_specs=[pl.BlockSpec((n, tr, cols), lambda i: (0, i, 0)), spec, spec, spec],
               out_specs=(spec, spec, spec, spec), compiler_params=_cp(("arbitrary",)))(slabs, w, m, v)


def _adam_layers(slabs, w, m, v, name, tr):
    nl, rows, cols = w.shape
    n = slabs[0].shape[0]
    tr = min(tr, rows)

    def body(*refs):
        s_refs, (w_ref, m_ref, v_ref, g_ref, dl_ref, nm_ref, nv_ref) = refs[:nl], refs[nl:]
        for l in range(nl):
            @pl.when(pl.program_id(0) == l)
            def _(l=l):
                g = s_refs[l][0].astype(f32)
                for j in range(1, n):
                    g = g + s_refs[l][j].astype(f32)
                dl, nm, nv = _adam_math(w_ref[0], g, m_ref[0], v_ref[0])
                g_ref[0] = g
                dl_ref[0] = dl
                nm_ref[0] = nm
                nv_ref[0] = nv

    spec = pl.BlockSpec((1, tr, cols), lambda l, i: (l, i, 0))
    s_specs = [pl.BlockSpec((n, tr, cols), lambda l, i, j=j: (0, jnp.where(l == j, i, 0), 0)) for j in range(nl)]
    shp = jax.ShapeDtypeStruct(w.shape, f32)
    return _pc(body, name=name, grid=(nl, rows // tr), out_shape=(shp, shp, shp, shp),
               in_specs=s_specs + [spec, spec, spec], out_specs=(spec, spec, spec, spec),
               compiler_params=_cp(("arbitrary", "arbitrary")))(*slabs, w, m, v)


def _sum_slabs(slabs, name):
    n, rows, cols = slabs.shape
    tr = min(rows, 1024)

    def body(s_ref, o_ref):
        g = s_ref[0]
        for j in range(1, n):
            g = g + s_ref[j]
        o_ref[...] = g

    return _pc(body, name=name, grid=(rows // tr,), out_shape=jax.ShapeDtypeStruct((rows, cols), f32),
               in_specs=[pl.BlockSpec((n, tr, cols), lambda i: (0, i, 0))], out_specs=pl.BlockSpec((tr, cols), lambda i: (i, 0)),
               compiler_params=_cp(("arbitrary",)))(slabs)


NAT_Z, NAT_SMALL, NAT_REST = 3072, 4096, 4112


def _repack_w(g_win):
    n, d, cols = g_win.shape
    tr = 256

    def body(g_ref, wm_ref, ws_ref):
        nat = jnp.concatenate([g_ref[j] for j in range(n)], axis=1)
        wm_ref[...] = jnp.concatenate([nat[:, NAT_Z:NAT_SMALL], nat[:, NAT_REST:], nat[:, :NAT_Z]], axis=1)
        ws_ref[...] = jnp.concatenate([nat[:, NAT_SMALL:NAT_REST], jnp.zeros((tr, LANES - 16), nat.dtype)], axis=1)

    return _pc(body, name="repack_w", grid=(d // tr,),
               out_shape=(jax.ShapeDtypeStruct((d, n * cols - 16), g_win.dtype), jax.ShapeDtypeStruct((d, LANES), g_win.dtype)),
               in_specs=[pl.BlockSpec((n, tr, cols), lambda i: (0, i, 0))],
               out_specs=(pl.BlockSpec((tr, n * cols - 16), lambda i: (i, 0)), pl.BlockSpec((tr, LANES), lambda i: (i, 0))),
               compiler_params=_cp(("arbitrary",)))(g_win)


def _grad_slabs(gwa, gwb, gws, cols):
    d = gwa.shape[0]
    tr = 256

    def body(a_ref, b_ref, s_ref, o_ref):
        a = a_ref[...]
        nat = jnp.concatenate([b_ref[...], a[:, :NAT_SMALL - NAT_Z], s_ref[:, 0:16], a[:, NAT_SMALL - NAT_Z:]], axis=1)
        for j in range(N_DEV):
            o_ref[j] = nat[:, j * cols:(j + 1) * cols].astype(BF)

    return _pc(body, name="grad_slabs", grid=(d // tr,), out_shape=jax.ShapeDtypeStruct((N_DEV, d, cols), BF),
               in_specs=[pl.BlockSpec((tr, gwa.shape[1]), lambda i: (i, 0)), pl.BlockSpec((tr, gwb.shape[1]), lambda i: (i, 0)),
                         pl.BlockSpec((tr, LANES), lambda i: (i, 0))],
               out_specs=pl.BlockSpec((N_DEV, tr, cols), lambda i: (0, i, 0)), compiler_params=_cp(("arbitrary",)))(gwa, gwb, gws)


def _inproj(x, ng, sc1, sh, wmain, wsmall):
    t, d = x.shape
    n = wmain.shape[1]
    tm, tn, rb = min(t, 1024), 1024, 128

    def body(x_ref, ng_ref, sc_ref, sh_ref, wm_ref, ws_ref, p_ref, ps_ref, h_ref, hs_ref):
        @pl.when(pl.program_id(1) == 0)
        def _():
            def rows(i, carry):
                r0 = pl.multiple_of(i * rb, rb)
                xv = x_ref[pl.ds(r0, rb), :]
                r = lax.rsqrt(jnp.mean(xv * xv, axis=1, keepdims=True) + EPS)
                hb = (((xv * r) * ng_ref[...]) * sc_ref[...] + sh_ref[...]).astype(BF)
                hs_ref[pl.ds(r0, rb), :] = hb
                h_ref[pl.ds(r0, rb), :] = hb
                return carry
            lax.fori_loop(0, tm // rb, rows, 0)
            ps_ref[...] = jnp.dot(hs_ref[...], ws_ref[...], preferred_element_type=f32)
        p_ref[...] = jnp.dot(hs_ref[...], wm_ref[...], preferred_element_type=f32).astype(BF)

    vec = pl.BlockSpec((1, d), lambda i, j: (0, 0))
    return _pc(body, name="inproj", grid=(t // tm, n // tn),
               out_shape=(jax.ShapeDtypeStruct((t, n), BF), jax.ShapeDtypeStruct((t, LANES), f32), jax.ShapeDtypeStruct((t, d), BF)),
               in_specs=[pl.BlockSpec((tm, d), lambda i, j: (i, 0)), vec, vec, vec,
                         pl.BlockSpec((d, tn), lambda i, j: (0, j)), pl.BlockSpec((d, LANES), lambda i, j: (0, 0))],
               out_specs=(pl.BlockSpec((tm, tn), lambda i, j: (i, j)), pl.BlockSpec((tm, LANES), lambda i, j: (i, 0)),
                          pl.BlockSpec((tm, d), lambda i, j: (i, 0))),
               scratch_shapes=[pltpu.VMEM((tm, d), BF)], compiler_params=_cp(("arbitrary", "arbitrary")))(x, ng, sc1, sh, wmain, wsmall)


P_Z, P_SGU, P_SGV, P_SGG, P_CVA, P_CVB, P_CVG, P_Q = 0, 1024, 1536, 2048, 2560, 3072, 3584, 4096


def _dn_specs(t, tb, rev):
    nt = t // tb
    ti = (lambda i: nt - 1 - i) if rev else (lambda i: i)
    qb = P_Q // 1024
    hb = tb // QK_HALO
    specs = []
    for s in range(3):
        specs.append(pl.BlockSpec((tb, 1024), lambda i, s=s: (ti(i), qb + s)))
    for s in range(3):
        specs.append(pl.BlockSpec((QK_HALO, 1024), lambda i, s=s: (jnp.maximum(ti(i) * hb - 1, 0), qb + s)))
    specs.append(pl.BlockSpec((DN_CONV, 3072), lambda i: (0, 0)))
    specs.append(pl.BlockSpec((tb, LANES), lambda i: (ti(i), 0)))
    specs.append(pl.BlockSpec((tb // CHUNK, 16, CHUNK), lambda i: (ti(i), 0, 0)))
    specs.append(pl.BlockSpec((1, LANES), lambda i: (0, 0)))
    specs.append(pl.BlockSpec((1, LANES), lambda i: (0, 0)))
    specs.append(pl.BlockSpec((16, CHUNK), lambda i: (0, 0)))
    specs.append(pl.BlockSpec((16, CHUNK), lambda i: (0, 0)))
    return specs, ti


def _dn_conv(first, tiles, halos, cw_ref, xs_ref, cpre_ref, tb):
    for s in range(3):
        c0 = s * 1024
        xs_ref[0:QK_HALO, c0:c0 + 1024] = jnp.where(first, 0.0, halos[s][...].astype(f32))
        xs_ref[QK_HALO:QK_HALO + tb, c0:c0 + 1024] = tiles[s][...].astype(f32)
    for sub in range(3072 // 256):
        cs = slice(sub * 256, (sub + 1) * 256)
        acc = cw_ref[0:1, cs] * xs_ref[pl.ds(QK_HALO - 3, tb), cs]
        for j in range(1, DN_CONV):
            acc = acc + cw_ref[j:j + 1, cs] * xs_ref[pl.ds(QK_HALO - 3 + j, tb), cs]
        cpre_ref[:, cs] = acc


def _dn_head(cpre_ref, r0, h, beta_all, gc_all, gcrow_ref, lane, tri_incl, tri_strict):
    rows = pl.ds(r0, CHUNK)
    cq = cpre_ref[rows, pl.ds(HD * h, HD)]
    ck = cpre_ref[rows, pl.ds(1024 + HD * h, HD)]
    cv = cpre_ref[rows, pl.ds(2048 + HD * h, HD)]
    sq, sk, v = _silu(cq), _silu(ck), _silu(cv)
    rq = lax.rsqrt(jnp.sum(sq * sq, axis=1, keepdims=True) + EPS)
    rk = lax.rsqrt(jnp.sum(sk * sk, axis=1, keepdims=True) + EPS)
    q, k = sq * rq, sk * rk
    beta = _lanecol(beta_all, lane, h)
    gcc = _lanecol(gc_all, lane, 8 + h)
    gcr = gcrow_ref[pl.ds(8 + h, 1), :]
    diff = gcc - gcr
    dmat = jnp.where(tri_incl, jnp.exp(jnp.where(tri_incl, diff, 0.0)), 0.0)
    gam = jnp.exp(gcc)
    glast = gcr[:, CHUNK - 1:CHUNK]
    gam_last = jnp.exp(glast)
    kscale = jnp.exp(glast - gcc)
    qs = q * (HD ** -0.5)
    kb = k * beta
    vb = v * beta
    kbg = kb * gam
    kk = _dot_nt(kb, k)
    lmat = jnp.where(tri_strict, kk * dmat, 0.0)
    pmat = _dot_nt(qs, k) * dmat
    return dict(cq=cq, ck=ck, cv=cv, sq=sq, sk=sk, v=v, rq=rq, rk=rk, q=q, k=k, beta=beta, gcc=gcc, dmat=dmat, gam=gam,
                gam_last=gam_last, kscale=kscale, qs=qs, kb=kb, vb=vb, kbg=kbg, lmat=lmat, pmat=pmat)


def _dn_gates(ps_ref, arow_ref, dtb_ref, pst_ref, acol_ref, dtc_ref):
    ps = ps_ref[...]
    beta_all = _sigmoid(ps)
    g_all = arow_ref[...] * _softplus(ps + dtb_ref[...])
    return ps, beta_all, g_all


def _dn_fwd(p, ps, pst, convw, arow, dtb, acol, dtc, comm=()):
    t = p.shape[0]
    nh = 1024 // HD
    tb = min(t, 256)
    nc = tb // CHUNK
    specs, _ = _dn_specs(t, tb, False)

    def compute(q_ref, k_ref, v_ref, hq_ref, hk_ref, hv_ref, cw_ref, ps_ref, pst_ref, arow_ref, dtb_ref, acol_ref, dtc_ref,
             o_ref, ss_ref, ti_ref, xs_ref, cpre_ref, st_ref, gcrow_ref, ball_ref, gall_ref):
        i = pl.program_id(0)

        @pl.when(i == 0)
        def _():
            st_ref[...] = jnp.zeros_like(st_ref)

        _dn_conv(i == 0, (q_ref, k_ref, v_ref), (hq_ref, hk_ref, hv_ref), cw_ref, xs_ref, cpre_ref, tb)
        _, beta_all, g_all = _dn_gates(ps_ref, arow_ref, dtb_ref, pst_ref, acol_ref, dtc_ref)
        ball_ref[...] = beta_all
        gall_ref[...] = g_all
        ri = lax.broadcasted_iota(jnp.int32, (CHUNK, CHUNK), 0)
        ci = lax.broadcasted_iota(jnp.int32, (CHUNK, CHUNK), 1)
        tri_incl, tri_strict = ri >= ci, ri > ci
        lower = tri_incl.astype(f32)
        upper = (ri <= ci).astype(f32)
        lane = lax.broadcasted_iota(jnp.int32, (CHUNK, LANES), 1)

        def chunk_group(cg, carry):
            hs = range(nh)
            mm = lambda a, b: jnp.dot(a, b, preferred_element_type=f32)
            cs = [cg * DN_GROUP + j for j in range(DN_GROUP)]
            r0s = [pl.multiple_of(c * CHUNK, CHUNK) for c in cs]
            hd = []
            for j in range(DN_GROUP):
                gc_all = _dot_hi(lower, gall_ref[pl.ds(r0s[j], CHUNK), :])
                g_rows = acol_ref[...] * _softplus(pst_ref[cs[j]] + dtc_ref[...])
                gcrow_ref[j] = _dot_hi(g_rows, upper)
                beta_c = ball_ref[pl.ds(r0s[j], CHUNK), :]
                hd += [_dn_head(cpre_ref, r0s[j], h, beta_c, gc_all, gcrow_ref.at[j], lane, tri_incl, tri_strict) for h in hs]
            tinv = _tri_inv([d["lmat"] for d in hd])
            tib = [x.astype(BF) for x in tinv]
            u = [mm(tib[i], hd[i]["vb"].astype(BF)) for i in range(len(hd))]
            w = [mm(tib[i], hd[i]["kbg"].astype(BF)).astype(BF) for i in range(len(hd))]
            qg = [(d["qs"] * d["gam"]).astype(BF) for d in hd]
            kt = [(d["k"] * d["kscale"]).astype(BF) for d in hd]
            pm = [d["pmat"].astype(BF) for d in hd]
            for j in range(DN_GROUP):
                ix = [j * nh + h for h in hs]
                s = [st_ref[h] for h in hs]
                sb = [x.astype(BF) for x in s]
                vnb = [(u[ix[h]] - mm(w[ix[h]], sb[h])).astype(BF) for h in hs]
                o = [mm(qg[ix[h]], sb[h]) + mm(pm[ix[h]], vnb[h]) for h in hs]
                snew = [s[h] * hd[ix[h]]["gam_last"] + lax.dot_general(kt[ix[h]], vnb[h], (((0,), (0,)), ((), ())),
                                                                      preferred_element_type=f32) for h in hs]
                for h in hs:
                    o_ref[pl.ds(r0s[j], CHUNK), pl.ds(HD * h, HD)] = o[h]
                    ss_ref[cs[j], h] = sb[h]
                    ti_ref[cs[j], h] = tinv[ix[h]]
                    st_ref[h] = snew[h]
            return carry

        lax.fori_loop(0, nc // DN_GROUP, chunk_group, 0)

    return _with_comm(
        compute, comm, "dn_fwd", (t // tb,), (p, p, p, p, p, p, convw, ps, pst, arow, dtb, acol, dtc), specs,
        (jax.ShapeDtypeStruct((t, 1024), f32), jax.ShapeDtypeStruct((t // CHUNK, nh, HD, HD), BF),
         jax.ShapeDtypeStruct((t // CHUNK, nh, CHUNK, CHUNK), f32)),
        (pl.BlockSpec((tb, 1024), lambda i: (i, 0)), pl.BlockSpec((nc, nh, HD, HD), lambda i: (i, 0, 0, 0)),
         pl.BlockSpec((nc, nh, CHUNK, CHUNK), lambda i: (i, 0, 0, 0))),
        [pltpu.VMEM((tb + QK_HALO, 3072), f32), pltpu.VMEM((tb, 3072), f32), pltpu.VMEM((nh, HD, HD), f32),
         pltpu.VMEM((DN_GROUP, 16, CHUNK), f32), pltpu.VMEM((tb, LANES), f32), pltpu.VMEM((tb, LANES), f32)])


def _dn_bwd(p, ps, pst, convw, arow, dtb, acol, dtc, do, ss, tinv_all, comm=()):
    t = p.shape[0]
    nh = 1024 // HD
    tb = min(t, 256)
    nc = tb // CHUNK
    nt = t // tb
    specs, ti = _dn_specs(t, tb, True)
    specs = specs + [pl.BlockSpec((tb, 1024), lambda i: (ti(i), 0)), pl.BlockSpec((nc, nh, HD, HD), lambda i: (ti(i), 0, 0, 0)),
                     pl.BlockSpec((nc, nh, CHUNK, CHUNK), lambda i: (ti(i), 0, 0, 0))]

    def compute(q_ref, k_ref, v_ref, hq_ref, hk_ref, hv_ref, cw_ref, ps_ref, pst_ref, arow_ref, dtb_ref, acol_ref, dtc_ref,
             do_ref, ss_ref, ti_ref,
             dp_ref, dps_ref, dcw_ref, dal_ref, ddt_ref,
             xs_ref, cpre_ref, dst_ref, gcrow_ref, ball_ref, gall_ref, dcs_ref, dball_ref, dgcall_ref):
        i = pl.program_id(0)

        @pl.when(i == 0)
        def _():
            dst_ref[...] = jnp.zeros_like(dst_ref)
            dcs_ref[...] = jnp.zeros_like(dcs_ref)
            dcw_ref[...] = jnp.zeros_like(dcw_ref)
            dal_ref[...] = jnp.zeros_like(dal_ref)
            ddt_ref[...] = jnp.zeros_like(ddt_ref)

        _dn_conv(i == nt - 1, (q_ref, k_ref, v_ref), (hq_ref, hk_ref, hv_ref), cw_ref, xs_ref, cpre_ref, tb)
        ps, beta_all, g_all = _dn_gates(ps_ref, arow_ref, dtb_ref, pst_ref, acol_ref, dtc_ref)
        ball_ref[...] = beta_all
        gall_ref[...] = g_all
        ri = lax.broadcasted_iota(jnp.int32, (CHUNK, CHUNK), 0)
        ci = lax.broadcasted_iota(jnp.int32, (CHUNK, CHUNK), 1)
        tri_incl, tri_strict = ri >= ci, ri > ci
        lower = tri_incl.astype(f32)
        upper = (ri <= ci).astype(f32)
        lane = lax.broadcasted_iota(jnp.int32, (CHUNK, LANES), 1)
        row_last = lax.broadcasted_iota(jnp.int32, (CHUNK, 1), 0) == CHUNK - 1

        def chunk_group(cg, carry):
            hs = range(nh)
            bf = lambda a: a.astype(BF)
            mm = lambda a, b: jnp.dot(a, b, preferred_element_type=f32)
            mnt = lambda a, b: lax.dot_general(a, b, (((1,), (1,)), ((), ())), preferred_element_type=f32)
            mtn = lambda a, b: lax.dot_general(a, b, (((0,), (0,)), ((), ())), preferred_element_type=f32)
            rsum = lambda a: jnp.sum(a, axis=1, keepdims=True)
            grp = range(DN_GROUP_BWD)
            cs = [nc - 1 - (cg * DN_GROUP_BWD + j) for j in grp]
            r0s = [pl.multiple_of(c * CHUNK, CHUNK) for c in cs]
            nn = range(DN_GROUP_BWD * nh)
            jof = [i // nh for i in nn]
            hof = [i % nh for i in nn]
            hd = []
            for j in grp:
                rows = pl.ds(r0s[j], CHUNK)
                gc_all = _dot_hi(lower, gall_ref[rows, :])
                g_rows = acol_ref[...] * _softplus(pst_ref[cs[j]] + dtc_ref[...])
                gcrow_ref[j] = _dot_hi(g_rows, upper)
                beta_c = ball_ref[rows, :]
                hd += [_dn_head(cpre_ref, r0s[j], h, beta_c, gc_all, gcrow_ref.at[j], lane, tri_incl, tri_strict) for h in hs]
            tib = [bf(ti_ref[cs[jof[i]], hof[i]]) for i in nn]
            sb = [ss_ref[cs[jof[i]], hof[i]] for i in nn]
            vbb = [bf(d["vb"]) for d in hd]
            kgb = [bf(d["kbg"]) for d in hd]
            u = [mm(tib[i], vbb[i]) for i in nn]
            wb = [bf(mm(tib[i], kgb[i])) for i in nn]
            vnb = [bf(u[i] - mm(wb[i], sb[i])) for i in nn]
            qgb = [bf(d["qs"] * d["gam"]) for d in hd]
            kt = [d["k"] * d["kscale"] for d in hd]
            ktb = [bf(x) for x in kt]
            pmb = [bf(d["pmat"]) for d in hd]
            dob = [bf(do_ref[pl.ds(r0s[jof[i]], CHUNK), pl.ds(HD * hof[i], HD)]) for i in nn]
            dvn0 = [mtn(pmb[i], dob[i]) for i in nn]
            dsn0 = [mtn(qgb[i], dob[i]) for i in nn]
            dpm = [jnp.where(tri_incl, mnt(dob[i], vnb[i]), 0.0) for i in nn]
            dqg = [mnt(dob[i], sb[i]) for i in nn]
            dvnb, dkt, dgl = [], [], []
            for j in grp:
                ix = [j * nh + h for h in hs]
                dsn = [dst_ref[h] for h in hs]
                dsnb = [bf(x) for x in dsn]
                dvn_j = [bf(dvn0[ix[h]] + mm(ktb[ix[h]], dsnb[h])) for h in hs]
                dsnew = [dsn0[ix[h]] + hd[ix[h]]["gam_last"] * dsn[h] - mtn(wb[ix[h]], dvn_j[h]) for h in hs]
                for h in hs:
                    dst_ref[h] = dsnew[h]
                dkt += [mnt(vnb[ix[h]], dsnb[h]) for h in hs]
                dgl += [jnp.sum(rsum(sb[ix[h]].astype(f32) * dsn[h]), axis=0, keepdims=True) for h in hs]
                dvnb += dvn_j
            dwb = [bf(-mnt(dvnb[i], sb[i])) for i in nn]
            dtm = [mnt(dvnb[i], vbb[i]) + mnt(dwb[i], kgb[i]) for i in nn]
            dvb = [mtn(tib[i], dvnb[i]) for i in nn]
            dkbg = [mtn(tib[i], dwb[i]) for i in nn]
            x1 = [bf(mtn(tib[i], bf(dtm[i]))) for i in nn]
            dl = [jnp.where(tri_strict, -mnt(x1[i], tib[i]), 0.0) for i in nn]
            mmat = [dl[i] * hd[i]["lmat"] + dpm[i] * hd[i]["pmat"] for i in nn]
            dkkb = [bf(dl[i] * hd[i]["dmat"]) for i in nn]
            dqkb = [bf(dpm[i] * hd[i]["dmat"]) for i in nn]
            kb16 = [bf(d["k"]) for d in hd]
            dkb = [mm(dkkb[i], kb16[i]) + dkbg[i] * hd[i]["gam"] for i in nn]
            dk = [mtn(dkkb[i], bf(hd[i]["kb"])) + mtn(dqkb[i], bf(hd[i]["qs"])) + dkt[i] * hd[i]["kscale"] + dkb[i] * hd[i]["beta"]
                  for i in nn]
            dq = [(mm(dqkb[i], kb16[i]) + dqg[i] * hd[i]["gam"]) * (HD ** -0.5) for i in nn]
            dbeta = [rsum(dkb[i] * hd[i]["k"] + dvb[i] * hd[i]["v"]) for i in nn]
            dgam = [rsum(dkbg[i] * hd[i]["kb"] + dqg[i] * hd[i]["qs"]) for i in nn]
            ktdk = [rsum(dkt[i] * kt[i]) for i in nn]
            csum = [rsum(jnp.where(ri == ci, jnp.sum(mmat[i], axis=0, keepdims=True), 0.0)) for i in nn]
            for j in grp:
                rows = pl.ds(r0s[j], CHUNK)
                dbeta_all = jnp.zeros((CHUNK, LANES), f32)
                dgc_all = jnp.zeros((CHUNK, LANES), f32)
                for h in hs:
                    i = j * nh + h
                    d = hd[i]
                    extra = jnp.sum(ktdk[i], axis=0, keepdims=True) + dgl[i] * d["gam_last"]
                    dgc = rsum(mmat[i]) - csum[i] + dgam[i] * d["gam"] - ktdk[i] + jnp.where(row_last, extra, 0.0)
                    dbeta_all = jnp.where(lane == h, dbeta[i], dbeta_all)
                    dgc_all = jnp.where(lane == 8 + h, dgc, dgc_all)
                    dsq = d["rq"] * (dq[i] - d["q"] * rsum(dq[i] * d["q"]))
                    dsk = d["rk"] * (dk[i] - d["k"] * rsum(dk[i] * d["k"]))
                    dcs_ref[rows, pl.ds(HD * h, HD)] = dsq * _dsilu(d["cq"])
                    dcs_ref[rows, pl.ds(1024 + HD * h, HD)] = dsk * _dsilu(d["ck"])
                    dcs_ref[rows, pl.ds(2048 + HD * h, HD)] = (dvb[i] * d["beta"]) * _dsilu(d["cv"])
                dball_ref[rows, :] = dbeta_all
                dgcall_ref[rows, :] = _dot_hi(upper, dgc_all)
            return carry

        lax.fori_loop(0, nc // DN_GROUP_BWD, chunk_group, 0)

        dbeta_t = dball_ref[...]
        dg_t = dgcall_ref[...]
        sg = _sigmoid(ps + dtb_ref[...])
        da = dg_t * arow_ref[...] * sg
        dps_ref[...] = dbeta_t * beta_all * (1.0 - beta_all) + da
        ddt_ref[...] += jnp.sum(da, axis=0, keepdims=True)
        dal_ref[...] += jnp.sum(dg_t * g_all, axis=0, keepdims=True)

        for sub in range(3072 // 256):
            cs = slice(sub * 256, (sub + 1) * 256)
            dc = dcs_ref[0:tb, cs]
            acc = cw_ref[DN_CONV - 1:DN_CONV, cs] * dc
            for j in range(DN_CONV - 1):
                acc = acc + cw_ref[j:j + 1, cs] * dcs_ref[pl.ds(DN_CONV - 1 - j, tb), cs]
            dp_ref[:, cs] = acc.astype(BF)
            for j in range(DN_CONV):
                dcw_ref[j:j + 1, cs] += jnp.sum(dc * xs_ref[pl.ds(QK_HALO - 3 + j, tb), cs], axis=0, keepdims=True)
        dcs_ref[tb:tb + QK_HALO, :] = dcs_ref[0:QK_HALO, :]

    vecl = pl.BlockSpec((1, LANES), lambda i: (0, 0))
    return _with_comm(
        compute, comm, "dn_bwd", (nt,), (p, p, p, p, p, p, convw, ps, pst, arow, dtb, acol, dtc, do, ss, tinv_all), specs,
        (jax.ShapeDtypeStruct((t, 3072), BF), jax.ShapeDtypeStruct((t, LANES), f32), jax.ShapeDtypeStruct((DN_CONV, 3072), f32),
         jax.ShapeDtypeStruct((1, LANES), f32), jax.ShapeDtypeStruct((1, LANES), f32)),
        (pl.BlockSpec((tb, 3072), lambda i: (ti(i), 0)), pl.BlockSpec((tb, LANES), lambda i: (ti(i), 0)),
         pl.BlockSpec((DN_CONV, 3072), lambda i: (0, 0)), vecl, vecl),
        [pltpu.VMEM((tb + QK_HALO, 3072), f32), pltpu.VMEM((tb, 3072), f32), pltpu.VMEM((nh, HD, HD), f32),
         pltpu.VMEM((DN_GROUP_BWD, 16, CHUNK), f32), pltpu.VMEM((tb, LANES), f32), pltpu.VMEM((tb, LANES), f32),
         pltpu.VMEM((tb + QK_HALO, 3072), f32), pltpu.VMEM((tb, LANES), f32), pltpu.VMEM((tb, LANES), f32)])


def _ln_fwd(x, g, b):
    mu = jnp.mean(x, axis=1, keepdims=True)
    xc = x - mu
    rstd = lax.rsqrt(jnp.mean(xc * xc, axis=1, keepdims=True) + LN_EPS)
    xhat = xc * rstd
    return xhat * g + b, xhat, rstd


def _ln_bwd(dy, xhat, rstd, g):
    dxh = dy * g
    return rstd * (dxh - jnp.mean(dxh, axis=1, keepdims=True) - xhat * jnp.mean(dxh * xhat, axis=1, keepdims=True))


def _br_specs(t, tb, rev):
    nt = t // tb
    ti = (lambda i: nt - 1 - i) if rev else (lambda i: i)
    hb = tb // CV_HALO
    col = lambda off: off // 512
    specs = [pl.BlockSpec((tb, 1024), lambda i: (ti(i), 0)),
             pl.BlockSpec((tb, 1024), lambda i: (ti(i), 0))]
    for off in (P_SGU, P_SGV, P_SGG, P_CVA, P_CVB, P_CVG):
        specs.append(pl.BlockSpec((tb, 512), lambda i, off=off: (ti(i), col(off))))
    for off in (P_CVA, P_CVB):
        specs.append(pl.BlockSpec((CV_HALO, 512), lambda i, off=off: (jnp.maximum(ti(i) * hb - 1, 0), col(off))))
    v512 = pl.BlockSpec((1, 512), lambda i: (0, 0))
    specs += [pl.BlockSpec((1, HD), lambda i: (0, 0)), v512, v512,
              pl.BlockSpec((SG_GROUPS, SG_BLOCK, SG_BLOCK), lambda i: (0, 0, 0)),
              pl.BlockSpec((SG_BLOCK, 512), lambda i: (0, 0)),
              pl.BlockSpec((CV_HALO, 512), lambda i: (0, 0)),
              v512, v512, v512]
    return specs, ti


def _sg_mask():
    r = lax.broadcasted_iota(jnp.int32, (SG_BLOCK, SG_BLOCK), 0) // CHUNK
    c = lax.broadcasted_iota(jnp.int32, (SG_BLOCK, SG_BLOCK), 1) // CHUNK
    return r >= c


def _cv_glu(first, a_ref, b_ref, ha_ref, hb_ref, gs_ref, tb):
    ha = jnp.where(first, 0.0, ha_ref[...].astype(f32))
    gs_ref[0:CV_HALO, :] = ha * _sigmoid(hb_ref[...].astype(f32))
    gs_ref[CV_HALO:CV_HALO + tb, :] = a_ref[...].astype(f32) * _sigmoid(b_ref[...].astype(f32))


SUBLANES = 8
CV_SHIFT_ROWS = CV_HALO - SUBLANES


def _cv_shift(src_ref, sh_ref, tb, cs):
    for b in range(1, SUBLANES):
        sh_ref[b - 1, 0:tb + CV_SHIFT_ROWS, cs] = src_ref[pl.ds(b, tb + CV_SHIFT_ROWS), cs]


def _cv_tap(src_ref, sh_ref, off, tb, cs):
    a, b = divmod(off, SUBLANES)
    if b == 0:
        return src_ref[pl.ds(SUBLANES * a, tb), cs]
    return sh_ref[b - 1, pl.ds(SUBLANES * a, tb), cs]


def _cv_conv(gs_ref, gsh_ref, cw_ref, cb_ref, tb, cs):
    acc = cb_ref[:, cs] + cw_ref[0:1, cs] * _cv_tap(gs_ref, gsh_ref, CV_HALO - (CV_K - 1), tb, cs)
    for j in range(1, CV_K):
        acc = acc + cw_ref[j:j + 1, cs] * _cv_tap(gs_ref, gsh_ref, CV_HALO - (CV_K - 1) + j, tb, cs)
    return acc


def _branch_fwd(o, p, dng, slg, slb, sgw, sgbias, cvw, cvb, clg, clb):
    t = o.shape[0]
    tb = min(t, 256)
    specs, _ = _br_specs(t, tb, False)

    def body(o_ref, z_ref, su_ref, sv_ref, sgt_ref, ca_ref, cb_ref, cg_ref, ha_ref, hb_ref,
             dng_ref, slg_ref, slb_ref, sgw_ref, sgb_ref, cvw_ref, cvb_ref, clg_ref, clb_ref, y_ref, cvo_ref, gs_ref, vln_ref, gsh_ref):
        i = pl.program_id(0)
        for h in range(1024 // HD):
            cs = slice(h * HD, (h + 1) * HD)
            oh = o_ref[:, cs]
            r = lax.rsqrt(jnp.mean(oh * oh, axis=1, keepdims=True) + EPS)
            y_ref[:, cs] = ((oh * r) * dng_ref[...] * _silu(z_ref[:, cs].astype(f32))).astype(BF)
        mask = _sg_mask()
        for g in range(SG_GROUPS):
            cs = slice(g * HD, (g + 1) * HD)
            vg = _gelu(sv_ref[:, cs].astype(f32))
            vln, _, _ = _ln_fwd(vg, slg_ref[:, cs], slb_ref[:, cs])
            vln_ref[...] = vln
            ws = jnp.where(mask, sgw_ref[g], 0.0)
            for nb in range(tb // SG_BLOCK):
                rs = slice(nb * SG_BLOCK, (nb + 1) * SG_BLOCK)
                mixed = _dot(ws, vln_ref[rs, :]) + sgb_ref[:, cs]
                u = _gelu(su_ref[rs, cs].astype(f32))
                y_ref[rs, 1024 + g * HD:1024 + (g + 1) * HD] = (u * mixed * _silu(sgt_ref[rs, cs].astype(f32))).astype(BF)
        _cv_glu(i == 0, ca_ref, cb_ref, ha_ref, hb_ref, gs_ref, tb)
        for g in range(CV_GROUPS):
            cs = slice(g * HD, (g + 1) * HD)
            _cv_shift(gs_ref, gsh_ref, tb, cs)
            dw = _cv_conv(gs_ref, gsh_ref, cvw_ref, cvb_ref, tb, cs)
            cvo_ref[:, cs] = dw
            ln, _, _ = _ln_fwd(dw, clg_ref[:, cs], clb_ref[:, cs])
            y_ref[:, 1536 + g * HD:1536 + (g + 1) * HD] = (_silu(ln) * _silu(cg_ref[:, cs].astype(f32))).astype(BF)

    return _pc(body, name="branch_fwd", grid=(t // tb,),
               out_shape=(jax.ShapeDtypeStruct((t, 2048), BF), jax.ShapeDtypeStruct((t, 512), f32)),
               in_specs=specs, out_specs=(pl.BlockSpec((tb, 2048), lambda i: (i, 0)), pl.BlockSpec((tb, 512), lambda i: (i, 0))),
               scratch_shapes=[pltpu.VMEM((tb + CV_HALO, 512), f32), pltpu.VMEM((tb, HD), f32),
                               pltpu.VMEM((SUBLANES - 1, tb + CV_SHIFT_ROWS, 512), f32)],
               compiler_params=_cp(("arbitrary",)))(o, p, p, p, p, p, p, p, p, p, dng, slg, slb, sgw, sgbias, cvw, cvb, clg, clb)


def _branch_bwd(o, p, dng, slg, slb, sgw, sgbias, cvw, cvb, clg, clb, dy, cvo):
    t = o.shape[0]
    tb = min(t, 256)
    nt = t // tb
    specs, ti = _br_specs(t, tb, True)
    specs = specs + [pl.BlockSpec((tb, 2048), lambda i: (ti(i), 0)), pl.BlockSpec((tb, 512), lambda i: (ti(i), 0))]

    def body(o_ref, z_ref, su_ref, sv_ref, sgt_ref, ca_ref, cb_ref, cg_ref, ha_ref, hb_ref,
             dng_ref, slg_ref, slb_ref, sgw_ref, sgb_ref, cvw_ref, cvb_ref, clg_ref, clb_ref, dy_ref, cvo_ref,
             dp_ref, do_ref, ddng_ref, dslg_ref, dslb_ref, dsgw_ref, dsgb_ref, dcvw_ref, dcvb_ref, dclg_ref, dclb_ref,
             gs_ref, vln_ref, xh_ref, dvl_ref, ddw_ref, dbias_ref, gsh_ref, dsh_ref):
        i = pl.program_id(0)

        @pl.when(i == 0)
        def _():
            for r in (ddng_ref, dslg_ref, dslb_ref, dsgw_ref, dsgb_ref, dcvw_ref, dcvb_ref, dclg_ref, dclb_ref, ddw_ref, dbias_ref):
                r[...] = jnp.zeros_like(r)

        for h in range(1024 // HD):
            cs = slice(h * HD, (h + 1) * HD)
            oh = o_ref[:, cs]
            zz = z_ref[:, cs].astype(f32)
            dyh = dy_ref[:, cs].astype(f32)
            r = lax.rsqrt(jnp.mean(oh * oh, axis=1, keepdims=True) + EPS)
            nrm = oh * r
            sz = _silu(zz)
            dn = dyh * dng_ref[...] * sz
            ddng_ref[...] += jnp.sum(dyh * nrm * sz, axis=0, keepdims=True)
            dp_ref[:, cs] = (dyh * nrm * dng_ref[...] * _dsilu(zz)).astype(BF)
            do_ref[:, cs] = r * (dn - nrm * jnp.mean(dn * nrm, axis=1, keepdims=True))
        mask = _sg_mask()
        for g in range(SG_GROUPS):
            cs = slice(g * HD, (g + 1) * HD)
            sv = sv_ref[:, cs].astype(f32)
            vln, xhat, rstd = _ln_fwd(_gelu(sv), slg_ref[:, cs], slb_ref[:, cs])
            vln_ref[...] = vln
            xh_ref[...] = xhat
            ws = jnp.where(mask, sgw_ref[g], 0.0)
            dws = jnp.zeros((SG_BLOCK, SG_BLOCK), f32)
            for nb in range(tb // SG_BLOCK):
                rs = slice(nb * SG_BLOCK, (nb + 1) * SG_BLOCK)
                dyb = dy_ref[rs, 1024 + g * HD:1024 + (g + 1) * HD].astype(f32)
                su = su_ref[rs, cs].astype(f32)
                gt = sgt_ref[rs, cs].astype(f32)
                mixed = _dot(ws, vln_ref[rs, :]) + sgb_ref[:, cs]
                u = _gelu(su)
                sgt = _silu(gt)
                dmixed = dyb * u * sgt
                dp_ref[rs, P_SGU + g * HD:P_SGU + (g + 1) * HD] = (dyb * mixed * sgt * _dgelu(su)).astype(BF)
                dp_ref[rs, P_SGG + g * HD:P_SGG + (g + 1) * HD] = (dyb * u * mixed * _dsilu(gt)).astype(BF)
                dvl_ref[rs, :] = _dot_tn(ws, dmixed)
                dws = dws + _dot_nt(dmixed, vln_ref[rs, :])
                dbias_ref[:, cs] += dmixed
            dsgw_ref[g] += jnp.where(mask, dws, 0.0)
            dvl = dvl_ref[...]
            xhat = xh_ref[...]
            dslg_ref[:, cs] += jnp.sum(dvl * xhat, axis=0, keepdims=True)
            dslb_ref[:, cs] += jnp.sum(dvl, axis=0, keepdims=True)
            dvg = _ln_bwd(dvl, xhat, rstd, slg_ref[:, cs])
            dp_ref[:, P_SGV + g * HD:P_SGV + (g + 1) * HD] = (dvg * _dgelu(sv)).astype(BF)
        _cv_glu(i == nt - 1, ca_ref, cb_ref, ha_ref, hb_ref, gs_ref, tb)
        for g in range(CV_GROUPS):
            cs = slice(g * HD, (g + 1) * HD)
            dyc = dy_ref[:, 1536 + g * HD:1536 + (g + 1) * HD].astype(f32)
            cg = cg_ref[:, cs].astype(f32)
            _cv_shift(gs_ref, gsh_ref, tb, cs)
            ln, xhat, rstd = _ln_fwd(cvo_ref[:, cs], clg_ref[:, cs], clb_ref[:, cs])
            dln = dyc * _silu(cg) * _dsilu(ln)
            dp_ref[:, P_CVG + g * HD:P_CVG + (g + 1) * HD] = (dyc * _silu(ln) * _dsilu(cg)).astype(BF)
            dclg_ref[:, cs] += jnp.sum(dln * xhat, axis=0, keepdims=True)
            dclb_ref[:, cs] += jnp.sum(dln, axis=0, keepdims=True)
            ddw = _ln_bwd(dln, xhat, rstd, clg_ref[:, cs])
            dcvb_ref[:, cs] += jnp.sum(ddw, axis=0, keepdims=True)
            ddw_ref[0:tb, cs] = ddw
            _cv_shift(ddw_ref, dsh_ref, tb, cs)
            dglu = cvw_ref[CV_K - 1:CV_K, cs] * ddw
            for j in range(CV_K - 1):
                dglu = dglu + cvw_ref[j:j + 1, cs] * _cv_tap(ddw_ref, dsh_ref, CV_K - 1 - j, tb, cs)
            for j in range(CV_K):
                tap = _cv_tap(gs_ref, gsh_ref, CV_HALO - (CV_K - 1) + j, tb, cs)
                dcvw_ref[j:j + 1, cs] += jnp.sum(ddw * tap, axis=0, keepdims=True)
            a = ca_ref[:, cs].astype(f32)
            sb = _sigmoid(cb_ref[:, cs].astype(f32))
            dp_ref[:, P_CVA + g * HD:P_CVA + (g + 1) * HD] = (dglu * sb).astype(BF)
            dp_ref[:, P_CVB + g * HD:P_CVB + (g + 1) * HD] = (dglu * a * sb * (1.0 - sb)).astype(BF)
        ddw_ref[tb:tb + CV_HALO, :] = ddw_ref[0:CV_HALO, :]

        @pl.when(i == nt - 1)
        def _():
            lane = lax.broadcasted_iota(jnp.int32, (SG_BLOCK, LANES), 1)
            acc = jnp.zeros((SG_BLOCK, LANES), f32)
            for g in range(SG_GROUPS):
                acc = jnp.where(lane == g, jnp.sum(dbias_ref[:, g * HD:(g + 1) * HD], axis=1, keepdims=True), acc)
            dsgb_ref[...] = acc

    v512 = pl.BlockSpec((1, 512), lambda i: (0, 0))
    return _pc(body, name="branch_bwd", grid=(nt,),
               out_shape=(jax.ShapeDtypeStruct((t, 4096), BF), jax.ShapeDtypeStruct((t, 1024), f32),
                          jax.ShapeDtypeStruct((1, HD), f32), jax.ShapeDtypeStruct((1, 512), f32), jax.ShapeDtypeStruct((1, 512), f32),
                          jax.ShapeDtypeStruct((SG_GROUPS, SG_BLOCK, SG_BLOCK), f32), jax.ShapeDtypeStruct((SG_BLOCK, LANES), f32),
                          jax.ShapeDtypeStruct((CV_HALO, 512), f32), jax.ShapeDtypeStruct((1, 512), f32),
                          jax.ShapeDtypeStruct((1, 512), f32), jax.ShapeDtypeStruct((1, 512), f32)),
               in_specs=specs,
               out_specs=(pl.BlockSpec((tb, 4096), lambda i: (ti(i), 0)), pl.BlockSpec((tb, 1024), lambda i: (ti(i), 0)),
                          pl.BlockSpec((1, HD), lambda i: (0, 0)), v512, v512,
                          pl.BlockSpec((SG_GROUPS, SG_BLOCK, SG_BLOCK), lambda i: (0, 0, 0)),
                          pl.BlockSpec((SG_BLOCK, LANES), lambda i: (0, 0)), pl.BlockSpec((CV_HALO, 512), lambda i: (0, 0)),
                          v512, v512, v512),
               scratch_shapes=[pltpu.VMEM((tb + CV_HALO, 512), f32), pltpu.VMEM((tb, HD), f32), pltpu.VMEM((tb, HD), f32),
                               pltpu.VMEM((tb, HD), f32), pltpu.VMEM((tb + CV_HALO, 512), f32), pltpu.VMEM((SG_BLOCK, 512), f32),
                               pltpu.VMEM((SUBLANES - 1, tb + CV_SHIFT_ROWS, 512), f32),
                               pltpu.VMEM((SUBLANES - 1, tb + CV_SHIFT_ROWS, 512), f32)],
               compiler_params=_cp(("arbitrary",)))(o, p, p, p, p, p, p, p, p, p, dng, slg, slb, sgw, sgbias, cvw, cvb, clg, clb, dy, cvo)


def _outproj(ycat, w, x, gate):
    t, d = x.shape
    tm = min(t, 512)

    def body(y_ref, w_ref, x_ref, g_ref, o_ref):
        o_ref[...] = x_ref[...] + g_ref[...] * jnp.dot(y_ref[...], w_ref[...], preferred_element_type=f32)

    return _pc(body, name="outproj", grid=(t // tm,), out_shape=jax.ShapeDtypeStruct((t, d), f32),
               in_specs=[pl.BlockSpec((tm, ycat.shape[1]), lambda i: (i, 0)), pl.BlockSpec(w.shape, lambda i: (0, 0)),
                         pl.BlockSpec((tm, d), lambda i: (i, 0)), pl.BlockSpec((1, d), lambda i: (0, 0))],
               out_specs=pl.BlockSpec((tm, d), lambda i: (i, 0)), compiler_params=_cp(("arbitrary",)))(ycat, w, x, gate)


def _outproj_bwd(dxo, w, gate):
    t, d = dxo.shape
    tm = min(t, 512)

    def body(d_ref, w_ref, g_ref, o_ref):
        o_ref[...] = _dot_nt(d_ref[...] * g_ref[...], w_ref[...]).astype(BF)

    return _pc(body, name="outproj_bwd", grid=(t // tm,), out_shape=jax.ShapeDtypeStruct((t, w.shape[0]), BF),
               in_specs=[pl.BlockSpec((tm, d), lambda i: (i, 0)), pl.BlockSpec(w.shape, lambda i: (0, 0)),
                         pl.BlockSpec((1, d), lambda i: (0, 0))],
               out_specs=pl.BlockSpec((tm, w.shape[0]), lambda i: (i, 0)), compiler_params=_cp(("arbitrary",)))(dxo, w, gate)


def _tn_acc(a, b, name):
    kk, m = a.shape
    n = b.shape[1]
    tk, tn = min(kk, 1024), min(n, 1024)

    def body(a_ref, b_ref, o_ref):
        @pl.when(pl.program_id(1) == 0)
        def _():
            o_ref[...] = jnp.zeros_like(o_ref)
        o_ref[...] += _dot_tn(a_ref[...], b_ref[...])

    return _pc(body, name=name, grid=(n // tn, kk // tk), out_shape=jax.ShapeDtypeStruct((m, n), f32),
               in_specs=[pl.BlockSpec((tk, m), lambda j, k: (k, 0)), pl.BlockSpec((tk, tn), lambda j, k: (k, j))],
               out_specs=pl.BlockSpec((m, tn), lambda j, k: (0, j)), compiler_params=_cp(("arbitrary", "arbitrary")))(a, b)


def _wout_grad(gmat, w, gate):
    m, n = gmat.shape
    tr = m // N_DEV

    def body(g_ref, w_ref, gt_ref, o_ref, dg_ref):
        @pl.when(pl.program_id(0) == 0)
        def _():
            dg_ref[...] = jnp.zeros_like(dg_ref)
        gm = g_ref[...]
        o_ref[0] = (gm * gt_ref[...]).astype(BF)
        dg_ref[...] += jnp.sum(gm * w_ref[...].astype(f32), axis=0, keepdims=True)

    return _pc(body, name="wout_grad", grid=(N_DEV,),
               out_shape=(jax.ShapeDtypeStruct((N_DEV, tr, n), BF), jax.ShapeDtypeStruct((1, n), f32)),
               in_specs=[pl.BlockSpec((tr, n), lambda i: (i, 0)), pl.BlockSpec((tr, n), lambda i: (i, 0)),
                         pl.BlockSpec((1, n), lambda i: (0, 0))],
               out_specs=(pl.BlockSpec((1, tr, n), lambda i: (i, 0, 0)), pl.BlockSpec((1, n), lambda i: (0, 0))),
               compiler_params=_cp(("arbitrary",)))(gmat, w, gate)


def _inproj_bwd(dpa, dpb, dps, wmain, wsmall, x, ng, sc1, dxo, comm=()):
    t, d = x.shape
    na, nb = dpa.shape[1] // 1024, dpb.shape[1] // 1024
    nk = na + nb
    tm, rb = min(t, 512), 128

    def body(a_ref, b_ref, s_ref, wm_ref, ws_ref, x_ref, ng_ref, sc_ref, dxo_ref, dx_ref, dsh_ref, dsc_ref, dng_ref, acc_ref):
        i, k = pl.program_id(0), pl.program_id(1)

        @pl.when((i == 0) & (k == 0))
        def _():
            dsh_ref[...] = jnp.zeros_like(dsh_ref)
            dsc_ref[...] = jnp.zeros_like(dsc_ref)
            dng_ref[...] = jnp.zeros_like(dng_ref)

        @pl.when(k == 0)
        def _():
            acc_ref[...] = _dot_nt(s_ref[...], ws_ref[...])

        @pl.when(k < na)
        def _():
            acc_ref[...] += _dot_nt(a_ref[...], wm_ref[...])

        @pl.when(k >= na)
        def _():
            acc_ref[...] += _dot_nt(b_ref[...], wm_ref[...])

        @pl.when(k == nk - 1)
        def _():
            def rows(j, carry):
                r0 = pl.multiple_of(j * rb, rb)
                rr = pl.ds(r0, rb)
                xv = x_ref[rr, :]
                dh = acc_ref[rr, :]
                r = lax.rsqrt(jnp.mean(xv * xv, axis=1, keepdims=True) + EPS)
                xn = xv * r
                dsh_ref[...] += jnp.sum(dh, axis=0, keepdims=True)
                dsc_ref[...] += jnp.sum(dh * (xn * ng_ref[...]), axis=0, keepdims=True)
                dng_ref[...] += jnp.sum(dh * sc_ref[...] * xn, axis=0, keepdims=True)
                dxn = dh * (ng_ref[...] * sc_ref[...])
                dx_ref[rr, :] = r * (dxn - xn * jnp.mean(dxn * xn, axis=1, keepdims=True)) + dxo_ref[rr, :]
                return carry
            lax.fori_loop(0, tm // rb, rows, 0)

    vec = pl.BlockSpec((1, d), lambda i, k: (0, 0))
    row = pl.BlockSpec((tm, d), lambda i, k: (i, 0))
    shp = jax.ShapeDtypeStruct((1, d), f32)
    return _with_comm(
        body, comm, "inproj_bwd", (t // tm, nk), (dpa, dpb, dps, wmain, wsmall, x, ng, sc1, dxo),
        [pl.BlockSpec((tm, 1024), lambda i, k: (i, jnp.minimum(k, na - 1))),
         pl.BlockSpec((tm, 1024), lambda i, k: (i, jnp.clip(k - na, 0, nb - 1))),
         pl.BlockSpec((tm, LANES), lambda i, k: (i, 0)),
         pl.BlockSpec((d, 1024), lambda i, k: (0, k)), pl.BlockSpec((d, LANES), lambda i, k: (0, 0)), row, vec, vec, row],
        (jax.ShapeDtypeStruct((t, d), f32), shp, shp, shp), (row, vec, vec, vec), [pltpu.VMEM((tm, d), f32)])


def _loss_head(x, fg, tgt):
    t, d = x.shape
    tm = min(t, 512)

    def body(x_ref, g_ref, t_ref, dx_ref, l_ref, dg_ref):
        @pl.when(pl.program_id(0) == 0)
        def _():
            l_ref[...] = jnp.zeros_like(l_ref)
            dg_ref[...] = jnp.zeros_like(dg_ref)
        xv = x_ref[...]
        r = lax.rsqrt(jnp.mean(xv * xv, axis=1, keepdims=True) + EPS)
        xn = xv * r
        err = xn * g_ref[...] - t_ref[...]
        l_ref[...] += 0.5 * jnp.sum(jnp.mean(err * err, axis=1, keepdims=True), axis=0, keepdims=True)
        dy = err * (1.0 / d)
        dg_ref[...] += jnp.sum(dy * xn, axis=0, keepdims=True)
        dxn = dy * g_ref[...]
        dx_ref[...] = r * (dxn - xn * jnp.mean(dxn * xn, axis=1, keepdims=True))

    row = pl.BlockSpec((tm, d), lambda i: (i, 0))
    vec = pl.BlockSpec((1, d), lambda i: (0, 0))
    return _pc(body, name="loss_head", grid=(t // tm,),
               out_shape=(jax.ShapeDtypeStruct((t, d), f32), jax.ShapeDtypeStruct((1, 1), f32), jax.ShapeDtypeStruct((1, d), f32)),
               in_specs=[row, vec, row], out_specs=(row, pl.BlockSpec((1, 1), lambda i: (0, 0)), vec),
               compiler_params=_cp(("arbitrary",)))(x, fg, tgt)


def _pack(arrs, rows_mult=8):
    flat = jnp.concatenate([a.reshape(-1).astype(f32) for a in arrs])
    n = flat.shape[0]
    per = rows_mult * LANES
    pad = (-n) % per
    if pad:
        flat = jnp.concatenate([flat, jnp.zeros((pad,), f32)])
    return flat.reshape(-1, LANES)


def _unpack(packed, shapes, lead=()):
    flat = packed.reshape(lead + (-1,))
    out, off = [], 0
    for s in shapes:
        n = 1
        for v in s:
            n *= v
        out.append(flat[..., off:off + n].reshape(lead + tuple(s)))
        off += n
    return out


def _pad_lanes(v, at):
    return jnp.pad(v.astype(f32), (at, LANES - at - v.shape[0])).reshape(1, LANES)


def kernel(x, c, norm_g, w_ada, b_ada, w_in, conv_qkv, a_log, dt_bias, dn_norm_g, sg_ln_g, sg_ln_b, sg_w, sg_b, cv_w, cv_b, cv_ln_g, cv_ln_b, w_out, final_g, loss_target, m_norm_g, m_w_ada, m_b_ada, m_w_in, m_conv_qkv, m_a_log, m_dt_bias, m_dn_norm_g, m_sg_ln_g, m_sg_ln_b, m_sg_w, m_sg_b, m_cv_w, m_cv_b, m_cv_ln_g, m_cv_ln_b, m_w_out, m_final_g, v_norm_g, v_w_ada, v_b_ada, v_w_in, v_conv_qkv, v_a_log, v_dt_bias, v_dn_norm_g, v_sg_ln_g, v_sg_ln_b, v_sg_w, v_sg_b, v_cv_w, v_cv_b, v_cv_ln_g, v_cv_ln_b, v_w_out, v_final_g):
    nl, d = norm_g.shape
    t = x.shape[1]
    nh = a_log.shape[1]
    xi, yi, ci = lax.axis_index("x"), lax.axis_index("y"), lax.axis_index("c")
    me = 4 * xi + 2 * yi + ci
    x0 = x[0]
    tgt = loss_target[0]
    ada_cols = w_ada.shape[2]
    in_cols = w_in.shape[2]
    cq_cols = conv_qkv.shape[2]
    cvw_cols = cv_w.shape[2]

    start_shapes = [c.shape, conv_qkv.shape, cv_w.shape]
    start_pack = _pack([c, conv_qkv, cv_w])
    wi_b, wo_b = w_in.astype(BF), w_out.astype(BF)
    g_small, g_win0 = _exchange([start_pack, wi_b[0]], ["gather2", "gather2"], "gather_start")
    c_all, cq_g, cvw_g = _unpack(g_small, start_shapes, lead=(N_DEV,))
    c_all = c_all.reshape(N_DEV, d)
    conv_full = jnp.moveaxis(cq_g, 0, 2).reshape(nl, DN_CONV, N_DEV * cq_cols)
    cvw_full = jnp.moveaxis(cvw_g, 0, 2).reshape(nl, CV_K, N_DEV * cvw_cols)
    w_packed = [_repack_w(g_win0)]
    w_out_full = []

    b_my = lax.dynamic_slice_in_dim(b_ada, me * ada_cols, ada_cols, axis=1).reshape(nl, 1, ada_cols)
    mod_part = _ada_fwd(c_all, w_ada, b_my)
    (mod_g,) = _exchange([mod_part], ["gather"], "gather_mod")
    mod_mine = lax.dynamic_index_in_dim(mod_g, me, axis=2, keepdims=False)
    mod = jnp.moveaxis(mod_mine, 0, 1).reshape(nl, N_DEV * ada_cols)
    shift, scale, gate = mod[:, :d], mod[:, d:2 * d], mod[:, 2 * d:]

    arow = [_pad_lanes(-jnp.exp(a_log[l]), 8) for l in range(nl)]
    dtb = [_pad_lanes(dt_bias[l], 8) for l in range(nl)]
    saved = []
    xl = x0
    for l in range(nl):
        sc1 = (1.0 + scale[l]).reshape(1, d)
        w_main_l, w_small_l = w_packed[l]
        p, ps, hb = _inproj(xl, norm_g[l].reshape(1, d), sc1, shift[l].reshape(1, d), w_main_l, w_small_l)
        pst = ps[:, :16].reshape(t // CHUNK, CHUNK, 16).transpose(0, 2, 1)
        acol = jnp.broadcast_to(arow[l][0, :16].reshape(16, 1), (16, CHUNK))
        dtc = jnp.broadcast_to(dtb[l][0, :16].reshape(16, 1), (16, CHUNK))
        ahead = (wo_b[l], wi_b[l + 1]) if l + 1 < nl else (wo_b[l],)
        o, ss, tinv, g_wout, *g_win_n = _dn_fwd(p, ps, pst, conv_full[l], arow[l], dtb[l], acol, dtc,
                                                comm=(ahead, ("gather2",) * len(ahead)))
        w_out_l = g_wout.reshape(N_DEV * w_out.shape[1], w_out.shape[2])
        w_out_full.append(w_out_l)
        if g_win_n:
            w_packed.append(_repack_w(g_win_n[0]))
        sgbias = jnp.repeat(sg_b[l].T, HD, axis=1)
        cvw_pad = jnp.pad(cvw_full[l], ((0, CV_HALO - CV_K), (0, 0)))
        br_par = (dn_norm_g[l].reshape(1, HD), sg_ln_g[l].reshape(1, -1), sg_ln_b[l].reshape(1, -1), sg_w[l], sgbias, cvw_pad,
                  cv_b[l].reshape(1, -1), cv_ln_g[l].reshape(1, -1), cv_ln_b[l].reshape(1, -1))
        ycat, cvo = _branch_fwd(o, p, *br_par)
        xn = _outproj(ycat, w_out_l, xl, gate[l].reshape(1, d))
        saved.append((xl, p, ps, pst, hb, o, ss, tinv, ycat, sc1, acol, dtc, br_par, cvo))
        xl = xn

    dx, loss_part, dfinal_g = _loss_head(xl, final_g.reshape(1, d), tgt)
    loss = lax.psum(loss_part[0, 0], ("x", "y", "c"))

    small_grads = [None] * nl
    dmods = [None] * nl
    win_r = [None] * nl
    wout_r = [None] * nl
    rep_names = ["b_ada", "norm_g", "a_log", "dt_bias", "dn_norm_g", "sg_ln_g", "sg_ln_b", "sg_w", "sg_b", "cv_b", "cv_ln_g",
                 "cv_ln_b", "final_g"]
    win_slab = None
    for l in reversed(range(nl)):
        xl, p, ps, pst, hb, o, ss, tinv, ycat, sc1, acol, dtc, br_par, cvo = saved[l]
        w_main_l, w_small_l = w_packed[l]
        w_out_l = w_out_full[l]
        gate_l = gate[l].reshape(1, d)
        dycat = _outproj_bwd(dx, w_out_l, gate_l)
        gmat = _tn_acc(ycat, dx, "wout_tn")
        wout_slab, dgate = _wout_grad(gmat, w_out_l, gate_l)
        dpa, do, ddng, dslg, dslb, dsgw, dsgb, dcvw, dcvb, dclg, dclb = _branch_bwd(o, p, *br_par, dycat, cvo)
        behind = (wout_slab,) if win_slab is None else (wout_slab, win_slab)
        dpb, dps, dconvw, dal, ddt, wout_r[l], *win_got = _dn_bwd(
            p, ps, pst, conv_full[l], arow[l], dtb[l], acol, dtc, do, ss, tinv, comm=(behind, ("scatter",) * len(behind)))
        if win_got:
            win_r[l + 1] = win_got[0]
        gwa = _tn_acc(hb, dpa, "win_tn_a")
        gwb = _tn_acc(hb, dpb, "win_tn_b")
        gws = _tn_acc(hb, dps, "win_tn_s")
        win_slab = _grad_slabs(gwa, gwb, gws, in_cols)
        small_grads[l] = [None, dal[0, 8:8 + nh], ddt[0, 8:8 + nh], ddng[0], dslg[0], dslb[0], dsgw, dsgb[:, :SG_GROUPS].T,
                          dcvb[0], dclg[0], dclb[0], dconvw, dcvw[:CV_K]]
        ib_args = (dpa, dpb, dps, w_main_l, w_small_l, xl, norm_g[l].reshape(1, d), sc1, dx)
        if l > 0:
            dx, dshift, dscale, dng = _inproj_bwd(*ib_args)
            dmods[l] = jnp.concatenate([dshift[0], dscale[0], dgate[0]])
            small_grads[l][0] = dng[0]
        else:
            zeros_d = jnp.zeros((d,), f32)
            dmods[0] = jnp.concatenate([zeros_d, zeros_d, dgate[0]])
            small_grads[0][0] = zeros_d
            stack = lambda j: jnp.stack([small_grads[i][j] for i in range(nl)])
            rep_grads = [jnp.stack(dmods)] + [stack(j) for j in range(11)] + [dfinal_g[0]]
            full_grads = [stack(11), stack(12)]
            grad_shapes = [g.shape for g in rep_grads + full_grads]
            gpack = _pack(rep_grads + full_grads, 1024)
            dx, dshift, dscale, dng, gpack_g, win_r[0] = _inproj_bwd(*ib_args, comm=((gpack, win_slab), ("gather2", "scatter")))
            late_shapes = [(d,), (d,), (d,)]
            (late_g,) = _exchange([_pack([dshift[0], dscale[0], dng[0]])], ["gather"], "exchange_late")

    gsum = _sum_slabs(gpack_g, "sum_small")
    gl = _unpack(gsum, grad_shapes)
    late = _unpack(_sum_slabs(late_g, "sum_late"), late_shapes)
    gl[0] = jnp.concatenate([jnp.concatenate([late[0], late[1], gl[0][0, 2 * d:]])[None], gl[0][1:]], axis=0)
    gl[1] = jnp.concatenate([late[2][None], gl[1][1:]], axis=0)
    g_rep = dict(zip(rep_names, gl[:len(rep_names)]))
    g_conv = lax.dynamic_slice_in_dim(gl[-2], me * cq_cols, cq_cols, axis=2)
    g_cvw = lax.dynamic_slice_in_dim(gl[-1], me * cvw_cols, cvw_cols, axis=2)

    g_win, d_win, nm_win, nv_win = _adam_layers(win_r, w_in, m_w_in, v_w_in, "adam_w_in", 128)
    g_wo, d_wo, nm_wo, nv_wo = _adam_layers(wout_r, w_out, m_w_out, v_w_out, "adam_w_out", 64)
    dmod_all = _unpack(gpack_g, grad_shapes, lead=(N_DEV,))[0]
    late_all = _unpack(late_g, late_shapes, lead=(N_DEV,))
    dmod_all = jnp.concatenate([jnp.concatenate([late_all[0], late_all[1], dmod_all[:, 0, 2 * d:]], axis=1)[:, None],
                                dmod_all[:, 1:]], axis=1)
    dmod_my = jnp.moveaxis(lax.dynamic_slice_in_dim(dmod_all, me * ada_cols, ada_cols, axis=2), 0, 1)
    g_wa, d_wa, nm_wa, nv_wa = _ada_bwd_adam(c_all, dmod_my, w_ada, m_w_ada, v_w_ada)

    small_w = dict(b_ada=b_ada, norm_g=norm_g, a_log=a_log, dt_bias=dt_bias, dn_norm_g=dn_norm_g, sg_ln_g=sg_ln_g, sg_ln_b=sg_ln_b,
                   sg_w=sg_w, sg_b=sg_b, cv_b=cv_b, cv_ln_g=cv_ln_g, cv_ln_b=cv_ln_b, final_g=final_g, conv_qkv=conv_qkv, cv_w=cv_w)
    small_m = dict(b_ada=m_b_ada, norm_g=m_norm_g, a_log=m_a_log, dt_bias=m_dt_bias, dn_norm_g=m_dn_norm_g, sg_ln_g=m_sg_ln_g,
                   sg_ln_b=m_sg_ln_b, sg_w=m_sg_w, sg_b=m_sg_b, cv_b=m_cv_b, cv_ln_g=m_cv_ln_g, cv_ln_b=m_cv_ln_b,
                   final_g=m_final_g, conv_qkv=m_conv_qkv, cv_w=m_cv_w)
    small_v = dict(b_ada=v_b_ada, norm_g=v_norm_g, a_log=v_a_log, dt_bias=v_dt_bias, dn_norm_g=v_dn_norm_g, sg_ln_g=v_sg_ln_g,
                   sg_ln_b=v_sg_ln_b, sg_w=v_sg_w, sg_b=v_sg_b, cv_b=v_cv_b, cv_ln_g=v_cv_ln_g, cv_ln_b=v_cv_ln_b,
                   final_g=v_final_g, conv_qkv=v_conv_qkv, cv_w=v_cv_w)
    small_g = dict(g_rep, conv_qkv=g_conv, cv_w=g_cvw)
    names = rep_names + ["conv_qkv", "cv_w"]
    shapes = [small_w[n].shape for n in names]
    gp = _pack([small_g[n] for n in names], 1024)
    sg_, sd_, sm_, sv_ = _adam_slabs(gp.reshape((1,) + gp.shape), _pack([small_w[n] for n in names], 1024),
                                     _pack([small_m[n] for n in names], 1024), _pack([small_v[n] for n in names], 1024),
                                     "adam_small", 1024)
    sgrad = dict(zip(names, _unpack(sg_, shapes)))
    sdelta = dict(zip(names, _unpack(sd_, shapes)))
    snm = dict(zip(names, _unpack(sm_, shapes)))
    snv = dict(zip(names, _unpack(sv_, shapes)))
    for dct, big in ((sgrad, (g_wa, g_win, g_wo)), (sdelta, (d_wa, d_win, d_wo)), (snm, (nm_wa, nm_win, nm_wo)), (snv, (nv_wa, nv_win, nv_wo))):
        dct["w_ada"] = big[0]
        dct["w_in"] = big[1].reshape(w_in.shape)
        dct["w_out"] = big[2].reshape(w_out.shape)

    order = ["norm_g", "w_ada", "b_ada", "w_in", "conv_qkv", "a_log", "dt_bias", "dn_norm_g", "sg_ln_g", "sg_ln_b", "sg_w", "sg_b",
             "cv_w", "cv_b", "cv_ln_g", "cv_ln_b", "w_out", "final_g"]
    outs = [loss, dx.reshape(x.shape)]
    for dct in (sgrad, sdelta, snm, snv):
        outs += [dct[n] for n in order]
    return tuple(outs)
```

```python
import functools

import jax
import jax.numpy as jnp
from jax import lax
from jax.experimental import pallas as pl
from jax.experimental.pallas import tpu as pltpu

f32 = jnp.float32
BF = jnp.bfloat16
N_DEV = 8
LANES = 128
CHUNK = 64
HD = 128
DN_CONV = 4
SG_BLOCK = 128
SG_GROUPS = 4
CV_GROUPS = 4
CV_K = 31
CV_HALO = 32
QK_HALO = 16
EPS = 1e-6
LN_EPS = 1e-5
ADAM_LR, ADAM_B1, ADAM_B2, ADAM_EPS, ADAM_WD, ADAM_STEP = 0.001, 0.9, 0.999, 1e-08, 0.01, 10
VMEM_LIMIT = 56 * 1024 * 1024
HI = lax.Precision.HIGHEST
MESH = pl.DeviceIdType.MESH


def _cp(sem=None, vmem=VMEM_LIMIT):
    return pltpu.CompilerParams(dimension_semantics=sem, vmem_limit_bytes=vmem)


def _pc(body, **kw):
    return pl.pallas_call(body, **kw)


def _dot(a, b):
    return jnp.dot(a.astype(BF), b.astype(BF), preferred_element_type=f32)


def _dot_nt(a, b):
    return lax.dot_general(a.astype(BF), b.astype(BF), (((1,), (1,)), ((), ())), preferred_element_type=f32)


def _dot_tn(a, b):
    return lax.dot_general(a.astype(BF), b.astype(BF), (((0,), (0,)), ((), ())), preferred_element_type=f32)


def _dot_hi(a, b):
    return jnp.dot(a, b, preferred_element_type=f32, precision=HI)


def _sigmoid(x):
    return 1.0 / (1.0 + jnp.exp(-x))


def _silu(x):
    return x * _sigmoid(x)


def _dsilu(x):
    s = _sigmoid(x)
    return s * (1.0 + x * (1.0 - s))


def _softplus(x):
    return jnp.maximum(x, 0.0) + jnp.log(1.0 + jnp.exp(-jnp.abs(x)))


def _gelu(x):
    return 0.5 * x * (1.0 + lax.erf(x * 0.7071067811865476))


def _dgelu(x):
    return 0.5 * (1.0 + lax.erf(x * 0.7071067811865476)) + x * jnp.exp(-0.5 * x * x) * 0.3989422804014327


def _lanecol(x, lane_iota, j):
    return jnp.sum(jnp.where(lane_iota == j, x, 0.0), axis=1, keepdims=True)


def _split(a):
    hi = a.astype(BF)
    return hi, (a - hi.astype(f32)).astype(BF)


def _dot3(a, b):
    mm = lambda u, v: jnp.dot(u, v, preferred_element_type=f32)
    return mm(a[0], b[0]) + (mm(a[0], b[1]) + mm(a[1], b[0]))


INV_BASE = 8
DN_GROUP = 4
DN_GROUP_BWD = 4


def _tri_inv(l_mats):
    n = l_mats[0].shape[0]
    r = lax.broadcasted_iota(jnp.int32, (n, n), 0)
    c = lax.broadcasted_iota(jnp.int32, (n, n), 1)
    same = lambda size: (r // size) == (c // size)
    diag = [jnp.where(same(INV_BASE), l, 0.0) for l in l_mats]
    xs = [(r == c).astype(f32) - l for l in diag]
    sp = [_split(l) for l in diag]
    size = 2
    while size < INV_BASE:
        ps = [_dot3(s, s) for s in sp]
        sp = [_split(p) for p in ps]
        sx = [_split(x) for x in xs]
        xs = [x + _dot3(a, b) for x, a, b in zip(xs, sx, sp)]
        size *= 2
    blk = INV_BASE
    while blk < n:
        join = same(2 * blk) & jnp.logical_not(same(blk))
        lo = [_split(jnp.where(join, l, 0.0)) for l in l_mats]
        sx = [_split(x) for x in xs]
        ys = [_split(_dot3(a, b)) for a, b in zip(lo, sx)]
        xs = [x - _dot3(a, b) for x, a, b in zip(xs, sx, ys)]
        blk *= 2
    return xs


def _me():
    x, y, c = lax.axis_index("x"), lax.axis_index("y"), lax.axis_index("c")
    return x, y, c, 4 * x + 2 * y + c


def _exchange(arrs, kinds, name):
    n = len(arrs)

    def body(*refs):
        for phase in ("first", "middle", "last"):
            _comm_phase(refs[:n], refs[n:2 * n], kinds, *refs[2 * n:], phase)

    any_spec = pl.BlockSpec(memory_space=pl.ANY)
    return _pc(body, name=name, out_shape=_comm_out_shapes(arrs, kinds), in_specs=[any_spec] * n, out_specs=tuple([any_spec] * n),
               scratch_shapes=_comm_sems(n))(*arrs)


def _comm_out_shapes(arrs, kinds):
    return tuple(jax.ShapeDtypeStruct(a.shape if k == "scatter" else (N_DEV,) + a.shape, a.dtype) for a, k in zip(arrs, kinds))


def _comm_sems(n):
    return [pltpu.SemaphoreType.DMA((n, N_DEV - 1)), pltpu.SemaphoreType.DMA((n, N_DEV - 1)), pltpu.SemaphoreType.DMA((n,))]


def _comm_phase(ins, outs, kinds, send, recv, loc, phase):
    x, y, c, me = _me()
    flips = ((1, 0), (0, 1), (1, 1))
    chips = [(jnp.bitwise_xor(x, fx), jnp.bitwise_xor(y, fy)) for fx, fy in flips]
    for a in range(len(ins)):
        slot = lambda px, py, pc, a=a: outs[a].at[4 * px + 2 * py + pc]

        def copy(k, src, block, to, a=a, slot=slot):
            return pltpu.make_async_remote_copy(src_ref=src, dst_ref=slot(*block), send_sem=send.at[a, k], recv_sem=recv.at[a, k],
                                                device_id=to, device_id_type=MESH)

        if kinds[a] == "gather2":
            mine = pltpu.make_async_copy(ins[a], slot(x, y, c), loc.at[a])
            first = [copy(0, ins[a], (x, y, c), (x, y, 1 - c))] + [copy(1 + j, ins[a], (x, y, c), (px, py, c))
                                                                  for j, (px, py) in enumerate(chips)]
            passed = [copy(4 + j, slot(px, py, c), (px, py, c), (x, y, 1 - c)) for j, (px, py) in enumerate(chips)]
            if phase == "first":
                for cp in [mine] + first:
                    cp.start()
            elif phase == "middle":
                for j, (px, py) in enumerate(chips):
                    copy(1 + j, ins[a], (px, py, c), (x, y, c)).wait_recv()
                    passed[j].start()
            else:
                copy(0, ins[a], (x, y, 1 - c), (x, y, c)).wait_recv()
                for j, (px, py) in enumerate(chips):
                    copy(4 + j, ins[a], (px, py, 1 - c), (x, y, c)).wait_recv()
                for cp in first + passed:
                    cp.wait_send()
                mine.wait()
        else:
            own = ins[a].at[me] if kinds[a] == "scatter" else ins[a]
            copies = [pltpu.make_async_copy(own, outs[a].at[me], loc.at[a])]
            for k in range(1, N_DEV):
                px = jnp.bitwise_xor(x, (k >> 2) & 1)
                py = jnp.bitwise_xor(y, (k >> 1) & 1)
                pc = jnp.bitwise_xor(c, k & 1)
                src = ins[a].at[4 * px + 2 * py + pc] if kinds[a] == "scatter" else ins[a]
                copies.append(pltpu.make_async_remote_copy(src_ref=src, dst_ref=outs[a].at[me], send_sem=send.at[a, k - 1],
                                                           recv_sem=recv.at[a, k - 1], device_id=(px, py, pc), device_id_type=MESH))
            if phase == "first":
                for cp in copies:
                    cp.start()
            elif phase == "last":
                for cp in copies:
                    cp.wait()


def _with_comm(compute, comm, name, grid, args, in_specs, out_shape, out_specs, scratch):
    arrs, kinds = comm if comm else ((), ())
    n, n_in, n_out, n_scr = len(arrs), len(args), len(out_shape), len(scratch)
    marks = {"first": [0] * len(grid), "middle": [(3 * grid[0]) // 4] + [0] * (len(grid) - 1), "last": [g - 1 for g in grid]}

    def at_step(which):
        ok = pl.program_id(0) == marks[which][0]
        for ax in range(1, len(grid)):
            ok = ok & (pl.program_id(ax) == marks[which][ax])
        return ok

    def body(*refs):
        ins, cin = refs[:n_in], refs[n_in:n_in + n]
        outs, cout = refs[n_in + n:n_in + n + n_out], refs[n_in + n + n_out:n_in + 2 * n + n_out]
        scr, sems = refs[n_in + 2 * n + n_out:n_in + 2 * n + n_out + n_scr], refs[n_in + 2 * n + n_out + n_scr:]
        for phase in ("first", "middle") if n else ():
            @pl.when(at_step(phase))
            def _(phase=phase):
                _comm_phase(cin, cout, kinds, *sems, phase)
        compute(*ins, *outs, *scr)
        if n:
            @pl.when(at_step("last"))
            def _():
                _comm_phase(cin, cout, kinds, *sems, "last")

    any_spec = pl.BlockSpec(memory_space=pl.ANY)
    return _pc(body, name=name + ("_comm" if n else ""), grid=tuple(grid),
               out_shape=tuple(out_shape) + (_comm_out_shapes(arrs, kinds) if n else ()),
               in_specs=list(in_specs) + [any_spec] * n, out_specs=tuple(out_specs) + (any_spec,) * n,
               scratch_shapes=list(scratch) + (_comm_sems(n) if n else []),
               compiler_params=_cp(("arbitrary",) * len(grid)))(*args, *arrs)


def _ada_fwd(c_all, w_ada, b_my):
    nl, d, cols = w_ada.shape

    def body(c_ref, w_ref, b_ref, o_ref):
        ca = _silu(c_ref[...])
        o_ref[0] = _dot(ca, w_ref[0]) + b_ref[0]

    return _pc(body, name="ada_fwd", grid=(nl,), out_shape=jax.ShapeDtypeStruct((nl, N_DEV, cols), f32),
               in_specs=[pl.BlockSpec((N_DEV, d), lambda l: (0, 0)), pl.BlockSpec((1, d, cols), lambda l: (l, 0, 0)),
                         pl.BlockSpec((1, 1, cols), lambda l: (l, 0, 0))],
               out_specs=pl.BlockSpec((1, N_DEV, cols), lambda l: (l, 0, 0)), compiler_params=_cp(("arbitrary",)))(c_all, w_ada, b_my)


def _adam_math(w, g, m, v):
    m = ADAM_B1 * m + (1.0 - ADAM_B1) * g
    v = ADAM_B2 * v + (1.0 - ADAM_B2) * (g * g)
    m_hat = m / (1.0 - ADAM_B1 ** ADAM_STEP)
    v_hat = v / (1.0 - ADAM_B2 ** ADAM_STEP)
    delta = -ADAM_LR * (m_hat / (jnp.sqrt(v_hat) + ADAM_EPS) + ADAM_WD * w)
    return delta, m, v


def _ada_bwd_adam(c_all, dmod_my, w, m, v):
    nl, d, cols = w.shape
    tk = 512

    def body(c_ref, dm_ref, w_ref, m_ref, v_ref, g_ref, dl_ref, nm_ref, nv_ref):
        ca = _silu(c_ref[...])
        g = _dot_tn(ca, dm_ref[0])
        dl, nm, nv = _adam_math(w_ref[0], g, m_ref[0], v_ref[0])
        g_ref[0] = g
        dl_ref[0] = dl
        nm_ref[0] = nm
        nv_ref[0] = nv

    wspec = pl.BlockSpec((1, tk, cols), lambda l, k: (l, k, 0))
    shp = jax.ShapeDtypeStruct(w.shape, f32)
    return _pc(body, name="ada_bwd_adam", grid=(nl, d // tk), out_shape=(shp, shp, shp, shp),
               in_specs=[pl.BlockSpec((N_DEV, tk), lambda l, k: (0, k)), pl.BlockSpec((1, N_DEV, cols), lambda l, k: (l, 0, 0)),
                         wspec, wspec, wspec],
               out_specs=(wspec, wspec, wspec, wspec), compiler_params=_cp(("arbitrary", "arbitrary")))(c_all, dmod_my, w, m, v)


def _adam_slabs(slabs, w, m, v, name, tr):
    n, rows, cols = slabs.shape
    tr = min(tr, rows)

    def body(s_ref, w_ref, m_ref, v_ref, g_ref, dl_ref, nm_ref, nv_ref):
        g = s_ref[0].astype(f32)
        for j in range(1, n):
            g = g + s_ref[j].astype(f32)
        dl, nm, nv = _adam_math(w_ref[...], g, m_ref[...], v_ref[...])
        g_ref[...] = g
        dl_ref[...] = dl
        nm_ref[...] = nm
        nv_ref[...] = nv

    spec = pl.BlockSpec((tr, cols), lambda i: (i, 0))
    shp = jax.ShapeDtypeStruct((rows, cols), f32)
    return _pc(body, name=name, grid=(rows // tr,), out_shape=(shp, shp, shp, shp),
               in_specs=[pl.BlockSpec((n, tr, cols), lambda i: (0, i, 0)), spec, spec, spec],
               out_specs=(spec, spec, spec, spec), compiler_params=_cp(("arbitrary",)))(slabs, w, m, v)


def _adam_layers(slabs, w, m, v, name, tr):
    nl, rows, cols = w.shape
    n = slabs[0].shape[0]
    tr = min(tr, rows)

    def body(*refs):
        s_refs, (w_ref, m_ref, v_ref, g_ref, dl_ref, nm_ref, nv_ref) = refs[:nl], refs[nl:]
        for l in range(nl):
            @pl.when(pl.program_id(0) == l)
            def _(l=l):
                g = s_refs[l][0].astype(f32)
                for j in range(1, n):
                    g = g + s_refs[l][j].astype(f32)
                dl, nm, nv = _adam_math(w_ref[0], g, m_ref[0], v_ref[0])
                g_ref[0] = g
                dl_ref[0] = dl
                nm_ref[0] = nm
                nv_ref[0] = nv

    spec = pl.BlockSpec((1, tr, cols), lambda l, i: (l, i, 0))
    s_specs = [pl.BlockSpec((n, tr, cols), lambda l, i, j=j: (0, jnp.where(l == j, i, 0), 0)) for j in range(nl)]
    shp = jax.ShapeDtypeStruct(w.shape, f32)
    return _pc(body, name=name, grid=(nl, rows // tr), out_shape=(shp, shp, shp, shp),
               in_specs=s_specs + [spec, spec, spec], out_specs=(spec, spec, spec, spec),
               compiler_params=_cp(("arbitrary", "arbitrary")))(*slabs, w, m, v)


def _sum_slabs(slabs, name):
    n, rows, cols = slabs.shape
    tr = min(rows, 1024)

    def body(s_ref, o_ref):
        g = s_ref[0]
        for j in range(1, n):
            g = g + s_ref[j]
        o_ref[...] = g

    return _pc(body, name=name, grid=(rows // tr,), out_shape=jax.ShapeDtypeStruct((rows, cols), f32),
               in_specs=[pl.BlockSpec((n, tr, cols), lambda i: (0, i, 0))], out_specs=pl.BlockSpec((tr, cols), lambda i: (i, 0)),
               compiler_params=_cp(("arbitrary",)))(slabs)


NAT_Z, NAT_SMALL, NAT_REST = 3072, 4096, 4112


def _repack_w(g_win):
    n, d, cols = g_win.shape
    tr = 256

    def body(g_ref, wm_ref, ws_ref):
        nat = jnp.concatenate([g_ref[j] for j in range(n)], axis=1)
        wm_ref[...] = jnp.concatenate([nat[:, NAT_Z:NAT_SMALL], nat[:, NAT_REST:], nat[:, :NAT_Z]], axis=1)
        ws_ref[...] = jnp.concatenate([nat[:, NAT_SMALL:NAT_REST], jnp.zeros((tr, LANES - 16), nat.dtype)], axis=1)

    return _pc(body, name="repack_w", grid=(d // tr,),
               out_shape=(jax.ShapeDtypeStruct((d, n * cols - 16), g_win.dtype), jax.ShapeDtypeStruct((d, LANES), g_win.dtype)),
               in_specs=[pl.BlockSpec((n, tr, cols), lambda i: (0, i, 0))],
               out_specs=(pl.BlockSpec((tr, n * cols - 16), lambda i: (i, 0)), pl.BlockSpec((tr, LANES), lambda i: (i, 0))),
               compiler_params=_cp(("arbitrary",)))(g_win)


def _grad_slabs(gwa, gwb, gws, cols):
    d = gwa.shape[0]
    tr = 256

    def body(a_ref, b_ref, s_ref, o_ref):
        a = a_ref[...]
        nat = jnp.concatenate([b_ref[...], a[:, :NAT_SMALL - NAT_Z], s_ref[:, 0:16], a[:, NAT_SMALL - NAT_Z:]], axis=1)
        for j in range(N_DEV):
            o_ref[j] = nat[:, j * cols:(j + 1) * cols].astype(BF)

    return _pc(body, name="grad_slabs", grid=(d // tr,), out_shape=jax.ShapeDtypeStruct((N_DEV, d, cols), BF),
               in_specs=[pl.BlockSpec((tr, gwa.shape[1]), lambda i: (i, 0)), pl.BlockSpec((tr, gwb.shape[1]), lambda i: (i, 0)),
                         pl.BlockSpec((tr, LANES), lambda i: (i, 0))],
               out_specs=pl.BlockSpec((N_DEV, tr, cols), lambda i: (0, i, 0)), compiler_params=_cp(("arbitrary",)))(gwa, gwb, gws)


def _inproj(x, ng, sc1, sh, wmain, wsmall):
    t, d = x.shape
    n = wmain.shape[1]
    tm, tn, rb = min(t, 1024), 1024, 128

    def body(x_ref, ng_ref, sc_ref, sh_ref, wm_ref, ws_ref, p_ref, ps_ref, h_ref, hs_ref):
        @pl.when(pl.program_id(1) == 0)
        def _():
            def rows(i, carry):
                r0 = pl.multiple_of(i * rb, rb)
                xv = x_ref[pl.ds(r0, rb), :]
                r = lax.rsqrt(jnp.mean(xv * xv, axis=1, keepdims=True) + EPS)
                hb = (((xv * r) * ng_ref[...]) * sc_ref[...] + sh_ref[...]).astype(BF)
                hs_ref[pl.ds(r0, rb), :] = hb
                h_ref[pl.ds(r0, rb), :] = hb
                return carry
            lax.fori_loop(0, tm // rb, rows, 0)
            ps_ref[...] = jnp.dot(hs_ref[...], ws_ref[...], preferred_element_type=f32)
        p_ref[...] = jnp.dot(hs_ref[...], wm_ref[...], preferred_element_type=f32).astype(BF)

    vec = pl.BlockSpec((1, d), lambda i, j: (0, 0))
    return _pc(body, name="inproj", grid=(t // tm, n // tn),
               out_shape=(jax.ShapeDtypeStruct((t, n), BF), jax.ShapeDtypeStruct((t, LANES), f32), jax.ShapeDtypeStruct((t, d), BF)),
               in_specs=[pl.BlockSpec((tm, d), lambda i, j: (i, 0)), vec, vec, vec,
                         pl.BlockSpec((d, tn), lambda i, j: (0, j)), pl.BlockSpec((d, LANES), lambda i, j: (0, 0))],
               out_specs=(pl.BlockSpec((tm, tn), lambda i, j: (i, j)), pl.BlockSpec((tm, LANES), lambda i, j: (i, 0)),
                          pl.BlockSpec((tm, d), lambda i, j: (i, 0))),
               scratch_shapes=[pltpu.VMEM((tm, d), BF)], compiler_params=_cp(("arbitrary", "arbitrary")))(x, ng, sc1, sh, wmain, wsmall)


P_Z, P_SGU, P_SGV, P_SGG, P_CVA, P_CVB, P_CVG, P_Q = 0, 1024, 1536, 2048, 2560, 3072, 3584, 4096


def _dn_specs(t, tb, rev):
    nt = t // tb
    ti = (lambda i: nt - 1 - i) if rev else (lambda i: i)
    qb = P_Q // 1024
    hb = tb // QK_HALO
    specs = []
    for s in range(3):
        specs.append(pl.BlockSpec((tb, 1024), lambda i, s=s: (ti(i), qb + s)))
    for s in range(3):
        specs.append(pl.BlockSpec((QK_HALO, 1024), lambda i, s=s: (jnp.maximum(ti(i) * hb - 1, 0), qb + s)))
    specs.append(pl.BlockSpec((DN_CONV, 3072), lambda i: (0, 0)))
    specs.append(pl.BlockSpec((tb, LANES), lambda i: (ti(i), 0)))
    specs.append(pl.BlockSpec((tb // CHUNK, 16, CHUNK), lambda i: (ti(i), 0, 0)))
    specs.append(pl.BlockSpec((1, LANES), lambda i: (0, 0)))
    specs.append(pl.BlockSpec((1, LANES), lambda i: (0, 0)))
    specs.append(pl.BlockSpec((16, CHUNK), lambda i: (0, 0)))
    specs.append(pl.BlockSpec((16, CHUNK), lambda i: (0, 0)))
    return specs, ti


def _dn_conv(first, tiles, halos, cw_ref, xs_ref, cpre_ref, tb):
    for s in range(3):
        c0 = s * 1024
        xs_ref[0:QK_HALO, c0:c0 + 1024] = jnp.where(first, 0.0, halos[s][...].astype(f32))
        xs_ref[QK_HALO:QK_HALO + tb, c0:c0 + 1024] = tiles[s][...].astype(f32)
    for sub in range(3072 // 256):
        cs = slice(sub * 256, (sub + 1) * 256)
        acc = cw_ref[0:1, cs] * xs_ref[pl.ds(QK_HALO - 3, tb), cs]
        for j in range(1, DN_CONV):
            acc = acc + cw_ref[j:j + 1, cs] * xs_ref[pl.ds(QK_HALO - 3 + j, tb), cs]
        cpre_ref[:, cs] = acc


def _dn_head(cpre_ref, r0, h, beta_all, gc_all, gcrow_ref, lane, tri_incl, tri_strict):
    rows = pl.ds(r0, CHUNK)
    cq = cpre_ref[rows, pl.ds(HD * h, HD)]
    ck = cpre_ref[rows, pl.ds(1024 + HD * h, HD)]
    cv = cpre_ref[rows, pl.ds(2048 + HD * h, HD)]
    sq, sk, v = _silu(cq), _silu(ck), _silu(cv)
    rq = lax.rsqrt(jnp.sum(sq * sq, axis=1, keepdims=True) + EPS)
    rk = lax.rsqrt(jnp.sum(sk * sk, axis=1, keepdims=True) + EPS)
    q, k = sq * rq, sk * rk
    beta = _lanecol(beta_all, lane, h)
    gcc = _lanecol(gc_all, lane, 8 + h)
    gcr = gcrow_ref[pl.ds(8 + h, 1), :]
    diff = gcc - gcr
    dmat = jnp.where(tri_incl, jnp.exp(jnp.where(tri_incl, diff, 0.0)), 0.0)
    gam = jnp.exp(gcc)
    glast = gcr[:, CHUNK - 1:CHUNK]
    gam_last = jnp.exp(glast)
    kscale = jnp.exp(glast - gcc)
    qs = q * (HD ** -0.5)
    kb = k * beta
    vb = v * beta
    kbg = kb * gam
    kk = _dot_nt(kb, k)
    lmat = jnp.where(tri_strict, kk * dmat, 0.0)
    pmat = _dot_nt(qs, k) * dmat
    return dict(cq=cq, ck=ck, cv=cv, sq=sq, sk=sk, v=v, rq=rq, rk=rk, q=q, k=k, beta=beta, gcc=gcc, dmat=dmat, gam=gam,
                gam_last=gam_last, kscale=kscale, qs=qs, kb=kb, vb=vb, kbg=kbg, lmat=lmat, pmat=pmat)


def _dn_gates(ps_ref, arow_ref, dtb_ref, pst_ref, acol_ref, dtc_ref):
    ps = ps_ref[...]
    beta_all = _sigmoid(ps)
    g_all = arow_ref[...] * _softplus(ps + dtb_ref[...])
    return ps, beta_all, g_all


def _dn_fwd(p, ps, pst, convw, arow, dtb, acol, dtc, comm=()):
    t = p.shape[0]
    nh = 1024 // HD
    tb = min(t, 256)
    nc = tb // CHUNK
    specs, _ = _dn_specs(t, tb, False)

    def compute(q_ref, k_ref, v_ref, hq_ref, hk_ref, hv_ref, cw_ref, ps_ref, pst_ref, arow_ref, dtb_ref, acol_ref, dtc_ref,
             o_ref, ss_ref, ti_ref, xs_ref, cpre_ref, st_ref, gcrow_ref, ball_ref, gall_ref):
        i = pl.program_id(0)

        @pl.when(i == 0)
        def _():
            st_ref[...] = jnp.zeros_like(st_ref)

        _dn_conv(i == 0, (q_ref, k_ref, v_ref), (hq_ref, hk_ref, hv_ref), cw_ref, xs_ref, cpre_ref, tb)
        _, beta_all, g_all = _dn_gates(ps_ref, arow_ref, dtb_ref, pst_ref, acol_ref, dtc_ref)
        ball_ref[...] = beta_all
        gall_ref[...] = g_all
        ri = lax.broadcasted_iota(jnp.int32, (CHUNK, CHUNK), 0)
        ci = lax.broadcasted_iota(jnp.int32, (CHUNK, CHUNK), 1)
        tri_incl, tri_strict = ri >= ci, ri > ci
        lower = tri_incl.astype(f32)
        upper = (ri <= ci).astype(f32)
        lane = lax.broadcasted_iota(jnp.int32, (CHUNK, LANES), 1)

        def chunk_group(cg, carry):
            hs = range(nh)
            mm = lambda a, b: jnp.dot(a, b, preferred_element_type=f32)
            cs = [cg * DN_GROUP + j for j in range(DN_GROUP)]
            r0s = [pl.multiple_of(c * CHUNK, CHUNK) for c in cs]
            hd = []
            for j in range(DN_GROUP):
                gc_all = _dot_hi(lower, gall_ref[pl.ds(r0s[j], CHUNK), :])
                g_rows = acol_ref[...] * _softplus(pst_ref[cs[j]] + dtc_ref[...])
                gcrow_ref[j] = _dot_hi(g_rows, upper)
                beta_c = ball_ref[pl.ds(r0s[j], CHUNK), :]
                hd += [_dn_head(cpre_ref, r0s[j], h, beta_c, gc_all, gcrow_ref.at[j], lane, tri_incl, tri_strict) for h in hs]
            tinv = _tri_inv([d["lmat"] for d in hd])
            tib = [x.astype(BF) for x in tinv]
            u = [mm(tib[i], hd[i]["vb"].astype(BF)) for i in range(len(hd))]
            w = [mm(tib[i], hd[i]["kbg"].astype(BF)).astype(BF) for i in range(len(hd))]
            qg = [(d["qs"] * d["gam"]).astype(BF) for d in hd]
            kt = [(d["k"] * d["kscale"]).astype(BF) for d in hd]
            pm = [d["pmat"].astype(BF) for d in hd]
            for j in range(DN_GROUP):
                ix = [j * nh + h for h in hs]
                s = [st_ref[h] for h in hs]
                sb = [x.astype(BF) for x in s]
                vnb = [(u[ix[h]] - mm(w[ix[h]], sb[h])).astype(BF) for h in hs]
                o = [mm(qg[ix[h]], sb[h]) + mm(pm[ix[h]], vnb[h]) for h in hs]
                snew = [s[h] * hd[ix[h]]["gam_last"] + lax.dot_general(kt[ix[h]], vnb[h], (((0,), (0,)), ((), ())),
                                                                      preferred_element_type=f32) for h in hs]
                for h in hs:
                    o_ref[pl.ds(r0s[j], CHUNK), pl.ds(HD * h, HD)] = o[h]
                    ss_ref[cs[j], h] = sb[h]
                    ti_ref[cs[j], h] = tinv[ix[h]]
                    st_ref[h] = snew[h]
            return carry

        lax.fori_loop(0, nc // DN_GROUP, chunk_group, 0)

    return _with_comm(
        compute, comm, "dn_fwd", (t // tb,), (p, p, p, p, p, p, convw, ps, pst, arow, dtb, acol, dtc), specs,
        (jax.ShapeDtypeStruct((t, 1024), f32), jax.ShapeDtypeStruct((t // CHUNK, nh, HD, HD), BF),
         jax.ShapeDtypeStruct((t // CHUNK, nh, CHUNK, CHUNK), f32)),
        (pl.BlockSpec((tb, 1024), lambda i: (i, 0)), pl.BlockSpec((nc, nh, HD, HD), lambda i: (i, 0, 0, 0)),
         pl.BlockSpec((nc, nh, CHUNK, CHUNK), lambda i: (i, 0, 0, 0))),
        [pltpu.VMEM((tb + QK_HALO, 3072), f32), pltpu.VMEM((tb, 3072), f32), pltpu.VMEM((nh, HD, HD), f32),
         pltpu.VMEM((DN_GROUP, 16, CHUNK), f32), pltpu.VMEM((tb, LANES), f32), pltpu.VMEM((tb, LANES), f32)])


def _dn_bwd(p, ps, pst, convw, arow, dtb, acol, dtc, do, ss, tinv_all, comm=()):
    t = p.shape[0]
    nh = 1024 // HD
    tb = min(t, 256)
    nc = tb // CHUNK
    nt = t // tb
    specs, ti = _dn_specs(t, tb, True)
    specs = specs + [pl.BlockSpec((tb, 1024), lambda i: (ti(i), 0)), pl.BlockSpec((nc, nh, HD, HD), lambda i: (ti(i), 0, 0, 0)),
                     pl.BlockSpec((nc, nh, CHUNK, CHUNK), lambda i: (ti(i), 0, 0, 0))]

    def compute(q_ref, k_ref, v_ref, hq_ref, hk_ref, hv_ref, cw_ref, ps_ref, pst_ref, arow_ref, dtb_ref, acol_ref, dtc_ref,
             do_ref, ss_ref, ti_ref,
             dp_ref, dps_ref, dcw_ref, dal_ref, ddt_ref,
             xs_ref, cpre_ref, dst_ref, gcrow_ref, ball_ref, gall_ref, dcs_ref, dball_ref, dgcall_ref):
        i = pl.program_id(0)

        @pl.when(i == 0)
        def _():
            dst_ref[...] = jnp.zeros_like(dst_ref)
            dcs_ref[...] = jnp.zeros_like(dcs_ref)
            dcw_ref[...] = jnp.zeros_like(dcw_ref)
            dal_ref[...] = jnp.zeros_like(dal_ref)
            ddt_ref[...] = jnp.zeros_like(ddt_ref)

        _dn_conv(i == nt - 1, (q_ref, k_ref, v_ref), (hq_ref, hk_ref, hv_ref), cw_ref, xs_ref, cpre_ref, tb)
        ps, beta_all, g_all = _dn_gates(ps_ref, arow_ref, dtb_ref, pst_ref, acol_ref, dtc_ref)
        ball_ref[...] = beta_all
        gall_ref[...] = g_all
        ri = lax.broadcasted_iota(jnp.int32, (CHUNK, CHUNK), 0)
        ci = lax.broadcasted_iota(jnp.int32, (CHUNK, CHUNK), 1)
        tri_incl, tri_strict = ri >= ci, ri > ci
        lower = tri_incl.astype(f32)
        upper = (ri <= ci).astype(f32)
        lane = lax.broadcasted_iota(jnp.int32, (CHUNK, LANES), 1)
        row_last = lax.broadcasted_iota(jnp.int32, (CHUNK, 1), 0) == CHUNK - 1

        def chunk_group(cg, carry):
            hs = range(nh)
            bf = lambda a: a.astype(BF)
            mm = lambda a, b: jnp.dot(a, b, preferred_element_type=f32)
            mnt = lambda a, b: lax.dot_general(a, b, (((1,), (1,)), ((), ())), preferred_element_type=f32)
            mtn = lambda a, b: lax.dot_general(a, b, (((0,), (0,)), ((), ())), preferred_element_type=f32)
            rsum = lambda a: jnp.sum(a, axis=1, keepdims=True)
            grp = range(DN_GROUP_BWD)
            cs = [nc - 1 - (cg * DN_GROUP_BWD + j) for j in grp]
            r0s = [pl.multiple_of(c * CHUNK, CHUNK) for c in cs]
            nn = range(DN_GROUP_BWD * nh)
            jof = [i // nh for i in nn]
            hof = [i % nh for i in nn]
            hd = []
            for j in grp:
                rows = pl.ds(r0s[j], CHUNK)
                gc_all = _dot_hi(lower, gall_ref[rows, :])
                g_rows = acol_ref[...] * _softplus(pst_ref[cs[j]] + dtc_ref[...])
                gcrow_ref[j] = _dot_hi(g_rows, upper)
                beta_c = ball_ref[rows, :]
                hd += [_dn_head(cpre_ref, r0s[j], h, beta_c, gc_all, gcrow_ref.at[j], lane, tri_incl, tri_strict) for h in hs]
            tib = [bf(ti_ref[cs[jof[i]], hof[i]]) for i in nn]
            sb = [ss_ref[cs[jof[i]], hof[i]] for i in nn]
            vbb = [bf(d["vb"]) for d in hd]
            kgb = [bf(d["kbg"]) for d in hd]
            u = [mm(tib[i], vbb[i]) for i in nn]
            wb = [bf(mm(tib[i], kgb[i])) for i in nn]
            vnb = [bf(u[i] - mm(wb[i], sb[i])) for i in nn]
            qgb = [bf(d["qs"] * d["gam"]) for d in hd]
            kt = [d["k"] * d["kscale"] for d in hd]
            ktb = [bf(x) for x in kt]
            pmb = [bf(d["pmat"]) for d in hd]
            dob = [bf(do_ref[pl.ds(r0s[jof[i]], CHUNK), pl.ds(HD * hof[i], HD)]) for i in nn]
            dvn0 = [mtn(pmb[i], dob[i]) for i in nn]
            dsn0 = [mtn(qgb[i], dob[i]) for i in nn]
            dpm = [jnp.where(tri_incl, mnt(dob[i], vnb[i]), 0.0) for i in nn]
            dqg = [mnt(dob[i], sb[i]) for i in nn]
            dvnb, dkt, dgl = [], [], []
            for j in grp:
                ix = [j * nh + h for h in hs]
                dsn = [dst_ref[h] for h in hs]
                dsnb = [bf(x) for x in dsn]
                dvn_j = [bf(dvn0[ix[h]] + mm(ktb[ix[h]], dsnb[h])) for h in hs]
                dsnew = [dsn0[ix[h]] + hd[ix[h]]["gam_last"] * dsn[h] - mtn(wb[ix[h]], dvn_j[h]) for h in hs]
                for h in hs:
                    dst_ref[h] = dsnew[h]
                dkt += [mnt(vnb[ix[h]], dsnb[h]) for h in hs]
                dgl += [jnp.sum(rsum(sb[ix[h]].astype(f32) * dsn[h]), axis=0, keepdims=True) for h in hs]
                dvnb += dvn_j
            dwb = [bf(-mnt(dvnb[i], sb[i])) for i in nn]
            dtm = [mnt(dvnb[i], vbb[i]) + mnt(dwb[i], kgb[i]) for i in nn]
            dvb = [mtn(tib[i], dvnb[i]) for i in nn]
            dkbg = [mtn(tib[i], dwb[i]) for i in nn]
            x1 = [bf(mtn(tib[i], bf(dtm[i]))) for i in nn]
            dl = [jnp.where(tri_strict, -mnt(x1[i], tib[i]), 0.0) for i in nn]
            mmat = [dl[i] * hd[i]["lmat"] + dpm[i] * hd[i]["pmat"] for i in nn]
            dkkb = [bf(dl[i] * hd[i]["dmat"]) for i in nn]
            dqkb = [bf(dpm[i] * hd[i]["dmat"]) for i in nn]
            kb16 = [bf(d["k"]) for d in hd]
            dkb = [mm(dkkb[i], kb16[i]) + dkbg[i] * hd[i]["gam"] for i in nn]
            dk = [mtn(dkkb[i], bf(hd[i]["kb"])) + mtn(dqkb[i], bf(hd[i]["qs"])) + dkt[i] * hd[i]["kscale"] + dkb[i] * hd[i]["beta"]
                  for i in nn]
            dq = [(mm(dqkb[i], kb16[i]) + dqg[i] * hd[i]["gam"]) * (HD ** -0.5) for i in nn]
            dbeta = [rsum(dkb[i] * hd[i]["k"] + dvb[i] * hd[i]["v"]) for i in nn]
            dgam = [rsum(dkbg[i] * hd[i]["kb"] + dqg[i] * hd[i]["qs"]) for i in nn]
            ktdk = [rsum(dkt[i] * kt[i]) for i in nn]
            csum = [rsum(jnp.where(ri == ci, jnp.sum(mmat[i], axis=0, keepdims=True), 0.0)) for i in nn]
            for j in grp:
                rows = pl.ds(r0s[j], CHUNK)
                dbeta_all = jnp.zeros((CHUNK, LANES), f32)
                dgc_all = jnp.zeros((CHUNK, LANES), f32)
                for h in hs:
                    i = j * nh + h
                    d = hd[i]
                    extra = jnp.sum(ktdk[i], axis=0, keepdims=True) + dgl[i] * d["gam_last"]
                    dgc = rsum(mmat[i]) - csum[i] + dgam[i] * d["gam"] - ktdk[i] + jnp.where(row_last, extra, 0.0)
                    dbeta_all = jnp.where(lane == h, dbeta[i], dbeta_all)
                    dgc_all = jnp.where(lane == 8 + h, dgc, dgc_all)
                    dsq = d["rq"] * (dq[i] - d["q"] * rsum(dq[i] * d["q"]))
                    dsk = d["rk"] * (dk[i] - d["k"] * rsum(dk[i] * d["k"]))
                    dcs_ref[rows, pl.ds(HD * h, HD)] = dsq * _dsilu(d["cq"])
                    dcs_ref[rows, pl.ds(1024 + HD * h, HD)] = dsk * _dsilu(d["ck"])
                    dcs_ref[rows, pl.ds(2048 + HD * h, HD)] = (dvb[i] * d["beta"]) * _dsilu(d["cv"])
                dball_ref[rows, :] = dbeta_all
                dgcall_ref[rows, :] = _dot_hi(upper, dgc_all)
            return carry

        lax.fori_loop(0, nc // DN_GROUP_BWD, chunk_group, 0)

        dbeta_t = dball_ref[...]
        dg_t = dgcall_ref[...]
        sg = _sigmoid(ps + dtb_ref[...])
        da = dg_t * arow_ref[...] * sg
        dps_ref[...] = dbeta_t * beta_all * (1.0 - beta_all) + da
        ddt_ref[...] += jnp.sum(da, axis=0, keepdims=True)
        dal_ref[...] += jnp.sum(dg_t * g_all, axis=0, keepdims=True)

        for sub in range(3072 // 256):
            cs = slice(sub * 256, (sub + 1) * 256)
            dc = dcs_ref[0:tb, cs]
            acc = cw_ref[DN_CONV - 1:DN_CONV, cs] * dc
            for j in range(DN_CONV - 1):
                acc = acc + cw_ref[j:j + 1, cs] * dcs_ref[pl.ds(DN_CONV - 1 - j, tb), cs]
            dp_ref[:, cs] = acc.astype(BF)
            for j in range(DN_CONV):
                dcw_ref[j:j + 1, cs] += jnp.sum(dc * xs_ref[pl.ds(QK_HALO - 3 + j, tb), cs], axis=0, keepdims=True)
        dcs_ref[tb:tb + QK_HALO, :] = dcs_ref[0:QK_HALO, :]

    vecl = pl.BlockSpec((1, LANES), lambda i: (0, 0))
    return _with_comm(
        compute, comm, "dn_bwd", (nt,), (p, p, p, p, p, p, convw, ps, pst, arow, dtb, acol, dtc, do, ss, tinv_all), specs,
        (jax.ShapeDtypeStruct((t, 3072), BF), jax.ShapeDtypeStruct((t, LANES), f32), jax.ShapeDtypeStruct((DN_CONV, 3072), f32),
         jax.ShapeDtypeStruct((1, LANES), f32), jax.ShapeDtypeStruct((1, LANES), f32)),
        (pl.BlockSpec((tb, 3072), lambda i: (ti(i), 0)), pl.BlockSpec((tb, LANES), lambda i: (ti(i), 0)),
         pl.BlockSpec((DN_CONV, 3072), lambda i: (0, 0)), vecl, vecl),
        [pltpu.VMEM((tb + QK_HALO, 3072), f32), pltpu.VMEM((tb, 3072), f32), pltpu.VMEM((nh, HD, HD), f32),
         pltpu.VMEM((DN_GROUP_BWD, 16, CHUNK), f32), pltpu.VMEM((tb, LANES), f32), pltpu.VMEM((tb, LANES), f32),
         pltpu.VMEM((tb + QK_HALO, 3072), f32), pltpu.VMEM((tb, LANES), f32), pltpu.VMEM((tb, LANES), f32)])


def _ln_fwd(x, g, b):
    mu = jnp.mean(x, axis=1, keepdims=True)
    xc = x - mu
    rstd = lax.rsqrt(jnp.mean(xc * xc, axis=1, keepdims=True) + LN_EPS)
    xhat = xc * rstd
    return xhat * g + b, xhat, rstd


def _ln_bwd(dy, xhat, rstd, g):
    dxh = dy * g
    return rstd * (dxh - jnp.mean(dxh, axis=1, keepdims=True) - xhat * jnp.mean(dxh * xhat, axis=1, keepdims=True))


def _br_specs(t, tb, rev):
    nt = t // tb
    ti = (lambda i: nt - 1 - i) if rev else (lambda i: i)
    hb = tb // CV_HALO
    col = lambda off: off // 512
    specs = [pl.BlockSpec((tb, 1024), lambda i: (ti(i), 0)),
             pl.BlockSpec((tb, 1024), lambda i: (ti(i), 0))]
    for off in (P_SGU, P_SGV, P_SGG, P_CVA, P_CVB, P_CVG):
        specs.append(pl.BlockSpec((tb, 512), lambda i, off=off: (ti(i), col(off))))
    for off in (P_CVA, P_CVB):
        specs.append(pl.BlockSpec((CV_HALO, 512), lambda i, off=off: (jnp.maximum(ti(i) * hb - 1, 0), col(off))))
    v512 = pl.BlockSpec((1, 512), lambda i: (0, 0))
    specs += [pl.BlockSpec((1, HD), lambda i: (0, 0)), v512, v512,
              pl.BlockSpec((SG_GROUPS, SG_BLOCK, SG_BLOCK), lambda i: (0, 0, 0)),
              pl.BlockSpec((SG_BLOCK, 512), lambda i: (0, 0)),
              pl.BlockSpec((CV_HALO, 512), lambda i: (0, 0)),
              v512, v512, v512]
    return specs, ti


def _sg_mask():
    r = lax.broadcasted_iota(jnp.int32, (SG_BLOCK, SG_BLOCK), 0) // CHUNK
    c = lax.broadcasted_iota(jnp.int32, (SG_BLOCK, SG_BLOCK), 1) // CHUNK
    return r >= c


def _cv_glu(first, a_ref, b_ref, ha_ref, hb_ref, gs_ref, tb):
    ha = jnp.where(first, 0.0, ha_ref[...].astype(f32))
    gs_ref[0:CV_HALO, :] = ha * _sigmoid(hb_ref[...].astype(f32))
    gs_ref[CV_HALO:CV_HALO + tb, :] = a_ref[...].astype(f32) * _sigmoid(b_ref[...].astype(f32))


SUBLANES = 8
CV_SHIFT_ROWS = CV_HALO - SUBLANES


def _cv_shift(src_ref, sh_ref, tb, cs):
    for b in range(1, SUBLANES):
        sh_ref[b - 1, 0:tb + CV_SHIFT_ROWS, cs] = src_ref[pl.ds(b, tb + CV_SHIFT_ROWS), cs]


def _cv_tap(src_ref, sh_ref, off, tb, cs):
    a, b = divmod(off, SUBLANES)
    if b == 0:
        return src_ref[pl.ds(SUBLANES * a, tb), cs]
    return sh_ref[b - 1, pl.ds(SUBLANES * a, tb), cs]


def _cv_conv(gs_ref, gsh_ref, cw_ref, cb_ref, tb, cs):
    acc = cb_ref[:, cs] + cw_ref[0:1, cs] * _cv_tap(gs_ref, gsh_ref, CV_HALO - (CV_K - 1), tb, cs)
    for j in range(1, CV_K):
        acc = acc + cw_ref[j:j + 1, cs] * _cv_tap(gs_ref, gsh_ref, CV_HALO - (CV_K - 1) + j, tb, cs)
    return acc


def _branch_fwd(o, p, dng, slg, slb, sgw, sgbias, cvw, cvb, clg, clb):
    t = o.shape[0]
    tb = min(t, 256)
    specs, _ = _br_specs(t, tb, False)

    def body(o_ref, z_ref, su_ref, sv_ref, sgt_ref, ca_ref, cb_ref, cg_ref, ha_ref, hb_ref,
             dng_ref, slg_ref, slb_ref, sgw_ref, sgb_ref, cvw_ref, cvb_ref, clg_ref, clb_ref, y_ref, cvo_ref, gs_ref, vln_ref, gsh_ref):
        i = pl.program_id(0)
        for h in range(1024 // HD):
            cs = slice(h * HD, (h + 1) * HD)
            oh = o_ref[:, cs]
            r = lax.rsqrt(jnp.mean(oh * oh, axis=1, keepdims=True) + EPS)
            y_ref[:, cs] = ((oh * r) * dng_ref[...] * _silu(z_ref[:, cs].astype(f32))).astype(BF)
        mask = _sg_mask()
        for g in range(SG_GROUPS):
            cs = slice(g * HD, (g + 1) * HD)
            vg = _gelu(sv_ref[:, cs].astype(f32))
            vln, _, _ = _ln_fwd(vg, slg_ref[:, cs], slb_ref[:, cs])
            vln_ref[...] = vln
            ws = jnp.where(mask, sgw_ref[g], 0.0)
            for nb in range(tb // SG_BLOCK):
                rs = slice(nb * SG_BLOCK, (nb + 1) * SG_BLOCK)
                mixed = _dot(ws, vln_ref[rs, :]) + sgb_ref[:, cs]
                u = _gelu(su_ref[rs, cs].astype(f32))
                y_ref[rs, 1024 + g * HD:1024 + (g + 1) * HD] = (u * mixed * _silu(sgt_ref[rs, cs].astype(f32))).astype(BF)
        _cv_glu(i == 0, ca_ref, cb_ref, ha_ref, hb_ref, gs_ref, tb)
        for g in range(CV_GROUPS):
            cs = slice(g * HD, (g + 1) * HD)
            _cv_shift(gs_ref, gsh_ref, tb, cs)
            dw = _cv_conv(gs_ref, gsh_ref, cvw_ref, cvb_ref, tb, cs)
            cvo_ref[:, cs] = dw
            ln, _, _ = _ln_fwd(dw, clg_ref[:, cs], clb_ref[:, cs])
            y_ref[:, 1536 + g * HD:1536 + (g + 1) * HD] = (_silu(ln) * _silu(cg_ref[:, cs].astype(f32))).astype(BF)

    return _pc(body, name="branch_fwd", grid=(t // tb,),
               out_shape=(jax.ShapeDtypeStruct((t, 2048), BF), jax.ShapeDtypeStruct((t, 512), f32)),
               in_specs=specs, out_specs=(pl.BlockSpec((tb, 2048), lambda i: (i, 0)), pl.BlockSpec((tb, 512), lambda i: (i, 0))),
               scratch_shapes=[pltpu.VMEM((tb + CV_HALO, 512), f32), pltpu.VMEM((tb, HD), f32),
                               pltpu.VMEM((SUBLANES - 1, tb + CV_SHIFT_ROWS, 512), f32)],
               compiler_params=_cp(("arbitrary",)))(o, p, p, p, p, p, p, p, p, p, dng, slg, slb, sgw, sgbias, cvw, cvb, clg, clb)


def _branch_bwd(o, p, dng, slg, slb, sgw, sgbias, cvw, cvb, clg, clb, dy, cvo):
    t = o.shape[0]
    tb = min(t, 256)
    nt = t // tb
    specs, ti = _br_specs(t, tb, True)
    specs = specs + [pl.BlockSpec((tb, 2048), lambda i: (ti(i), 0)), pl.BlockSpec((tb, 512), lambda i: (ti(i), 0))]

    def body(o_ref, z_ref, su_ref, sv_ref, sgt_ref, ca_ref, cb_ref, cg_ref, ha_ref, hb_ref,
             dng_ref, slg_ref, slb_ref, sgw_ref, sgb_ref, cvw_ref, cvb_ref, clg_ref, clb_ref, dy_ref, cvo_ref,
             dp_ref, do_ref, ddng_ref, dslg_ref, dslb_ref, dsgw_ref, dsgb_ref, dcvw_ref, dcvb_ref, dclg_ref, dclb_ref,
             gs_ref, vln_ref, xh_ref, dvl_ref, ddw_ref, dbias_ref, gsh_ref, dsh_ref):
        i = pl.program_id(0)

        @pl.when(i == 0)
        def _():
            for r in (ddng_ref, dslg_ref, dslb_ref, dsgw_ref, dsgb_ref, dcvw_ref, dcvb_ref, dclg_ref, dclb_ref, ddw_ref, dbias_ref):
                r[...] = jnp.zeros_like(r)

        for h in range(1024 // HD):
            cs = slice(h * HD, (h + 1) * HD)
            oh = o_ref[:, cs]
            zz = z_ref[:, cs].astype(f32)
            dyh = dy_ref[:, cs].astype(f32)
            r = lax.rsqrt(jnp.mean(oh * oh, axis=1, keepdims=True) + EPS)
            nrm = oh * r
            sz = _silu(zz)
            dn = dyh * dng_ref[...] * sz
            ddng_ref[...] += jnp.sum(dyh * nrm * sz, axis=0, keepdims=True)
            dp_ref[:, cs] = (dyh * nrm * dng_ref[...] * _dsilu(zz)).astype(BF)
            do_ref[:, cs] = r * (dn - nrm * jnp.mean(dn * nrm, axis=1, keepdims=True))
        mask = _sg_mask()
        for g in range(SG_GROUPS):
            cs = slice(g * HD, (g + 1) * HD)
            sv = sv_ref[:, cs].astype(f32)
            vln, xhat, rstd = _ln_fwd(_gelu(sv), slg_ref[:, cs], slb_ref[:, cs])
            vln_ref[...] = vln
            xh_ref[...] = xhat
            ws = jnp.where(mask, sgw_ref[g], 0.0)
            dws = jnp.zeros((SG_BLOCK, SG_BLOCK), f32)
            for nb in range(tb // SG_BLOCK):
                rs = slice(nb * SG_BLOCK, (nb + 1) * SG_BLOCK)
                dyb = dy_ref[rs, 1024 + g * HD:1024 + (g + 1) * HD].astype(f32)
                su = su_ref[rs, cs].astype(f32)
                gt = sgt_ref[rs, cs].astype(f32)
                mixed = _dot(ws, vln_ref[rs, :]) + sgb_ref[:, cs]
                u = _gelu(su)
                sgt = _silu(gt)
                dmixed = dyb * u * sgt
                dp_ref[rs, P_SGU + g * HD:P_SGU + (g + 1) * HD] = (dyb * mixed * sgt * _dgelu(su)).astype(BF)
                dp_ref[rs, P_SGG + g * HD:P_SGG + (g + 1) * HD] = (dyb * u * mixed * _dsilu(gt)).astype(BF)
                dvl_ref[rs, :] = _dot_tn(ws, dmixed)
                dws = dws + _dot_nt(dmixed, vln_ref[rs, :])
                dbias_ref[:, cs] += dmixed
            dsgw_ref[g] += jnp.where(mask, dws, 0.0)
            dvl = dvl_ref[...]
            xhat = xh_ref[...]
            dslg_ref[:, cs] += jnp.sum(dvl * xhat, axis=0, keepdims=True)
            dslb_ref[:, cs] += jnp.sum(dvl, axis=0, keepdims=True)
            dvg = _ln_bwd(dvl, xhat, rstd, slg_ref[:, cs])
            dp_ref[:, P_SGV + g * HD:P_SGV + (g + 1) * HD] = (dvg * _dgelu(sv)).astype(BF)
        _cv_glu(i == nt - 1, ca_ref, cb_ref, ha_ref, hb_ref, gs_ref, tb)
        for g in range(CV_GROUPS):
            cs = slice(g * HD, (g + 1) * HD)
            dyc = dy_ref[:, 1536 + g * HD:1536 + (g + 1) * HD].astype(f32)
            cg = cg_ref[:, cs].astype(f32)
            _cv_shift(gs_ref, gsh_ref, tb, cs)
            ln, xhat, rstd = _ln_fwd(cvo_ref[:, cs], clg_ref[:, cs], clb_ref[:, cs])
            dln = dyc * _silu(cg) * _dsilu(ln)
            dp_ref[:, P_CVG + g * HD:P_CVG + (g + 1) * HD] = (dyc * _silu(ln) * _dsilu(cg)).astype(BF)
            dclg_ref[:, cs] += jnp.sum(dln * xhat, axis=0, keepdims=True)
            dclb_ref[:, cs] += jnp.sum(dln, axis=0, keepdims=True)
            ddw = _ln_bwd(dln, xhat, rstd, clg_ref[:, cs])
            dcvb_ref[:, cs] += jnp.sum(ddw, axis=0, keepdims=True)
            ddw_ref[0:tb, cs] = ddw
            _cv_shift(ddw_ref, dsh_ref, tb, cs)
            dglu = cvw_ref[CV_K - 1:CV_K, cs] * ddw
            for j in range(CV_K - 1):
                dglu = dglu + cvw_ref[j:j + 1, cs] * _cv_tap(ddw_ref, dsh_ref, CV_K - 1 - j, tb, cs)
            for j in range(CV_K):
                tap = _cv_tap(gs_ref, gsh_ref, CV_HALO - (CV_K - 1) + j, tb, cs)
                dcvw_ref[j:j + 1, cs] += jnp.sum(ddw * tap, axis=0, keepdims=True)
            a = ca_ref[:, cs].astype(f32)
            sb = _sigmoid(cb_ref[:, cs].astype(f32))
            dp_ref[:, P_CVA + g * HD:P_CVA + (g + 1) * HD] = (dglu * sb).astype(BF)
            dp_ref[:, P_CVB + g * HD:P_CVB + (g + 1) * HD] = (dglu * a * sb * (1.0 - sb)).astype(BF)
        ddw_ref[tb:tb + CV_HALO, :] = ddw_ref[0:CV_HALO, :]

        @pl.when(i == nt - 1)
        def _():
            lane = lax.broadcasted_iota(jnp.int32, (SG_BLOCK, LANES), 1)
            acc = jnp.zeros((SG_BLOCK, LANES), f32)
            for g in range(SG_GROUPS):
                acc = jnp.where(lane == g, jnp.sum(dbias_ref[:, g * HD:(g + 1) * HD], axis=1, keepdims=True), acc)
            dsgb_ref[...] = acc

    v512 = pl.BlockSpec((1, 512), lambda i: (0, 0))
    return _pc(body, name="branch_bwd", grid=(nt,),
               out_shape=(jax.ShapeDtypeStruct((t, 4096), BF), jax.ShapeDtypeStruct((t, 1024), f32),
                          jax.ShapeDtypeStruct((1, HD), f32), jax.ShapeDtypeStruct((1, 512), f32), jax.ShapeDtypeStruct((1, 512), f32),
                          jax.ShapeDtypeStruct((SG_GROUPS, SG_BLOCK, SG_BLOCK), f32), jax.ShapeDtypeStruct((SG_BLOCK, LANES), f32),
                          jax.ShapeDtypeStruct((CV_HALO, 512), f32), jax.ShapeDtypeStruct((1, 512), f32),
                          jax.ShapeDtypeStruct((1, 512), f32), jax.ShapeDtypeStruct((1, 512), f32)),
               in_specs=specs,
               out_specs=(pl.BlockSpec((tb, 4096), lambda i: (ti(i), 0)), pl.BlockSpec((tb, 1024), lambda i: (ti(i), 0)),
                          pl.BlockSpec((1, HD), lambda i: (0, 0)), v512, v512,
                          pl.BlockSpec((SG_GROUPS, SG_BLOCK, SG_BLOCK), lambda i: (0, 0, 0)),
                          pl.BlockSpec((SG_BLOCK, LANES), lambda i: (0, 0)), pl.BlockSpec((CV_HALO, 512), lambda i: (0, 0)),
                          v512, v512, v512),
               scratch_shapes=[pltpu.VMEM((tb + CV_HALO, 512), f32), pltpu.VMEM((tb, HD), f32), pltpu.VMEM((tb, HD), f32),
                               pltpu.VMEM((tb, HD), f32), pltpu.VMEM((tb + CV_HALO, 512), f32), pltpu.VMEM((SG_BLOCK, 512), f32),
                               pltpu.VMEM((SUBLANES - 1, tb + CV_SHIFT_ROWS, 512), f32),
                               pltpu.VMEM((SUBLANES - 1, tb + CV_SHIFT_ROWS, 512), f32)],
               compiler_params=_cp(("arbitrary",)))(o, p, p, p, p, p, p, p, p, p, dng, slg, slb, sgw, sgbias, cvw, cvb, clg, clb, dy, cvo)


def _outproj(ycat, w, x, gate):
    t, d = x.shape
    tm = min(t, 512)

    def body(y_ref, w_ref, x_ref, g_ref, o_ref):
        o_ref[...] = x_ref[...] + g_ref[...] * jnp.dot(y_ref[...], w_ref[...], preferred_element_type=f32)

    return _pc(body, name="outproj", grid=(t // tm,), out_shape=jax.ShapeDtypeStruct((t, d), f32),
               in_specs=[pl.BlockSpec((tm, ycat.shape[1]), lambda i: (i, 0)), pl.BlockSpec(w.shape, lambda i: (0, 0)),
                         pl.BlockSpec((tm, d), lambda i: (i, 0)), pl.BlockSpec((1, d), lambda i: (0, 0))],
               out_specs=pl.BlockSpec((tm, d), lambda i: (i, 0)), compiler_params=_cp(("arbitrary",)))(ycat, w, x, gate)


def _outproj_bwd(dxo, w, gate):
    t, d = dxo.shape
    tm = min(t, 512)

    def body(d_ref, w_ref, g_ref, o_ref):
        o_ref[...] = _dot_nt(d_ref[...] * g_ref[...], w_ref[...]).astype(BF)

    return _pc(body, name="outproj_bwd", grid=(t // tm,), out_shape=jax.ShapeDtypeStruct((t, w.shape[0]), BF),
               in_specs=[pl.BlockSpec((tm, d), lambda i: (i, 0)), pl.BlockSpec(w.shape, lambda i: (0, 0)),
                         pl.BlockSpec((1, d), lambda i: (0, 0))],
               out_specs=pl.BlockSpec((tm, w.shape[0]), lambda i: (i, 0)), compiler_params=_cp(("arbitrary",)))(dxo, w, gate)


def _tn_acc(a, b, name):
    kk, m = a.shape
    n = b.shape[1]
    tk, tn = min(kk, 1024), min(n, 1024)

    def body(a_ref, b_ref, o_ref):
        @pl.when(pl.program_id(1) == 0)
        def _():
            o_ref[...] = jnp.zeros_like(o_ref)
        o_ref[...] += _dot_tn(a_ref[...], b_ref[...])

    return _pc(body, name=name, grid=(n // tn, kk // tk), out_shape=jax.ShapeDtypeStruct((m, n), f32),
               in_specs=[pl.BlockSpec((tk, m), lambda j, k: (k, 0)), pl.BlockSpec((tk, tn), lambda j, k: (k, j))],
               out_specs=pl.BlockSpec((m, tn), lambda j, k: (0, j)), compiler_params=_cp(("arbitrary", "arbitrary")))(a, b)


def _wout_grad(gmat, w, gate):
    m, n = gmat.shape
    tr = m // N_DEV

    def body(g_ref, w_ref, gt_ref, o_ref, dg_ref):
        @pl.when(pl.program_id(0) == 0)
        def _():
            dg_ref[...] = jnp.zeros_like(dg_ref)
        gm = g_ref[...]
        o_ref[0] = (gm * gt_ref[...]).astype(BF)
        dg_ref[...] += jnp.sum(gm * w_ref[...].astype(f32), axis=0, keepdims=True)

    return _pc(body, name="wout_grad", grid=(N_DEV,),
               out_shape=(jax.ShapeDtypeStruct((N_DEV, tr, n), BF), jax.ShapeDtypeStruct((1, n), f32)),
               in_specs=[pl.BlockSpec((tr, n), lambda i: (i, 0)), pl.BlockSpec((tr, n), lambda i: (i, 0)),
                         pl.BlockSpec((1, n), lambda i: (0, 0))],
               out_specs=(pl.BlockSpec((1, tr, n), lambda i: (i, 0, 0)), pl.BlockSpec((1, n), lambda i: (0, 0))),
               compiler_params=_cp(("arbitrary",)))(gmat, w, gate)


def _inproj_bwd(dpa, dpb, dps, wmain, wsmall, x, ng, sc1, dxo, comm=()):
    t, d = x.shape
    na, nb = dpa.shape[1] // 1024, dpb.shape[1] // 1024
    nk = na + nb
    tm, rb = min(t, 512), 128

    def body(a_ref, b_ref, s_ref, wm_ref, ws_ref, x_ref, ng_ref, sc_ref, dxo_ref, dx_ref, dsh_ref, dsc_ref, dng_ref, acc_ref):
        i, k = pl.program_id(0), pl.program_id(1)

        @pl.when((i == 0) & (k == 0))
        def _():
            dsh_ref[...] = jnp.zeros_like(dsh_ref)
            dsc_ref[...] = jnp.zeros_like(dsc_ref)
            dng_ref[...] = jnp.zeros_like(dng_ref)

        @pl.when(k == 0)
        def _():
            acc_ref[...] = _dot_nt(s_ref[...], ws_ref[...])

        @pl.when(k < na)
        def _():
            acc_ref[...] += _dot_nt(a_ref[...], wm_ref[...])

        @pl.when(k >= na)
        def _():
            acc_ref[...] += _dot_nt(b_ref[...], wm_ref[...])

        @pl.when(k == nk - 1)
        def _():
            def rows(j, carry):
                r0 = pl.multiple_of(j * rb, rb)
                rr = pl.ds(r0, rb)
                xv = x_ref[rr, :]
                dh = acc_ref[rr, :]
                r = lax.rsqrt(jnp.mean(xv * xv, axis=1, keepdims=True) + EPS)
                xn = xv * r
                dsh_ref[...] += jnp.sum(dh, axis=0, keepdims=True)
                dsc_ref[...] += jnp.sum(dh * (xn * ng_ref[...]), axis=0, keepdims=True)
                dng_ref[...] += jnp.sum(dh * sc_ref[...] * xn, axis=0, keepdims=True)
                dxn = dh * (ng_ref[...] * sc_ref[...])
                dx_ref[rr, :] = r * (dxn - xn * jnp.mean(dxn * xn, axis=1, keepdims=True)) + dxo_ref[rr, :]
                return carry
            lax.fori_loop(0, tm // rb, rows, 0)

    vec = pl.BlockSpec((1, d), lambda i, k: (0, 0))
    row = pl.BlockSpec((tm, d), lambda i, k: (i, 0))
    shp = jax.ShapeDtypeStruct((1, d), f32)
    return _with_comm(
        body, comm, "inproj_bwd", (t // tm, nk), (dpa, dpb, dps, wmain, wsmall, x, ng, sc1, dxo),
        [pl.BlockSpec((tm, 1024), lambda i, k: (i, jnp.minimum(k, na - 1))),
         pl.BlockSpec((tm, 1024), lambda i, k: (i, jnp.clip(k - na, 0, nb - 1))),
         pl.BlockSpec((tm, LANES), lambda i, k: (i, 0)),
         pl.BlockSpec((d, 1024), lambda i, k: (0, k)), pl.BlockSpec((d, LANES), lambda i, k: (0, 0)), row, vec, vec, row],
        (jax.ShapeDtypeStruct((t, d), f32), shp, shp, shp), (row, vec, vec, vec), [pltpu.VMEM((tm, d), f32)])


def _loss_head(x, fg, tgt):
    t, d = x.shape
    tm = min(t, 512)

    def body(x_ref, g_ref, t_ref, dx_ref, l_ref, dg_ref):
        @pl.when(pl.program_id(0) == 0)
        def _():
            l_ref[...] = jnp.zeros_like(l_ref)
            dg_ref[...] = jnp.zeros_like(dg_ref)
        xv = x_ref[...]
        r = lax.rsqrt(jnp.mean(xv * xv, axis=1, keepdims=True) + EPS)
        xn = xv * r
        err = xn * g_ref[...] - t_ref[...]
        l_ref[...] += 0.5 * jnp.sum(jnp.mean(err * err, axis=1, keepdims=True), axis=0, keepdims=True)
        dy = err * (1.0 / d)
        dg_ref[...] += jnp.sum(dy * xn, axis=0, keepdims=True)
        dxn = dy * g_ref[...]
        dx_ref[...] = r * (dxn - xn * jnp.mean(dxn * xn, axis=1, keepdims=True))

    row = pl.BlockSpec((tm, d), lambda i: (i, 0))
    vec = pl.BlockSpec((1, d), lambda i: (0, 0))
    return _pc(body, name="loss_head", grid=(t // tm,),
               out_shape=(jax.ShapeDtypeStruct((t, d), f32), jax.ShapeDtypeStruct((1, 1), f32), jax.ShapeDtypeStruct((1, d), f32)),
               in_specs=[row, vec, row], out_specs=(row, pl.BlockSpec((1, 1), lambda i: (0, 0)), vec),
               compiler_params=_cp(("arbitrary",)))(x, fg, tgt)


def _pack(arrs, rows_mult=8):
    flat = jnp.concatenate([a.reshape(-1).astype(f32) for a in arrs])
    n = flat.shape[0]
    per = rows_mult * LANES
    pad = (-n) % per
    if pad:
        flat = jnp.concatenate([flat, jnp.zeros((pad,), f32)])
    return flat.reshape(-1, LANES)


def _unpack(packed, shapes, lead=()):
    flat = packed.reshape(lead + (-1,))
    out, off = [], 0
    for s in shapes:
        n = 1
        for v in s:
            n *= v
        out.append(flat[..., off:off + n].reshape(lead + tuple(s)))
        off += n
    return out


def _pad_lanes(v, at):
    return jnp.pad(v.astype(f32), (at, LANES - at - v.shape[0])).reshape(1, LANES)


def kernel(x, c, norm_g, w_ada, b_ada, w_in, conv_qkv, a_log, dt_bias, dn_norm_g, sg_ln_g, sg_ln_b, sg_w, sg_b, cv_w, cv_b, cv_ln_g, cv_ln_b, w_out, final_g, loss_target, m_norm_g, m_w_ada, m_b_ada, m_w_in, m_conv_qkv, m_a_log, m_dt_bias, m_dn_norm_g, m_sg_ln_g, m_sg_ln_b, m_sg_w, m_sg_b, m_cv_w, m_cv_b, m_cv_ln_g, m_cv_ln_b, m_w_out, m_final_g, v_norm_g, v_w_ada, v_b_ada, v_w_in, v_conv_qkv, v_a_log, v_dt_bias, v_dn_norm_g, v_sg_ln_g, v_sg_ln_b, v_sg_w, v_sg_b, v_cv_w, v_cv_b, v_cv_ln_g, v_cv_ln_b, v_w_out, v_final_g):
    nl, d = norm_g.shape
    t = x.shape[1]
    nh = a_log.shape[1]
    xi, yi, ci = lax.axis_index("x"), lax.axis_index("y"), lax.axis_index("c")
    me = 4 * xi + 2 * yi + ci
    x0 = x[0]
    tgt = loss_target[0]
    ada_cols = w_ada.shape[2]
    in_cols = w_in.shape[2]
    cq_cols = conv_qkv.shape[2]
    cvw_cols = cv_w.shape[2]

    start_shapes = [c.shape, conv_qkv.shape, cv_w.shape]
    start_pack = _pack([c, conv_qkv, cv_w])
    wi_b, wo_b = w_in.astype(BF), w_out.astype(BF)
    g_small, g_win0 = _exchange([start_pack, wi_b[0]], ["gather2", "gather2"], "gather_start")
    c_all, cq_g, cvw_g = _unpack(g_small, start_shapes, lead=(N_DEV,))
    c_all = c_all.reshape(N_DEV, d)
    conv_full = jnp.moveaxis(cq_g, 0, 2).reshape(nl, DN_CONV, N_DEV * cq_cols)
    cvw_full = jnp.moveaxis(cvw_g, 0, 2).reshape(nl, CV_K, N_DEV * cvw_cols)
    w_packed = [_repack_w(g_win0)]
    w_out_full = []

    b_my = lax.dynamic_slice_in_dim(b_ada, me * ada_cols, ada_cols, axis=1).reshape(nl, 1, ada_cols)
    mod_part = _ada_fwd(c_all, w_ada, b_my)
    (mod_g,) = _exchange([mod_part], ["gather"], "gather_mod")
    mod_mine = lax.dynamic_index_in_dim(mod_g, me, axis=2, keepdims=False)
    mod = jnp.moveaxis(mod_mine, 0, 1).reshape(nl, N_DEV * ada_cols)
    shift, scale, gate = mod[:, :d], mod[:, d:2 * d], mod[:, 2 * d:]

    arow = [_pad_lanes(-jnp.exp(a_log[l]), 8) for l in range(nl)]
    dtb = [_pad_lanes(dt_bias[l], 8) for l in range(nl)]
    saved = []
    xl = x0
    for l in range(nl):
        sc1 = (1.0 + scale[l]).reshape(1, d)
        w_main_l, w_small_l = w_packed[l]
        p, ps, hb = _inproj(xl, norm_g[l].reshape(1, d), sc1, shift[l].reshape(1, d), w_main_l, w_small_l)
        pst = ps[:, :16].reshape(t // CHUNK, CHUNK, 16).transpose(0, 2, 1)
        acol = jnp.broadcast_to(arow[l][0, :16].reshape(16, 1), (16, CHUNK))
        dtc = jnp.broadcast_to(dtb[l][0, :16].reshape(16, 1), (16, CHUNK))
        ahead = (wo_b[l], wi_b[l + 1]) if l + 1 < nl else (wo_b[l],)
        o, ss, tinv, g_wout, *g_win_n = _dn_fwd(p, ps, pst, conv_full[l], arow[l], dtb[l], acol, dtc,
                                                comm=(ahead, ("gather2",) * len(ahead)))
        w_out_l = g_wout.reshape(N_DEV * w_out.shape[1], w_out.shape[2])
        w_out_full.append(w_out_l)
        if g_win_n:
            w_packed.append(_repack_w(g_win_n[0]))
        sgbias = jnp.repeat(sg_b[l].T, HD, axis=1)
        cvw_pad = jnp.pad(cvw_full[l], ((0, CV_HALO - CV_K), (0, 0)))
        br_par = (dn_norm_g[l].reshape(1, HD), sg_ln_g[l].reshape(1, -1), sg_ln_b[l].reshape(1, -1), sg_w[l], sgbias, cvw_pad,
                  cv_b[l].reshape(1, -1), cv_ln_g[l].reshape(1, -1), cv_ln_b[l].reshape(1, -1))
        ycat, cvo = _branch_fwd(o, p, *br_par)
        xn = _outproj(ycat, w_out_l, xl, gate[l].reshape(1, d))
        saved.append((xl, p, ps, pst, hb, o, ss, tinv, ycat, sc1, acol, dtc, br_par, cvo))
        xl = xn

    dx, loss_part, dfinal_g = _loss_head(xl, final_g.reshape(1, d), tgt)
    loss = lax.psum(loss_part[0, 0], ("x", "y", "c"))

    small_grads = [None] * nl
    dmods = [None] * nl
    win_r = [None] * nl
    wout_r = [None] * nl
    rep_names = ["b_ada", "norm_g", "a_log", "dt_bias", "dn_norm_g", "sg_ln_g", "sg_ln_b", "sg_w", "sg_b", "cv_b", "cv_ln_g",
                 "cv_ln_b", "final_g"]
    win_slab = None
    for l in reversed(range(nl)):
        xl, p, ps, pst, hb, o, ss, tinv, ycat, sc1, acol, dtc, br_par, cvo = saved[l]
        w_main_l, w_small_l = w_packed[l]
        w_out_l = w_out_full[l]
        gate_l = gate[l].reshape(1, d)
        dycat = _outproj_bwd(dx, w_out_l, gate_l)
        gmat = _tn_acc(ycat, dx, "wout_tn")
        wout_slab, dgate = _wout_grad(gmat, w_out_l, gate_l)
        dpa, do, ddng, dslg, dslb, dsgw, dsgb, dcvw, dcvb, dclg, dclb = _branch_bwd(o, p, *br_par, dycat, cvo)
        behind = (wout_slab,) if win_slab is None else (wout_slab, win_slab)
        kinds = ("scatter",) * len(behind)
        if l == 0:
            zeros_d = jnp.zeros((d,), f32)
            dmods[0] = jnp.concatenate([zeros_d, zeros_d, dgate[0]])
            small_grads[0] = [zeros_d, jnp.zeros((nh,), f32), jnp.zeros((nh,), f32), ddng[0], dslg[0], dslb[0], dsgw,
                              dsgb[:, :SG_GROUPS].T, dcvb[0], dclg[0], dclb[0], jnp.zeros((DN_CONV, 3072), f32), dcvw[:CV_K]]
            stack = lambda j: jnp.stack([small_grads[i][j] for i in range(nl)])
            rep_grads = [jnp.stack(dmods)] + [stack(j) for j in range(11)] + [dfinal_g[0]]
            full_grads = [stack(11), stack(12)]
            grad_shapes = [g.shape for g in rep_grads + full_grads]
            behind = (_pack(rep_grads + full_grads, 1024),) + behind
            kinds = ("gather2",) + kinds
        dpb, dps, dconvw, dal, ddt, *got = _dn_bwd(
            p, ps, pst, conv_full[l], arow[l], dtb[l], acol, dtc, do, ss, tinv, comm=(behind, kinds))
        if l == 0:
            gpack_g, got = got[0], got[1:]
        wout_r[l] = got[0]
        if len(got) > 1:
            win_r[l + 1] = got[1]
        gwa = _tn_acc(hb, dpa, "win_tn_a")
        gwb = _tn_acc(hb, dpb, "win_tn_b")
        gws = _tn_acc(hb, dps, "win_tn_s")
        win_slab = _grad_slabs(gwa, gwb, gws, in_cols)
        ib_args = (dpa, dpb, dps, w_main_l, w_small_l, xl, norm_g[l].reshape(1, d), sc1, dx)
        if l > 0:
            dx, dshift, dscale, dng = _inproj_bwd(*ib_args)
            dmods[l] = jnp.concatenate([dshift[0], dscale[0], dgate[0]])
            small_grads[l] = [dng[0], dal[0, 8:8 + nh], ddt[0, 8:8 + nh], ddng[0], dslg[0], dslb[0], dsgw, dsgb[:, :SG_GROUPS].T,
                              dcvb[0], dclg[0], dclb[0], dconvw, dcvw[:CV_K]]
        else:
            dx, dshift, dscale, dng, win_r[0] = _inproj_bwd(*ib_args, comm=((win_slab,), ("scatter",)))
            late_items = [dshift[0], dscale[0], dng[0], dal[0, 8:8 + nh], ddt[0, 8:8 + nh], dconvw]
            late_shapes = [g.shape for g in late_items]
            (late_g,) = _exchange([_pack(late_items)], ["gather"], "exchange_late")

    gsum = _sum_slabs(gpack_g, "sum_small")
    gl = _unpack(gsum, grad_shapes)
    late = _unpack(_sum_slabs(late_g, "sum_late"), late_shapes)
    gl[0] = jnp.concatenate([jnp.concatenate([late[0], late[1], gl[0][0, 2 * d:]])[None], gl[0][1:]], axis=0)
    for idx, val in ((1, late[2]), (2, late[3]), (3, late[4]), (len(gl) - 2, late[5])):
        gl[idx] = jnp.concatenate([val[None], gl[idx][1:]], axis=0)
    g_rep = dict(zip(rep_names, gl[:len(rep_names)]))
    g_conv = lax.dynamic_slice_in_dim(gl[-2], me * cq_cols, cq_cols, axis=2)
    g_cvw = lax.dynamic_slice_in_dim(gl[-1], me * cvw_cols, cvw_cols, axis=2)

    g_win, d_win, nm_win, nv_win = _adam_layers(win_r, w_in, m_w_in, v_w_in, "adam_w_in", 128)
    g_wo, d_wo, nm_wo, nv_wo = _adam_layers(wout_r, w_out, m_w_out, v_w_out, "adam_w_out", 64)
    dmod_all = _unpack(gpack_g, grad_shapes, lead=(N_DEV,))[0]
    late_all = _unpack(late_g, late_shapes, lead=(N_DEV,))
    dmod_all = jnp.concatenate([jnp.concatenate([late_all[0], late_all[1], dmod_all[:, 0, 2 * d:]], axis=1)[:, None],
                                dmod_all[:, 1:]], axis=1)
    dmod_my = jnp.moveaxis(lax.dynamic_slice_in_dim(dmod_all, me * ada_cols, ada_cols, axis=2), 0, 1)
    g_wa, d_wa, nm_wa, nv_wa = _ada_bwd_adam(c_all, dmod_my, w_ada, m_w_ada, v_w_ada)

    small_w = dict(b_ada=b_ada, norm_g=norm_g, a_log=a_log, dt_bias=dt_bias, dn_norm_g=dn_norm_g, sg_ln_g=sg_ln_g, sg_ln_b=sg_ln_b,
                   sg_w=sg_w, sg_b=sg_b, cv_b=cv_b, cv_ln_g=cv_ln_g, cv_ln_b=cv_ln_b, final_g=final_g, conv_qkv=conv_qkv, cv_w=cv_w)
    small_m = dict(b_ada=m_b_ada, norm_g=m_norm_g, a_log=m_a_log, dt_bias=m_dt_bias, dn_norm_g=m_dn_norm_g, sg_ln_g=m_sg_ln_g,
                   sg_ln_b=m_sg_ln_b, sg_w=m_sg_w, sg_b=m_sg_b, cv_b=m_cv_b, cv_ln_g=m_cv_ln_g, cv_ln_b=m_cv_ln_b,
                   final_g=m_final_g, conv_qkv=m_conv_qkv, cv_w=m_cv_w)
    small_v = dict(b_ada=v_b_ada, norm_g=v_norm_g, a_log=v_a_log, dt_bias=v_dt_bias, dn_norm_g=v_dn_norm_g, sg_ln_g=v_sg_ln_g,
                   sg_ln_b=v_sg_ln_b, sg_w=v_sg_w, sg_b=v_sg_b, cv_b=v_cv_b, cv_ln_g=v_cv_ln_g, cv_ln_b=v_cv_ln_b,
                   final_g=v_final_g, conv_qkv=v_conv_qkv, cv_w=v_cv_w)
    small_g = dict(g_rep, conv_qkv=g_conv, cv_w=g_cvw)
    names = rep_names + ["conv_qkv", "cv_w"]
    shapes = [small_w[n].shape for n in names]
    gp = _pack([small_g[n] for n in names], 1024)
    sg_, sd_, sm_, sv_ = _adam_slabs(gp.reshape((1,) + gp.shape), _pack([small_w[n] for n in names], 1024),
                                     _pack([small_m[n] for n in names], 1024), _pack([small_v[n] for n in names], 1024),
                                     "adam_small", 1024)
    sgrad = dict(zip(names, _unpack(sg_, shapes)))
    sdelta = dict(zip(names, _unpack(sd_, shapes)))
    snm = dict(zip(names, _unpack(sm_, shapes)))
    snv = dict(zip(names, _unpack(sv_, shapes)))
    for dct, big in ((sgrad, (g_wa, g_win, g_wo)), (sdelta, (d_wa, d_win, d_wo)), (snm, (nm_wa, nm_win, nm_wo)), (snv, (nv_wa, nv_win, nv_wo))):
        dct["w_ada"] = big[0]
        dct["w_in"] = big[1].reshape(w_in.shape)
        dct["w_out"] = big[2].reshape(w_out.shape)

    order = ["norm_g", "w_ada", "b_ada", "w_in", "conv_qkv", "a_log", "dt_bias", "dn_norm_g", "sg_ln_g", "sg_ln_b", "sg_w", "sg_b",
             "cv_w", "cv_b", "cv_ln_g", "cv_ln_b", "w_out", "final_g"]
    outs = [loss, dx.reshape(x.shape)]
    for dct in (sgrad, sdelta, snm, snv):
        outs += [dct[n] for n in order]
    return tuple(outs)
```

```python
import functools

import jax
import jax.numpy as jnp
from jax import lax
from jax.experimental import pallas as pl
from jax.experimental.pallas import tpu as pltpu

f32 = jnp.float32
BF = jnp.bfloat16
N_DEV = 8
LANES = 128
CHUNK = 64
HD = 128
DN_CONV = 4
SG_BLOCK = 128
SG_GROUPS = 4
CV_GROUPS = 4
CV_K = 31
CV_HALO = 32
QK_HALO = 16
EPS = 1e-6
LN_EPS = 1e-5
ADAM_LR, ADAM_B1, ADAM_B2, ADAM_EPS, ADAM_WD, ADAM_STEP = 0.001, 0.9, 0.999, 1e-08, 0.01, 10
VMEM_LIMIT = 56 * 1024 * 1024
HI = lax.Precision.HIGHEST
MESH = pl.DeviceIdType.MESH


def _cp(sem=None, vmem=VMEM_LIMIT):
    return pltpu.CompilerParams(dimension_semantics=sem, vmem_limit_bytes=vmem)


def _pc(body, **kw):
    return pl.pallas_call(body, **kw)


def _dot(a, b):
    return jnp.dot(a.astype(BF), b.astype(BF), preferred_element_type=f32)


def _dot_nt(a, b):
    return lax.dot_general(a.astype(BF), b.astype(BF), (((1,), (1,)), ((), ())), preferred_element_type=f32)


def _dot_tn(a, b):
    return lax.dot_general(a.astype(BF), b.astype(BF), (((0,), (0,)), ((), ())), preferred_element_type=f32)


def _dot_hi(a, b):
    return jnp.dot(a, b, preferred_element_type=f32, precision=HI)


def _sigmoid(x):
    return 1.0 / (1.0 + jnp.exp(-x))


def _silu(x):
    return x * _sigmoid(x)


def _dsilu(x):
    s = _sigmoid(x)
    return s * (1.0 + x * (1.0 - s))


def _softplus(x):
    return jnp.maximum(x, 0.0) + jnp.log(1.0 + jnp.exp(-jnp.abs(x)))


def _gelu(x):
    return 0.5 * x * (1.0 + lax.erf(x * 0.7071067811865476))


def _dgelu(x):
    return 0.5 * (1.0 + lax.erf(x * 0.7071067811865476)) + x * jnp.exp(-0.5 * x * x) * 0.3989422804014327


def _lanecol(x, lane_iota, j):
    return jnp.sum(jnp.where(lane_iota == j, x, 0.0), axis=1, keepdims=True)


def _split(a):
    hi = a.astype(BF)
    return hi, (a - hi.astype(f32)).astype(BF)


def _dot3(a, b):
    mm = lambda u, v: jnp.dot(u, v, preferred_element_type=f32)
    return mm(a[0], b[0]) + (mm(a[0], b[1]) + mm(a[1], b[0]))


INV_BASE = 8
DN_GROUP = 4
DN_GROUP_BWD = 4


def _tri_inv(l_mats):
    n = l_mats[0].shape[0]
    r = lax.broadcasted_iota(jnp.int32, (n, n), 0)
    c = lax.broadcasted_iota(jnp.int32, (n, n), 1)
    same = lambda size: (r // size) == (c // size)
    diag = [jnp.where(same(INV_BASE), l, 0.0) for l in l_mats]
    xs = [(r == c).astype(f32) - l for l in diag]
    sp = [_split(l) for l in diag]
    size = 2
    while size < INV_BASE:
        ps = [_dot3(s, s) for s in sp]
        sp = [_split(p) for p in ps]
        sx = [_split(x) for x in xs]
        xs = [x + _dot3(a, b) for x, a, b in zip(xs, sx, sp)]
        size *= 2
    blk = INV_BASE
    while blk < n:
        join = same(2 * blk) & jnp.logical_not(same(blk))
        lo = [_split(jnp.where(join, l, 0.0)) for l in l_mats]
        sx = [_split(x) for x in xs]
        ys = [_split(_dot3(a, b)) for a, b in zip(lo, sx)]
        xs = [x - _dot3(a, b) for x, a, b in zip(xs, sx, ys)]
        blk *= 2
    return xs


def _me():
    x, y, c = lax.axis_index("x"), lax.axis_index("y"), lax.axis_index("c")
    return x, y, c, 4 * x + 2 * y + c


def _exchange(arrs, kinds, name):
    n = len(arrs)

    def body(*refs):
        for phase in ("first", "middle", "last"):
            _comm_phase(refs[:n], refs[n:2 * n], kinds, *refs[2 * n:], phase)

    any_spec = pl.BlockSpec(memory_space=pl.ANY)
    return _pc(body, name=name, out_shape=_comm_out_shapes(arrs, kinds), in_specs=[any_spec] * n, out_specs=tuple([any_spec] * n),
               scratch_shapes=_comm_sems(n))(*arrs)


def _comm_out_shapes(arrs, kinds):
    return tuple(jax.ShapeDtypeStruct(a.shape if k == "scatter" else (N_DEV,) + a.shape, a.dtype) for a, k in zip(arrs, kinds))


def _comm_sems(n):
    return [pltpu.SemaphoreType.DMA((n, N_DEV - 1)), pltpu.SemaphoreType.DMA((n, N_DEV - 1)), pltpu.SemaphoreType.DMA((n,))]


def _comm_phase(ins, outs, kinds, send, recv, loc, phase):
    x, y, c, me = _me()
    flips = ((1, 0), (0, 1), (1, 1))
    chips = [(jnp.bitwise_xor(x, fx), jnp.bitwise_xor(y, fy)) for fx, fy in flips]
    for a in range(len(ins)):
        slot = lambda px, py, pc, a=a: outs[a].at[4 * px + 2 * py + pc]

        def copy(k, src, block, to, a=a, slot=slot):
            return pltpu.make_async_remote_copy(src_ref=src, dst_ref=slot(*block), send_sem=send.at[a, k], recv_sem=recv.at[a, k],
                                                device_id=to, device_id_type=MESH)

        if kinds[a] == "gather2":
            mine = pltpu.make_async_copy(ins[a], slot(x, y, c), loc.at[a])
            first = [copy(0, ins[a], (x, y, c), (x, y, 1 - c))] + [copy(1 + j, ins[a], (x, y, c), (px, py, c))
                                                                  for j, (px, py) in enumerate(chips)]
            passed = [copy(4 + j, slot(px, py, c), (px, py, c), (x, y, 1 - c)) for j, (px, py) in enumerate(chips)]
            if phase == "first":
                for cp in [mine] + first:
                    cp.start()
            elif phase == "middle":
                for j, (px, py) in enumerate(chips):
                    copy(1 + j, ins[a], (px, py, c), (x, y, c)).wait_recv()
                    passed[j].start()
            else:
                copy(0, ins[a], (x, y, 1 - c), (x, y, c)).wait_recv()
                for j, (px, py) in enumerate(chips):
                    copy(4 + j, ins[a], (px, py, 1 - c), (x, y, c)).wait_recv()
                for cp in first + passed:
                    cp.wait_send()
                mine.wait()
        else:
            own = ins[a].at[me] if kinds[a] == "scatter" else ins[a]
            copies = [pltpu.make_async_copy(own, outs[a].at[me], loc.at[a])]
            for k in range(1, N_DEV):
                px = jnp.bitwise_xor(x, (k >> 2) & 1)
                py = jnp.bitwise_xor(y, (k >> 1) & 1)
                pc = jnp.bitwise_xor(c, k & 1)
                src = ins[a].at[4 * px + 2 * py + pc] if kinds[a] == "scatter" else ins[a]
                copies.append(pltpu.make_async_remote_copy(src_ref=src, dst_ref=outs[a].at[me], send_sem=send.at[a, k - 1],
                                                           recv_sem=recv.at[a, k - 1], device_id=(px, py, pc), device_id_type=MESH))
            if phase == "first":
                for cp in copies:
                    cp.start()
            elif phase == "last":
                for cp in copies:
                    cp.wait()


def _with_comm(compute, comm, name, grid, args, in_specs, out_shape, out_specs, scratch):
    arrs, kinds = comm if comm else ((), ())
    n, n_in, n_out, n_scr = len(arrs), len(args), len(out_shape), len(scratch)
    marks = {"first": [0] * len(grid), "middle": [(3 * grid[0]) // 4] + [0] * (len(grid) - 1), "last": [g - 1 for g in grid]}

    def at_step(which):
        ok = pl.program_id(0) == marks[which][0]
        for ax in range(1, len(grid)):
            ok = ok & (pl.program_id(ax) == marks[which][ax])
        return ok

    def body(*refs):
        ins, cin = refs[:n_in], refs[n_in:n_in + n]
        outs, cout = refs[n_in + n:n_in + n + n_out], refs[n_in + n + n_out:n_in + 2 * n + n_out]
        scr, sems = refs[n_in + 2 * n + n_out:n_in + 2 * n + n_out + n_scr], refs[n_in + 2 * n + n_out + n_scr:]
        for phase in ("first", "middle") if n else ():
            @pl.when(at_step(phase))
            def _(phase=phase):
                _comm_phase(cin, cout, kinds, *sems, phase)
        compute(*ins, *outs, *scr)
        if n:
            @pl.when(at_step("last"))
            def _():
                _comm_phase(cin, cout, kinds, *sems, "last")

    any_spec = pl.BlockSpec(memory_space=pl.ANY)
    return _pc(body, name=name + ("_comm" if n else ""), grid=tuple(grid),
               out_shape=tuple(out_shape) + (_comm_out_shapes(arrs, kinds) if n else ()),
               in_specs=list(in_specs) + [any_spec] * n, out_specs=tuple(out_specs) + (any_spec,) * n,
               scratch_shapes=list(scratch) + (_comm_sems(n) if n else []),
               compiler_params=_cp(("arbitrary",) * len(grid)))(*args, *arrs)


def _ada_fwd(c_all, w_ada, b_my):
    nl, d, cols = w_ada.shape

    def body(c_ref, w_ref, b_ref, o_ref):
        ca = _silu(c_ref[...])
        o_ref[0] = _dot(ca, w_ref[0]) + b_ref[0]

    return _pc(body, name="ada_fwd", grid=(nl,), out_shape=jax.ShapeDtypeStruct((nl, N_DEV, cols), f32),
               in_specs=[pl.BlockSpec((N_DEV, d), lambda l: (0, 0)), pl.BlockSpec((1, d, cols), lambda l: (l, 0, 0)),
                         pl.BlockSpec((1, 1, cols), lambda l: (l, 0, 0))],
               out_specs=pl.BlockSpec((1, N_DEV, cols), lambda l: (l, 0, 0)), compiler_params=_cp(("arbitrary",)))(c_all, w_ada, b_my)


def _adam_math(w, g, m, v):
    m = ADAM_B1 * m + (1.0 - ADAM_B1) * g
    v = ADAM_B2 * v + (1.0 - ADAM_B2) * (g * g)
    m_hat = m / (1.0 - ADAM_B1 ** ADAM_STEP)
    v_hat = v / (1.0 - ADAM_B2 ** ADAM_STEP)
    delta = -ADAM_LR * (m_hat / (jnp.sqrt(v_hat) + ADAM_EPS) + ADAM_WD * w)
    return delta, m, v


def _ada_bwd_adam(c_all, dmod_my, w, m, v):
    nl, d, cols = w.shape
    tk = 512

    def body(c_ref, dm_ref, w_ref, m_ref, v_ref, g_ref, dl_ref, nm_ref, nv_ref):
        ca = _silu(c_ref[...])
        g = _dot_tn(ca, dm_ref[0])
        dl, nm, nv = _adam_math(w_ref[0], g, m_ref[0], v_ref[0])
        g_ref[0] = g
        dl_ref[0] = dl
        nm_ref[0] = nm
        nv_ref[0] = nv

    wspec = pl.BlockSpec((1, tk, cols), lambda l, k: (l, k, 0))
    shp = jax.ShapeDtypeStruct(w.shape, f32)
    return _pc(body, name="ada_bwd_adam", grid=(nl, d // tk), out_shape=(shp, shp, shp, shp),
               in_specs=[pl.BlockSpec((N_DEV, tk), lambda l, k: (0, k)), pl.BlockSpec((1, N_DEV, cols), lambda l, k: (l, 0, 0)),
                         wspec, wspec, wspec],
               out_specs=(wspec, wspec, wspec, wspec), compiler_params=_cp(("arbitrary", "arbitrary")))(c_all, dmod_my, w, m, v)


def _adam_slabs(slabs, w, m, v, name, tr):
    n, rows, cols = slabs.shape
    tr = min(tr, rows)

    def body(s_ref, w_ref, m_ref, v_ref, g_ref, dl_ref, nm_ref, nv_ref):
        g = s_ref[0].astype(f32)
        for j in range(1, n):
            g = g + s_ref[j].astype(f32)
        dl, nm, nv = _adam_math(w_ref[...], g, m_ref[...], v_ref[...])
        g_ref[...] = g
        dl_ref[...] = dl
        nm_ref[...] = nm
        nv_ref[...] = nv

    spec = pl.BlockSpec((tr, cols), lambda i: (i, 0))
    shp = jax.ShapeDtypeStruct((rows, cols), f32)
    return _pc(body, name=name, grid=(rows // tr,), out_shape=(shp, shp, shp, shp),
               in_specs=[pl.BlockSpec((n, tr, cols), lambda i: (0, i, 0)), spec, spec, spec],
               out_specs=(spec, spec, spec, spec), compiler_params=_cp(("arbitrary",)))(slabs, w, m, v)


def _adam_layers(slabs, w, m, v, name, tr):
    nl, rows, cols = w.shape
    n = slabs[0].shape[0]
    tr = min(tr, rows)

    def body(*refs):
        s_refs, (w_ref, m_ref, v_ref, g_ref, dl_ref, nm_ref, nv_ref) = refs[:nl], refs[nl:]
        for l in range(nl):
            @pl.when(pl.program_id(0) == l)
            def _(l=l):
                g = s_refs[l][0].astype(f32)
                for j in range(1, n):
                    g = g + s_refs[l][j].astype(f32)
                dl, nm, nv = _adam_math(w_ref[0], g, m_ref[0], v_ref[0])
                g_ref[0] = g
                dl_ref[0] = dl
                nm_ref[0] = nm
                nv_ref[0] = nv

    spec = pl.BlockSpec((1, tr, cols), lambda l, i: (l, i, 0))
    s_specs = [pl.BlockSpec((n, tr, cols), lambda l, i, j=j: (0, jnp.where(l == j, i, 0), 0)) for j in range(nl)]
    shp = jax.ShapeDtypeStruct(w.shape, f32)
    return _pc(body, name=name, grid=(nl, rows // tr), out_shape=(shp, shp, shp, shp),
               in_specs=s_specs + [spec, spec, spec], out_specs=(spec, spec, spec, spec),
               compiler_params=_cp(("arbitrary", "arbitrary")))(*slabs, w, m, v)


def _sum_slabs(slabs, name):
    n, rows, cols = slabs.shape
    tr = min(rows, 1024)

    def body(s_ref, o_ref):
        g = s_ref[0]
        for j in range(1, n):
            g = g + s_ref[j]
        o_ref[...] = g

    return _pc(body, name=name, grid=(rows // tr,), out_shape=jax.ShapeDtypeStruct((rows, cols), f32),
               in_specs=[pl.BlockSpec((n, tr, cols), lambda i: (0, i, 0))], out_specs=pl.BlockSpec((tr, cols), lambda i: (i, 0)),
               compiler_params=_cp(("arbitrary",)))(slabs)


NAT_Z, NAT_SMALL, NAT_REST = 3072, 4096, 4112


def _repack_w(g_win):
    n, d, cols = g_win.shape
    tr = 256

    def body(g_ref, wm_ref, ws_ref):
        nat = jnp.concatenate([g_ref[j] for j in range(n)], axis=1)
        wm_ref[...] = jnp.concatenate([nat[:, NAT_Z:NAT_SMALL], nat[:, NAT_REST:], nat[:, :NAT_Z]], axis=1)
        ws_ref[...] = jnp.concatenate([nat[:, NAT_SMALL:NAT_REST], jnp.zeros((tr, LANES - 16), nat.dtype)], axis=1)

    return _pc(body, name="repack_w", grid=(d // tr,),
               out_shape=(jax.ShapeDtypeStruct((d, n * cols - 16), g_win.dtype), jax.ShapeDtypeStruct((d, LANES), g_win.dtype)),
               in_specs=[pl.BlockSpec((n, tr, cols), lambda i: (0, i, 0))],
               out_specs=(pl.BlockSpec((tr, n * cols - 16), lambda i: (i, 0)), pl.BlockSpec((tr, LANES), lambda i: (i, 0))),
               compiler_params=_cp(("arbitrary",)))(g_win)


def _grad_slabs(gwa, gwb, gws, cols):
    d = gwa.shape[0]
    tr = 256

    def body(a_ref, b_ref, s_ref, o_ref):
        a = a_ref[...]
        nat = jnp.concatenate([b_ref[...], a[:, :NAT_SMALL - NAT_Z], s_ref[:, 0:16], a[:, NAT_SMALL - NAT_Z:]], axis=1)
        for j in range(N_DEV):
            o_ref[j] = nat[:, j * cols:(j + 1) * cols].astype(BF)

    return _pc(body, name="grad_slabs", grid=(d // tr,), out_shape=jax.ShapeDtypeStruct((N_DEV, d, cols), BF),
               in_specs=[pl.BlockSpec((tr, gwa.shape[1]), lambda i: (i, 0)), pl.BlockSpec((tr, gwb.shape[1]), lambda i: (i, 0)),
                         pl.BlockSpec((tr, LANES), lambda i: (i, 0))],
               out_specs=pl.BlockSpec((N_DEV, tr, cols), lambda i: (0, i, 0)), compiler_params=_cp(("arbitrary",)))(gwa, gwb, gws)


def _inproj(x, ng, sc1, sh, wmain, wsmall):
    t, d = x.shape
    n = wmain.shape[1]
    tm, tn, rb = min(t, 1024), 1024, 128

    def body(x_ref, ng_ref, sc_ref, sh_ref, wm_ref, ws_ref, p_ref, ps_ref, h_ref, hs_ref):
        @pl.when(pl.program_id(1) == 0)
        def _():
            def rows(i, carry):
                r0 = pl.multiple_of(i * rb, rb)
                xv = x_ref[pl.ds(r0, rb), :]
                r = lax.rsqrt(jnp.mean(xv * xv, axis=1, keepdims=True) + EPS)
                hb = (((xv * r) * ng_ref[...]) * sc_ref[...] + sh_ref[...]).astype(BF)
                hs_ref[pl.ds(r0, rb), :] = hb
                h_ref[pl.ds(r0, rb), :] = hb
                return carry
            lax.fori_loop(0, tm // rb, rows, 0)
            ps_ref[...] = jnp.dot(hs_ref[...], ws_ref[...], preferred_element_type=f32)
        p_ref[...] = jnp.dot(hs_ref[...], wm_ref[...], preferred_element_type=f32).astype(BF)

    vec = pl.BlockSpec((1, d), lambda i, j: (0, 0))
    return _pc(body, name="inproj", grid=(t // tm, n // tn),
               out_shape=(jax.ShapeDtypeStruct((t, n), BF), jax.ShapeDtypeStruct((t, LANES), f32), jax.ShapeDtypeStruct((t, d), BF)),
               in_specs=[pl.BlockSpec((tm, d), lambda i, j: (i, 0)), vec, vec, vec,
                         pl.BlockSpec((d, tn), lambda i, j: (0, j)), pl.BlockSpec((d, LANES), lambda i, j: (0, 0))],
               out_specs=(pl.BlockSpec((tm, tn), lambda i, j: (i, j)), pl.BlockSpec((tm, LANES), lambda i, j: (i, 0)),
                          pl.BlockSpec((tm, d), lambda i, j: (i, 0))),
               scratch_shapes=[pltpu.VMEM((tm, d), BF)], compiler_params=_cp(("arbitrary", "arbitrary")))(x, ng, sc1, sh, wmain, wsmall)


P_Z, P_SGU, P_SGV, P_SGG, P_CVA, P_CVB, P_CVG, P_Q = 0, 1024, 1536, 2048, 2560, 3072, 3584, 4096


def _dn_specs(t, tb, rev):
    nt = t // tb
    ti = (lambda i: nt - 1 - i) if rev else (lambda i: i)
    qb = P_Q // 1024
    hb = tb // QK_HALO
    specs = []
    for s in range(3):
        specs.append(pl.BlockSpec((tb, 1024), lambda i, s=s: (ti(i), qb + s)))
    for s in range(3):
        specs.append(pl.BlockSpec((QK_HALO, 1024), lambda i, s=s: (jnp.maximum(ti(i) * hb - 1, 0), qb + s)))
    specs.append(pl.BlockSpec((DN_CONV, 3072), lambda i: (0, 0)))
    specs.append(pl.BlockSpec((tb, LANES), lambda i: (ti(i), 0)))
    specs.append(pl.BlockSpec((tb // CHUNK, 16, CHUNK), lambda i: (ti(i), 0, 0)))
    specs.append(pl.BlockSpec((1, LANES), lambda i: (0, 0)))
    specs.append(pl.BlockSpec((1, LANES), lambda i: (0, 0)))
    specs.append(pl.BlockSpec((16, CHUNK), lambda i: (0, 0)))
    specs.append(pl.BlockSpec((16, CHUNK), lambda i: (0, 0)))
    return specs, ti


def _dn_conv(first, tiles, halos, cw_ref, xs_ref, cpre_ref, tb):
    for s in range(3):
        c0 = s * 1024
        xs_ref[0:QK_HALO, c0:c0 + 1024] = jnp.where(first, 0.0, halos[s][...].astype(f32))
        xs_ref[QK_HALO:QK_HALO + tb, c0:c0 + 1024] = tiles[s][...].astype(f32)
    for sub in range(3072 // 256):
        cs = slice(sub * 256, (sub + 1) * 256)
        acc = cw_ref[0:1, cs] * xs_ref[pl.ds(QK_HALO - 3, tb), cs]
        for j in range(1, DN_CONV):
            acc = acc + cw_ref[j:j + 1, cs] * xs_ref[pl.ds(QK_HALO - 3 + j, tb), cs]
        cpre_ref[:, cs] = acc


def _dn_head(cpre_ref, r0, h, beta_all, gc_all, gcrow_ref, lane, tri_incl, tri_strict):
    rows = pl.ds(r0, CHUNK)
    cq = cpre_ref[rows, pl.ds(HD * h, HD)]
    ck = cpre_ref[rows, pl.ds(1024 + HD * h, HD)]
    cv = cpre_ref[rows, pl.ds(2048 + HD * h, HD)]
    sq, sk, v = _silu(cq), _silu(ck), _silu(cv)
    rq = lax.rsqrt(jnp.sum(sq * sq, axis=1, keepdims=True) + EPS)
    rk = lax.rsqrt(jnp.sum(sk * sk, axis=1, keepdims=True) + EPS)
    q, k = sq * rq, sk * rk
    beta = _lanecol(beta_all, lane, h)
    gcc = _lanecol(gc_all, lane, 8 + h)
    gcr = gcrow_ref[pl.ds(8 + h, 1), :]
    diff = gcc - gcr
    dmat = jnp.where(tri_incl, jnp.exp(jnp.where(tri_incl, diff, 0.0)), 0.0)
    gam = jnp.exp(gcc)
    glast = gcr[:, CHUNK - 1:CHUNK]
    gam_last = jnp.exp(glast)
    kscale = jnp.exp(glast - gcc)
    qs = q * (HD ** -0.5)
    kb = k * beta
    vb = v * beta
    kbg = kb * gam
    kk = _dot_nt(kb, k)
    lmat = jnp.where(tri_strict, kk * dmat, 0.0)
    pmat = _dot_nt(qs, k) * dmat
    return dict(cq=cq, ck=ck, cv=cv, sq=sq, sk=sk, v=v, rq=rq, rk=rk, q=q, k=k, beta=beta, gcc=gcc, dmat=dmat, gam=gam,
                gam_last=gam_last, kscale=kscale, qs=qs, kb=kb, vb=vb, kbg=kbg, lmat=lmat, pmat=pmat)


def _dn_gates(ps_ref, arow_ref, dtb_ref, pst_ref, acol_ref, dtc_ref):
    ps = ps_ref[...]
    beta_all = _sigmoid(ps)
    g_all = arow_ref[...] * _softplus(ps + dtb_ref[...])
    return ps, beta_all, g_all


def _dn_fwd(p, ps, pst, convw, arow, dtb, acol, dtc, comm=()):
    t = p.shape[0]
    nh = 1024 // HD
    tb = min(t, 256)
    nc = tb // CHUNK
    specs, _ = _dn_specs(t, tb, False)

    def compute(q_ref, k_ref, v_ref, hq_ref, hk_ref, hv_ref, cw_ref, ps_ref, pst_ref, arow_ref, dtb_ref, acol_ref, dtc_ref,
             o_ref, ss_ref, ti_ref, xs_ref, cpre_ref, st_ref, gcrow_ref, ball_ref, gall_ref):
        i = pl.program_id(0)

        @pl.when(i == 0)
        def _():
            st_ref[...] = jnp.zeros_like(st_ref)

        _dn_conv(i == 0, (q_ref, k_ref, v_ref), (hq_ref, hk_ref, hv_ref), cw_ref, xs_ref, cpre_ref, tb)
        _, beta_all, g_all = _dn_gates(ps_ref, arow_ref, dtb_ref, pst_ref, acol_ref, dtc_ref)
        ball_ref[...] = beta_all
        gall_ref[...] = g_all
        ri = lax.broadcasted_iota(jnp.int32, (CHUNK, CHUNK), 0)
        ci = lax.broadcasted_iota(jnp.int32, (CHUNK, CHUNK), 1)
        tri_incl, tri_strict = ri >= ci, ri > ci
        lower = tri_incl.astype(f32)
        upper = (ri <= ci).astype(f32)
        lane = lax.broadcasted_iota(jnp.int32, (CHUNK, LANES), 1)

        def chunk_group(cg, carry):
            hs = range(nh)
            mm = lambda a, b: jnp.dot(a, b, preferred_element_type=f32)
            cs = [cg * DN_GROUP + j for j in range(DN_GROUP)]
            r0s = [pl.multiple_of(c * CHUNK, CHUNK) for c in cs]
            hd = []
            for j in range(DN_GROUP):
                gc_all = _dot_hi(lower, gall_ref[pl.ds(r0s[j], CHUNK), :])
                g_rows = acol_ref[...] * _softplus(pst_ref[cs[j]] + dtc_ref[...])
                gcrow_ref[j] = _dot_hi(g_rows, upper)
                beta_c = ball_ref[pl.ds(r0s[j], CHUNK), :]
                hd += [_dn_head(cpre_ref, r0s[j], h, beta_c, gc_all, gcrow_ref.at[j], lane, tri_incl, tri_strict) for h in hs]
            tinv = _tri_inv([d["lmat"] for d in hd])
            tib = [x.astype(BF) for x in tinv]
            u = [mm(tib[i], hd[i]["vb"].astype(BF)) for i in range(len(hd))]
            w = [mm(tib[i], hd[i]["kbg"].astype(BF)).astype(BF) for i in range(len(hd))]
            qg = [(d["qs"] * d["gam"]).astype(BF) for d in hd]
            kt = [(d["k"] * d["kscale"]).astype(BF) for d in hd]
            pm = [d["pmat"].astype(BF) for d in hd]
            for j in range(DN_GROUP):
                ix = [j * nh + h for h in hs]
                s = [st_ref[h] for h in hs]
                sb = [x.astype(BF) for x in s]
                vnb = [(u[ix[h]] - mm(w[ix[h]], sb[h])).astype(BF) for h in hs]
                o = [mm(qg[ix[h]], sb[h]) + mm(pm[ix[h]], vnb[h]) for h in hs]
                snew = [s[h] * hd[ix[h]]["gam_last"] + lax.dot_general(kt[ix[h]], vnb[h], (((0,), (0,)), ((), ())),
                                                                      preferred_element_type=f32) for h in hs]
                for h in hs:
                    o_ref[pl.ds(r0s[j], CHUNK), pl.ds(HD * h, HD)] = o[h]
                    ss_ref[cs[j], h] = sb[h]
                    ti_ref[cs[j], h] = tinv[ix[h]]
                    st_ref[h] = snew[h]
            return carry

        lax.fori_loop(0, nc // DN_GROUP, chunk_group, 0)

    return _with_comm(
        compute, comm, "dn_fwd", (t // tb,), (p, p, p, p, p, p, convw, ps, pst, arow, dtb, acol, dtc), specs,
        (jax.ShapeDtypeStruct((t, 1024), f32), jax.ShapeDtypeStruct((t // CHUNK, nh, HD, HD), BF),
         jax.ShapeDtypeStruct((t // CHUNK, nh, CHUNK, CHUNK), f32)),
        (pl.BlockSpec((tb, 1024), lambda i: (i, 0)), pl.BlockSpec((nc, nh, HD, HD), lambda i: (i, 0, 0, 0)),
         pl.BlockSpec((nc, nh, CHUNK, CHUNK), lambda i: (i, 0, 0, 0))),
        [pltpu.VMEM((tb + QK_HALO, 3072), f32), pltpu.VMEM((tb, 3072), f32), pltpu.VMEM((nh, HD, HD), f32),
         pltpu.VMEM((DN_GROUP, 16, CHUNK), f32), pltpu.VMEM((tb, LANES), f32), pltpu.VMEM((tb, LANES), f32)])


def _dn_bwd(p, ps, pst, convw, arow, dtb, acol, dtc, do, ss, tinv_all, comm=()):
    t = p.shape[0]
    nh = 1024 // HD
    tb = min(t, 256)
    nc = tb // CHUNK
    nt = t // tb
    specs, ti = _dn_specs(t, tb, True)
    specs = specs + [pl.BlockSpec((tb, 1024), lambda i: (ti(i), 0)), pl.BlockSpec((nc, nh, HD, HD), lambda i: (ti(i), 0, 0, 0)),
                     pl.BlockSpec((nc, nh, CHUNK, CHUNK), lambda i: (ti(i), 0, 0, 0))]

    def compute(q_ref, k_ref, v_ref, hq_ref, hk_ref, hv_ref, cw_ref, ps_ref, pst_ref, arow_ref, dtb_ref, acol_ref, dtc_ref,
             do_ref, ss_ref, ti_ref,
             dp_ref, dps_ref, dcw_ref, dal_ref, ddt_ref,
             xs_ref, cpre_ref, dst_ref, gcrow_ref, ball_ref, gall_ref, dcs_ref, dball_ref, dgcall_ref):
        i = pl.program_id(0)

        @pl.when(i == 0)
        def _():
            dst_ref[...] = jnp.zeros_like(dst_ref)
            dcs_ref[...] = jnp.zeros_like(dcs_ref)
            dcw_ref[...] = jnp.zeros_like(dcw_ref)
            dal_ref[...] = jnp.zeros_like(dal_ref)
            ddt_ref[...] = jnp.zeros_like(ddt_ref)

        _dn_conv(i == nt - 1, (q_ref, k_ref, v_ref), (hq_ref, hk_ref, hv_ref), cw_ref, xs_ref, cpre_ref, tb)
        ps, beta_all, g_all = _dn_gates(ps_ref, arow_ref, dtb_ref, pst_ref, acol_ref, dtc_ref)
        ball_ref[...] = beta_all
        gall_ref[...] = g_all
        ri = lax.broadcasted_iota(jnp.int32, (CHUNK, CHUNK), 0)
        ci = lax.broadcasted_iota(jnp.int32, (CHUNK, CHUNK), 1)
        tri_incl, tri_strict = ri >= ci, ri > ci
        lower = tri_incl.astype(f32)
        upper = (ri <= ci).astype(f32)
        lane = lax.broadcasted_iota(jnp.int32, (CHUNK, LANES), 1)
        row_last = lax.broadcasted_iota(jnp.int32, (CHUNK, 1), 0) == CHUNK - 1

        def chunk_group(cg, carry):
            hs = range(nh)
            bf = lambda a: a.astype(BF)
            mm = lambda a, b: jnp.dot(a, b, preferred_element_type=f32)
            mnt = lambda a, b: lax.dot_general(a, b, (((1,), (1,)), ((), ())), preferred_element_type=f32)
            mtn = lambda a, b: lax.dot_general(a, b, (((0,), (0,)), ((), ())), preferred_element_type=f32)
            rsum = lambda a: jnp.sum(a, axis=1, keepdims=True)
            grp = range(DN_GROUP_BWD)
            cs = [nc - 1 - (cg * DN_GROUP_BWD + j) for j in grp]
            r0s = [pl.multiple_of(c * CHUNK, CHUNK) for c in cs]
            nn = range(DN_GROUP_BWD * nh)
            jof = [i // nh for i in nn]
            hof = [i % nh for i in nn]
            hd = []
            for j in grp:
                rows = pl.ds(r0s[j], CHUNK)
                gc_all = _dot_hi(lower, gall_ref[rows, :])
                g_rows = acol_ref[...] * _softplus(pst_ref[cs[j]] + dtc_ref[...])
                gcrow_ref[j] = _dot_hi(g_rows, upper)
                beta_c = ball_ref[rows, :]
                hd += [_dn_head(cpre_ref, r0s[j], h, beta_c, gc_all, gcrow_ref.at[j], lane, tri_incl, tri_strict) for h in hs]
            tib = [bf(ti_ref[cs[jof[i]], hof[i]]) for i in nn]
            sb = [ss_ref[cs[jof[i]], hof[i]] for i in nn]
            vbb = [bf(d["vb"]) for d in hd]
            kgb = [bf(d["kbg"]) for d in hd]
            u = [mm(tib[i], vbb[i]) for i in nn]
            wb = [bf(mm(tib[i], kgb[i])) for i in nn]
            vnb = [bf(u[i] - mm(wb[i], sb[i])) for i in nn]
            qgb = [bf(d["qs"] * d["gam"]) for d in hd]
            kt = [d["k"] * d["kscale"] for d in hd]
            ktb = [bf(x) for x in kt]
            pmb = [bf(d["pmat"]) for d in hd]
            dob = [bf(do_ref[pl.ds(r0s[jof[i]], CHUNK), pl.ds(HD * hof[i], HD)]) for i in nn]
            dvn0 = [mtn(pmb[i], dob[i]) for i in nn]
            dsn0 = [mtn(qgb[i], dob[i]) for i in nn]
            dpm = [jnp.where(tri_incl, mnt(dob[i], vnb[i]), 0.0) for i in nn]
            dqg = [mnt(dob[i], sb[i]) for i in nn]
            dvnb, dkt, dgl = [], [], []
            for j in grp:
                ix = [j * nh + h for h in hs]
                dsn = [dst_ref[h] for h in hs]
                dsnb = [bf(x) for x in dsn]
                dvn_j = [bf(dvn0[ix[h]] + mm(ktb[ix[h]], dsnb[h])) for h in hs]
                dsnew = [dsn0[ix[h]] + hd[ix[h]]["gam_last"] * dsn[h] - mtn(wb[ix[h]], dvn_j[h]) for h in hs]
                for h in hs:
                    dst_ref[h] = dsnew[h]
                dkt += [mnt(vnb[ix[h]], dsnb[h]) for h in hs]
                dgl += [jnp.sum(rsum(sb[ix[h]].astype(f32) * dsn[h]), axis=0, keepdims=True) for h in hs]
                dvnb += dvn_j
            dwb = [bf(-mnt(dvnb[i], sb[i])) for i in nn]
            dtm = [mnt(dvnb[i], vbb[i]) + mnt(dwb[i], kgb[i]) for i in nn]
            dvb = [mtn(tib[i], dvnb[i]) for i in nn]
            dkbg = [mtn(tib[i], dwb[i]) for i in nn]
            x1 = [bf(mtn(tib[i], bf(dtm[i]))) for i in nn]
            dl = [jnp.where(tri_strict, -mnt(x1[i], tib[i]), 0.0) for i in nn]
            mmat = [dl[i] * hd[i]["lmat"] + dpm[i] * hd[i]["pmat"] for i in nn]
            dkkb = [bf(dl[i] * hd[i]["dmat"]) for i in nn]
            dqkb = [bf(dpm[i] * hd[i]["dmat"]) for i in nn]
            kb16 = [bf(d["k"]) for d in hd]
            dkb = [mm(dkkb[i], kb16[i]) + dkbg[i] * hd[i]["gam"] for i in nn]
            dk = [mtn(dkkb[i], bf(hd[i]["kb"])) + mtn(dqkb[i], bf(hd[i]["qs"])) + dkt[i] * hd[i]["kscale"] + dkb[i] * hd[i]["beta"]
                  for i in nn]
            dq = [(mm(dqkb[i], kb16[i]) + dqg[i] * hd[i]["gam"]) * (HD ** -0.5) for i in nn]
            dbeta = [rsum(dkb[i] * hd[i]["k"] + dvb[i] * hd[i]["v"]) for i in nn]
            dgam = [rsum(dkbg[i] * hd[i]["kb"] + dqg[i] * hd[i]["qs"]) for i in nn]
            ktdk = [rsum(dkt[i] * kt[i]) for i in nn]
            csum = [rsum(jnp.where(ri == ci, jnp.sum(mmat[i], axis=0, keepdims=True), 0.0)) for i in nn]
            for j in grp:
                rows = pl.ds(r0s[j], CHUNK)
                dbeta_all = jnp.zeros((CHUNK, LANES), f32)
                dgc_all = jnp.zeros((CHUNK, LANES), f32)
                for h in hs:
                    i = j * nh + h
                    d = hd[i]
                    extra = jnp.sum(ktdk[i], axis=0, keepdims=True) + dgl[i] * d["gam_last"]
                    dgc = rsum(mmat[i]) - csum[i] + dgam[i] * d["gam"] - ktdk[i] + jnp.where(row_last, extra, 0.0)
                    dbeta_all = jnp.where(lane == h, dbeta[i], dbeta_all)
                    dgc_all = jnp.where(lane == 8 + h, dgc, dgc_all)
                    dsq = d["rq"] * (dq[i] - d["q"] * rsum(dq[i] * d["q"]))
                    dsk = d["rk"] * (dk[i] - d["k"] * rsum(dk[i] * d["k"]))
                    dcs_ref[rows, pl.ds(HD * h, HD)] = dsq * _dsilu(d["cq"])
                    dcs_ref[rows, pl.ds(1024 + HD * h, HD)] = dsk * _dsilu(d["ck"])
                    dcs_ref[rows, pl.ds(2048 + HD * h, HD)] = (dvb[i] * d["beta"]) * _dsilu(d["cv"])
                dball_ref[rows, :] = dbeta_all
                dgcall_ref[rows, :] = _dot_hi(upper, dgc_all)
            return carry

        lax.fori_loop(0, nc // DN_GROUP_BWD, chunk_group, 0)

        dbeta_t = dball_ref[...]
        dg_t = dgcall_ref[...]
        sg = _sigmoid(ps + dtb_ref[...])
        da = dg_t * arow_ref[...] * sg
        dps_ref[...] = dbeta_t * beta_all * (1.0 - beta_all) + da
        ddt_ref[...] += jnp.sum(da, axis=0, keepdims=True)
        dal_ref[...] += jnp.sum(dg_t * g_all, axis=0, keepdims=True)

        for sub in range(3072 // 256):
            cs = slice(sub * 256, (sub + 1) * 256)
            dc = dcs_ref[0:tb, cs]
            acc = cw_ref[DN_CONV - 1:DN_CONV, cs] * dc
            for j in range(DN_CONV - 1):
                acc = acc + cw_ref[j:j + 1, cs] * dcs_ref[pl.ds(DN_CONV - 1 - j, tb), cs]
            dp_ref[:, cs] = acc.astype(BF)
            for j in range(DN_CONV):
                dcw_ref[j:j + 1, cs] += jnp.sum(dc * xs_ref[pl.ds(QK_HALO - 3 + j, tb), cs], axis=0, keepdims=True)
        dcs_ref[tb:tb + QK_HALO, :] = dcs_ref[0:QK_HALO, :]

    vecl = pl.BlockSpec((1, LANES), lambda i: (0, 0))
    return _with_comm(
        compute, comm, "dn_bwd", (nt,), (p, p, p, p, p, p, convw, ps, pst, arow, dtb, acol, dtc, do, ss, tinv_all), specs,
        (jax.ShapeDtypeStruct((t, 3072), BF), jax.ShapeDtypeStruct((t, LANES), f32), jax.ShapeDtypeStruct((DN_CONV, 3072), f32),
         jax.ShapeDtypeStruct((1, LANES), f32), jax.ShapeDtypeStruct((1, LANES), f32)),
        (pl.BlockSpec((tb, 3072), lambda i: (ti(i), 0)), pl.BlockSpec((tb, LANES), lambda i: (ti(i), 0)),
         pl.BlockSpec((DN_CONV, 3072), lambda i: (0, 0)), vecl, vecl),
        [pltpu.VMEM((tb + QK_HALO, 3072), f32), pltpu.VMEM((tb, 3072), f32), pltpu.VMEM((nh, HD, HD), f32),
         pltpu.VMEM((DN_GROUP_BWD, 16, CHUNK), f32), pltpu.VMEM((tb, LANES), f32), pltpu.VMEM((tb, LANES), f32),
         pltpu.VMEM((tb + QK_HALO, 3072), f32), pltpu.VMEM((tb, LANES), f32), pltpu.VMEM((tb, LANES), f32)])


def _ln_fwd(x, g, b):
    mu = jnp.mean(x, axis=1, keepdims=True)
    xc = x - mu
    rstd = lax.rsqrt(jnp.mean(xc * xc, axis=1, keepdims=True) + LN_EPS)
    xhat = xc * rstd
    return xhat * g + b, xhat, rstd


def _ln_bwd(dy, xhat, rstd, g):
    dxh = dy * g
    return rstd * (dxh - jnp.mean(dxh, axis=1, keepdims=True) - xhat * jnp.mean(dxh * xhat, axis=1, keepdims=True))


def _br_specs(t, tb, rev):
    nt = t // tb
    ti = (lambda i: nt - 1 - i) if rev else (lambda i: i)
    hb = tb // CV_HALO
    col = lambda off: off // 512
    specs = [pl.BlockSpec((tb, 1024), lambda i: (ti(i), 0)),
             pl.BlockSpec((tb, 1024), lambda i: (ti(i), 0))]
    for off in (P_SGU, P_SGV, P_SGG, P_CVA, P_CVB, P_CVG):
        specs.append(pl.BlockSpec((tb, 512), lambda i, off=off: (ti(i), col(off))))
    for off in (P_CVA, P_CVB):
        specs.append(pl.BlockSpec((CV_HALO, 512), lambda i, off=off: (jnp.maximum(ti(i) * hb - 1, 0), col(off))))
    v512 = pl.BlockSpec((1, 512), lambda i: (0, 0))
    specs += [pl.BlockSpec((1, HD), lambda i: (0, 0)), v512, v512,
              pl.BlockSpec((SG_GROUPS, SG_BLOCK, SG_BLOCK), lambda i: (0, 0, 0)),
              pl.BlockSpec((SG_BLOCK, 512), lambda i: (0, 0)),
              pl.BlockSpec((CV_HALO, 512), lambda i: (0, 0)),
              v512, v512, v512]
    return specs, ti


def _sg_mask():
    r = lax.broadcasted_iota(jnp.int32, (SG_BLOCK, SG_BLOCK), 0) // CHUNK
    c = lax.broadcasted_iota(jnp.int32, (SG_BLOCK, SG_BLOCK), 1) // CHUNK
    return r >= c


def _cv_glu(first, a_ref, b_ref, ha_ref, hb_ref, gs_ref, tb):
    ha = jnp.where(first, 0.0, ha_ref[...].astype(f32))
    gs_ref[0:CV_HALO, :] = ha * _sigmoid(hb_ref[...].astype(f32))
    gs_ref[CV_HALO:CV_HALO + tb, :] = a_ref[...].astype(f32) * _sigmoid(b_ref[...].astype(f32))


SUBLANES = 8
CV_SHIFT_ROWS = CV_HALO - SUBLANES


def _cv_shift(src_ref, sh_ref, tb, cs):
    for b in range(1, SUBLANES):
        sh_ref[b - 1, 0:tb + CV_SHIFT_ROWS, cs] = src_ref[pl.ds(b, tb + CV_SHIFT_ROWS), cs]


def _cv_tap(src_ref, sh_ref, off, tb, cs):
    a, b = divmod(off, SUBLANES)
    if b == 0:
        return src_ref[pl.ds(SUBLANES * a, tb), cs]
    return sh_ref[b - 1, pl.ds(SUBLANES * a, tb), cs]


def _cv_conv(gs_ref, gsh_ref, cw_ref, cb_ref, tb, cs):
    acc = cb_ref[:, cs] + cw_ref[0:1, cs] * _cv_tap(gs_ref, gsh_ref, CV_HALO - (CV_K - 1), tb, cs)
    for j in range(1, CV_K):
        acc = acc + cw_ref[j:j + 1, cs] * _cv_tap(gs_ref, gsh_ref, CV_HALO - (CV_K - 1) + j, tb, cs)
    return acc


def _branch_fwd(o, p, dng, slg, slb, sgw, sgbias, cvw, cvb, clg, clb):
    t = o.shape[0]
    tb = min(t, 256)
    specs, _ = _br_specs(t, tb, False)

    def body(o_ref, z_ref, su_ref, sv_ref, sgt_ref, ca_ref, cb_ref, cg_ref, ha_ref, hb_ref,
             dng_ref, slg_ref, slb_ref, sgw_ref, sgb_ref, cvw_ref, cvb_ref, clg_ref, clb_ref, y_ref, cvo_ref, gs_ref, vln_ref, gsh_ref):
        i = pl.program_id(0)
        for h in range(1024 // HD):
            cs = slice(h * HD, (h + 1) * HD)
            oh = o_ref[:, cs]
            r = lax.rsqrt(jnp.mean(oh * oh, axis=1, keepdims=True) + EPS)
            y_ref[:, cs] = ((oh * r) * dng_ref[...] * _silu(z_ref[:, cs].astype(f32))).astype(BF)
        mask = _sg_mask()
        for g in range(SG_GROUPS):
            cs = slice(g * HD, (g + 1) * HD)
            vg = _gelu(sv_ref[:, cs].astype(f32))
            vln, _, _ = _ln_fwd(vg, slg_ref[:, cs], slb_ref[:, cs])
            vln_ref[...] = vln
            ws = jnp.where(mask, sgw_ref[g], 0.0)
            for nb in range(tb // SG_BLOCK):
                rs = slice(nb * SG_BLOCK, (nb + 1) * SG_BLOCK)
                mixed = _dot(ws, vln_ref[rs, :]) + sgb_ref[:, cs]
                u = _gelu(su_ref[rs, cs].astype(f32))
                y_ref[rs, 1024 + g * HD:1024 + (g + 1) * HD] = (u * mixed * _silu(sgt_ref[rs, cs].astype(f32))).astype(BF)
        _cv_glu(i == 0, ca_ref, cb_ref, ha_ref, hb_ref, gs_ref, tb)
        for g in range(CV_GROUPS):
            cs = slice(g * HD, (g + 1) * HD)
            _cv_shift(gs_ref, gsh_ref, tb, cs)
            dw = _cv_conv(gs_ref, gsh_ref, cvw_ref, cvb_ref, tb, cs)
            cvo_ref[:, cs] = dw
            ln, _, _ = _ln_fwd(dw, clg_ref[:, cs], clb_ref[:, cs])
            y_ref[:, 1536 + g * HD:1536 + (g + 1) * HD] = (_silu(ln) * _silu(cg_ref[:, cs].astype(f32))).astype(BF)

    return _pc(body, name="branch_fwd", grid=(t // tb,),
               out_shape=(jax.ShapeDtypeStruct((t, 2048), BF), jax.ShapeDtypeStruct((t, 512), f32)),
               in_specs=specs, out_specs=(pl.BlockSpec((tb, 2048), lambda i: (i, 0)), pl.BlockSpec((tb, 512), lambda i: (i, 0))),
               scratch_shapes=[pltpu.VMEM((tb + CV_HALO, 512), f32), pltpu.VMEM((tb, HD), f32),
                               pltpu.VMEM((SUBLANES - 1, tb + CV_SHIFT_ROWS, 512), f32)],
               compiler_params=_cp(("arbitrary",)))(o, p, p, p, p, p, p, p, p, p, dng, slg, slb, sgw, sgbias, cvw, cvb, clg, clb)


def _branch_bwd(o, p, dng, slg, slb, sgw, sgbias, cvw, cvb, clg, clb, dy, cvo):
    t = o.shape[0]
    tb = min(t, 256)
    nt = t // tb
    specs, ti = _br_specs(t, tb, True)
    specs = specs + [pl.BlockSpec((tb, 2048), lambda i: (ti(i), 0)), pl.BlockSpec((tb, 512), lambda i: (ti(i), 0))]

    def body(o_ref, z_ref, su_ref, sv_ref, sgt_ref, ca_ref, cb_ref, cg_ref, ha_ref, hb_ref,
             dng_ref, slg_ref, slb_ref, sgw_ref, sgb_ref, cvw_ref, cvb_ref, clg_ref, clb_ref, dy_ref, cvo_ref,
             dp_ref, do_ref, ddng_ref, dslg_ref, dslb_ref, dsgw_ref, dsgb_ref, dcvw_ref, dcvb_ref, dclg_ref, dclb_ref,
             gs_ref, vln_ref, xh_ref, dvl_ref, ddw_ref, dbias_ref, gsh_ref, dsh_ref):
        i = pl.program_id(0)

        @pl.when(i == 0)
        def _():
            for r in (ddng_ref, dslg_ref, dslb_ref, dsgw_ref, dsgb_ref, dcvw_ref, dcvb_ref, dclg_ref, dclb_ref, ddw_ref, dbias_ref):
                r[...] = jnp.zeros_like(r)

        for h in range(1024 // HD):
            cs = slice(h * HD, (h + 1) * HD)
            oh = o_ref[:, cs]
            zz = z_ref[:, cs].astype(f32)
            dyh = dy_ref[:, cs].astype(f32)
            r = lax.rsqrt(jnp.mean(oh * oh, axis=1, keepdims=True) + EPS)
            nrm = oh * r
            sz = _silu(zz)
            dn = dyh * dng_ref[...] * sz
            ddng_ref[...] += jnp.sum(dyh * nrm * sz, axis=0, keepdims=True)
            dp_ref[:, cs] = (dyh * nrm * dng_ref[...] * _dsilu(zz)).astype(BF)
            do_ref[:, cs] = r * (dn - nrm * jnp.mean(dn * nrm, axis=1, keepdims=True))
        mask = _sg_mask()
        for g in range(SG_GROUPS):
            cs = slice(g * HD, (g + 1) * HD)
            sv = sv_ref[:, cs].astype(f32)
            vln, xhat, rstd = _ln_fwd(_gelu(sv), slg_ref[:, cs], slb_ref[:, cs])
            vln_ref[...] = vln
            xh_ref[...] = xhat
            ws = jnp.where(mask, sgw_ref[g], 0.0)
            dws = jnp.zeros((SG_BLOCK, SG_BLOCK), f32)
            for nb in range(tb // SG_BLOCK):
                rs = slice(nb * SG_BLOCK, (nb + 1) * SG_BLOCK)
                dyb = dy_ref[rs, 1024 + g * HD:1024 + (g + 1) * HD].astype(f32)
                su = su_ref[rs, cs].astype(f32)
                gt = sgt_ref[rs, cs].astype(f32)
                mixed = _dot(ws, vln_ref[rs, :]) + sgb_ref[:, cs]
                u = _gelu(su)
                sgt = _silu(gt)
                dmixed = dyb * u * sgt
                dp_ref[rs, P_SGU + g * HD:P_SGU + (g + 1) * HD] = (dyb * mixed * sgt * _dgelu(su)).astype(BF)
                dp_ref[rs, P_SGG + g * HD:P_SGG + (g + 1) * HD] = (dyb * u * mixed * _dsilu(gt)).astype(BF)
                dvl_ref[rs, :] = _dot_tn(ws, dmixed)
                dws = dws + _dot_nt(dmixed, vln_ref[rs, :])
                dbias_ref[:, cs] += dmixed
            dsgw_ref[g] += jnp.where(mask, dws, 0.0)
            dvl = dvl_ref[...]
            xhat = xh_ref[...]
            dslg_ref[:, cs] += jnp.sum(dvl * xhat, axis=0, keepdims=True)
            dslb_ref[:, cs] += jnp.sum(dvl, axis=0, keepdims=True)
            dvg = _ln_bwd(dvl, xhat, rstd, slg_ref[:, cs])
            dp_ref[:, P_SGV + g * HD:P_SGV + (g + 1) * HD] = (dvg * _dgelu(sv)).astype(BF)
        _cv_glu(i == nt - 1, ca_ref, cb_ref, ha_ref, hb_ref, gs_ref, tb)
        for g in range(CV_GROUPS):
            cs = slice(g * HD, (g + 1) * HD)
            dyc = dy_ref[:, 1536 + g * HD:1536 + (g + 1) * HD].astype(f32)
            cg = cg_ref[:, cs].astype(f32)
            _cv_shift(gs_ref, gsh_ref, tb, cs)
            ln, xhat, rstd = _ln_fwd(cvo_ref[:, cs], clg_ref[:, cs], clb_ref[:, cs])
            dln = dyc * _silu(cg) * _dsilu(ln)
            dp_ref[:, P_CVG + g * HD:P_CVG + (g + 1) * HD] = (dyc * _silu(ln) * _dsilu(cg)).astype(BF)
            dclg_ref[:, cs] += jnp.sum(dln * xhat, axis=0, keepdims=True)
            dclb_ref[:, cs] += jnp.sum(dln, axis=0, keepdims=True)
            ddw = _ln_bwd(dln, xhat, rstd, clg_ref[:, cs])
            dcvb_ref[:, cs] += jnp.sum(ddw, axis=0, keepdims=True)
            ddw_ref[0:tb, cs] = ddw
            _cv_shift(ddw_ref, dsh_ref, tb, cs)
            dglu = cvw_ref[CV_K - 1:CV_K, cs] * ddw
            for j in range(CV_K - 1):
                dglu = dglu + cvw_ref[j:j + 1, cs] * _cv_tap(ddw_ref, dsh_ref, CV_K - 1 - j, tb, cs)
            for j in range(CV_K):
                tap = _cv_tap(gs_ref, gsh_ref, CV_HALO - (CV_K - 1) + j, tb, cs)
                dcvw_ref[j:j + 1, cs] += jnp.sum(ddw * tap, axis=0, keepdims=True)
            a = ca_ref[:, cs].astype(f32)
            sb = _sigmoid(cb_ref[:, cs].astype(f32))
            dp_ref[:, P_CVA + g * HD:P_CVA + (g + 1) * HD] = (dglu * sb).astype(BF)
            dp_ref[:, P_CVB + g * HD:P_CVB + (g + 1) * HD] = (dglu * a * sb * (1.0 - sb)).astype(BF)
        ddw_ref[tb:tb + CV_HALO, :] = ddw_ref[0:CV_HALO, :]

        @pl.when(i == nt - 1)
        def _():
            lane = lax.broadcasted_iota(jnp.int32, (SG_BLOCK, LANES), 1)
            acc = jnp.zeros((SG_BLOCK, LANES), f32)
            for g in range(SG_GROUPS):
                acc = jnp.where(lane == g, jnp.sum(dbias_ref[:, g * HD:(g + 1) * HD], axis=1, keepdims=True), acc)
            dsgb_ref[...] = acc

    v512 = pl.BlockSpec((1, 512), lambda i: (0, 0))
    return _pc(body, name="branch_bwd", grid=(nt,),
               out_shape=(jax.ShapeDtypeStruct((t, 4096), BF), jax.ShapeDtypeStruct((t, 1024), f32),
                          jax.ShapeDtypeStruct((1, HD), f32), jax.ShapeDtypeStruct((1, 512), f32), jax.ShapeDtypeStruct((1, 512), f32),
                          jax.ShapeDtypeStruct((SG_GROUPS, SG_BLOCK, SG_BLOCK), f32), jax.ShapeDtypeStruct((SG_BLOCK, LANES), f32),
                          jax.ShapeDtypeStruct((CV_HALO, 512), f32), jax.ShapeDtypeStruct((1, 512), f32),
                          jax.ShapeDtypeStruct((1, 512), f32), jax.ShapeDtypeStruct((1, 512), f32)),
               in_specs=specs,
               out_specs=(pl.BlockSpec((tb, 4096), lambda i: (ti(i), 0)), pl.BlockSpec((tb, 1024), lambda i: (ti(i), 0)),
                          pl.BlockSpec((1, HD), lambda i: (0, 0)), v512, v512,
                          pl.BlockSpec((SG_GROUPS, SG_BLOCK, SG_BLOCK), lambda i: (0, 0, 0)),
                          pl.BlockSpec((SG_BLOCK, LANES), lambda i: (0, 0)), pl.BlockSpec((CV_HALO, 512), lambda i: (0, 0)),
                          v512, v512, v512),
               scratch_shapes=[pltpu.VMEM((tb + CV_HALO, 512), f32), pltpu.VMEM((tb, HD), f32), pltpu.VMEM((tb, HD), f32),
                               pltpu.VMEM((tb, HD), f32), pltpu.VMEM((tb + CV_HALO, 512), f32), pltpu.VMEM((SG_BLOCK, 512), f32),
                               pltpu.VMEM((SUBLANES - 1, tb + CV_SHIFT_ROWS, 512), f32),
                               pltpu.VMEM((SUBLANES - 1, tb + CV_SHIFT_ROWS, 512), f32)],
               compiler_params=_cp(("arbitrary",)))(o, p, p, p, p, p, p, p, p, p, dng, slg, slb, sgw, sgbias, cvw, cvb, clg, clb, dy, cvo)


def _outproj(ycat, w, x, gate):
    t, d = x.shape
    tm = min(t, 512)

    def body(y_ref, w_ref, x_ref, g_ref, o_ref):
        o_ref[...] = x_ref[...] + g_ref[...] * jnp.dot(y_ref[...], w_ref[...], preferred_element_type=f32)

    return _pc(body, name="outproj", grid=(t // tm,), out_shape=jax.ShapeDtypeStruct((t, d), f32),
               in_specs=[pl.BlockSpec((tm, ycat.shape[1]), lambda i: (i, 0)), pl.BlockSpec(w.shape, lambda i: (0, 0)),
                         pl.BlockSpec((tm, d), lambda i: (i, 0)), pl.BlockSpec((1, d), lambda i: (0, 0))],
               out_specs=pl.BlockSpec((tm, d), lambda i: (i, 0)), compiler_params=_cp(("arbitrary",)))(ycat, w, x, gate)


def _outproj_bwd(dxo, w, gate):
    t, d = dxo.shape
    tm = min(t, 512)

    def body(d_ref, w_ref, g_ref, o_ref):
        o_ref[...] = _dot_nt(d_ref[...] * g_ref[...], w_ref[...]).astype(BF)

    return _pc(body, name="outproj_bwd", grid=(t // tm,), out_shape=jax.ShapeDtypeStruct((t, w.shape[0]), BF),
               in_specs=[pl.BlockSpec((tm, d), lambda i: (i, 0)), pl.BlockSpec(w.shape, lambda i: (0, 0)),
                         pl.BlockSpec((1, d), lambda i: (0, 0))],
               out_specs=pl.BlockSpec((tm, w.shape[0]), lambda i: (i, 0)), compiler_params=_cp(("arbitrary",)))(dxo, w, gate)


def _tn_acc(a, b, name):
    kk, m = a.shape
    n = b.shape[1]
    tk, tn = min(kk, 1024), min(n, 1024)

    def body(a_ref, b_ref, o_ref):
        @pl.when(pl.program_id(1) == 0)
        def _():
            o_ref[...] = jnp.zeros_like(o_ref)
        o_ref[...] += _dot_tn(a_ref[...], b_ref[...])

    return _pc(body, name=name, grid=(n // tn, kk // tk), out_shape=jax.ShapeDtypeStruct((m, n), f32),
               in_specs=[pl.BlockSpec((tk, m), lambda j, k: (k, 0)), pl.BlockSpec((tk, tn), lambda j, k: (k, j))],
               out_specs=pl.BlockSpec((m, tn), lambda j, k: (0, j)), compiler_params=_cp(("arbitrary", "arbitrary")))(a, b)


def _tn_acc_pair(a, b, s, name):
    kk, m = a.shape
    n = b.shape[1]
    tk, tn = min(kk, 1024), min(n, 1024)

    def body(a_ref, b_ref, s_ref, o_ref, os_ref):
        j, k = pl.program_id(0), pl.program_id(1)

        @pl.when(k == 0)
        def _():
            o_ref[...] = jnp.zeros_like(o_ref)
        o_ref[...] += _dot_tn(a_ref[...], b_ref[...])

        @pl.when((j == 0) & (k == 0))
        def _():
            os_ref[...] = jnp.zeros_like(os_ref)

        @pl.when(j == 0)
        def _():
            os_ref[...] += _dot_tn(a_ref[...], s_ref[...])

    return _pc(body, name=name, grid=(n // tn, kk // tk),
               out_shape=(jax.ShapeDtypeStruct((m, n), f32), jax.ShapeDtypeStruct((m, LANES), f32)),
               in_specs=[pl.BlockSpec((tk, m), lambda j, k: (k, 0)), pl.BlockSpec((tk, tn), lambda j, k: (k, j)),
                         pl.BlockSpec((tk, LANES), lambda j, k: (k, 0))],
               out_specs=(pl.BlockSpec((m, tn), lambda j, k: (0, j)), pl.BlockSpec((m, LANES), lambda j, k: (0, 0))),
               compiler_params=_cp(("arbitrary", "arbitrary")))(a, b, s)


def _wout_grad(gmat, w, gate):
    m, n = gmat.shape
    tr = m // N_DEV

    def body(g_ref, w_ref, gt_ref, o_ref, dg_ref):
        @pl.when(pl.program_id(0) == 0)
        def _():
            dg_ref[...] = jnp.zeros_like(dg_ref)
        gm = g_ref[...]
        o_ref[0] = (gm * gt_ref[...]).astype(BF)
        dg_ref[...] += jnp.sum(gm * w_ref[...].astype(f32), axis=0, keepdims=True)

    return _pc(body, name="wout_grad", grid=(N_DEV,),
               out_shape=(jax.ShapeDtypeStruct((N_DEV, tr, n), BF), jax.ShapeDtypeStruct((1, n), f32)),
               in_specs=[pl.BlockSpec((tr, n), lambda i: (i, 0)), pl.BlockSpec((tr, n), lambda i: (i, 0)),
                         pl.BlockSpec((1, n), lambda i: (0, 0))],
               out_specs=(pl.BlockSpec((1, tr, n), lambda i: (i, 0, 0)), pl.BlockSpec((1, n), lambda i: (0, 0))),
               compiler_params=_cp(("arbitrary",)))(gmat, w, gate)


def _inproj_bwd(dpa, dpb, dps, wmain, wsmall, x, ng, sc1, dxo, comm=()):
    t, d = x.shape
    na, nb = dpa.shape[1] // 1024, dpb.shape[1] // 1024
    nk = na + nb
    tm, rb = min(t, 512), 128

    def body(a_ref, b_ref, s_ref, wm_ref, ws_ref, x_ref, ng_ref, sc_ref, dxo_ref, dx_ref, dsh_ref, dsc_ref, dng_ref, acc_ref):
        i, k = pl.program_id(0), pl.program_id(1)

        @pl.when((i == 0) & (k == 0))
        def _():
            dsh_ref[...] = jnp.zeros_like(dsh_ref)
            dsc_ref[...] = jnp.zeros_like(dsc_ref)
            dng_ref[...] = jnp.zeros_like(dng_ref)

        @pl.when(k == 0)
        def _():
            acc_ref[...] = _dot_nt(s_ref[...], ws_ref[...])

        @pl.when(k < na)
        def _():
            acc_ref[...] += _dot_nt(a_ref[...], wm_ref[...])

        @pl.when(k >= na)
        def _():
            acc_ref[...] += _dot_nt(b_ref[...], wm_ref[...])

        @pl.when(k == nk - 1)
        def _():
            def rows(j, carry):
                r0 = pl.multiple_of(j * rb, rb)
                rr = pl.ds(r0, rb)
                xv = x_ref[rr, :]
                dh = acc_ref[rr, :]
                r = lax.rsqrt(jnp.mean(xv * xv, axis=1, keepdims=True) + EPS)
                xn = xv * r
                dsh_ref[...] += jnp.sum(dh, axis=0, keepdims=True)
                dsc_ref[...] += jnp.sum(dh * (xn * ng_ref[...]), axis=0, keepdims=True)
                dng_ref[...] += jnp.sum(dh * sc_ref[...] * xn, axis=0, keepdims=True)
                dxn = dh * (ng_ref[...] * sc_ref[...])
                dx_ref[rr, :] = r * (dxn - xn * jnp.mean(dxn * xn, axis=1, keepdims=True)) + dxo_ref[rr, :]
                return carry
            lax.fori_loop(0, tm // rb, rows, 0)

    vec = pl.BlockSpec((1, d), lambda i, k: (0, 0))
    row = pl.BlockSpec((tm, d), lambda i, k: (i, 0))
    shp = jax.ShapeDtypeStruct((1, d), f32)
    return _with_comm(
        body, comm, "inproj_bwd", (t // tm, nk), (dpa, dpb, dps, wmain, wsmall, x, ng, sc1, dxo),
        [pl.BlockSpec((tm, 1024), lambda i, k: (i, jnp.minimum(k, na - 1))),
         pl.BlockSpec((tm, 1024), lambda i, k: (i, jnp.clip(k - na, 0, nb - 1))),
         pl.BlockSpec((tm, LANES), lambda i, k: (i, 0)),
         pl.BlockSpec((d, 1024), lambda i, k: (0, k)), pl.BlockSpec((d, LANES), lambda i, k: (0, 0)), row, vec, vec, row],
        (jax.ShapeDtypeStruct((t, d), f32), shp, shp, shp), (row, vec, vec, vec), [pltpu.VMEM((tm, d), f32)])


def _loss_head(x, fg, tgt):
    t, d = x.shape
    tm = min(t, 512)

    def body(x_ref, g_ref, t_ref, dx_ref, l_ref, dg_ref):
        @pl.when(pl.program_id(0) == 0)
        def _():
            l_ref[...] = jnp.zeros_like(l_ref)
            dg_ref[...] = jnp.zeros_like(dg_ref)
        xv = x_ref[...]
        r = lax.rsqrt(jnp.mean(xv * xv, axis=1, keepdims=True) + EPS)
        xn = xv * r
        err = xn * g_ref[...] - t_ref[...]
        l_ref[...] += 0.5 * jnp.sum(jnp.mean(err * err, axis=1, keepdims=True), axis=0, keepdims=True)
        dy = err * (1.0 / d)
        dg_ref[...] += jnp.sum(dy * xn, axis=0, keepdims=True)
        dxn = dy * g_ref[...]
        dx_ref[...] = r * (dxn - xn * jnp.mean(dxn * xn, axis=1, keepdims=True))

    row = pl.BlockSpec((tm, d), lambda i: (i, 0))
    vec = pl.BlockSpec((1, d), lambda i: (0, 0))
    return _pc(body, name="loss_head", grid=(t // tm,),
               out_shape=(jax.ShapeDtypeStruct((t, d), f32), jax.ShapeDtypeStruct((1, 1), f32), jax.ShapeDtypeStruct((1, d), f32)),
               in_specs=[row, vec, row], out_specs=(row, pl.BlockSpec((1, 1), lambda i: (0, 0)), vec),
               compiler_params=_cp(("arbitrary",)))(x, fg, tgt)


def _pack(arrs, rows_mult=8):
    flat = jnp.concatenate([a.reshape(-1).astype(f32) for a in arrs])
    n = flat.shape[0]
    per = rows_mult * LANES
    pad = (-n) % per
    if pad:
        flat = jnp.concatenate([flat, jnp.zeros((pad,), f32)])
    return flat.reshape(-1, LANES)


def _unpack(packed, shapes, lead=()):
    flat = packed.reshape(lead + (-1,))
    out, off = [], 0
    for s in shapes:
        n = 1
        for v in s:
            n *= v
        out.append(flat[..., off:off + n].reshape(lead + tuple(s)))
        off += n
    return out


def _pad_lanes(v, at):
    return jnp.pad(v.astype(f32), (at, LANES - at - v.shape[0])).reshape(1, LANES)


def kernel(x, c, norm_g, w_ada, b_ada, w_in, conv_qkv, a_log, dt_bias, dn_norm_g, sg_ln_g, sg_ln_b, sg_w, sg_b, cv_w, cv_b, cv_ln_g, cv_ln_b, w_out, final_g, loss_target, m_norm_g, m_w_ada, m_b_ada, m_w_in, m_conv_qkv, m_a_log, m_dt_bias, m_dn_norm_g, m_sg_ln_g, m_sg_ln_b, m_sg_w, m_sg_b, m_cv_w, m_cv_b, m_cv_ln_g, m_cv_ln_b, m_w_out, m_final_g, v_norm_g, v_w_ada, v_b_ada, v_w_in, v_conv_qkv, v_a_log, v_dt_bias, v_dn_norm_g, v_sg_ln_g, v_sg_ln_b, v_sg_w, v_sg_b, v_cv_w, v_cv_b, v_cv_ln_g, v_cv_ln_b, v_w_out, v_final_g):
    nl, d = norm_g.shape
    t = x.shape[1]
    nh = a_log.shape[1]
    xi, yi, ci = lax.axis_index("x"), lax.axis_index("y"), lax.axis_index("c")
    me = 4 * xi + 2 * yi + ci
    x0 = x[0]
    tgt = loss_target[0]
    ada_cols = w_ada.shape[2]
    in_cols = w_in.shape[2]
    cq_cols = conv_qkv.shape[2]
    cvw_cols = cv_w.shape[2]

    start_shapes = [c.shape, conv_qkv.shape, cv_w.shape]
    start_pack = _pack([c, conv_qkv, cv_w])
    wi_b, wo_b = w_in.astype(BF), w_out.astype(BF)
    g_small, g_win0 = _exchange([start_pack, wi_b[0]], ["gather2", "gather2"], "gather_start")
    c_all, cq_g, cvw_g = _unpack(g_small, start_shapes, lead=(N_DEV,))
    c_all = c_all.reshape(N_DEV, d)
    conv_full = jnp.moveaxis(cq_g, 0, 2).reshape(nl, DN_CONV, N_DEV * cq_cols)
    cvw_full = jnp.moveaxis(cvw_g, 0, 2).reshape(nl, CV_K, N_DEV * cvw_cols)
    w_packed = [_repack_w(g_win0)]
    w_out_full = []

    b_my = lax.dynamic_slice_in_dim(b_ada, me * ada_cols, ada_cols, axis=1).reshape(nl, 1, ada_cols)
    mod_part = _ada_fwd(c_all, w_ada, b_my)
    (mod_g,) = _exchange([mod_part], ["gather"], "gather_mod")
    mod_mine = lax.dynamic_index_in_dim(mod_g, me, axis=2, keepdims=False)
    mod = jnp.moveaxis(mod_mine, 0, 1).reshape(nl, N_DEV * ada_cols)
    shift, scale, gate = mod[:, :d], mod[:, d:2 * d], mod[:, 2 * d:]

    arow = [_pad_lanes(-jnp.exp(a_log[l]), 8) for l in range(nl)]
    dtb = [_pad_lanes(dt_bias[l], 8) for l in range(nl)]
    saved = []
    xl = x0
    for l in range(nl):
        sc1 = (1.0 + scale[l]).reshape(1, d)
        w_main_l, w_small_l = w_packed[l]
        p, ps, hb = _inproj(xl, norm_g[l].reshape(1, d), sc1, shift[l].reshape(1, d), w_main_l, w_small_l)
        pst = ps[:, :16].reshape(t // CHUNK, CHUNK, 16).transpose(0, 2, 1)
        acol = jnp.broadcast_to(arow[l][0, :16].reshape(16, 1), (16, CHUNK))
        dtc = jnp.broadcast_to(dtb[l][0, :16].reshape(16, 1), (16, CHUNK))
        ahead = (wo_b[l], wi_b[l + 1]) if l + 1 < nl else (wo_b[l],)
        o, ss, tinv, g_wout, *g_win_n = _dn_fwd(p, ps, pst, conv_full[l], arow[l], dtb[l], acol, dtc,
                                                comm=(ahead, ("gather2",) * len(ahead)))
        w_out_l = g_wout.reshape(N_DEV * w_out.shape[1], w_out.shape[2])
        w_out_full.append(w_out_l)
        if g_win_n:
            w_packed.append(_repack_w(g_win_n[0]))
        sgbias = jnp.repeat(sg_b[l].T, HD, axis=1)
        cvw_pad = jnp.pad(cvw_full[l], ((0, CV_HALO - CV_K), (0, 0)))
        br_par = (dn_norm_g[l].reshape(1, HD), sg_ln_g[l].reshape(1, -1), sg_ln_b[l].reshape(1, -1), sg_w[l], sgbias, cvw_pad,
                  cv_b[l].reshape(1, -1), cv_ln_g[l].reshape(1, -1), cv_ln_b[l].reshape(1, -1))
        ycat, cvo = _branch_fwd(o, p, *br_par)
        xn = _outproj(ycat, w_out_l, xl, gate[l].reshape(1, d))
        saved.append((xl, p, ps, pst, hb, o, ss, tinv, ycat, sc1, acol, dtc, br_par, cvo))
        xl = xn

    dx, loss_part, dfinal_g = _loss_head(xl, final_g.reshape(1, d), tgt)
    loss = lax.psum(loss_part[0, 0], ("x", "y", "c"))

    small_grads = [None] * nl
    dmods = [None] * nl
    win_r = [None] * nl
    wout_r = [None] * nl
    rep_names = ["b_ada", "norm_g", "a_log", "dt_bias", "dn_norm_g", "sg_ln_g", "sg_ln_b", "sg_w", "sg_b", "cv_b", "cv_ln_g",
                 "cv_ln_b", "final_g"]
    win_slab = None
    for l in reversed(range(nl)):
        xl, p, ps, pst, hb, o, ss, tinv, ycat, sc1, acol, dtc, br_par, cvo = saved[l]
        w_main_l, w_small_l = w_packed[l]
        w_out_l = w_out_full[l]
        gate_l = gate[l].reshape(1, d)
        dycat = _outproj_bwd(dx, w_out_l, gate_l)
        gmat = _tn_acc(ycat, dx, "wout_tn")
        wout_slab, dgate = _wout_grad(gmat, w_out_l, gate_l)
        dpa, do, ddng, dslg, dslb, dsgw, dsgb, dcvw, dcvb, dclg, dclb = _branch_bwd(o, p, *br_par, dycat, cvo)
        behind = (wout_slab,) if win_slab is None else (wout_slab, win_slab)
        dpb, dps, dconvw, dal, ddt, wout_r[l], *win_got = _dn_bwd(
            p, ps, pst, conv_full[l], arow[l], dtb[l], acol, dtc, do, ss, tinv, comm=(behind, ("scatter",) * len(behind)))
        if win_got:
            win_r[l + 1] = win_got[0]
        gwa = _tn_acc(hb, dpa, "win_tn_a")
        gwb, gws = _tn_acc_pair(hb, dpb, dps, "win_tn_b")
        win_slab = _grad_slabs(gwa, gwb, gws, in_cols)
        small_grads[l] = [None, dal[0, 8:8 + nh], ddt[0, 8:8 + nh], ddng[0], dslg[0], dslb[0], dsgw, dsgb[:, :SG_GROUPS].T,
                          dcvb[0], dclg[0], dclb[0], dconvw, dcvw[:CV_K]]
        ib_args = (dpa, dpb, dps, w_main_l, w_small_l, xl, norm_g[l].reshape(1, d), sc1, dx)
        if l > 0:
            dx, dshift, dscale, dng = _inproj_bwd(*ib_args)
            dmods[l] = jnp.concatenate([dshift[0], dscale[0], dgate[0]])
            small_grads[l][0] = dng[0]
        else:
            zeros_d = jnp.zeros((d,), f32)
            dmods[0] = jnp.concatenate([zeros_d, zeros_d, dgate[0]])
            small_grads[0][0] = zeros_d
            stack = lambda j: jnp.stack([small_grads[i][j] for i in range(nl)])
            rep_grads = [jnp.stack(dmods)] + [stack(j) for j in range(11)] + [dfinal_g[0]]
            full_grads = [stack(11), stack(12)]
            grad_shapes = [g.shape for g in rep_grads + full_grads]
            gpack = _pack(rep_grads + full_grads, 1024)
            dx, dshift, dscale, dng, gpack_g, win_r[0] = _inproj_bwd(*ib_args, comm=((gpack, win_slab), ("gather2", "scatter")))
            late_shapes = [(d,), (d,), (d,)]
            (late_g,) = _exchange([_pack([dshift[0], dscale[0], dng[0]])], ["gather"], "exchange_late")

    gsum = _sum_slabs(gpack_g, "sum_small")
    gl = _unpack(gsum, grad_shapes)
    late = _unpack(_sum_slabs(late_g, "sum_late"), late_shapes)
    gl[0] = jnp.concatenate([jnp.concatenate([late[0], late[1], gl[0][0, 2 * d:]])[None], gl[0][1:]], axis=0)
    gl[1] = jnp.concatenate([late[2][None], gl[1][1:]], axis=0)
    g_rep = dict(zip(rep_names, gl[:len(rep_names)]))
    g_conv = lax.dynamic_slice_in_dim(gl[-2], me * cq_cols, cq_cols, axis=2)
    g_cvw = lax.dynamic_slice_in_dim(gl[-1], me * cvw_cols, cvw_cols, axis=2)

    g_win, d_win, nm_win, nv_win = _adam_layers(win_r, w_in, m_w_in, v_w_in, "adam_w_in", 128)
    g_wo, d_wo, nm_wo, nv_wo = _adam_layers(wout_r, w_out, m_w_out, v_w_out, "adam_w_out", 64)
    dmod_all = _unpack(gpack_g, grad_shapes, lead=(N_DEV,))[0]
    late_all = _unpack(late_g, late_shapes, lead=(N_DEV,))
    dmod_all = jnp.concatenate([jnp.concatenate([late_all[0], late_all[1], dmod_all[:, 0, 2 * d:]], axis=1)[:, None],
                                dmod_all[:, 1:]], axis=1)
    dmod_my = jnp.moveaxis(lax.dynamic_slice_in_dim(dmod_all, me * ada_cols, ada_cols, axis=2), 0, 1)
    g_wa, d_wa, nm_wa, nv_wa = _ada_bwd_adam(c_all, dmod_my, w_ada, m_w_ada, v_w_ada)

    small_w = dict(b_ada=b_ada, norm_g=norm_g, a_log=a_log, dt_bias=dt_bias, dn_norm_g=dn_norm_g, sg_ln_g=sg_ln_g, sg_ln_b=sg_ln_b,
                   sg_w=sg_w, sg_b=sg_b, cv_b=cv_b, cv_ln_g=cv_ln_g, cv_ln_b=cv_ln_b, final_g=final_g, conv_qkv=conv_qkv, cv_w=cv_w)
    small_m = dict(b_ada=m_b_ada, norm_g=m_norm_g, a_log=m_a_log, dt_bias=m_dt_bias, dn_norm_g=m_dn_norm_g, sg_ln_g=m_sg_ln_g,
                   sg_ln_b=m_sg_ln_b, sg_w=m_sg_w, sg_b=m_sg_b, cv_b=m_cv_b, cv_ln_g=m_cv_ln_g, cv_ln_b=m_cv_ln_b,
                   final_g=m_final_g, conv_qkv=m_conv_qkv, cv_w=m_cv_w)
    small_v = dict(b_ada=v_b_ada, norm_g=v_norm_g, a_log=v_a_log, dt_bias=v_dt_bias, dn_norm_g=v_dn_norm_g, sg_ln_g=v_sg_ln_g,
                   sg_ln_b=v_sg_ln_b, sg_w=v_sg_w, sg_b=v_sg_b, cv_b=v_cv_b, cv_ln_g=v_cv_ln_g, cv_ln_b=v_cv_ln_b,
                   final_g=v_final_g, conv_qkv=v_conv_qkv, cv_w=v_cv_w)
    small_g = dict(g_rep, conv_qkv=g_conv, cv_w=g_cvw)
    names = rep_names + ["conv_qkv", "cv_w"]
    shapes = [small_w[n].shape for n in names]
    gp = _pack([small_g[n] for n in names], 1024)
    sg_, sd_, sm_, sv_ = _adam_slabs(gp.reshape((1,) + gp.shape), _pack([small_w[n] for n in names], 1024),
                                     _pack([small_m[n] for n in names], 1024), _pack([small_v[n] for n in names], 1024),
                                     "adam_small", 1024)
    sgrad = dict(zip(names, _unpack(sg_, shapes)))
    sdelta = dict(zip(names, _unpack(sd_, shapes)))
    snm = dict(zip(names, _unpack(sm_, shapes)))
    snv = dict(zip(names, _unpack(sv_, shapes)))
    for dct, big in ((sgrad, (g_wa, g_win, g_wo)), (sdelta, (d_wa, d_win, d_wo)), (snm, (nm_wa, nm_win, nm_wo)), (snv, (nv_wa, nv_win, nv_wo))):
        dct["w_ada"] = big[0]
        dct["w_in"] = big[1].reshape(w_in.shape)
        dct["w_out"] = big[2].reshape(w_out.shape)

    order = ["norm_g", "w_ada", "b_ada", "w_in", "conv_qkv", "a_log", "dt_bias", "dn_norm_g", "sg_ln_g", "sg_ln_b", "sg_w", "sg_b",
             "cv_w", "cv_b", "cv_ln_g", "cv_ln_b", "w_out", "final_g"]
    outs = [loss, dx.reshape(x.shape)]
    for dct in (sgrad, sdelta, snm, snv):
        outs += [dct[n] for n in order]
    return tuple(outs)
```
